```python
import math
import jax, jax.numpy as jnp
from jax import lax
import numpy as np

D_MODEL = 1024
BATCH = 8
SEQ = 4096
DEPTH = 1

N_META = 16
BLOCK = 128
PAD = BLOCK - N_META
RET_HEADS = 4
RET_DK = 128
RET_DV = 128
FOX_HEADS = 8
FOX_HD = 64
MIX_WIDTH = RET_HEADS * RET_DV + FOX_HEADS * FOX_HD
IN_SIZES = (RET_HEADS * RET_DK, RET_HEADS * RET_DK, RET_HEADS * RET_DV, RET_HEADS * RET_DV, FOX_HEADS * FOX_HD, FOX_HEADS * FOX_HD, FOX_HEADS * FOX_HD, FOX_HEADS)
IN_COLS = sum(IN_SIZES)
N_EXPERTS = 64
TOP_K = 6
N_GROUPS = 8
TOPK_GROUPS = 4
EXPERT_FF = 256
SHARED_FF = 256
ROUTED_SCALE = 2.5
DISPATCH_BLOCK = 128
ROPE_BASE = 10000.0
LN_EPS = 1e-5
NEG_INF = -1e30
ALPHA = (2.0 * DEPTH) ** 0.25
BETA = (8.0 * DEPTH) ** -0.25

kernel_name = 'hybrid_retention_fox_moe_block'


def layer_norm(x, g, b):
    xf = x.astype(jnp.float32)
    xc = xf - jnp.mean(xf, -1, keepdims=True)
    var = jnp.mean(xc * xc, -1, keepdims=True)
    y = xc * lax.rsqrt(var + LN_EPS) * g.astype(jnp.float32) + b.astype(jnp.float32)
    return y.astype(x.dtype)


def head_norm(y):
    yc = y - jnp.mean(y, -1, keepdims=True)
    return yc * lax.rsqrt(jnp.mean(yc * yc, -1, keepdims=True) + LN_EPS)


def rotary(t, pos):
    half = t.shape[-1] // 2
    inv = ROPE_BASE ** (-jnp.arange(half, dtype=jnp.float32) / half)
    ang = pos[:, None] * inv[None, :]
    cos = jnp.cos(ang)[None, :, None, :]
    sin = jnp.sin(ang)[None, :, None, :]
    t1, t2 = t[..., :half], t[..., half:]
    return jnp.concatenate([t1 * cos - t2 * sin, t2 * cos + t1 * sin], -1)


def retention(q, k, v):
    b, lp, h, dk = q.shape
    dv = v.shape[-1]
    nc = lp // BLOCK
    lg = jnp.log1p(-jnp.exp2(-5.0 - jnp.arange(h, dtype=jnp.float32)))
    idx = jnp.arange(BLOCK, dtype=jnp.float32)
    rel = idx[:, None] - idx[None, :]
    causal = rel >= 0
    dmask = jnp.where(causal[None], jnp.exp(jnp.where(causal, rel, 0.0)[None] * lg[:, None, None]), 0.0)
    zeta = jnp.exp((BLOCK - 1.0 - idx)[None, :] * lg[:, None])
    xi = jnp.exp((idx + 1.0)[None, :] * lg[:, None])
    g_chunk = jnp.exp(BLOCK * lg)
    qc = q.reshape(b, nc, BLOCK, h, dk)
    kc = k.reshape(b, nc, BLOCK, h, dk)
    vc = v.reshape(b, nc, BLOCK, h, dv)
    scores = jnp.einsum('bnihd,bnjhd->bnhij', qc, kc) * dmask[None, None]
    inner = jnp.einsum('bnhij,bnjhe->bnihe', scores, vc)
    kv = jnp.einsum('bnjhd,hj,bnjhe->bnhde', kc, zeta, vc)

    def step(state, kv_n):
        return g_chunk[None, :, None, None] * state + kv_n, state

    _, prev = lax.scan(step, jnp.zeros((b, h, dk, dv), q.dtype), jnp.moveaxis(kv, 1, 0))
    prev = jnp.moveaxis(prev, 0, 1)
    cross = jnp.einsum('bnihd,hi,bnhde->bnihe', qc, xi, prev)
    return (inner + cross).reshape(b, lp, h, dv)


def forgetting_attention(q, k, v, logf):
    b, lp, h, d = q.shape
    scale = d ** -0.5
    c = jnp.transpose(jnp.cumsum(logf, axis=1), (0, 2, 1))
    kpos = jnp.arange(lp)
    outs = []
    for i in range(lp // BLOCK):
        q0, kend = i * BLOCK, (i + 1) * BLOCK
        s = jnp.einsum('bqhd,bkhd->bhqk', q[:, q0:kend], k[:, :kend]) * scale
        s = s + c[:, :, q0:kend, None] - c[:, :, None, :kend]
        qpos = jnp.arange(q0, kend)
        mask = (kpos[None, :kend] <= qpos[:, None]) & (kpos[None, :kend] >= PAD)
        p = jax.nn.softmax(jnp.where(mask[None, None], s, NEG_INF), axis=-1)
        outs.append(jnp.einsum('bhqk,bkhd->bqhd', p, v[:, :kend]))
    return jnp.concatenate(outs, axis=1)


def hybrid_mixer(h, w_in, b_forget, w_out):
    b, l, _ = h.shape
    lp = l + PAD
    proj = jnp.einsum('bld,dc->blc', h, w_in).astype(jnp.float32)
    points, acc = [], 0
    for sz in IN_SIZES[:-1]:
        acc += sz
        points.append(acc)
    rq, rk, rv, rg, fq, fk, fv, ff = jnp.split(proj, points, axis=-1)
    logf = jax.nn.log_sigmoid(ff + b_forget.astype(jnp.float32))

    def pad(t):
        return jnp.pad(t, ((0, 0), (PAD, 0), (0, 0)))

    def heads(t, nh):
        return pad(t).reshape(b, lp, nh, -1)

    pos = jnp.arange(lp, dtype=jnp.float32) - PAD
    ret = retention(rotary(heads(rq, RET_HEADS), pos),
                    rotary(heads(rk, RET_HEADS), pos) * (RET_DK ** -0.5),
                    heads(rv, RET_HEADS))[:, PAD:]
    ret = head_norm(ret).reshape(b, l, RET_HEADS * RET_DV) * jax.nn.silu(rg)
    fox = forgetting_attention(heads(fq, FOX_HEADS), heads(fk, FOX_HEADS), heads(fv, FOX_HEADS), pad(logf))[:, PAD:]
    fox = fox.reshape(b, l, FOX_HEADS * FOX_HD)
    mixed = jnp.concatenate([ret, fox], axis=-1).astype(h.dtype)
    return jnp.einsum('blc,cd->bld', mixed, w_out)


def swiglu(t, wg, wu, wd):
    return (jax.nn.silu(t @ wg) * (t @ wu)) @ wd


def routed_experts(t, w_router, router_bias, we_gate, we_up, we_down):
    n, d = t.shape
    scores = jax.nn.sigmoid((t @ w_router).astype(jnp.float32))
    biased = scores + router_bias.astype(jnp.float32)
    grp = biased.reshape(n, N_GROUPS, N_EXPERTS // N_GROUPS)
    group_score = jnp.sum(lax.top_k(grp, 2)[0], axis=-1)
    _, top_groups = lax.top_k(group_score, TOPK_GROUPS)
    gmask = jnp.any(top_groups[:, :, None] == jnp.arange(N_GROUPS)[None, None, :], axis=1)
    emask = jnp.repeat(gmask, N_EXPERTS // N_GROUPS, axis=1)
    _, top_e = lax.top_k(jnp.where(emask, biased, NEG_INF), TOP_K)
    gates = jnp.take_along_axis(scores, top_e, axis=1)
    gates = gates / jnp.sum(gates, -1, keepdims=True) * ROUTED_SCALE
    nk = n * TOP_K
    e_flat = top_e.reshape(nk)
    tok_flat = jnp.repeat(jnp.arange(n, dtype=jnp.int32), TOP_K)
    order = jnp.argsort(e_flat)
    e_s, tok_s, g_s = e_flat[order], tok_flat[order], gates.reshape(nk)[order]
    counts = jnp.bincount(e_flat, length=N_EXPERTS)
    padded = (counts + DISPATCH_BLOCK - 1) // DISPATCH_BLOCK * DISPATCH_BLOCK
    start = jnp.cumsum(counts) - counts
    pend = jnp.cumsum(padded)
    pstart = pend - padded
    dest = pstart[e_s] + jnp.arange(nk, dtype=jnp.int32) - start[e_s]
    n_blocks = -(-(nk + N_EXPERTS * (DISPATCH_BLOCK - 1)) // DISPATCH_BLOCK)
    rows = n_blocks * DISPATCH_BLOCK
    buf_tok = jnp.full((rows,), n, jnp.int32).at[dest].set(tok_s)
    buf_gate = jnp.zeros((rows,), jnp.float32).at[dest].set(g_s)
    blk_e = jnp.minimum(jnp.searchsorted(pend, jnp.arange(n_blocks, dtype=jnp.int32) * DISPATCH_BLOCK, side='right'), N_EXPERTS - 1)
    t_pad = jnp.concatenate([t, jnp.zeros((1, d), t.dtype)], axis=0)

    def step(acc, blk):
        tok, gate, e = blk
        y = swiglu(t_pad[tok], we_gate[e], we_up[e], we_down[e])
        return acc.at[tok].add(y * gate[:, None].astype(y.dtype)), None

    acc, _ = lax.scan(step, jnp.zeros((n + 1, d), t.dtype),
                      (buf_tok.reshape(n_blocks, DISPATCH_BLOCK), buf_gate.reshape(n_blocks, DISPATCH_BLOCK), blk_e))
    return acc[:n]


def setup_inputs(seed: int = 0) -> dict:
    key = jax.random.key(seed)
    ks = jax.random.split(key, 19)

    def nrm(k, shape, scale):
        return jax.random.normal(k, shape, jnp.float32) * scale

    return {
        'x': nrm(ks[0], (BATCH, SEQ, D_MODEL), 1.0),
        'meta': nrm(ks[1], (N_META, D_MODEL), 1.0),
        'ln0_g': 1.0 + nrm(ks[2], (D_MODEL,), 0.01),
        'ln0_b': nrm(ks[3], (D_MODEL,), 0.01),
        'w_in': nrm(ks[4], (DEPTH, D_MODEL, IN_COLS), D_MODEL ** -0.5),
        'b_forget': jax.random.uniform(ks[5], (DEPTH, FOX_HEADS), jnp.float32, 1.0, 4.0),
        'w_out': nrm(ks[6], (DEPTH, MIX_WIDTH, D_MODEL), BETA * MIX_WIDTH ** -0.5),
        'ln1_g': 1.0 + nrm(ks[7], (DEPTH, D_MODEL), 0.01),
        'ln1_b': nrm(ks[8], (DEPTH, D_MODEL), 0.01),
        'w_router': nrm(ks[9], (DEPTH, D_MODEL, N_EXPERTS), D_MODEL ** -0.5),
        'router_bias': nrm(ks[10], (DEPTH, N_EXPERTS), 0.01),
        'we_gate': nrm(ks[11], (DEPTH, N_EXPERTS, D_MODEL, EXPERT_FF), D_MODEL ** -0.5),
        'we_up': nrm(ks[12], (DEPTH, N_EXPERTS, D_MODEL, EXPERT_FF), D_MODEL ** -0.5),
        'we_down': nrm(ks[13], (DEPTH, N_EXPERTS, EXPERT_FF, D_MODEL), BETA * EXPERT_FF ** -0.5),
        'ws_gate': nrm(ks[14], (DEPTH, D_MODEL, SHARED_FF), D_MODEL ** -0.5),
        'ws_up': nrm(ks[15], (DEPTH, D_MODEL, SHARED_FF), D_MODEL ** -0.5),
        'ws_down': nrm(ks[16], (DEPTH, SHARED_FF, D_MODEL), BETA * SHARED_FF ** -0.5),
        'ln2_g': 1.0 + nrm(ks[17], (DEPTH, D_MODEL), 0.01),
        'ln2_b': nrm(ks[18], (DEPTH, D_MODEL), 0.01),
    }


def reference(x, meta, ln0_g, ln0_b, w_in, b_forget, w_out, ln1_g, ln1_b, w_router, router_bias,
              we_gate, we_up, we_down, ws_gate, ws_up, ws_down, ln2_g, ln2_b):
    b, s, d = x.shape
    h = jnp.concatenate([jnp.broadcast_to(meta[None].astype(x.dtype), (b, N_META, d)), x], axis=1)
    h = layer_norm(h, ln0_g, ln0_b)
    l = h.shape[1]
    for i in range(DEPTH):
        h = layer_norm(ALPHA * h + hybrid_mixer(h, w_in[i], b_forget[i], w_out[i]), ln1_g[i], ln1_b[i])
        t = h.reshape(b * l, d)
        moe = swiglu(t, ws_gate[i], ws_up[i], ws_down[i]) + routed_experts(t, w_router[i], router_bias[i], we_gate[i], we_up[i], we_down[i])
        h = layer_norm(ALPHA * h + moe.reshape(b, l, d), ln2_g[i], ln2_b[i])
    return h[:, N_META:]
```

```python
import functools

import jax
import jax.numpy as jnp
from jax import lax
from jax.experimental import pallas as pl
from jax.experimental.pallas import tpu as pltpu

D_MODEL = 1024
N_META = 16
BLOCK = 128
PAD = BLOCK - N_META
RET_HEADS = 4
RET_DK = 128
FOX_HEADS = 8
FOX_HD = 64
N_EXPERTS = 64
TOP_K = 6
N_GROUPS = 8
GROUP_SIZE = N_EXPERTS // N_GROUPS
TOPK_GROUPS = 4
EXPERT_FF = 256
SHARED_FF = 256
ROUTED_SCALE = 2.5
ROPE_BASE = 10000.0
LN_EPS = 1e-5
NEG_INF = -1e30
ALPHA = 2.0 ** 0.25
HEAD_W = 512

LANES = 128
SUBLANES = 8
ROW_TILES = D_MODEL // LANES

TM_PROJ = 512
T_FOX = 512
TT = 256
EBLK = 256
VMEM_LIMIT = 48 * 1024 * 1024

_F32 = jnp.float32
_BF16 = jnp.bfloat16


def _ln(x, g, b):
    xc = x - jnp.mean(x, -1, keepdims=True)
    var = jnp.mean(xc * xc, -1, keepdims=True)
    return xc * lax.rsqrt(var + LN_EPS) * g + b


def _dot(a, b):
    return jnp.dot(a, b, preferred_element_type=_F32)


def _dot_nt(a, b):
    return lax.dot_general(a, b, (((1,), (1,)), ((), ())), preferred_element_type=_F32)


def _dot_tn(a, b):
    return lax.dot_general(a, b, (((0,), (0,)), ((), ())), preferred_element_type=_F32)


def _silu(x):
    return x * jax.nn.sigmoid(x)


def _params(n_axes):
    return pltpu.CompilerParams(dimension_semantics=("arbitrary",) * n_axes, vmem_limit_bytes=VMEM_LIMIT)


def _inproj_body(x_ref, g_ref, b_ref, w_ref, bf_ref, cos_ref, sin_ref,
                 rq_ref, rk_ref, rv_ref, rg_ref, fq_ref, fk_ref, fv_ref, c_ref, carry_ref, *, meta):
    tm = x_ref.shape[0]
    h = _ln(x_ref[...], g_ref[...], b_ref[...])
    if meta:
        valid = lax.broadcasted_iota(jnp.int32, (tm, 1), 0) >= PAD
        h = jnp.where(valid, h, 0.0)
    hb = h.astype(_BF16)
    cos = cos_ref[...]
    sin = sin_ref[...]

    def proj(g):
        return _dot(hb, w_ref[:, g * HEAD_W:(g + 1) * HEAD_W])

    def rope_store(p, out_ref, scale):
        for hd in range(RET_HEADS):
            t = p[:, hd * LANES:(hd + 1) * LANES]
            r = t * cos + pltpu.roll(t, LANES // 2, axis=1) * sin
            out_ref[:, hd * LANES:(hd + 1) * LANES] = (r * scale).astype(_BF16)

    rope_store(proj(0), rq_ref, 1.0)
    rope_store(proj(1), rk_ref, RET_DK ** -0.5)
    rv_ref[...] = proj(2).astype(_BF16)
    rg_ref[...] = _silu(proj(3)).astype(_BF16)
    fq_ref[...] = (proj(4) * (FOX_HD ** -0.5)).astype(_BF16)
    fk_ref[...] = proj(5).astype(_BF16)
    fv_ref[...] = proj(6).astype(_BF16)

    z = _dot(hb, w_ref[:, 7 * HEAD_W:7 * HEAD_W + LANES]) + bf_ref[...]
    logf = jnp.minimum(z, 0.0) - jnp.log1p(jnp.exp(-jnp.abs(z)))
    if meta:
        logf = jnp.where(valid, logf, 0.0)
    l1 = logf.astype(_BF16)
    r1 = logf - l1.astype(_F32)
    l2 = r1.astype(_BF16)
    l3 = (r1 - l2.astype(_F32)).astype(_BF16)
    row = lax.broadcasted_iota(jnp.int32, (tm, tm), 0)
    col = lax.broadcasted_iota(jnp.int32, (tm, tm), 1)
    tri = (col <= row).astype(_BF16)
    c = _dot(tri, l1) + _dot(tri, l2) + _dot(tri, l3)
    if meta:
        c = c - c[tm - 1:tm, :]
    else:
        @pl.when(pl.program_id(1) == 0)
        def _():
            carry_ref[...] = jnp.zeros_like(carry_ref)

        c = c + carry_ref[...]
        carry_ref[...] = c[tm - 1:tm, :]
    c_ref[...] = c[:, :FOX_HEADS]


def _inproj(x2d, ln_g, ln_b, w_all, bf_pad, cos_t, sin_t, *, nb, meta):
    n = x2d.shape[0]
    s = n // nb
    tm = min(TM_PROJ, s)
    nj = s // tm
    row_spec = lambda w: pl.BlockSpec((tm, w), lambda b, j: (b * nj + j, 0))
    const = lambda shape: pl.BlockSpec(shape, lambda b, j: (0, 0))
    pos_spec = pl.BlockSpec((tm, LANES), lambda b, j: (j, 0))
    outs = [jax.ShapeDtypeStruct((n, HEAD_W), _BF16)] * 7 + [jax.ShapeDtypeStruct((n, FOX_HEADS), _F32)]
    return pl.pallas_call(
        functools.partial(_inproj_body, meta=meta),
        grid=(nb, nj),
        in_specs=[row_spec(D_MODEL), const((1, D_MODEL)), const((1, D_MODEL)), const(w_all.shape),
                  const((1, LANES)), pos_spec, pos_spec],
        out_specs=[row_spec(HEAD_W)] * 7 + [row_spec(FOX_HEADS)],
        out_shape=outs,
        scratch_shapes=[pltpu.VMEM((1, LANES), _F32)],
        compiler_params=_params(2),
        name="inproj_meta" if meta else "inproj",
    )(x2d, ln_g, ln_b, w_all, bf_pad, cos_t, sin_t)


def _ret_body(q_ref, k_ref, v_ref, g_ref, km_ref, vm_ref, dm_ref, xi_ref, zeta_ref, gch_ref, o_ref, st_ref):
    def kv_update(k, v, hd):
        vz = (v.astype(_F32) * zeta_ref[hd]).astype(_BF16)
        return _dot_tn(k, vz)

    @pl.when(pl.program_id(1) == 0)
    def _():
        for hd in range(RET_HEADS):
            sl = slice(hd * LANES, (hd + 1) * LANES)
            st_ref[hd] = kv_update(km_ref[:, sl], vm_ref[:, sl], hd)

    for hd in range(RET_HEADS):
        sl = slice(hd * LANES, (hd + 1) * LANES)
        q = q_ref[:, sl]
        k = k_ref[:, sl]
        v = v_ref[:, sl]
        st = st_ref[hd]
        scores = _dot_nt(q, k) * dm_ref[hd]
        o = _dot(scores.astype(_BF16), v) + _dot(q, st.astype(_BF16)) * xi_ref[hd]
        oc = o - jnp.mean(o, -1, keepdims=True)
        y = oc * lax.rsqrt(jnp.mean(oc * oc, -1, keepdims=True) + LN_EPS)
        o_ref[:, sl] = (y * g_ref[:, sl].astype(_F32)).astype(_BF16)
        st_ref[hd] = gch_ref[hd] * st + kv_update(k, v, hd)


def _retention(rq, rk, rv, rg, rk_m, rv_m, dmask, xi, zeta, gch, *, nb):
    n = rq.shape[0]
    nc = n // nb // BLOCK
    row_spec = pl.BlockSpec((BLOCK, HEAD_W), lambda b, j: (b * nc + j, 0))
    meta_spec = pl.BlockSpec((BLOCK, HEAD_W), lambda b, j: (0, 0))
    tab = lambda shape: pl.BlockSpec(shape, lambda b, j: (0,) * len(shape))
    return pl.pallas_call(
        _ret_body,
        grid=(nb, nc),
        in_specs=[row_spec] * 4 + [meta_spec] * 2
        + [tab((RET_HEADS, BLOCK, BLOCK)), tab((RET_HEADS, BLOCK, 1)), tab((RET_HEADS, BLOCK, 1)),
           pl.BlockSpec(memory_space=pltpu.SMEM)],
        out_specs=row_spec,
        out_shape=jax.ShapeDtypeStruct((n, HEAD_W), _BF16),
        scratch_shapes=[pltpu.VMEM((RET_HEADS, RET_DK, LANES), _F32)],
        compiler_params=_params(2),
        name="retention",
    )(rq, rk, rv, rg, rk_m, rv_m, dmask, xi, zeta, gch)


def _fox_body(q_ref, k_ref, v_ref, km_ref, vm_ref, cq_ref, ck_ref, cm_ref, o_ref, m_sc, l_sc, acc_sc):
    t = q_ref.shape[0]
    qi = pl.program_id(2)
    q = q_ref[...]
    lane = lax.broadcasted_iota(jnp.int32, (t, LANES), 1)
    qh = [jnp.where(lane < FOX_HD, q, jnp.zeros_like(q)), jnp.where(lane >= FOX_HD, q, jnp.zeros_like(q))]
    cq = cq_ref[...]

    km = km_ref[...]
    vm = vm_ref[...]
    for hh in range(2):
        s = _dot_nt(qh[hh], km) + (cq[:, hh:hh + 1] - cm_ref[hh:hh + 1, :])
        s = jnp.where(lane >= PAD, s, NEG_INF)
        m = jnp.max(s, axis=1, keepdims=True)
        p = jnp.exp(s - m)
        m_sc[hh] = m
        l_sc[hh] = jnp.sum(p, axis=1, keepdims=True)
        acc_sc[hh] = _dot(p.astype(_BF16), vm)

    def step(ki, masked):
        off = pl.multiple_of(ki * t, t)
        k = k_ref[pl.ds(off, t), :]
        v = v_ref[pl.ds(off, t), :]
        ck = ck_ref[ki]
        for hh in range(2):
            s = _dot_nt(qh[hh], k) + (cq[:, hh:hh + 1] - ck[hh:hh + 1, :])
            if masked:
                row = lax.broadcasted_iota(jnp.int32, (t, t), 0)
                col = lax.broadcasted_iota(jnp.int32, (t, t), 1)
                s = jnp.where(col <= row, s, NEG_INF)
            m_prev = m_sc[hh]
            m_new = jnp.maximum(m_prev, jnp.max(s, axis=1, keepdims=True))
            alpha = jnp.exp(m_prev - m_new)
            p = jnp.exp(s - m_new)
            l_sc[hh] = alpha * l_sc[hh] + jnp.sum(p, axis=1, keepdims=True)
            acc_sc[hh] = alpha * acc_sc[hh] + _dot(p.astype(_BF16), v)
            m_sc[hh] = m_new

    def loop_body(ki, carry):
        step(ki, False)
        return carry

    lax.fori_loop(0, qi, loop_body, 0)
    step(qi, True)
    out = jnp.where(lane < FOX_HD, acc_sc[0] / l_sc[0], acc_sc[1] / l_sc[1])
    o_ref[...] = out.astype(_BF16)


def _fox(fq, fk, fv, fk_m, fv_m, cq, ck, cm, *, nb):
    s = fq.shape[1]
    t = T_FOX
    nq = s // t
    pairs = FOX_HEADS // 2
    return pl.pallas_call(
        _fox_body,
        grid=(nb, pairs, nq),
        in_specs=[
            pl.BlockSpec((None, t, LANES), lambda b, p, i: (b, i, p)),
            pl.BlockSpec((None, s, LANES), lambda b, p, i: (b, 0, p)),
            pl.BlockSpec((None, s, LANES), lambda b, p, i: (b, 0, p)),
            pl.BlockSpec((BLOCK, LANES), lambda b, p, i: (0, p)),
            pl.BlockSpec((BLOCK, LANES), lambda b, p, i: (0, p)),
            pl.BlockSpec((None, None, t, 2), lambda b, p, i: (b, p, i, 0)),
            pl.BlockSpec((None, None, nq, 2, t), lambda b, p, i: (b, p, 0, 0, 0)),
            pl.BlockSpec((None, 2, BLOCK), lambda b, p, i: (p, 0, 0)),
        ],
        out_specs=pl.BlockSpec((None, t, LANES), lambda b, p, i: (b, i, p)),
        out_shape=jax.ShapeDtypeStruct((nb, s, HEAD_W), _BF16),
        scratch_shapes=[pltpu.VMEM((2, t, 1), _F32), pltpu.VMEM((2, t, 1), _F32), pltpu.VMEM((2, t, LANES), _F32)],
        compiler_params=_params(3),
        name="fox",
    )(fq, fk, fv, fk_m, fv_m, cq, ck, cm)


def _post_body(x_ref, ret_ref, fox_ref, g0_ref, b0_ref, wo_ref, g1_ref, b1_ref, wr_ref, rb_ref, wgu_ref, wd_ref,
               base_ref, trow_ref, sel_ref, gate_ref, rank_ref, cnt_ref, run_ref):
    tm = x_ref.shape[0]
    i = pl.program_id(0)

    @pl.when(i == 0)
    def _():
        run_ref[...] = jnp.zeros_like(run_ref)

    h0 = _ln(x_ref[...], g0_ref[...], b0_ref[...])
    y = _dot(ret_ref[...], wo_ref[:HEAD_W, :]) + _dot(fox_ref[...], wo_ref[HEAD_W:, :])
    h1 = _ln(ALPHA * h0 + y, g1_ref[...], b1_ref[...])
    tb = h1.astype(_BF16)
    for s in range(ROW_TILES):
        trow_ref[pl.ds(s, tm, stride=ROW_TILES), :] = h1[:, s * LANES:(s + 1) * LANES]

    gu = _dot(tb, wgu_ref[...])
    mid = (_silu(gu[:, :SHARED_FF]) * gu[:, SHARED_FF:]).astype(_BF16)
    base_ref[...] = ALPHA * h1 + _dot(mid, wd_ref[...])

    scores = jax.nn.sigmoid(_dot_nt(wr_ref[...], tb))
    biased = scores + rb_ref[...]
    sub = lax.broadcasted_iota(jnp.int32, (GROUP_SIZE, tm), 0).astype(_F32)
    gscore = []
    for g in range(N_GROUPS):
        v = biased[g * GROUP_SIZE:(g + 1) * GROUP_SIZE, :]
        m1 = jnp.max(v, axis=0, keepdims=True)
        i1 = jnp.min(jnp.where(v == m1, sub, float(GROUP_SIZE)), axis=0, keepdims=True)
        m2 = jnp.max(jnp.where(sub == i1, -jnp.inf, v), axis=0, keepdims=True)
        gscore.append(m1 + m2)
    masked = []
    for g in range(N_GROUPS):
        beaten = jnp.zeros((1, tm), _F32)
        for o in range(N_GROUPS):
            if o == g:
                continue
            wins = (gscore[o] >= gscore[g]) if o < g else (gscore[o] > gscore[g])
            beaten = beaten + wins.astype(_F32)
        keep = beaten < float(TOPK_GROUPS)
        masked.append(jnp.where(keep, biased[g * GROUP_SIZE:(g + 1) * GROUP_SIZE, :], NEG_INF))
    work = jnp.concatenate(masked, axis=0)
    eid = lax.broadcasted_iota(jnp.int32, (N_EXPERTS, tm), 0).astype(_F32)
    hots, sels, raws = [], [], []
    for _ in range(TOP_K):
        m = jnp.max(work, axis=0, keepdims=True)
        idx = jnp.min(jnp.where(work == m, eid, float(N_EXPERTS)), axis=0, keepdims=True)
        hot = eid == idx
        hots.append(hot)
        sels.append(idx.astype(jnp.int32))
        raws.append(jnp.sum(jnp.where(hot, scores, 0.0), axis=0, keepdims=True))
        work = jnp.where(hot, -jnp.inf, work)
    total = raws[0]
    for r in raws[1:]:
        total = total + r
    inv = ROUTED_SCALE / total
    member = hots[0]
    for hot in hots[1:]:
        member = member | hot
    member_f = member.astype(_F32)
    row = lax.broadcasted_iota(jnp.int32, (tm, tm), 0)
    col = lax.broadcasted_iota(jnp.int32, (tm, tm), 1)
    before = (row < col).astype(_BF16)
    rank_e = _dot(member_f.astype(_BF16), before) + run_ref[...]
    pad_rows = sel_ref.shape[0] - TOP_K
    sel_ref[...] = jnp.concatenate(sels + [jnp.zeros((pad_rows, tm), jnp.int32)], axis=0)
    gate_ref[...] = jnp.concatenate([r * inv for r in raws] + [jnp.zeros((pad_rows, tm), _F32)], axis=0)
    ranks = [jnp.sum(jnp.where(hot, rank_e, 0.0), axis=0, keepdims=True).astype(jnp.int32) for hot in hots]
    rank_ref[...] = jnp.concatenate(ranks + [jnp.zeros((pad_rows, tm), jnp.int32)], axis=0)
    run_ref[...] = run_ref[...] + jnp.sum(member_f, axis=1, keepdims=True)
    cnt_ref[...] = run_ref[...]


def _post(x2d, ret, fox, ln0_g, ln0_b, w_out, ln1_g, ln1_b, w_rt, rbias, w_gu, w_sd):
    n = x2d.shape[0]
    tm = TM_PROJ
    row_spec = lambda w: pl.BlockSpec((tm, w), lambda i: (i, 0))
    col_spec = pl.BlockSpec((SUBLANES, tm), lambda i: (0, i))
    const = lambda shape: pl.BlockSpec(shape, lambda i: (0, 0))
    return pl.pallas_call(
        _post_body,
        grid=(n // tm,),
        in_specs=[row_spec(D_MODEL), row_spec(HEAD_W), row_spec(HEAD_W), const((1, D_MODEL)), const((1, D_MODEL)),
                  const(w_out.shape), const((1, D_MODEL)), const((1, D_MODEL)), const(w_rt.shape),
                  const((N_EXPERTS, 1)), const(w_gu.shape), const(w_sd.shape)],
        out_specs=[row_spec(D_MODEL), pl.BlockSpec((tm * ROW_TILES, LANES), lambda i: (i, 0)),
                   col_spec, col_spec, col_spec, const((N_EXPERTS, 1))],
        out_shape=[jax.ShapeDtypeStruct((n, D_MODEL), _F32), jax.ShapeDtypeStruct((n * ROW_TILES, LANES), _F32),
                   jax.ShapeDtypeStruct((SUBLANES, n), jnp.int32), jax.ShapeDtypeStruct((SUBLANES, n), _F32),
                   jax.ShapeDtypeStruct((SUBLANES, n), jnp.int32), jax.ShapeDtypeStruct((N_EXPERTS, 1), _F32)],
        scratch_shapes=[pltpu.VMEM((N_EXPERTS, 1), _F32)],
        compiler_params=_params(1),
        name="post_mixer",
    )(x2d, ret, fox, ln0_g, ln0_b, w_out, ln1_g, ln1_b, w_rt, rbias, w_gu, w_sd)


def _plan_body(sel_ref, rank_ref, cnt_ref, dest_ref, blk_ref, fill_ref, used_ref, last_ref):
    cnt = cnt_ref[...]
    padded = jnp.ceil(cnt * (1.0 / EBLK)) * EBLK
    er = lax.broadcasted_iota(jnp.int32, (N_EXPERTS, N_EXPERTS), 0)
    ec = lax.broadcasted_iota(jnp.int32, (N_EXPERTS, N_EXPERTS), 1)
    padded_row = jnp.sum(jnp.where(er == ec, padded, 0.0), axis=0, keepdims=True)
    pstart = jnp.sum(jnp.where(ec < er, padded_row, 0.0), axis=1, keepdims=True)
    pend = pstart + padded
    sel = sel_ref[...]
    dest = rank_ref[...]
    for e in range(N_EXPERTS):
        dest = dest + jnp.where(sel == e, pstart[e:e + 1, :].astype(jnp.int32), 0)
    dest_ref[...] = dest
    nblk = blk_ref.shape[1]
    first_row = (lax.broadcasted_iota(jnp.int32, (N_EXPERTS, nblk), 1) * EBLK).astype(_F32)
    owner = jnp.minimum(jnp.sum((pend <= first_row).astype(_F32), axis=0, keepdims=True), N_EXPERTS - 1.0)
    blk_ref[...] = owner.astype(jnp.int32)
    mine = lax.broadcasted_iota(jnp.int32, (N_EXPERTS, nblk), 0).astype(_F32) == owner
    live_end = jnp.sum(jnp.where(mine, pstart + cnt, 0.0), axis=0, keepdims=True)
    fill_ref[...] = jnp.clip(live_end - first_row[:1, :], 0.0, float(EBLK)).astype(jnp.int32)
    used_ref[...] = (pend[N_EXPERTS - 1:, :] * (1.0 / EBLK)).astype(jnp.int32)
    last = jnp.where(padded > 0.0, pend * (1.0 / EBLK) - 1.0, -1.0)
    last_ref[...] = jnp.sum(jnp.where(er == ec, last, 0.0), axis=0, keepdims=True).astype(jnp.int32)


def _plan(sel, rank, cnt, nblk_pad):
    n = sel.shape[1]
    full = lambda shape: pl.BlockSpec(shape, lambda i: (0, 0))
    return pl.pallas_call(
        _plan_body,
        grid=(1,),
        in_specs=[full(sel.shape), full(rank.shape), full(cnt.shape)],
        out_specs=[full(sel.shape), full((1, nblk_pad)), full((1, nblk_pad)), full((1, 1)), full((1, N_EXPERTS))],
        out_shape=[jax.ShapeDtypeStruct((SUBLANES, n), jnp.int32), jax.ShapeDtypeStruct((1, nblk_pad), jnp.int32),
                   jax.ShapeDtypeStruct((1, nblk_pad), jnp.int32), jax.ShapeDtypeStruct((1, 1), jnp.int32),
                   jax.ShapeDtypeStruct((1, N_EXPERTS), jnp.int32)],
        compiler_params=_params(1),
        name="plan",
    )(sel, rank, cnt)


def _row_copy(src, src_row, dst, dst_row, sem):
    return pltpu.make_async_copy(src.at[pl.ds(pl.multiple_of(src_row * ROW_TILES, ROW_TILES), ROW_TILES), :],
                                 dst.at[pl.ds(pl.multiple_of(dst_row * ROW_TILES, ROW_TILES), ROW_TILES), :], sem)


def _dispatch_body(dest_ref, last_ref, used_ref, t_ref, xs_ref, zero_sc, sem, zsem):
    base = pl.program_id(0) * TT
    blk_rows = EBLK * ROW_TILES
    nblk = xs_ref.shape[0] // blk_rows

    @pl.when(pl.program_id(0) == 0)
    def _():
        zero_sc[...] = jnp.zeros_like(zero_sc)

        def blk_copy(b):
            return pltpu.make_async_copy(zero_sc, xs_ref.at[pl.ds(pl.multiple_of(b * blk_rows, blk_rows), blk_rows), :],
                                         zsem)

        def for_padded_blocks(act):
            def per_expert(e, carry):
                @pl.when(last_ref[e] >= 0)
                def _():
                    act(blk_copy(last_ref[e]))

                return carry

            def per_tail(b, carry):
                act(blk_copy(b))
                return carry

            lax.fori_loop(0, N_EXPERTS, per_expert, 0)
            lax.fori_loop(used_ref[0], nblk, per_tail, 0)

        for_padded_blocks(lambda cp: cp.start())
        for_padded_blocks(lambda cp: cp.wait())

    def issue(i, carry):
        for k in range(TOP_K):
            _row_copy(t_ref, base + i, xs_ref, dest_ref[i * SUBLANES + k], sem).start()
        return carry

    lax.fori_loop(0, TT, issue, 0)
    rows = TT * TOP_K * ROW_TILES
    pltpu.make_async_copy(t_ref.at[pl.ds(0, rows), :], xs_ref.at[pl.ds(0, rows), :], sem).wait()


def _dispatch(dest_flat, last_blk, used, trow, total_rows):
    n = trow.shape[0] // ROW_TILES
    smem = pl.BlockSpec(memory_space=pltpu.SMEM)
    return pl.pallas_call(
        _dispatch_body,
        grid=(n // TT,),
        in_specs=[pl.BlockSpec((TT * SUBLANES,), lambda i: (i,), memory_space=pltpu.SMEM), smem, smem,
                  pl.BlockSpec(memory_space=pl.ANY)],
        out_specs=pl.BlockSpec(memory_space=pl.ANY),
        out_shape=jax.ShapeDtypeStruct((total_rows * ROW_TILES, LANES), _F32),
        scratch_shapes=[pltpu.VMEM((EBLK * ROW_TILES, LANES), _F32), pltpu.SemaphoreType.DMA(()),
                        pltpu.SemaphoreType.DMA(())],
        compiler_params=_params(1),
        name="dispatch",
    )(dest_flat, last_blk, used, trow)


def _expert_body(blk_ref, used_ref, fill_ref, xs_ref, wg_ref, wu_ref, wd_ref, y_ref, wgu_sc, wd_sc):
    i = pl.program_id(0)
    prev = blk_ref[jnp.maximum(i - 1, 0)]
    fresh = (i == 0) | (blk_ref[i] != prev)

    @pl.when(fresh)
    def _():
        wgu_sc[:, :EXPERT_FF] = wg_ref[...].astype(_BF16)
        wgu_sc[:, EXPERT_FF:] = wu_ref[...].astype(_BF16)
        wd_sc[...] = wd_ref[...].astype(_BF16)

    @pl.when(i < used_ref[0])
    def _():
        x = jnp.concatenate([xs_ref[pl.ds(s, EBLK, stride=ROW_TILES), :] for s in range(ROW_TILES)], axis=1)
        live = lax.broadcasted_iota(jnp.int32, (EBLK, 1), 0) < fill_ref[i]
        x = jnp.where(live, x, 0.0).astype(_BF16)
        gu = _dot(x, wgu_sc[...])
        mid = (_silu(gu[:, :EXPERT_FF]) * gu[:, EXPERT_FF:]).astype(_BF16)
        y = _dot(mid, wd_sc[...])
        for s in range(ROW_TILES):
            y_ref[pl.ds(s, EBLK, stride=ROW_TILES), :] = y[:, s * LANES:(s + 1) * LANES]

    @pl.when(i >= used_ref[0])
    def _():
        y_ref[...] = jnp.zeros_like(y_ref)


def _experts(blk_e, used, fill, xs, we_gate, we_up, we_down):
    nblk = xs.shape[0] // (EBLK * ROW_TILES)
    last = lambda i, used: jnp.minimum(i, jnp.maximum(used[0] - 1, 0))
    w_spec = lambda shape: pl.BlockSpec((None,) + shape, lambda i, blk, used, fill: (blk[i], 0, 0))
    return pl.pallas_call(
        _expert_body,
        grid_spec=pltpu.PrefetchScalarGridSpec(
            num_scalar_prefetch=3,
            grid=(nblk,),
            in_specs=[pl.BlockSpec((EBLK * ROW_TILES, LANES), lambda i, blk, used, fill: (last(i, used), 0)),
                      w_spec((D_MODEL, EXPERT_FF)), w_spec((D_MODEL, EXPERT_FF)), w_spec((EXPERT_FF, D_MODEL))],
            out_specs=pl.BlockSpec((EBLK * ROW_TILES, LANES), lambda i, blk, used, fill: (i, 0)),
            scratch_shapes=[pltpu.VMEM((D_MODEL, 2 * EXPERT_FF), _BF16), pltpu.VMEM((EXPERT_FF, D_MODEL), _BF16)],
        ),
        out_shape=jax.ShapeDtypeStruct(xs.shape, _F32),
        compiler_params=_params(1),
        name="experts",
    )(blk_e, used, fill, xs, we_gate, we_up, we_down)


def _combine_body(dest_ref, y_ref, base_ref, gate_ref, g2_ref, b2_ref, o_ref, z_sc, sem):
    def issue(i, carry):
        for k in range(TOP_K):
            _row_copy(y_ref, dest_ref[i * SUBLANES + k], z_sc, k * TT + i, sem).start()
        return carry

    lax.fori_loop(0, TT, issue, 0)
    rows = TT * TOP_K * ROW_TILES
    pltpu.make_async_copy(y_ref.at[pl.ds(0, rows), :], z_sc, sem).wait()
    acc = base_ref[...]
    gates = gate_ref[...]
    for k in range(TOP_K):
        z = jnp.concatenate([z_sc[pl.ds(k * TT * ROW_TILES + s, TT, stride=ROW_TILES), :] for s in range(ROW_TILES)],
                            axis=1)
        acc = acc + gates[:, k:k + 1] * z
    o_ref[...] = _ln(acc, g2_ref[...], b2_ref[...])


def _combine(dest_flat, y, base, gates_t, ln2_g, ln2_b):
    n = base.shape[0]
    const = lambda shape: pl.BlockSpec(shape, lambda i: (0, 0))
    return pl.pallas_call(
        _combine_body,
        grid=(n // TT,),
        in_specs=[pl.BlockSpec((TT * SUBLANES,), lambda i: (i,), memory_space=pltpu.SMEM),
                  pl.BlockSpec(memory_space=pl.ANY),
                  pl.BlockSpec((TT, D_MODEL), lambda i: (i, 0)),
                  pl.BlockSpec((TT, SUBLANES), lambda i: (i, 0)),
                  const((1, D_MODEL)), const((1, D_MODEL))],
        out_specs=pl.BlockSpec((TT, D_MODEL), lambda i: (i, 0)),
        out_shape=jax.ShapeDtypeStruct((n, D_MODEL), _F32),
        scratch_shapes=[pltpu.VMEM((TT * TOP_K * ROW_TILES, LANES), _F32), pltpu.SemaphoreType.DMA(())],
        compiler_params=_params(1),
        name="combine",
    )(dest_flat, y, base, gates_t, ln2_g, ln2_b)


def _rope_tables(pos):
    half = RET_DK // 2
    inv = ROPE_BASE ** (-jnp.arange(half, dtype=_F32) / half)
    ang = pos[:, None] * inv[None, :]
    cos = jnp.cos(ang)
    sin = jnp.sin(ang)
    return jnp.concatenate([cos, cos], -1), jnp.concatenate([-sin, sin], -1)


def _decay_tables():
    lg = jnp.log1p(-jnp.exp2(-5.0 - jnp.arange(RET_HEADS, dtype=_F32)))
    idx = jnp.arange(BLOCK, dtype=_F32)
    rel = idx[:, None] - idx[None, :]
    causal = rel >= 0
    dmask = jnp.where(causal[None], jnp.exp(jnp.where(causal, rel, 0.0)[None] * lg[:, None, None]), 0.0)
    zeta = jnp.exp((BLOCK - 1.0 - idx)[None, :] * lg[:, None])
    xi = jnp.exp((idx + 1.0)[None, :] * lg[:, None])
    return dmask, xi[:, :, None], zeta[:, :, None], jnp.exp(BLOCK * lg)


def kernel(x, meta, ln0_g, ln0_b, w_in, b_forget, w_out, ln1_g, ln1_b, w_router, router_bias, we_gate, we_up,
           we_down, ws_gate, ws_up, ws_down, ln2_g, ln2_b):
    nb, s, d = x.shape
    assert d == D_MODEL and meta.shape == (N_META, D_MODEL) and w_in.shape[0] == 1
    assert s % TM_PROJ == 0 and s % T_FOX == 0 and (nb * s) % TT == 0
    n = nb * s
    x2d = x.reshape(n, d)
    row2 = lambda v: v.reshape(1, -1).astype(_F32)
    main_cols = 7 * HEAD_W
    w_all = jnp.concatenate(
        [w_in[0, :, :main_cols], w_in[0, :, main_cols:], jnp.zeros((d, LANES - FOX_HEADS), w_in.dtype)],
        axis=1).astype(_BF16)
    bf_pad = jnp.concatenate([b_forget[0].astype(_F32), jnp.zeros((LANES - FOX_HEADS,), _F32)]).reshape(1, LANES)
    g0, b0 = row2(ln0_g), row2(ln0_b)

    cos_x, sin_x = _rope_tables(jnp.arange(s, dtype=_F32) + float(N_META))
    cos_m, sin_m = _rope_tables(jnp.arange(BLOCK, dtype=_F32) - float(PAD))
    meta_blk = jnp.concatenate([jnp.zeros((PAD, d), _F32), meta.astype(_F32)], axis=0)

    rq, rk, rv, rg, fq, fk, fv, cx = _inproj(x2d, g0, b0, w_all, bf_pad, cos_x, sin_x, nb=nb, meta=False)
    _, rk_m, rv_m, _, _, fk_m, fv_m, cm = _inproj(meta_blk, g0, b0, w_all, bf_pad, cos_m, sin_m, nb=1, meta=True)

    dmask, xi, zeta, gch = _decay_tables()
    ret = _retention(rq, rk, rv, rg, rk_m, rv_m, dmask, xi, zeta, gch, nb=nb)

    pairs = FOX_HEADS // 2
    cx4 = cx.reshape(nb, s, pairs, 2)
    cq = jnp.transpose(cx4, (0, 2, 1, 3))
    ck = jnp.transpose(cx4.reshape(nb, s // T_FOX, T_FOX, pairs, 2), (0, 3, 1, 4, 2))
    cmt = jnp.transpose(cm.reshape(BLOCK, pairs, 2), (1, 2, 0))
    hw = lambda a: a.reshape(nb, s, HEAD_W)
    fox = _fox(hw(fq), hw(fk), hw(fv), fk_m, fv_m, cq, ck, cmt, nb=nb).reshape(n, HEAD_W)

    w_gu = jnp.concatenate([ws_gate[0], ws_up[0]], axis=1).astype(_BF16)
    base, trow, sel, gates, rank, cnt = _post(
        x2d, ret, fox, g0, b0, w_out[0].astype(_BF16), row2(ln1_g[0]), row2(ln1_b[0]),
        jnp.transpose(w_router[0]).astype(_BF16), router_bias[0].astype(_F32).reshape(N_EXPERTS, 1), w_gu,
        ws_down[0].astype(_BF16))

    nblk = n * TOP_K // EBLK + N_EXPERTS
    nblk_pad = -(-nblk // LANES) * LANES
    dest, blk_e, fill, used, last_blk = _plan(sel, rank, cnt, nblk_pad)
    dest_flat = jnp.transpose(dest).reshape(-1)
    used = used.reshape(-1)

    xs = _dispatch(dest_flat, last_blk.reshape(-1), used, trow, nblk * EBLK)
    y = _experts(blk_e.reshape(-1), used, fill.reshape(-1), xs, we_gate[0], we_up[0], we_down[0])
    out = _combine(dest_flat, y, base, jnp.transpose(gates), row2(ln2_g[0]), row2(ln2_b[0]))
    return out.reshape(nb, s, d)
```

```python
import functools

import jax
import jax.numpy as jnp
from jax import lax
from jax.experimental import pallas as pl
from jax.experimental.pallas import tpu as pltpu

D_MODEL = 1024
N_META = 16
BLOCK = 128
PAD = BLOCK - N_META
RET_HEADS = 4
RET_DK = 128
FOX_HEADS = 8
FOX_HD = 64
N_EXPERTS = 64
TOP_K = 6
N_GROUPS = 8
GROUP_SIZE = N_EXPERTS // N_GROUPS
TOPK_GROUPS = 4
EXPERT_FF = 256
SHARED_FF = 256
ROUTED_SCALE = 2.5
ROPE_BASE = 10000.0
LN_EPS = 1e-5
NEG_INF = -1e30
ALPHA = 2.0 ** 0.25
HEAD_W = 512

LANES = 128
SUBLANES = 8
ROW_TILES = D_MODEL // LANES

TM_PROJ = 512
T_FOX = 512
TT = 256
EBLK = 256
VMEM_LIMIT = 48 * 1024 * 1024

_F32 = jnp.float32
_BF16 = jnp.bfloat16


def _ln(x, g, b):
    xc = x - jnp.mean(x, -1, keepdims=True)
    var = jnp.mean(xc * xc, -1, keepdims=True)
    return xc * lax.rsqrt(var + LN_EPS) * g + b


def _dot(a, b):
    return jnp.dot(a, b, preferred_element_type=_F32)


def _dot_nt(a, b):
    return lax.dot_general(a, b, (((1,), (1,)), ((), ())), preferred_element_type=_F32)


def _dot_tn(a, b):
    return lax.dot_general(a, b, (((0,), (0,)), ((), ())), preferred_element_type=_F32)


def _silu(x):
    return x * jax.nn.sigmoid(x)


def _params(n_axes):
    return pltpu.CompilerParams(dimension_semantics=("arbitrary",) * n_axes, vmem_limit_bytes=VMEM_LIMIT)


def _inproj_body(x_ref, g_ref, b_ref, w_ref, bf_ref, cos_ref, sin_ref,
                 rq_ref, rk_ref, rv_ref, rg_ref, fq_ref, fk_ref, fv_ref, c_ref, carry_ref, *, meta):
    tm = x_ref.shape[0]
    h = _ln(x_ref[...], g_ref[...], b_ref[...])
    if meta:
        valid = lax.broadcasted_iota(jnp.int32, (tm, 1), 0) >= PAD
        h = jnp.where(valid, h, 0.0)
    hb = h.astype(_BF16)
    cos = cos_ref[...]
    sin = sin_ref[...]

    def proj(g):
        return _dot(hb, w_ref[:, g * HEAD_W:(g + 1) * HEAD_W])

    def rope_store(p, out_ref, scale):
        for hd in range(RET_HEADS):
            t = p[:, hd * LANES:(hd + 1) * LANES]
            r = t * cos + pltpu.roll(t, LANES // 2, axis=1) * sin
            out_ref[:, hd * LANES:(hd + 1) * LANES] = (r * scale).astype(_BF16)

    rope_store(proj(0), rq_ref, 1.0)
    rope_store(proj(1), rk_ref, RET_DK ** -0.5)
    rv_ref[...] = proj(2).astype(_BF16)
    rg_ref[...] = _silu(proj(3)).astype(_BF16)
    fq_ref[...] = (proj(4) * (FOX_HD ** -0.5)).astype(_BF16)
    fk_ref[...] = proj(5).astype(_BF16)
    fv_ref[...] = proj(6).astype(_BF16)

    z = _dot(hb, w_ref[:, 7 * HEAD_W:7 * HEAD_W + LANES]) + bf_ref[...]
    logf = jnp.minimum(z, 0.0) - jnp.log1p(jnp.exp(-jnp.abs(z)))
    if meta:
        logf = jnp.where(valid, logf, 0.0)
    l1 = logf.astype(_BF16)
    r1 = logf - l1.astype(_F32)
    l2 = r1.astype(_BF16)
    l3 = (r1 - l2.astype(_F32)).astype(_BF16)
    row = lax.broadcasted_iota(jnp.int32, (tm, tm), 0)
    col = lax.broadcasted_iota(jnp.int32, (tm, tm), 1)
    tri = (col <= row).astype(_BF16)
    c = _dot(tri, l1) + _dot(tri, l2) + _dot(tri, l3)
    if meta:
        c = c - c[tm - 1:tm, :]
    else:
        @pl.when(pl.program_id(1) == 0)
        def _():
            carry_ref[...] = jnp.zeros_like(carry_ref)

        c = c + carry_ref[...]
        carry_ref[...] = c[tm - 1:tm, :]
    c_ref[...] = c[:, :FOX_HEADS]


def _inproj(x2d, ln_g, ln_b, w_all, bf_pad, cos_t, sin_t, *, nb, meta):
    n = x2d.shape[0]
    s = n // nb
    tm = min(TM_PROJ, s)
    nj = s // tm
    row_spec = lambda w: pl.BlockSpec((tm, w), lambda b, j: (b * nj + j, 0))
    const = lambda shape: pl.BlockSpec(shape, lambda b, j: (0, 0))
    pos_spec = pl.BlockSpec((tm, LANES), lambda b, j: (j, 0))
    outs = [jax.ShapeDtypeStruct((n, HEAD_W), _BF16)] * 7 + [jax.ShapeDtypeStruct((n, FOX_HEADS), _F32)]
    return pl.pallas_call(
        functools.partial(_inproj_body, meta=meta),
        grid=(nb, nj),
        in_specs=[row_spec(D_MODEL), const((1, D_MODEL)), const((1, D_MODEL)), const(w_all.shape),
                  const((1, LANES)), pos_spec, pos_spec],
        out_specs=[row_spec(HEAD_W)] * 7 + [row_spec(FOX_HEADS)],
        out_shape=outs,
        scratch_shapes=[pltpu.VMEM((1, LANES), _F32)],
        compiler_params=_params(2),
        name="inproj_meta" if meta else "inproj",
    )(x2d, ln_g, ln_b, w_all, bf_pad, cos_t, sin_t)


def _ret_body(q_ref, k_ref, v_ref, g_ref, km_ref, vm_ref, dm_ref, xi_ref, zeta_ref, gch_ref, o_ref, st_ref):
    def kv_update(k, v, hd):
        vz = (v.astype(_F32) * zeta_ref[hd]).astype(_BF16)
        return _dot_tn(k, vz)

    @pl.when(pl.program_id(1) == 0)
    def _():
        for hd in range(RET_HEADS):
            sl = slice(hd * LANES, (hd + 1) * LANES)
            st_ref[hd] = kv_update(km_ref[:, sl], vm_ref[:, sl], hd)

    for hd in range(RET_HEADS):
        sl = slice(hd * LANES, (hd + 1) * LANES)
        q = q_ref[:, sl]
        k = k_ref[:, sl]
        v = v_ref[:, sl]
        st = st_ref[hd]
        scores = _dot_nt(q, k) * dm_ref[hd]
        o = _dot(scores.astype(_BF16), v) + _dot(q, st.astype(_BF16)) * xi_ref[hd]
        oc = o - jnp.mean(o, -1, keepdims=True)
        y = oc * lax.rsqrt(jnp.mean(oc * oc, -1, keepdims=True) + LN_EPS)
        o_ref[:, sl] = (y * g_ref[:, sl].astype(_F32)).astype(_BF16)
        st_ref[hd] = gch_ref[hd] * st + kv_update(k, v, hd)


def _retention(rq, rk, rv, rg, rk_m, rv_m, dmask, xi, zeta, gch, *, nb):
    n = rq.shape[0]
    nc = n // nb // BLOCK
    row_spec = pl.BlockSpec((BLOCK, HEAD_W), lambda b, j: (b * nc + j, 0))
    meta_spec = pl.BlockSpec((BLOCK, HEAD_W), lambda b, j: (0, 0))
    tab = lambda shape: pl.BlockSpec(shape, lambda b, j: (0,) * len(shape))
    return pl.pallas_call(
        _ret_body,
        grid=(nb, nc),
        in_specs=[row_spec] * 4 + [meta_spec] * 2
        + [tab((RET_HEADS, BLOCK, BLOCK)), tab((RET_HEADS, BLOCK, 1)), tab((RET_HEADS, BLOCK, 1)),
           pl.BlockSpec(memory_space=pltpu.SMEM)],
        out_specs=row_spec,
        out_shape=jax.ShapeDtypeStruct((n, HEAD_W), _BF16),
        scratch_shapes=[pltpu.VMEM((RET_HEADS, RET_DK, LANES), _F32)],
        compiler_params=_params(2),
        name="retention",
    )(rq, rk, rv, rg, rk_m, rv_m, dmask, xi, zeta, gch)


def _fox_body(q_ref, k_ref, v_ref, km_ref, vm_ref, cq_ref, ck_ref, cm_ref, o_ref, m_sc, l_sc, acc_sc):
    t = q_ref.shape[0]
    qi = pl.program_id(2)
    q = q_ref[...]
    lane = lax.broadcasted_iota(jnp.int32, (t, LANES), 1)
    qh = [jnp.where(lane < FOX_HD, q, jnp.zeros_like(q)), jnp.where(lane >= FOX_HD, q, jnp.zeros_like(q))]
    cq = cq_ref[...]

    km = km_ref[...]
    vm = vm_ref[...]
    for hh in range(2):
        s = _dot_nt(qh[hh], km) + (cq[:, hh:hh + 1] - cm_ref[hh:hh + 1, :])
        s = jnp.where(lane >= PAD, s, NEG_INF)
        m = jnp.max(s, axis=1, keepdims=True)
        p = jnp.exp(s - m)
        m_sc[hh] = m
        l_sc[hh] = jnp.sum(p, axis=1, keepdims=True)
        acc_sc[hh] = _dot(p.astype(_BF16), vm)

    def step(ki, masked):
        off = pl.multiple_of(ki * t, t)
        k = k_ref[pl.ds(off, t), :]
        v = v_ref[pl.ds(off, t), :]
        ck = ck_ref[ki]
        for hh in range(2):
            s = _dot_nt(qh[hh], k) + (cq[:, hh:hh + 1] - ck[hh:hh + 1, :])
            if masked:
                row = lax.broadcasted_iota(jnp.int32, (t, t), 0)
                col = lax.broadcasted_iota(jnp.int32, (t, t), 1)
                s = jnp.where(col <= row, s, NEG_INF)
            m_prev = m_sc[hh]
            m_new = jnp.maximum(m_prev, jnp.max(s, axis=1, keepdims=True))
            alpha = jnp.exp(m_prev - m_new)
            p = jnp.exp(s - m_new)
            l_sc[hh] = alpha * l_sc[hh] + jnp.sum(p, axis=1, keepdims=True)
            acc_sc[hh] = alpha * acc_sc[hh] + _dot(p.astype(_BF16), v)
            m_sc[hh] = m_new

    def loop_body(ki, carry):
        step(ki, False)
        return carry

    lax.fori_loop(0, qi, loop_body, 0)
    step(qi, True)
    out = jnp.where(lane < FOX_HD, acc_sc[0] / l_sc[0], acc_sc[1] / l_sc[1])
    o_ref[...] = out.astype(_BF16)


def _fox(fq, fk, fv, fk_m, fv_m, cq, ck, cm, *, nb):
    s = fq.shape[1]
    t = T_FOX
    nq = s // t
    pairs = FOX_HEADS // 2
    return pl.pallas_call(
        _fox_body,
        grid=(nb, pairs, nq),
        in_specs=[
            pl.BlockSpec((None, t, LANES), lambda b, p, i: (b, i, p)),
            pl.BlockSpec((None, s, LANES), lambda b, p, i: (b, 0, p)),
            pl.BlockSpec((None, s, LANES), lambda b, p, i: (b, 0, p)),
            pl.BlockSpec((BLOCK, LANES), lambda b, p, i: (0, p)),
            pl.BlockSpec((BLOCK, LANES), lambda b, p, i: (0, p)),
            pl.BlockSpec((None, None, t, 2), lambda b, p, i: (b, p, i, 0)),
            pl.BlockSpec((None, None, nq, 2, t), lambda b, p, i: (b, p, 0, 0, 0)),
            pl.BlockSpec((None, 2, BLOCK), lambda b, p, i: (p, 0, 0)),
        ],
        out_specs=pl.BlockSpec((None, t, LANES), lambda b, p, i: (b, i, p)),
        out_shape=jax.ShapeDtypeStruct((nb, s, HEAD_W), _BF16),
        scratch_shapes=[pltpu.VMEM((2, t, 1), _F32), pltpu.VMEM((2, t, 1), _F32), pltpu.VMEM((2, t, LANES), _F32)],
        compiler_params=_params(3),
        name="fox",
    )(fq, fk, fv, fk_m, fv_m, cq, ck, cm)


def _post_body(x_ref, ret_ref, fox_ref, g0_ref, b0_ref, wo_ref, g1_ref, b1_ref, wr_ref, rb_ref, wgu_ref, wd_ref,
               base_ref, trow_ref, sel_ref, gate_ref, rank_ref, cnt_ref, run_ref):
    tm = x_ref.shape[0]
    i = pl.program_id(0)

    @pl.when(i == 0)
    def _():
        run_ref[...] = jnp.zeros_like(run_ref)

    h0 = _ln(x_ref[...], g0_ref[...], b0_ref[...])
    y = _dot(ret_ref[...], wo_ref[:HEAD_W, :]) + _dot(fox_ref[...], wo_ref[HEAD_W:, :])
    h1 = _ln(ALPHA * h0 + y, g1_ref[...], b1_ref[...])
    tb = h1.astype(_BF16)
    for s in range(ROW_TILES):
        trow_ref[pl.ds(s, tm, stride=ROW_TILES), :] = h1[:, s * LANES:(s + 1) * LANES]

    gu = _dot(tb, wgu_ref[...])
    mid = (_silu(gu[:, :SHARED_FF]) * gu[:, SHARED_FF:]).astype(_BF16)
    base_ref[...] = ALPHA * h1 + _dot(mid, wd_ref[...])

    scores = jax.nn.sigmoid(_dot_nt(wr_ref[...], tb))
    biased = scores + rb_ref[...]
    sub = lax.broadcasted_iota(jnp.int32, (GROUP_SIZE, tm), 0).astype(_F32)
    gscore = []
    for g in range(N_GROUPS):
        v = biased[g * GROUP_SIZE:(g + 1) * GROUP_SIZE, :]
        m1 = jnp.max(v, axis=0, keepdims=True)
        i1 = jnp.min(jnp.where(v == m1, sub, float(GROUP_SIZE)), axis=0, keepdims=True)
        m2 = jnp.max(jnp.where(sub == i1, -jnp.inf, v), axis=0, keepdims=True)
        gscore.append(m1 + m2)
    masked = []
    for g in range(N_GROUPS):
        beaten = jnp.zeros((1, tm), _F32)
        for o in range(N_GROUPS):
            if o == g:
                continue
            wins = (gscore[o] >= gscore[g]) if o < g else (gscore[o] > gscore[g])
            beaten = beaten + wins.astype(_F32)
        keep = beaten < float(TOPK_GROUPS)
        masked.append(jnp.where(keep, biased[g * GROUP_SIZE:(g + 1) * GROUP_SIZE, :], NEG_INF))
    work = jnp.concatenate(masked, axis=0)
    eid = lax.broadcasted_iota(jnp.int32, (N_EXPERTS, tm), 0).astype(_F32)
    hots, sels, raws = [], [], []
    for _ in range(TOP_K):
        m = jnp.max(work, axis=0, keepdims=True)
        idx = jnp.min(jnp.where(work == m, eid, float(N_EXPERTS)), axis=0, keepdims=True)
        hot = eid == idx
        hots.append(hot)
        sels.append(idx.astype(jnp.int32))
        raws.append(jnp.sum(jnp.where(hot, scores, 0.0), axis=0, keepdims=True))
        work = jnp.where(hot, -jnp.inf, work)
    total = raws[0]
    for r in raws[1:]:
        total = total + r
    inv = ROUTED_SCALE / total
    member = hots[0]
    for hot in hots[1:]:
        member = member | hot
    member_f = member.astype(_F32)
    row = lax.broadcasted_iota(jnp.int32, (tm, tm), 0)
    col = lax.broadcasted_iota(jnp.int32, (tm, tm), 1)
    before = (row < col).astype(_BF16)
    rank_e = _dot(member_f.astype(_BF16), before) + run_ref[...]
    pad_rows = sel_ref.shape[0] - TOP_K
    sel_ref[...] = jnp.concatenate(sels + [jnp.zeros((pad_rows, tm), jnp.int32)], axis=0)
    gate_ref[...] = jnp.concatenate([r * inv for r in raws] + [jnp.zeros((pad_rows, tm), _F32)], axis=0)
    ranks = [jnp.sum(jnp.where(hot, rank_e, 0.0), axis=0, keepdims=True).astype(jnp.int32) for hot in hots]
    rank_ref[...] = jnp.concatenate(ranks + [jnp.zeros((pad_rows, tm), jnp.int32)], axis=0)
    run_ref[...] = run_ref[...] + jnp.sum(member_f, axis=1, keepdims=True)
    cnt_ref[...] = run_ref[...]


def _post(x2d, ret, fox, ln0_g, ln0_b, w_out, ln1_g, ln1_b, w_rt, rbias, w_gu, w_sd):
    n = x2d.shape[0]
    tm = TM_PROJ
    row_spec = lambda w: pl.BlockSpec((tm, w), lambda i: (i, 0))
    col_spec = pl.BlockSpec((SUBLANES, tm), lambda i: (0, i))
    const = lambda shape: pl.BlockSpec(shape, lambda i: (0, 0))
    return pl.pallas_call(
        _post_body,
        grid=(n // tm,),
        in_specs=[row_spec(D_MODEL), row_spec(HEAD_W), row_spec(HEAD_W), const((1, D_MODEL)), const((1, D_MODEL)),
                  const(w_out.shape), const((1, D_MODEL)), const((1, D_MODEL)), const(w_rt.shape),
                  const((N_EXPERTS, 1)), const(w_gu.shape), const(w_sd.shape)],
        out_specs=[row_spec(D_MODEL), pl.BlockSpec((tm * ROW_TILES, LANES), lambda i: (i, 0)),
                   col_spec, col_spec, col_spec, const((N_EXPERTS, 1))],
        out_shape=[jax.ShapeDtypeStruct((n, D_MODEL), _F32), jax.ShapeDtypeStruct((n * ROW_TILES, LANES), _F32),
                   jax.ShapeDtypeStruct((SUBLANES, n), jnp.int32), jax.ShapeDtypeStruct((SUBLANES, n), _F32),
                   jax.ShapeDtypeStruct((SUBLANES, n), jnp.int32), jax.ShapeDtypeStruct((N_EXPERTS, 1), _F32)],
        scratch_shapes=[pltpu.VMEM((N_EXPERTS, 1), _F32)],
        compiler_params=_params(1),
        name="post_mixer",
    )(x2d, ret, fox, ln0_g, ln0_b, w_out, ln1_g, ln1_b, w_rt, rbias, w_gu, w_sd)


def _plan_body(sel_ref, rank_ref, cnt_ref, dest_ref, blk_ref, fill_ref, used_ref, last_ref):
    cnt = cnt_ref[...]
    padded = jnp.ceil(cnt * (1.0 / EBLK)) * EBLK
    er = lax.broadcasted_iota(jnp.int32, (N_EXPERTS, N_EXPERTS), 0)
    ec = lax.broadcasted_iota(jnp.int32, (N_EXPERTS, N_EXPERTS), 1)
    padded_row = jnp.sum(jnp.where(er == ec, padded, 0.0), axis=0, keepdims=True)
    pstart = jnp.sum(jnp.where(ec < er, padded_row, 0.0), axis=1, keepdims=True)
    pend = pstart + padded
    sel = sel_ref[...]
    dest = rank_ref[...]
    for e in range(N_EXPERTS):
        dest = dest + jnp.where(sel == e, pstart[e:e + 1, :].astype(jnp.int32), 0)
    dest_ref[...] = dest
    nblk = blk_ref.shape[1]
    first_row = (lax.broadcasted_iota(jnp.int32, (N_EXPERTS, nblk), 1) * EBLK).astype(_F32)
    owner = jnp.minimum(jnp.sum((pend <= first_row).astype(_F32), axis=0, keepdims=True), N_EXPERTS - 1.0)
    blk_ref[...] = owner.astype(jnp.int32)
    mine = lax.broadcasted_iota(jnp.int32, (N_EXPERTS, nblk), 0).astype(_F32) == owner
    live_end = jnp.sum(jnp.where(mine, pstart + cnt, 0.0), axis=0, keepdims=True)
    fill_ref[...] = jnp.clip(live_end - first_row[:1, :], 0.0, float(EBLK)).astype(jnp.int32)
    used_ref[...] = (pend[N_EXPERTS - 1:, :] * (1.0 / EBLK)).astype(jnp.int32)
    last = jnp.where(padded > 0.0, pend * (1.0 / EBLK) - 1.0, -1.0)
    last_ref[...] = jnp.sum(jnp.where(er == ec, last, 0.0), axis=0, keepdims=True).astype(jnp.int32)


def _plan(sel, rank, cnt, nblk_pad):
    n = sel.shape[1]
    full = lambda shape: pl.BlockSpec(shape, lambda i: (0, 0))
    return pl.pallas_call(
        _plan_body,
        grid=(1,),
        in_specs=[full(sel.shape), full(rank.shape), full(cnt.shape)],
        out_specs=[full(sel.shape), full((1, nblk_pad)), full((1, nblk_pad)), full((1, 1)), full((1, N_EXPERTS))],
        out_shape=[jax.ShapeDtypeStruct((SUBLANES, n), jnp.int32), jax.ShapeDtypeStruct((1, nblk_pad), jnp.int32),
                   jax.ShapeDtypeStruct((1, nblk_pad), jnp.int32), jax.ShapeDtypeStruct((1, 1), jnp.int32),
                   jax.ShapeDtypeStruct((1, N_EXPERTS), jnp.int32)],
        compiler_params=_params(1),
        name="plan",
    )(sel, rank, cnt)


def _row_copy(src, src_row, dst, dst_row, sem):
    return pltpu.make_async_copy(src.at[pl.ds(pl.multiple_of(src_row * ROW_TILES, ROW_TILES), ROW_TILES), :],
                                 dst.at[pl.ds(pl.multiple_of(dst_row * ROW_TILES, ROW_TILES), ROW_TILES), :], sem)


def _dispatch_body(dest_ref, last_ref, used_ref, t_ref, xs_ref, zero_sc, sem, zsem):
    blk_rows = EBLK * ROW_TILES
    nblk = xs_ref.shape[0] // blk_rows

    @pl.when(pl.program_id(0) == 0)
    def _():
        zero_sc[...] = jnp.zeros_like(zero_sc)

        def blk_copy(b):
            return pltpu.make_async_copy(zero_sc, xs_ref.at[pl.ds(pl.multiple_of(b * blk_rows, blk_rows), blk_rows), :],
                                         zsem)

        def for_padded_blocks(act):
            def per_expert(e, carry):
                @pl.when(last_ref[e] >= 0)
                def _():
                    act(blk_copy(last_ref[e]))

                return carry

            def per_tail(b, carry):
                act(blk_copy(b))
                return carry

            lax.fori_loop(0, N_EXPERTS, per_expert, 0)
            lax.fori_loop(used_ref[0], nblk, per_tail, 0)

        for_padded_blocks(lambda cp: cp.start())
        for_padded_blocks(lambda cp: cp.wait())

    def issue(i, carry):
        for k in range(TOP_K):
            _row_copy(t_ref, i, xs_ref, dest_ref[i * SUBLANES + k], sem).start()
        return carry

    lax.fori_loop(0, TT, issue, 0)
    for _ in range(TOP_K):
        pltpu.make_async_copy(t_ref, xs_ref.at[pl.ds(0, TT * ROW_TILES), :], sem).wait()


def _dispatch(dest_flat, last_blk, used, trow, total_rows):
    n = trow.shape[0] // ROW_TILES
    smem = pl.BlockSpec(memory_space=pltpu.SMEM)
    return pl.pallas_call(
        _dispatch_body,
        grid=(n // TT,),
        in_specs=[pl.BlockSpec((TT * SUBLANES,), lambda i: (i,), memory_space=pltpu.SMEM), smem, smem,
                  pl.BlockSpec((TT * ROW_TILES, LANES), lambda i: (i, 0))],
        out_specs=pl.BlockSpec(memory_space=pl.ANY),
        out_shape=jax.ShapeDtypeStruct((total_rows * ROW_TILES, LANES), _F32),
        scratch_shapes=[pltpu.VMEM((EBLK * ROW_TILES, LANES), _F32), pltpu.SemaphoreType.DMA(()),
                        pltpu.SemaphoreType.DMA(())],
        compiler_params=_params(1),
        name="dispatch",
    )(dest_flat, last_blk, used, trow)


def _expert_body(blk_ref, used_ref, fill_ref, xs_ref, wg_ref, wu_ref, wd_ref, y_ref, wgu_sc, wd_sc):
    i = pl.program_id(0)
    prev = blk_ref[jnp.maximum(i - 1, 0)]
    fresh = (i == 0) | (blk_ref[i] != prev)

    @pl.when(fresh)
    def _():
        wgu_sc[:, :EXPERT_FF] = wg_ref[...].astype(_BF16)
        wgu_sc[:, EXPERT_FF:] = wu_ref[...].astype(_BF16)
        wd_sc[...] = wd_ref[...].astype(_BF16)

    @pl.when(i < used_ref[0])
    def _():
        x = jnp.concatenate([xs_ref[pl.ds(s, EBLK, stride=ROW_TILES), :] for s in range(ROW_TILES)], axis=1)
        live = lax.broadcasted_iota(jnp.int32, (EBLK, 1), 0) < fill_ref[i]
        x = jnp.where(live, x, 0.0).astype(_BF16)
        gu = _dot(x, wgu_sc[...])
        mid = (_silu(gu[:, :EXPERT_FF]) * gu[:, EXPERT_FF:]).astype(_BF16)
        y = _dot(mid, wd_sc[...])
        for s in range(ROW_TILES):
            y_ref[pl.ds(s, EBLK, stride=ROW_TILES), :] = y[:, s * LANES:(s + 1) * LANES]

    @pl.when(i >= used_ref[0])
    def _():
        y_ref[...] = jnp.zeros_like(y_ref)


def _experts(blk_e, used, fill, xs, we_gate, we_up, we_down):
    nblk = xs.shape[0] // (EBLK * ROW_TILES)
    last = lambda i, used: jnp.minimum(i, jnp.maximum(used[0] - 1, 0))
    w_spec = lambda shape: pl.BlockSpec((None,) + shape, lambda i, blk, used, fill: (blk[i], 0, 0))
    return pl.pallas_call(
        _expert_body,
        grid_spec=pltpu.PrefetchScalarGridSpec(
            num_scalar_prefetch=3,
            grid=(nblk,),
            in_specs=[pl.BlockSpec((EBLK * ROW_TILES, LANES), lambda i, blk, used, fill: (last(i, used), 0)),
                      w_spec((D_MODEL, EXPERT_FF)), w_spec((D_MODEL, EXPERT_FF)), w_spec((EXPERT_FF, D_MODEL))],
            out_specs=pl.BlockSpec((EBLK * ROW_TILES, LANES), lambda i, blk, used, fill: (i, 0)),
            scratch_shapes=[pltpu.VMEM((D_MODEL, 2 * EXPERT_FF), _BF16), pltpu.VMEM((EXPERT_FF, D_MODEL), _BF16)],
        ),
        out_shape=jax.ShapeDtypeStruct(xs.shape, _F32),
        compiler_params=_params(1),
        name="experts",
    )(blk_e, used, fill, xs, we_gate, we_up, we_down)


def _combine_body(dest_ref, y_ref, base_ref, gate_ref, g2_ref, b2_ref, o_ref, z_sc, sem):
    def issue(i, carry):
        for k in range(TOP_K):
            _row_copy(y_ref, dest_ref[i * SUBLANES + k], z_sc, k * TT + i, sem).start()
        return carry

    lax.fori_loop(0, TT, issue, 0)
    rows = TT * TOP_K * ROW_TILES
    pltpu.make_async_copy(y_ref.at[pl.ds(0, rows), :], z_sc, sem).wait()
    acc = base_ref[...]
    gates = gate_ref[...]
    for k in range(TOP_K):
        z = jnp.concatenate([z_sc[pl.ds(k * TT * ROW_TILES + s, TT, stride=ROW_TILES), :] for s in range(ROW_TILES)],
                            axis=1)
        acc = acc + gates[:, k:k + 1] * z
    o_ref[...] = _ln(acc, g2_ref[...], b2_ref[...])


def _combine(dest_flat, y, base, gates_t, ln2_g, ln2_b):
    n = base.shape[0]
    const = lambda shape: pl.BlockSpec(shape, lambda i: (0, 0))
    return pl.pallas_call(
        _combine_body,
        grid=(n // TT,),
        in_specs=[pl.BlockSpec((TT * SUBLANES,), lambda i: (i,), memory_space=pltpu.SMEM),
                  pl.BlockSpec(memory_space=pl.ANY),
                  pl.BlockSpec((TT, D_MODEL), lambda i: (i, 0)),
                  pl.BlockSpec((TT, SUBLANES), lambda i: (i, 0)),
                  const((1, D_MODEL)), const((1, D_MODEL))],
        out_specs=pl.BlockSpec((TT, D_MODEL), lambda i: (i, 0)),
        out_shape=jax.ShapeDtypeStruct((n, D_MODEL), _F32),
        scratch_shapes=[pltpu.VMEM((TT * TOP_K * ROW_TILES, LANES), _F32), pltpu.SemaphoreType.DMA(())],
        compiler_params=_params(1),
        name="combine",
    )(dest_flat, y, base, gates_t, ln2_g, ln2_b)


def _rope_tables(pos):
    half = RET_DK // 2
    inv = ROPE_BASE ** (-jnp.arange(half, dtype=_F32) / half)
    ang = pos[:, None] * inv[None, :]
    cos = jnp.cos(ang)
    sin = jnp.sin(ang)
    return jnp.concatenate([cos, cos], -1), jnp.concatenate([-sin, sin], -1)


def _decay_tables():
    lg = jnp.log1p(-jnp.exp2(-5.0 - jnp.arange(RET_HEADS, dtype=_F32)))
    idx = jnp.arange(BLOCK, dtype=_F32)
    rel = idx[:, None] - idx[None, :]
    causal = rel >= 0
    dmask = jnp.where(causal[None], jnp.exp(jnp.where(causal, rel, 0.0)[None] * lg[:, None, None]), 0.0)
    zeta = jnp.exp((BLOCK - 1.0 - idx)[None, :] * lg[:, None])
    xi = jnp.exp((idx + 1.0)[None, :] * lg[:, None])
    return dmask, xi[:, :, None], zeta[:, :, None], jnp.exp(BLOCK * lg)


def kernel(x, meta, ln0_g, ln0_b, w_in, b_forget, w_out, ln1_g, ln1_b, w_router, router_bias, we_gate, we_up,
           we_down, ws_gate, ws_up, ws_down, ln2_g, ln2_b):
    nb, s, d = x.shape
    assert d == D_MODEL and meta.shape == (N_META, D_MODEL) and w_in.shape[0] == 1
    assert s % TM_PROJ == 0 and s % T_FOX == 0 and (nb * s) % TT == 0
    n = nb * s
    x2d = x.reshape(n, d)
    row2 = lambda v: v.reshape(1, -1).astype(_F32)
    main_cols = 7 * HEAD_W
    w_all = jnp.concatenate(
        [w_in[0, :, :main_cols], w_in[0, :, main_cols:], jnp.zeros((d, LANES - FOX_HEADS), w_in.dtype)],
        axis=1).astype(_BF16)
    bf_pad = jnp.concatenate([b_forget[0].astype(_F32), jnp.zeros((LANES - FOX_HEADS,), _F32)]).reshape(1, LANES)
    g0, b0 = row2(ln0_g), row2(ln0_b)

    cos_x, sin_x = _rope_tables(jnp.arange(s, dtype=_F32) + float(N_META))
    cos_m, sin_m = _rope_tables(jnp.arange(BLOCK, dtype=_F32) - float(PAD))
    meta_blk = jnp.concatenate([jnp.zeros((PAD, d), _F32), meta.astype(_F32)], axis=0)

    rq, rk, rv, rg, fq, fk, fv, cx = _inproj(x2d, g0, b0, w_all, bf_pad, cos_x, sin_x, nb=nb, meta=False)
    _, rk_m, rv_m, _, _, fk_m, fv_m, cm = _inproj(meta_blk, g0, b0, w_all, bf_pad, cos_m, sin_m, nb=1, meta=True)

    dmask, xi, zeta, gch = _decay_tables()
    ret = _retention(rq, rk, rv, rg, rk_m, rv_m, dmask, xi, zeta, gch, nb=nb)

    pairs = FOX_HEADS // 2
    cx4 = cx.reshape(nb, s, pairs, 2)
    cq = jnp.transpose(cx4, (0, 2, 1, 3))
    ck = jnp.transpose(cx4.reshape(nb, s // T_FOX, T_FOX, pairs, 2), (0, 3, 1, 4, 2))
    cmt = jnp.transpose(cm.reshape(BLOCK, pairs, 2), (1, 2, 0))
    hw = lambda a: a.reshape(nb, s, HEAD_W)
    fox = _fox(hw(fq), hw(fk), hw(fv), fk_m, fv_m, cq, ck, cmt, nb=nb).reshape(n, HEAD_W)

    w_gu = jnp.concatenate([ws_gate[0], ws_up[0]], axis=1).astype(_BF16)
    base, trow, sel, gates, rank, cnt = _post(
        x2d, ret, fox, g0, b0, w_out[0].astype(_BF16), row2(ln1_g[0]), row2(ln1_b[0]),
        jnp.transpose(w_router[0]).astype(_BF16), router_bias[0].astype(_F32).reshape(N_EXPERTS, 1), w_gu,
        ws_down[0].astype(_BF16))

    nblk = n * TOP_K // EBLK + N_EXPERTS
    nblk_pad = -(-nblk // LANES) * LANES
    dest, blk_e, fill, used, last_blk = _plan(sel, rank, cnt, nblk_pad)
    dest_flat = jnp.transpose(dest).reshape(-1)
    used = used.reshape(-1)

    xs = _dispatch(dest_flat, last_blk.reshape(-1), used, trow, nblk * EBLK)
    y = _experts(blk_e.reshape(-1), used, fill.reshape(-1), xs, we_gate[0], we_up[0], we_down[0])
    out = _combine(dest_flat, y, base, jnp.transpose(gates), row2(ln2_g[0]), row2(ln2_b[0]))
    return out.reshape(nb, s, d)
```

```python
import functools

import jax
import jax.numpy as jnp
import numpy as np
from jax import lax
from jax.experimental import pallas as pl
from jax.experimental.pallas import tpu as pltpu

D_MODEL = 1024
N_META = 16
BLOCK = 128
PAD = BLOCK - N_META
RET_HEADS = 4
RET_DK = 128
FOX_HEADS = 8
FOX_HD = 64
N_EXPERTS = 64
TOP_K = 6
N_GROUPS = 8
GROUP_SIZE = N_EXPERTS // N_GROUPS
TOPK_GROUPS = 4
EXPERT_FF = 256
SHARED_FF = 256
ROUTED_SCALE = 2.5
ROPE_BASE = 10000.0
LN_EPS = 1e-5
NEG_INF = -1e30
ALPHA = 2.0 ** 0.25
HEAD_W = 512
LOG2E = 1.4426950408889634

LANES = 128
FOX_W = FOX_HEADS * LANES
SUBLANES = 8
ROW_TILES = D_MODEL // LANES

TM_PROJ = 512
T_FOX = 512
FOX_CHUNK = 64
TT = 256
EBLK = 256
VMEM_LIMIT = 48 * 1024 * 1024

_F32 = jnp.float32
_BF16 = jnp.bfloat16


def _ln(x, g, b):
    xc = x - jnp.mean(x, -1, keepdims=True)
    var = jnp.mean(xc * xc, -1, keepdims=True)
    return xc * lax.rsqrt(var + LN_EPS) * g + b


def _dot(a, b):
    return jnp.dot(a, b, preferred_element_type=_F32)


def _dot_nt(a, b):
    return lax.dot_general(a, b, (((1,), (1,)), ((), ())), preferred_element_type=_F32)


def _dot_tn(a, b):
    return lax.dot_general(a, b, (((0,), (0,)), ((), ())), preferred_element_type=_F32)


def _silu(x):
    return x * jax.nn.sigmoid(x)


def _params(n_axes):
    return pltpu.CompilerParams(dimension_semantics=("arbitrary",) * n_axes, vmem_limit_bytes=VMEM_LIMIT)


def _inproj_body(x_ref, g_ref, b_ref, w_ref, bf_ref, cos_ref, sin_ref, own_ref, pq_ref, pk_ref, oq_ref, ok_ref, ov_ref,
                 rq_ref, rk_ref, rv_ref, rg_ref, fq_ref, fk_ref, fv_ref, carry_ref, *, meta):
    tm = x_ref.shape[0]
    h = _ln(x_ref[...], g_ref[...], b_ref[...])
    if meta:
        valid = lax.broadcasted_iota(jnp.int32, (tm, 1), 0) >= PAD
        h = jnp.where(valid, h, 0.0)
    hb = h.astype(_BF16)
    cos = cos_ref[...]
    sin = sin_ref[...]

    def proj(g):
        return _dot(hb, w_ref[:, g * HEAD_W:(g + 1) * HEAD_W])

    def rope_store(p, out_ref, scale):
        for hd in range(RET_HEADS):
            t = p[:, hd * LANES:(hd + 1) * LANES]
            r = t * cos + pltpu.roll(t, LANES // 2, axis=1) * sin
            out_ref[:, hd * LANES:(hd + 1) * LANES] = (r * scale).astype(_BF16)

    rope_store(proj(0), rq_ref, 1.0)
    rope_store(proj(1), rk_ref, RET_DK ** -0.5)
    rv_ref[...] = proj(2).astype(_BF16)
    rg_ref[...] = _silu(proj(3)).astype(_BF16)

    z = _dot(hb, w_ref[:, 7 * HEAD_W:7 * HEAD_W + LANES]) + bf_ref[...]
    logf = jnp.minimum(z, 0.0) - jnp.log1p(jnp.exp(-jnp.abs(z)))
    if meta:
        logf = jnp.where(valid, logf, 0.0)
    l1 = logf.astype(_BF16)
    r1 = logf - l1.astype(_F32)
    l2 = r1.astype(_BF16)
    l3 = (r1 - l2.astype(_F32)).astype(_BF16)
    row = lax.broadcasted_iota(jnp.int32, (tm, tm), 0)
    col = lax.broadcasted_iota(jnp.int32, (tm, tm), 1)
    tri = (col <= row).astype(_BF16)
    c = _dot(tri, l1) + _dot(tri, l2) + _dot(tri, l3)
    if meta:
        c = c - c[tm - 1:tm, :]
    else:
        @pl.when(pl.program_id(1) == 0)
        def _():
            carry_ref[...] = jnp.zeros_like(carry_ref)

        c = c + carry_ref[...]
        carry_ref[...] = c[tm - 1:tm, :]

    head_lane = lax.broadcasted_iota(jnp.int32, (1, LANES), 1) < FOX_HEADS
    cl = jnp.where(head_lane, c, 0.0) * LOG2E
    c1 = cl.astype(_BF16).astype(_F32)
    r1 = cl - c1
    c2 = r1.astype(_BF16).astype(_F32)
    c3 = (r1 - c2).astype(_BF16).astype(_F32)
    csplit = (c1 + pltpu.roll(c2, FOX_HEADS, axis=1) + pltpu.roll(c3, 2 * FOX_HEADS, axis=1)).astype(_BF16)
    own = own_ref[...] > 0.0

    def per_head(p):
        return jnp.concatenate([p[:, (hd // 2) * LANES:(hd // 2 + 1) * LANES] for hd in range(FOX_HEADS)], axis=1)

    fq_ref[...] = jnp.where(own, per_head(proj(4) * (FOX_HD ** -0.5 * LOG2E)),
                            _dot(csplit, pq_ref[...]) + oq_ref[...]).astype(_BF16)
    fk_ref[...] = jnp.where(own, per_head(proj(5)), _dot(csplit, pk_ref[...]) + ok_ref[...]).astype(_BF16)
    fv_ref[...] = jnp.where(own, per_head(proj(6)), ov_ref[...]).astype(_BF16)


def _fox_lane_tables():
    own = np.zeros((1, FOX_W), np.float32)
    pq = np.zeros((LANES, FOX_W), np.float32)
    pk = np.zeros((LANES, FOX_W), np.float32)
    oq = np.zeros((1, FOX_W), np.float32)
    ok = np.zeros((1, FOX_W), np.float32)
    ov = np.zeros((1, FOX_W), np.float32)
    for hd in range(FOX_HEADS):
        data = hd * LANES + (hd % 2) * FOX_HD
        extra = hd * LANES + (1 - hd % 2) * FOX_HD
        own[0, data:data + FOX_HD] = 1.0
        for term in range(3):
            pq[term * FOX_HEADS + hd, extra + term] = 1.0
            ok[0, extra + term] = 1.0
            oq[0, extra + 3 + term] = 1.0
            pk[term * FOX_HEADS + hd, extra + 3 + term] = -1.0
        ov[0, extra] = 1.0
    return (jnp.asarray(own), jnp.asarray(pq, _BF16), jnp.asarray(pk, _BF16), jnp.asarray(oq), jnp.asarray(ok),
            jnp.asarray(ov))


def _inproj(x2d, ln_g, ln_b, w_all, bf_pad, cos_t, sin_t, *, nb, meta):
    n = x2d.shape[0]
    s = n // nb
    tm = min(TM_PROJ, s)
    nj = s // tm
    row_spec = lambda w: pl.BlockSpec((tm, w), lambda b, j: (b * nj + j, 0))
    const = lambda shape: pl.BlockSpec(shape, lambda b, j: (0, 0))
    pos_spec = pl.BlockSpec((tm, LANES), lambda b, j: (j, 0))
    tables = _fox_lane_tables()
    outs = [jax.ShapeDtypeStruct((n, HEAD_W), _BF16)] * 4 + [jax.ShapeDtypeStruct((n, FOX_W), _BF16)] * 3
    return pl.pallas_call(
        functools.partial(_inproj_body, meta=meta),
        grid=(nb, nj),
        in_specs=[row_spec(D_MODEL), const((1, D_MODEL)), const((1, D_MODEL)), const(w_all.shape),
                  const((1, LANES)), pos_spec, pos_spec] + [const(t.shape) for t in tables],
        out_specs=[row_spec(HEAD_W)] * 4 + [row_spec(FOX_W)] * 3,
        out_shape=outs,
        scratch_shapes=[pltpu.VMEM((1, LANES), _F32)],
        compiler_params=_params(2),
        name="inproj_meta" if meta else "inproj",
    )(x2d, ln_g, ln_b, w_all, bf_pad, cos_t, sin_t, *tables)


def _ret_body(q_ref, k_ref, v_ref, g_ref, km_ref, vm_ref, dm_ref, xi_ref, zeta_ref, gch_ref, o_ref, st_ref):
    def kv_update(k, v, hd):
        vz = (v.astype(_F32) * zeta_ref[hd]).astype(_BF16)
        return _dot_tn(k, vz)

    @pl.when(pl.program_id(1) == 0)
    def _():
        for hd in range(RET_HEADS):
            sl = slice(hd * LANES, (hd + 1) * LANES)
            st_ref[hd] = kv_update(km_ref[:, sl], vm_ref[:, sl], hd)

    for hd in range(RET_HEADS):
        sl = slice(hd * LANES, (hd + 1) * LANES)
        q = q_ref[:, sl]
        k = k_ref[:, sl]
        v = v_ref[:, sl]
        st = st_ref[hd]
        scores = _dot_nt(q, k) * dm_ref[hd]
        o = _dot(scores.astype(_BF16), v) + _dot(q, st.astype(_BF16)) * xi_ref[hd]
        oc = o - jnp.mean(o, -1, keepdims=True)
        y = oc * lax.rsqrt(jnp.mean(oc * oc, -1, keepdims=True) + LN_EPS)
        o_ref[:, sl] = (y * g_ref[:, sl].astype(_F32)).astype(_BF16)
        st_ref[hd] = gch_ref[hd] * st + kv_update(k, v, hd)


def _retention(rq, rk, rv, rg, rk_m, rv_m, dmask, xi, zeta, gch, *, nb):
    n = rq.shape[0]
    nc = n // nb // BLOCK
    row_spec = pl.BlockSpec((BLOCK, HEAD_W), lambda b, j: (b * nc + j, 0))
    meta_spec = pl.BlockSpec((BLOCK, HEAD_W), lambda b, j: (0, 0))
    tab = lambda shape: pl.BlockSpec(shape, lambda b, j: (0,) * len(shape))
    return pl.pallas_call(
        _ret_body,
        grid=(nb, nc),
        in_specs=[row_spec] * 4 + [meta_spec] * 2
        + [tab((RET_HEADS, BLOCK, BLOCK)), tab((RET_HEADS, BLOCK, 1)), tab((RET_HEADS, BLOCK, 1)),
           pl.BlockSpec(memory_space=pltpu.SMEM)],
        out_specs=row_spec,
        out_shape=jax.ShapeDtypeStruct((n, HEAD_W), _BF16),
        scratch_shapes=[pltpu.VMEM((RET_HEADS, RET_DK, LANES), _F32)],
        compiler_params=_params(2),
        name="retention",
    )(rq, rk, rv, rg, rk_m, rv_m, dmask, xi, zeta, gch)


def _fox_body(q_ref, k_ref, v_ref, km_ref, vm_ref, o_ref, m_sc, acc_sc, s_sc, p_sc):
    t = q_ref.shape[0]
    qi = pl.program_id(2)

    def attend(k_of, v_of, mask, first, tk):
        heads = [slice(hh * LANES, (hh + 1) * LANES) for hh in range(2)]
        for hh, hs in enumerate(heads):
            s_sc[hh, :, :tk] = _dot_nt(q_ref[:, hs], k_of(hs))
        for hh, hs in enumerate(heads):
            for c in range(t // FOX_CHUNK):
                rows = slice(c * FOX_CHUNK, (c + 1) * FOX_CHUNK)
                s = s_sc[hh, rows, :tk]
                if mask is not None:
                    s = mask(s, c)
                mx = jnp.max(s, axis=1, keepdims=True)
                if first:
                    m_new = mx
                else:
                    m_prev = m_sc[hh, rows, :]
                    m_new = jnp.maximum(m_prev, mx)
                    acc_sc[hh, rows, :] = jnp.exp2(m_prev - m_new) * acc_sc[hh, rows, :]
                p_sc[hh, rows, :tk] = jnp.exp2(s - m_new).astype(_BF16)
                m_sc[hh, rows, :] = m_new
            pv = _dot(p_sc[hh, :, :tk], v_of(hs))
            if first:
                acc_sc[hh] = pv
            else:
                acc_sc[hh] += pv

    def meta_mask(s, c):
        key = lax.broadcasted_iota(jnp.int32, s.shape, 1)
        return jnp.where(key >= PAD, s, NEG_INF)

    def causal_mask(s, c):
        key = lax.broadcasted_iota(jnp.int32, s.shape, 1)
        query = lax.broadcasted_iota(jnp.int32, s.shape, 0) + c * FOX_CHUNK
        return jnp.where(key <= query, s, NEG_INF)

    attend(lambda hs: km_ref[:, hs], lambda hs: vm_ref[:, hs], meta_mask, True, BLOCK)

    def tile(ki, mask):
        off = pl.multiple_of(ki * t, t)
        attend(lambda hs: k_ref[pl.ds(off, t), hs], lambda hs: v_ref[pl.ds(off, t), hs], mask, False, t)

    def loop_body(ki, carry):
        tile(ki, None)
        return carry

    lax.fori_loop(0, qi, loop_body, 0)
    tile(qi, causal_mask)

    outs = []
    for hh in range(2):
        acc = acc_sc[hh]
        ones_lane = (1 - hh) * FOX_HD
        outs.append(acc / acc[:, ones_lane:ones_lane + 1])
    lane = lax.broadcasted_iota(jnp.int32, (t, LANES), 1)
    o_ref[...] = jnp.where(lane < FOX_HD, outs[0], outs[1]).astype(_BF16)


def _fox(fq, fk, fv, fk_m, fv_m, *, nb):
    s = fq.shape[1]
    t = T_FOX
    pair_w = 2 * LANES
    return pl.pallas_call(
        _fox_body,
        grid=(nb, FOX_HEADS // 2, s // t),
        in_specs=[
            pl.BlockSpec((None, t, pair_w), lambda b, p, i: (b, i, p)),
            pl.BlockSpec((None, s, pair_w), lambda b, p, i: (b, 0, p)),
            pl.BlockSpec((None, s, pair_w), lambda b, p, i: (b, 0, p)),
            pl.BlockSpec((BLOCK, pair_w), lambda b, p, i: (0, p)),
            pl.BlockSpec((BLOCK, pair_w), lambda b, p, i: (0, p)),
        ],
        out_specs=pl.BlockSpec((None, t, LANES), lambda b, p, i: (b, i, p)),
        out_shape=jax.ShapeDtypeStruct((nb, s, HEAD_W), _BF16),
        scratch_shapes=[pltpu.VMEM((2, t, 1), _F32), pltpu.VMEM((2, t, LANES), _F32), pltpu.VMEM((2, t, t), _F32),
                        pltpu.VMEM((2, t, t), _BF16)],
        compiler_params=_params(3),
        name="fox",
    )(fq, fk, fv, fk_m, fv_m)


def _post_body(x_ref, ret_ref, fox_ref, g0_ref, b0_ref, wo_ref, g1_ref, b1_ref, wr_ref, rb_ref, wgu_ref, wd_ref,
               base_ref, trow_ref, sel_ref, gate_ref, rank_ref, cnt_ref, run_ref):
    tm = x_ref.shape[0]
    i = pl.program_id(0)

    @pl.when(i == 0)
    def _():
        run_ref[...] = jnp.zeros_like(run_ref)

    h0 = _ln(x_ref[...], g0_ref[...], b0_ref[...])
    y = _dot(ret_ref[...], wo_ref[:HEAD_W, :]) + _dot(fox_ref[...], wo_ref[HEAD_W:, :])
    h1 = _ln(ALPHA * h0 + y, g1_ref[...], b1_ref[...])
    tb = h1.astype(_BF16)
    for s in range(ROW_TILES):
        trow_ref[pl.ds(s, tm, stride=ROW_TILES), :] = h1[:, s * LANES:(s + 1) * LANES]

    gu = _dot(tb, wgu_ref[...])
    mid = (_silu(gu[:, :SHARED_FF]) * gu[:, SHARED_FF:]).astype(_BF16)
    base_ref[...] = ALPHA * h1 + _dot(mid, wd_ref[...])

    scores = jax.nn.sigmoid(_dot_nt(wr_ref[...], tb))
    biased = scores + rb_ref[...]
    sub = lax.broadcasted_iota(jnp.int32, (GROUP_SIZE, tm), 0).astype(_F32)
    gscore = []
    for g in range(N_GROUPS):
        v = biased[g * GROUP_SIZE:(g + 1) * GROUP_SIZE, :]
        m1 = jnp.max(v, axis=0, keepdims=True)
        i1 = jnp.min(jnp.where(v == m1, sub, float(GROUP_SIZE)), axis=0, keepdims=True)
        m2 = jnp.max(jnp.where(sub == i1, -jnp.inf, v), axis=0, keepdims=True)
        gscore.append(m1 + m2)
    masked = []
    for g in range(N_GROUPS):
        beaten = jnp.zeros((1, tm), _F32)
        for o in range(N_GROUPS):
            if o == g:
                continue
            wins = (gscore[o] >= gscore[g]) if o < g else (gscore[o] > gscore[g])
            beaten = beaten + wins.astype(_F32)
        keep = beaten < float(TOPK_GROUPS)
        masked.append(jnp.where(keep, biased[g * GROUP_SIZE:(g + 1) * GROUP_SIZE, :], NEG_INF))
    work = jnp.concatenate(masked, axis=0)
    eid = lax.broadcasted_iota(jnp.int32, (N_EXPERTS, tm), 0).astype(_F32)
    hots, sels, raws = [], [], []
    for _ in range(TOP_K):
        m = jnp.max(work, axis=0, keepdims=True)
        idx = jnp.min(jnp.where(work == m, eid, float(N_EXPERTS)), axis=0, keepdims=True)
        hot = eid == idx
        hots.append(hot)
        sels.append(idx.astype(jnp.int32))
        raws.append(jnp.sum(jnp.where(hot, scores, 0.0), axis=0, keepdims=True))
        work = jnp.where(hot, -jnp.inf, work)
    total = raws[0]
    for r in raws[1:]:
        total = total + r
    inv = ROUTED_SCALE / total
    member = hots[0]
    for hot in hots[1:]:
        member = member | hot
    member_f = member.astype(_F32)
    row = lax.broadcasted_iota(jnp.int32, (tm, tm), 0)
    col = lax.broadcasted_iota(jnp.int32, (tm, tm), 1)
    before = (row < col).astype(_BF16)
    rank_e = _dot(member_f.astype(_BF16), before) + run_ref[...]
    pad_rows = sel_ref.shape[0] - TOP_K
    sel_ref[...] = jnp.concatenate(sels + [jnp.zeros((pad_rows, tm), jnp.int32)], axis=0)
    gate_ref[...] = jnp.concatenate([r * inv for r in raws] + [jnp.zeros((pad_rows, tm), _F32)], axis=0)
    ranks = [jnp.sum(jnp.where(hot, rank_e, 0.0), axis=0, keepdims=True).astype(jnp.int32) for hot in hots]
    rank_ref[...] = jnp.concatenate(ranks + [jnp.zeros((pad_rows, tm), jnp.int32)], axis=0)
    run_ref[...] = run_ref[...] + jnp.sum(member_f, axis=1, keepdims=True)
    cnt_ref[...] = run_ref[...]


def _post(x2d, ret, fox, ln0_g, ln0_b, w_out, ln1_g, ln1_b, w_rt, rbias, w_gu, w_sd):
    n = x2d.shape[0]
    tm = TM_PROJ
    row_spec = lambda w: pl.BlockSpec((tm, w), lambda i: (i, 0))
    col_spec = pl.BlockSpec((SUBLANES, tm), lambda i: (0, i))
    const = lambda shape: pl.BlockSpec(shape, lambda i: (0, 0))
    return pl.pallas_call(
        _post_body,
        grid=(n // tm,),
        in_specs=[row_spec(D_MODEL), row_spec(HEAD_W), row_spec(HEAD_W), const((1, D_MODEL)), const((1, D_MODEL)),
                  const(w_out.shape), const((1, D_MODEL)), const((1, D_MODEL)), const(w_rt.shape),
                  const((N_EXPERTS, 1)), const(w_gu.shape), const(w_sd.shape)],
        out_specs=[row_spec(D_MODEL), pl.BlockSpec((tm * ROW_TILES, LANES), lambda i: (i, 0)),
                   col_spec, col_spec, col_spec, const((N_EXPERTS, 1))],
        out_shape=[jax.ShapeDtypeStruct((n, D_MODEL), _F32), jax.ShapeDtypeStruct((n * ROW_TILES, LANES), _F32),
                   jax.ShapeDtypeStruct((SUBLANES, n), jnp.int32), jax.ShapeDtypeStruct((SUBLANES, n), _F32),
                   jax.ShapeDtypeStruct((SUBLANES, n), jnp.int32), jax.ShapeDtypeStruct((N_EXPERTS, 1), _F32)],
        scratch_shapes=[pltpu.VMEM((N_EXPERTS, 1), _F32)],
        compiler_params=_params(1),
        name="post_mixer",
    )(x2d, ret, fox, ln0_g, ln0_b, w_out, ln1_g, ln1_b, w_rt, rbias, w_gu, w_sd)


def _plan_body(sel_ref, rank_ref, cnt_ref, dest_ref, blk_ref, fill_ref, used_ref, last_ref):
    cnt = cnt_ref[...]
    padded = jnp.ceil(cnt * (1.0 / EBLK)) * EBLK
    er = lax.broadcasted_iota(jnp.int32, (N_EXPERTS, N_EXPERTS), 0)
    ec = lax.broadcasted_iota(jnp.int32, (N_EXPERTS, N_EXPERTS), 1)
    padded_row = jnp.sum(jnp.where(er == ec, padded, 0.0), axis=0, keepdims=True)
    pstart = jnp.sum(jnp.where(ec < er, padded_row, 0.0), axis=1, keepdims=True)
    pend = pstart + padded
    sel = sel_ref[...]
    dest = rank_ref[...]
    for e in range(N_EXPERTS):
        dest = dest + jnp.where(sel == e, pstart[e:e + 1, :].astype(jnp.int32), 0)
    dest_ref[...] = dest
    nblk = blk_ref.shape[1]
    first_row = (lax.broadcasted_iota(jnp.int32, (N_EXPERTS, nblk), 1) * EBLK).astype(_F32)
    owner = jnp.minimum(jnp.sum((pend <= first_row).astype(_F32), axis=0, keepdims=True), N_EXPERTS - 1.0)
    blk_ref[...] = owner.astype(jnp.int32)
    mine = lax.broadcasted_iota(jnp.int32, (N_EXPERTS, nblk), 0).astype(_F32) == owner
    live_end = jnp.sum(jnp.where(mine, pstart + cnt, 0.0), axis=0, keepdims=True)
    fill_ref[...] = jnp.clip(live_end - first_row[:1, :], 0.0, float(EBLK)).astype(jnp.int32)
    used_ref[...] = (pend[N_EXPERTS - 1:, :] * (1.0 / EBLK)).astype(jnp.int32)
    last = jnp.where(padded > 0.0, pend * (1.0 / EBLK) - 1.0, -1.0)
    last_ref[...] = jnp.sum(jnp.where(er == ec, last, 0.0), axis=0, keepdims=True).astype(jnp.int32)


def _plan(sel, rank, cnt, nblk_pad):
    n = sel.shape[1]
    full = lambda shape: pl.BlockSpec(shape, lambda i: (0, 0))
    return pl.pallas_call(
        _plan_body,
        grid=(1,),
        in_specs=[full(sel.shape), full(rank.shape), full(cnt.shape)],
        out_specs=[full(sel.shape), full((1, nblk_pad)), full((1, nblk_pad)), full((1, 1)), full((1, N_EXPERTS))],
        out_shape=[jax.ShapeDtypeStruct((SUBLANES, n), jnp.int32), jax.ShapeDtypeStruct((1, nblk_pad), jnp.int32),
                   jax.ShapeDtypeStruct((1, nblk_pad), jnp.int32), jax.ShapeDtypeStruct((1, 1), jnp.int32),
                   jax.ShapeDtypeStruct((1, N_EXPERTS), jnp.int32)],
        compiler_params=_params(1),
        name="plan",
    )(sel, rank, cnt)


def _row_copy(src, src_row, dst, dst_row, sem):
    return pltpu.make_async_copy(src.at[pl.ds(pl.multiple_of(src_row * ROW_TILES, ROW_TILES), ROW_TILES), :],
                                 dst.at[pl.ds(pl.multiple_of(dst_row * ROW_TILES, ROW_TILES), ROW_TILES), :], sem)


def _dispatch_body(dest_ref, last_ref, used_ref, t_ref, xs_ref, zero_sc, sem, zsem):
    blk_rows = EBLK * ROW_TILES
    nblk = xs_ref.shape[0] // blk_rows

    @pl.when(pl.program_id(0) == 0)
    def _():
        zero_sc[...] = jnp.zeros_like(zero_sc)

        def blk_copy(b):
            return pltpu.make_async_copy(zero_sc, xs_ref.at[pl.ds(pl.multiple_of(b * blk_rows, blk_rows), blk_rows), :],
                                         zsem)

        def for_padded_blocks(act):
            def per_expert(e, carry):
                @pl.when(last_ref[e] >= 0)
                def _():
                    act(blk_copy(last_ref[e]))

                return carry

            def per_tail(b, carry):
                act(blk_copy(b))
                return carry

            lax.fori_loop(0, N_EXPERTS, per_expert, 0)
            lax.fori_loop(used_ref[0], nblk, per_tail, 0)

        for_padded_blocks(lambda cp: cp.start())
        for_padded_blocks(lambda cp: cp.wait())

    def issue(i, carry):
        for k in range(TOP_K):
            _row_copy(t_ref, i, xs_ref, dest_ref[i * SUBLANES + k], sem).start()
        return carry

    lax.fori_loop(0, TT, issue, 0)
    for _ in range(TOP_K):
        pltpu.make_async_copy(t_ref, xs_ref.at[pl.ds(0, TT * ROW_TILES), :], sem).wait()


def _dispatch(dest_flat, last_blk, used, trow, total_rows):
    n = trow.shape[0] // ROW_TILES
    smem = pl.BlockSpec(memory_space=pltpu.SMEM)
    return pl.pallas_call(
        _dispatch_body,
        grid=(n // TT,),
        in_specs=[pl.BlockSpec((TT * SUBLANES,), lambda i: (i,), memory_space=pltpu.SMEM), smem, smem,
                  pl.BlockSpec((TT * ROW_TILES, LANES), lambda i: (i, 0))],
        out_specs=pl.BlockSpec(memory_space=pl.ANY),
        out_shape=jax.ShapeDtypeStruct((total_rows * ROW_TILES, LANES), _F32),
        scratch_shapes=[pltpu.VMEM((EBLK * ROW_TILES, LANES), _F32), pltpu.SemaphoreType.DMA(()),
                        pltpu.SemaphoreType.DMA(())],
        compiler_params=_params(1),
        name="dispatch",
    )(dest_flat, last_blk, used, trow)


def _expert_body(blk_ref, used_ref, fill_ref, xs_ref, wg_ref, wu_ref, wd_ref, y_ref, wgu_sc, wd_sc):
    i = pl.program_id(0)
    prev = blk_ref[jnp.maximum(i - 1, 0)]
    fresh = (i == 0) | (blk_ref[i] != prev)

    @pl.when(fresh)
    def _():
        wgu_sc[:, :EXPERT_FF] = wg_ref[...].astype(_BF16)
        wgu_sc[:, EXPERT_FF:] = wu_ref[...].astype(_BF16)
        wd_sc[...] = wd_ref[...].astype(_BF16)

    @pl.when(i < used_ref[0])
    def _():
        x = jnp.concatenate([xs_ref[pl.ds(s, EBLK, stride=ROW_TILES), :] for s in range(ROW_TILES)], axis=1)
        live = lax.broadcasted_iota(jnp.int32, (EBLK, 1), 0) < fill_ref[i]
        x = jnp.where(live, x, 0.0).astype(_BF16)
        gu = _dot(x, wgu_sc[...])
        mid = (_silu(gu[:, :EXPERT_FF]) * gu[:, EXPERT_FF:]).astype(_BF16)
        y = _dot(mid, wd_sc[...])
        for s in range(ROW_TILES):
            y_ref[pl.ds(s, EBLK, stride=ROW_TILES), :] = y[:, s * LANES:(s + 1) * LANES]

    @pl.when(i >= used_ref[0])
    def _():
        y_ref[...] = jnp.zeros_like(y_ref)


def _experts(blk_e, used, fill, xs, we_gate, we_up, we_down):
    nblk = xs.shape[0] // (EBLK * ROW_TILES)
    last = lambda i, used: jnp.minimum(i, jnp.maximum(used[0] - 1, 0))
    w_spec = lambda shape: pl.BlockSpec((None,) + shape, lambda i, blk, used, fill: (blk[i], 0, 0))
    return pl.pallas_call(
        _expert_body,
        grid_spec=pltpu.PrefetchScalarGridSpec(
            num_scalar_prefetch=3,
            grid=(nblk,),
            in_specs=[pl.BlockSpec((EBLK * ROW_TILES, LANES), lambda i, blk, used, fill: (last(i, used), 0)),
                      w_spec((D_MODEL, EXPERT_FF)), w_spec((D_MODEL, EXPERT_FF)), w_spec((EXPERT_FF, D_MODEL))],
            out_specs=pl.BlockSpec((EBLK * ROW_TILES, LANES), lambda i, blk, used, fill: (i, 0)),
            scratch_shapes=[pltpu.VMEM((D_MODEL, 2 * EXPERT_FF), _BF16), pltpu.VMEM((EXPERT_FF, D_MODEL), _BF16)],
        ),
        out_shape=jax.ShapeDtypeStruct(xs.shape, _F32),
        compiler_params=_params(1),
        name="experts",
    )(blk_e, used, fill, xs, we_gate, we_up, we_down)


def _combine_body(dest_ref, y_ref, base_ref, gate_ref, g2_ref, b2_ref, o_ref, z_sc, sem):
    def issue(i, carry):
        for k in range(TOP_K):
            _row_copy(y_ref, dest_ref[i * SUBLANES + k], z_sc, k * TT + i, sem).start()
        return carry

    lax.fori_loop(0, TT, issue, 0)
    rows = TT * TOP_K * ROW_TILES
    pltpu.make_async_copy(y_ref.at[pl.ds(0, rows), :], z_sc, sem).wait()
    acc = base_ref[...]
    gates = gate_ref[...]
    for k in range(TOP_K):
        z = jnp.concatenate([z_sc[pl.ds(k * TT * ROW_TILES + s, TT, stride=ROW_TILES), :] for s in range(ROW_TILES)],
                            axis=1)
        acc = acc + gates[:, k:k + 1] * z
    o_ref[...] = _ln(acc, g2_ref[...], b2_ref[...])


def _combine(dest_flat, y, base, gates_t, ln2_g, ln2_b):
    n = base.shape[0]
    const = lambda shape: pl.BlockSpec(shape, lambda i: (0, 0))
    return pl.pallas_call(
        _combine_body,
        grid=(n // TT,),
        in_specs=[pl.BlockSpec((TT * SUBLANES,), lambda i: (i,), memory_space=pltpu.SMEM),
                  pl.BlockSpec(memory_space=pl.ANY),
                  pl.BlockSpec((TT, D_MODEL), lambda i: (i, 0)),
                  pl.BlockSpec((TT, SUBLANES), lambda i: (i, 0)),
                  const((1, D_MODEL)), const((1, D_MODEL))],
        out_specs=pl.BlockSpec((TT, D_MODEL), lambda i: (i, 0)),
        out_shape=jax.ShapeDtypeStruct((n, D_MODEL), _F32),
        scratch_shapes=[pltpu.VMEM((TT * TOP_K * ROW_TILES, LANES), _F32), pltpu.SemaphoreType.DMA(())],
        compiler_params=_params(1),
        name="combine",
    )(dest_flat, y, base, gates_t, ln2_g, ln2_b)


def _rope_tables(pos):
    half = RET_DK // 2
    inv = ROPE_BASE ** (-jnp.arange(half, dtype=_F32) / half)
    ang = pos[:, None] * inv[None, :]
    cos = jnp.cos(ang)
    sin = jnp.sin(ang)
    return jnp.concatenate([cos, cos], -1), jnp.concatenate([-sin, sin], -1)


def _decay_tables():
    lg = jnp.log1p(-jnp.exp2(-5.0 - jnp.arange(RET_HEADS, dtype=_F32)))
    idx = jnp.arange(BLOCK, dtype=_F32)
    rel = idx[:, None] - idx[None, :]
    causal = rel >= 0
    dmask = jnp.where(causal[None], jnp.exp(jnp.where(causal, rel, 0.0)[None] * lg[:, None, None]), 0.0)
    zeta = jnp.exp((BLOCK - 1.0 - idx)[None, :] * lg[:, None])
    xi = jnp.exp((idx + 1.0)[None, :] * lg[:, None])
    return dmask, xi[:, :, None], zeta[:, :, None], jnp.exp(BLOCK * lg)


def kernel(x, meta, ln0_g, ln0_b, w_in, b_forget, w_out, ln1_g, ln1_b, w_router, router_bias, we_gate, we_up,
           we_down, ws_gate, ws_up, ws_down, ln2_g, ln2_b):
    nb, s, d = x.shape
    assert d == D_MODEL and meta.shape == (N_META, D_MODEL) and w_in.shape[0] == 1
    assert s % TM_PROJ == 0 and s % T_FOX == 0 and (nb * s) % TT == 0
    n = nb * s
    x2d = x.reshape(n, d)
    row2 = lambda v: v.reshape(1, -1).astype(_F32)
    main_cols = 7 * HEAD_W
    w_all = jnp.concatenate(
        [w_in[0, :, :main_cols], w_in[0, :, main_cols:], jnp.zeros((d, LANES - FOX_HEADS), w_in.dtype)],
        axis=1).astype(_BF16)
    bf_pad = jnp.concatenate([b_forget[0].astype(_F32), jnp.zeros((LANES - FOX_HEADS,), _F32)]).reshape(1, LANES)
    g0, b0 = row2(ln0_g), row2(ln0_b)

    cos_x, sin_x = _rope_tables(jnp.arange(s, dtype=_F32) + float(N_META))
    cos_m, sin_m = _rope_tables(jnp.arange(BLOCK, dtype=_F32) - float(PAD))
    meta_blk = jnp.concatenate([jnp.zeros((PAD, d), _F32), meta.astype(_F32)], axis=0)

    rq, rk, rv, rg, fq, fk, fv = _inproj(x2d, g0, b0, w_all, bf_pad, cos_x, sin_x, nb=nb, meta=False)
    _, rk_m, rv_m, _, _, fk_m, fv_m = _inproj(meta_blk, g0, b0, w_all, bf_pad, cos_m, sin_m, nb=1, meta=True)

    dmask, xi, zeta, gch = _decay_tables()
    ret = _retention(rq, rk, rv, rg, rk_m, rv_m, dmask, xi, zeta, gch, nb=nb)

    per_batch = lambda a: a.reshape(nb, s, FOX_W)
    fox = _fox(per_batch(fq), per_batch(fk), per_batch(fv), fk_m, fv_m, nb=nb).reshape(n, HEAD_W)

    w_gu = jnp.concatenate([ws_gate[0], ws_up[0]], axis=1).astype(_BF16)
    base, trow, sel, gates, rank, cnt = _post(
        x2d, ret, fox, g0, b0, w_out[0].astype(_BF16), row2(ln1_g[0]), row2(ln1_b[0]),
        jnp.transpose(w_router[0]).astype(_BF16), router_bias[0].astype(_F32).reshape(N_EXPERTS, 1), w_gu,
        ws_down[0].astype(_BF16))

    nblk = n * TOP_K // EBLK + N_EXPERTS
    nblk_pad = -(-nblk // LANES) * LANES
    dest, blk_e, fill, used, last_blk = _plan(sel, rank, cnt, nblk_pad)
    dest_flat = jnp.transpose(dest).reshape(-1)
    used = used.reshape(-1)

    xs = _dispatch(dest_flat, last_blk.reshape(-1), used, trow, nblk * EBLK)
    y = _experts(blk_e.reshape(-1), used, fill.reshape(-1), xs, we_gate[0], we_up[0], we_down[0])
    out = _combine(dest_flat, y, base, jnp.transpose(gates), row2(ln2_g[0]), row2(ln2_b[0]))
    return out.reshape(nb, s, d)
```

```python
import functools

import jax
import jax.numpy as jnp
import numpy as np
from jax import lax
from jax.experimental import pallas as pl
from jax.experimental.pallas import tpu as pltpu

D_MODEL = 1024
N_META = 16
BLOCK = 128
PAD = BLOCK - N_META
RET_HEADS = 4
RET_DK = 128
FOX_HEADS = 8
FOX_HD = 64
N_EXPERTS = 64
TOP_K = 6
N_GROUPS = 8
GROUP_SIZE = N_EXPERTS // N_GROUPS
TOPK_GROUPS = 4
EXPERT_FF = 256
SHARED_FF = 256
ROUTED_SCALE = 2.5
ROPE_BASE = 10000.0
LN_EPS = 1e-5
NEG_INF = -1e30
ALPHA = 2.0 ** 0.25
HEAD_W = 512
LOG2E = 1.4426950408889634

LANES = 128
FOX_W = FOX_HEADS * LANES
SUBLANES = 8
ROW_TILES = D_MODEL // LANES
ROW_DTYPE = jnp.float32

TM_PROJ = 512
T_FOX = 512
FOX_CHUNK = 64
TT = 256
EBLK = 256
VMEM_LIMIT = 48 * 1024 * 1024

_F32 = jnp.float32
_BF16 = jnp.bfloat16


def _ln(x, g, b):
    xc = x - jnp.mean(x, -1, keepdims=True)
    var = jnp.mean(xc * xc, -1, keepdims=True)
    return xc * lax.rsqrt(var + LN_EPS) * g + b


def _dot(a, b):
    return jnp.dot(a, b, preferred_element_type=_F32)


def _dot_nt(a, b):
    return lax.dot_general(a, b, (((1,), (1,)), ((), ())), preferred_element_type=_F32)


def _dot_tn(a, b):
    return lax.dot_general(a, b, (((0,), (0,)), ((), ())), preferred_element_type=_F32)


def _silu(x):
    return x * jax.nn.sigmoid(x)


def _params(n_axes):
    return pltpu.CompilerParams(dimension_semantics=("arbitrary",) * n_axes, vmem_limit_bytes=VMEM_LIMIT)


def _inproj_body(x_ref, g_ref, b_ref, w_ref, bf_ref, cos_ref, sin_ref, own_ref, pq_ref, pk_ref, oq_ref, ok_ref, ov_ref,
                 rq_ref, rk_ref, rv_ref, rg_ref, fq_ref, fk_ref, fv_ref, carry_ref, *, meta):
    tm = x_ref.shape[0]
    h = _ln(x_ref[...], g_ref[...], b_ref[...])
    if meta:
        valid = lax.broadcasted_iota(jnp.int32, (tm, 1), 0) >= PAD
        h = jnp.where(valid, h, 0.0)
    hb = h.astype(_BF16)
    cos = cos_ref[...]
    sin = sin_ref[...]

    def proj(g):
        return _dot(hb, w_ref[:, g * HEAD_W:(g + 1) * HEAD_W])

    def rope_store(p, out_ref, scale):
        for hd in range(RET_HEADS):
            t = p[:, hd * LANES:(hd + 1) * LANES]
            r = t * cos + pltpu.roll(t, LANES // 2, axis=1) * sin
            out_ref[:, hd * LANES:(hd + 1) * LANES] = (r * scale).astype(_BF16)

    rope_store(proj(0), rq_ref, 1.0)
    rope_store(proj(1), rk_ref, RET_DK ** -0.5)
    rv_ref[...] = proj(2).astype(_BF16)
    rg_ref[...] = _silu(proj(3)).astype(_BF16)

    z = _dot(hb, w_ref[:, 7 * HEAD_W:7 * HEAD_W + LANES]) + bf_ref[...]
    logf = jnp.minimum(z, 0.0) - jnp.log1p(jnp.exp(-jnp.abs(z)))
    if meta:
        logf = jnp.where(valid, logf, 0.0)
    l1 = logf.astype(_BF16)
    r1 = logf - l1.astype(_F32)
    l2 = r1.astype(_BF16)
    l3 = (r1 - l2.astype(_F32)).astype(_BF16)
    row = lax.broadcasted_iota(jnp.int32, (tm, tm), 0)
    col = lax.broadcasted_iota(jnp.int32, (tm, tm), 1)
    tri = (col <= row).astype(_BF16)
    c = _dot(tri, l1) + _dot(tri, l2) + _dot(tri, l3)
    if meta:
        c = c - c[tm - 1:tm, :]
    else:
        @pl.when(pl.program_id(1) == 0)
        def _():
            carry_ref[...] = jnp.zeros_like(carry_ref)

        c = c + carry_ref[...]
        carry_ref[...] = c[tm - 1:tm, :]

    head_lane = lax.broadcasted_iota(jnp.int32, (1, LANES), 1) < FOX_HEADS
    cl = jnp.where(head_lane, c, 0.0) * LOG2E
    c1 = cl.astype(_BF16).astype(_F32)
    r1 = cl - c1
    c2 = r1.astype(_BF16).astype(_F32)
    c3 = (r1 - c2).astype(_BF16).astype(_F32)
    csplit = (c1 + pltpu.roll(c2, FOX_HEADS, axis=1) + pltpu.roll(c3, 2 * FOX_HEADS, axis=1)).astype(_BF16)
    own = own_ref[...] > 0.0

    def per_head(p):
        return jnp.concatenate([p[:, (hd // 2) * LANES:(hd // 2 + 1) * LANES] for hd in range(FOX_HEADS)], axis=1)

    fq_ref[...] = jnp.where(own, per_head(proj(4) * (FOX_HD ** -0.5 * LOG2E)),
                            _dot(csplit, pq_ref[...]) + oq_ref[...]).astype(_BF16)
    fk_ref[...] = jnp.where(own, per_head(proj(5)), _dot(csplit, pk_ref[...]) + ok_ref[...]).astype(_BF16)
    fv_ref[...] = jnp.where(own, per_head(proj(6)), ov_ref[...]).astype(_BF16)


def _fox_lane_tables():
    own = np.zeros((1, FOX_W), np.float32)
    pq = np.zeros((LANES, FOX_W), np.float32)
    pk = np.zeros((LANES, FOX_W), np.float32)
    oq = np.zeros((1, FOX_W), np.float32)
    ok = np.zeros((1, FOX_W), np.float32)
    ov = np.zeros((1, FOX_W), np.float32)
    for hd in range(FOX_HEADS):
        data = hd * LANES + (hd % 2) * FOX_HD
        extra = hd * LANES + (1 - hd % 2) * FOX_HD
        own[0, data:data + FOX_HD] = 1.0
        for term in range(3):
            pq[term * FOX_HEADS + hd, extra + term] = 1.0
            ok[0, extra + term] = 1.0
            oq[0, extra + 3 + term] = 1.0
            pk[term * FOX_HEADS + hd, extra + 3 + term] = -1.0
        ov[0, extra] = 1.0
    return (jnp.asarray(own), jnp.asarray(pq, _BF16), jnp.asarray(pk, _BF16), jnp.asarray(oq), jnp.asarray(ok),
            jnp.asarray(ov))


def _inproj(x2d, ln_g, ln_b, w_all, bf_pad, cos_t, sin_t, *, nb, meta):
    n = x2d.shape[0]
    s = n // nb
    tm = min(TM_PROJ, s)
    nj = s // tm
    row_spec = lambda w: pl.BlockSpec((tm, w), lambda b, j: (b * nj + j, 0))
    const = lambda shape: pl.BlockSpec(shape, lambda b, j: (0, 0))
    pos_spec = pl.BlockSpec((tm, LANES), lambda b, j: (j, 0))
    tables = _fox_lane_tables()
    outs = [jax.ShapeDtypeStruct((n, HEAD_W), _BF16)] * 4 + [jax.ShapeDtypeStruct((n, FOX_W), _BF16)] * 3
    return pl.pallas_call(
        functools.partial(_inproj_body, meta=meta),
        grid=(nb, nj),
        in_specs=[row_spec(D_MODEL), const((1, D_MODEL)), const((1, D_MODEL)), const(w_all.shape),
                  const((1, LANES)), pos_spec, pos_spec] + [const(t.shape) for t in tables],
        out_specs=[row_spec(HEAD_W)] * 4 + [row_spec(FOX_W)] * 3,
        out_shape=outs,
        scratch_shapes=[pltpu.VMEM((1, LANES), _F32)],
        compiler_params=_params(2),
        name="inproj_meta" if meta else "inproj",
    )(x2d, ln_g, ln_b, w_all, bf_pad, cos_t, sin_t, *tables)


def _ret_body(q_ref, k_ref, v_ref, g_ref, km_ref, vm_ref, dm_ref, xi_ref, zeta_ref, gch_ref, o_ref, st_ref):
    def kv_update(k, v, hd):
        vz = (v.astype(_F32) * zeta_ref[hd]).astype(_BF16)
        return _dot_tn(k, vz)

    @pl.when(pl.program_id(1) == 0)
    def _():
        for hd in range(RET_HEADS):
            sl = slice(hd * LANES, (hd + 1) * LANES)
            st_ref[hd] = kv_update(km_ref[:, sl], vm_ref[:, sl], hd)

    for hd in range(RET_HEADS):
        sl = slice(hd * LANES, (hd + 1) * LANES)
        q = q_ref[:, sl]
        k = k_ref[:, sl]
        v = v_ref[:, sl]
        st = st_ref[hd]
        scores = _dot_nt(q, k) * dm_ref[hd]
        o = _dot(scores.astype(_BF16), v) + _dot(q, st.astype(_BF16)) * xi_ref[hd]
        oc = o - jnp.mean(o, -1, keepdims=True)
        y = oc * lax.rsqrt(jnp.mean(oc * oc, -1, keepdims=True) + LN_EPS)
        o_ref[:, sl] = (y * g_ref[:, sl].astype(_F32)).astype(_BF16)
        st_ref[hd] = gch_ref[hd] * st + kv_update(k, v, hd)


def _retention(rq, rk, rv, rg, rk_m, rv_m, dmask, xi, zeta, gch, *, nb):
    n = rq.shape[0]
    nc = n // nb // BLOCK
    row_spec = pl.BlockSpec((BLOCK, HEAD_W), lambda b, j: (b * nc + j, 0))
    meta_spec = pl.BlockSpec((BLOCK, HEAD_W), lambda b, j: (0, 0))
    tab = lambda shape: pl.BlockSpec(shape, lambda b, j: (0,) * len(shape))
    return pl.pallas_call(
        _ret_body,
        grid=(nb, nc),
        in_specs=[row_spec] * 4 + [meta_spec] * 2
        + [tab((RET_HEADS, BLOCK, BLOCK)), tab((RET_HEADS, BLOCK, 1)), tab((RET_HEADS, BLOCK, 1)),
           pl.BlockSpec(memory_space=pltpu.SMEM)],
        out_specs=row_spec,
        out_shape=jax.ShapeDtypeStruct((n, HEAD_W), _BF16),
        scratch_shapes=[pltpu.VMEM((RET_HEADS, RET_DK, LANES), _F32)],
        compiler_params=_params(2),
        name="retention",
    )(rq, rk, rv, rg, rk_m, rv_m, dmask, xi, zeta, gch)


def _fox_body(q_ref, k_ref, v_ref, km_ref, vm_ref, o_ref, m_sc, acc_sc, s_sc, p_sc):
    t = q_ref.shape[0]
    qi = pl.program_id(2)

    heads = [slice(hh * LANES, (hh + 1) * LANES) for hh in range(2)]

    def logits(hh, k, buf, tk):
        s_sc[buf, hh, :, :tk] = _dot_nt(q_ref[:, heads[hh]], k)

    def update(hh, v, buf, tk, mask, first):
        for c in range(t // FOX_CHUNK):
            rows = slice(c * FOX_CHUNK, (c + 1) * FOX_CHUNK)
            s = s_sc[buf, hh, rows, :tk]
            if mask is not None:
                s = mask(s, c)
            mx = jnp.max(s, axis=1, keepdims=True)
            if first:
                m_new = mx
            else:
                m_prev = m_sc[hh, rows, :]
                m_new = jnp.maximum(m_prev, mx)
                acc_sc[hh, rows, :] = jnp.exp2(m_prev - m_new) * acc_sc[hh, rows, :]
            p_sc[hh, rows, :tk] = jnp.exp2(s - m_new).astype(_BF16)
            m_sc[hh, rows, :] = m_new
        pv = _dot(p_sc[hh, :, :tk], v)
        if first:
            acc_sc[hh] = pv
        else:
            acc_sc[hh] += pv

    def meta_mask(s, c):
        key = lax.broadcasted_iota(jnp.int32, s.shape, 1)
        return jnp.where(key >= PAD, s, NEG_INF)

    def causal_mask(s, c):
        key = lax.broadcasted_iota(jnp.int32, s.shape, 1)
        query = lax.broadcasted_iota(jnp.int32, s.shape, 0) + c * FOX_CHUNK
        return jnp.where(key <= query, s, NEG_INF)

    def key_tile(ref, ki, hh):
        return ref[pl.ds(pl.multiple_of(ki * t, t), t), heads[hh]]

    for hh in range(2):
        logits(hh, km_ref[:, heads[hh]], 1, BLOCK)
    for hh in range(2):
        logits(hh, key_tile(k_ref, 0, hh), 0, t)
        update(hh, vm_ref[:, heads[hh]], 1, BLOCK, meta_mask, True)

    def step(ki, buf, mask, more):
        for hh in range(2):
            if more:
                logits(hh, key_tile(k_ref, ki + 1, hh), 1 - buf, t)
            update(hh, key_tile(v_ref, ki, hh), buf, t, mask, False)

    def pair_body(j, carry):
        step(2 * j, 0, None, True)
        step(2 * j + 1, 1, None, True)
        return carry

    lax.fori_loop(0, lax.shift_right_logical(qi, 1), pair_body, 0)
    odd = lax.rem(qi, 2) == 1

    @pl.when(odd)
    def _():
        step(qi - 1, 0, None, True)
        step(qi, 1, causal_mask, False)

    @pl.when(jnp.logical_not(odd))
    def _():
        step(qi, 0, causal_mask, False)

    outs = []
    for hh in range(2):
        acc = acc_sc[hh]
        ones_lane = (1 - hh) * FOX_HD
        outs.append(acc / acc[:, ones_lane:ones_lane + 1])
    lane = lax.broadcasted_iota(jnp.int32, (t, LANES), 1)
    o_ref[...] = jnp.where(lane < FOX_HD, outs[0], outs[1]).astype(_BF16)


def _fox(fq, fk, fv, fk_m, fv_m, *, nb):
    s = fq.shape[1]
    t = T_FOX
    pair_w = 2 * LANES
    return pl.pallas_call(
        _fox_body,
        grid=(nb, FOX_HEADS // 2, s // t),
        in_specs=[
            pl.BlockSpec((None, t, pair_w), lambda b, p, i: (b, i, p)),
            pl.BlockSpec((None, s, pair_w), lambda b, p, i: (b, 0, p)),
            pl.BlockSpec((None, s, pair_w), lambda b, p, i: (b, 0, p)),
            pl.BlockSpec((BLOCK, pair_w), lambda b, p, i: (0, p)),
            pl.BlockSpec((BLOCK, pair_w), lambda b, p, i: (0, p)),
        ],
        out_specs=pl.BlockSpec((None, t, LANES), lambda b, p, i: (b, i, p)),
        out_shape=jax.ShapeDtypeStruct((nb, s, HEAD_W), _BF16),
        scratch_shapes=[pltpu.VMEM((2, t, 1), _F32), pltpu.VMEM((2, t, LANES), _F32), pltpu.VMEM((2, 2, t, t), _F32),
                        pltpu.VMEM((2, t, t), _BF16)],
        compiler_params=_params(3),
        name="fox",
    )(fq, fk, fv, fk_m, fv_m)


def _post_body(x_ref, ret_ref, fox_ref, g0_ref, b0_ref, wo_ref, g1_ref, b1_ref, wr_ref, rb_ref, wgu_ref, wd_ref,
               base_ref, trow_ref, sel_ref, gate_ref, rank_ref, cnt_ref, run_ref):
    tm = x_ref.shape[0]
    i = pl.program_id(0)

    @pl.when(i == 0)
    def _():
        run_ref[...] = jnp.zeros_like(run_ref)

    h0 = _ln(x_ref[...], g0_ref[...], b0_ref[...])
    y = _dot(ret_ref[...], wo_ref[:HEAD_W, :]) + _dot(fox_ref[...], wo_ref[HEAD_W:, :])
    h1 = _ln(ALPHA * h0 + y, g1_ref[...], b1_ref[...])
    tb = h1.astype(_BF16)
    _store_rows(trow_ref, h1)

    gu = _dot(tb, wgu_ref[...])
    mid = (_silu(gu[:, :SHARED_FF]) * gu[:, SHARED_FF:]).astype(_BF16)
    base_ref[...] = ALPHA * h1 + _dot(mid, wd_ref[...])

    scores = jax.nn.sigmoid(_dot_nt(wr_ref[...], tb))
    biased = scores + rb_ref[...]
    sub = lax.broadcasted_iota(jnp.int32, (GROUP_SIZE, tm), 0).astype(_F32)
    gscore = []
    for g in range(N_GROUPS):
        v = biased[g * GROUP_SIZE:(g + 1) * GROUP_SIZE, :]
        m1 = jnp.max(v, axis=0, keepdims=True)
        i1 = jnp.min(jnp.where(v == m1, sub, float(GROUP_SIZE)), axis=0, keepdims=True)
        m2 = jnp.max(jnp.where(sub == i1, -jnp.inf, v), axis=0, keepdims=True)
        gscore.append(m1 + m2)
    masked = []
    for g in range(N_GROUPS):
        beaten = jnp.zeros((1, tm), _F32)
        for o in range(N_GROUPS):
            if o == g:
                continue
            wins = (gscore[o] >= gscore[g]) if o < g else (gscore[o] > gscore[g])
            beaten = beaten + wins.astype(_F32)
        keep = beaten < float(TOPK_GROUPS)
        masked.append(jnp.where(keep, biased[g * GROUP_SIZE:(g + 1) * GROUP_SIZE, :], NEG_INF))
    work = jnp.concatenate(masked, axis=0)
    eid = lax.broadcasted_iota(jnp.int32, (N_EXPERTS, tm), 0).astype(_F32)
    hots, sels, raws = [], [], []
    for _ in range(TOP_K):
        m = jnp.max(work, axis=0, keepdims=True)
        idx = jnp.min(jnp.where(work == m, eid, float(N_EXPERTS)), axis=0, keepdims=True)
        hot = eid == idx
        hots.append(hot)
        sels.append(idx.astype(jnp.int32))
        raws.append(jnp.sum(jnp.where(hot, scores, 0.0), axis=0, keepdims=True))
        work = jnp.where(hot, -jnp.inf, work)
    total = raws[0]
    for r in raws[1:]:
        total = total + r
    inv = ROUTED_SCALE / total
    member = hots[0]
    for hot in hots[1:]:
        member = member | hot
    member_f = member.astype(_F32)
    row = lax.broadcasted_iota(jnp.int32, (tm, tm), 0)
    col = lax.broadcasted_iota(jnp.int32, (tm, tm), 1)
    before = (row < col).astype(_BF16)
    rank_e = _dot(member_f.astype(_BF16), before) + run_ref[...]
    pad_rows = sel_ref.shape[0] - TOP_K
    sel_ref[...] = jnp.concatenate(sels + [jnp.zeros((pad_rows, tm), jnp.int32)], axis=0)
    gate_ref[...] = jnp.concatenate([r * inv for r in raws] + [jnp.zeros((pad_rows, tm), _F32)], axis=0)
    ranks = [jnp.sum(jnp.where(hot, rank_e, 0.0), axis=0, keepdims=True).astype(jnp.int32) for hot in hots]
    rank_ref[...] = jnp.concatenate(ranks + [jnp.zeros((pad_rows, tm), jnp.int32)], axis=0)
    run_ref[...] = run_ref[...] + jnp.sum(member_f, axis=1, keepdims=True)
    cnt_ref[...] = run_ref[...]


def _post(x2d, ret, fox, ln0_g, ln0_b, w_out, ln1_g, ln1_b, w_rt, rbias, w_gu, w_sd):
    n = x2d.shape[0]
    tm = TM_PROJ
    row_spec = lambda w: pl.BlockSpec((tm, w), lambda i: (i, 0))
    col_spec = pl.BlockSpec((SUBLANES, tm), lambda i: (0, i))
    const = lambda shape: pl.BlockSpec(shape, lambda i: (0, 0))
    return pl.pallas_call(
        _post_body,
        grid=(n // tm,),
        in_specs=[row_spec(D_MODEL), row_spec(HEAD_W), row_spec(HEAD_W), const((1, D_MODEL)), const((1, D_MODEL)),
                  const(w_out.shape), const((1, D_MODEL)), const((1, D_MODEL)), const(w_rt.shape),
                  const((N_EXPERTS, 1)), const(w_gu.shape), const(w_sd.shape)],
        out_specs=[row_spec(D_MODEL), pl.BlockSpec((tm * ROW_TILES, LANES), lambda i: (i, 0)),
                   col_spec, col_spec, col_spec, const((N_EXPERTS, 1))],
        out_shape=[jax.ShapeDtypeStruct((n, D_MODEL), _F32), jax.ShapeDtypeStruct((n * ROW_TILES, LANES), ROW_DTYPE),
                   jax.ShapeDtypeStruct((SUBLANES, n), jnp.int32), jax.ShapeDtypeStruct((SUBLANES, n), _F32),
                   jax.ShapeDtypeStruct((SUBLANES, n), jnp.int32), jax.ShapeDtypeStruct((N_EXPERTS, 1), _F32)],
        scratch_shapes=[pltpu.VMEM((N_EXPERTS, 1), _F32)],
        compiler_params=_params(1),
        name="post_mixer",
    )(x2d, ret, fox, ln0_g, ln0_b, w_out, ln1_g, ln1_b, w_rt, rbias, w_gu, w_sd)


def _plan_body(sel_ref, rank_ref, cnt_ref, dest_ref, blk_ref, fill_ref, used_ref, last_ref):
    cnt = cnt_ref[...]
    padded = jnp.ceil(cnt * (1.0 / EBLK)) * EBLK
    er = lax.broadcasted_iota(jnp.int32, (N_EXPERTS, N_EXPERTS), 0)
    ec = lax.broadcasted_iota(jnp.int32, (N_EXPERTS, N_EXPERTS), 1)
    padded_row = jnp.sum(jnp.where(er == ec, padded, 0.0), axis=0, keepdims=True)
    pstart = jnp.sum(jnp.where(ec < er, padded_row, 0.0), axis=1, keepdims=True)
    pend = pstart + padded
    sel = sel_ref[...]
    dest = rank_ref[...]
    for e in range(N_EXPERTS):
        dest = dest + jnp.where(sel == e, pstart[e:e + 1, :].astype(jnp.int32), 0)
    dest_ref[...] = dest
    nblk = blk_ref.shape[1]
    first_row = (lax.broadcasted_iota(jnp.int32, (N_EXPERTS, nblk), 1) * EBLK).astype(_F32)
    owner = jnp.minimum(jnp.sum((pend <= first_row).astype(_F32), axis=0, keepdims=True), N_EXPERTS - 1.0)
    blk_ref[...] = owner.astype(jnp.int32)
    mine = lax.broadcasted_iota(jnp.int32, (N_EXPERTS, nblk), 0).astype(_F32) == owner
    live_end = jnp.sum(jnp.where(mine, pstart + cnt, 0.0), axis=0, keepdims=True)
    fill_ref[...] = jnp.clip(live_end - first_row[:1, :], 0.0, float(EBLK)).astype(jnp.int32)
    used_ref[...] = (pend[N_EXPERTS - 1:, :] * (1.0 / EBLK)).astype(jnp.int32)
    last = jnp.where(padded > 0.0, pend * (1.0 / EBLK) - 1.0, -1.0)
    last_ref[...] = jnp.sum(jnp.where(er == ec, last, 0.0), axis=0, keepdims=True).astype(jnp.int32)


def _plan(sel, rank, cnt, nblk_pad):
    n = sel.shape[1]
    full = lambda shape: pl.BlockSpec(shape, lambda i: (0, 0))
    return pl.pallas_call(
        _plan_body,
        grid=(1,),
        in_specs=[full(sel.shape), full(rank.shape), full(cnt.shape)],
        out_specs=[full(sel.shape), full((1, nblk_pad)), full((1, nblk_pad)), full((1, 1)), full((1, N_EXPERTS))],
        out_shape=[jax.ShapeDtypeStruct((SUBLANES, n), jnp.int32), jax.ShapeDtypeStruct((1, nblk_pad), jnp.int32),
                   jax.ShapeDtypeStruct((1, nblk_pad), jnp.int32), jax.ShapeDtypeStruct((1, 1), jnp.int32),
                   jax.ShapeDtypeStruct((1, N_EXPERTS), jnp.int32)],
        compiler_params=_params(1),
        name="plan",
    )(sel, rank, cnt)


def _store_rows(ref, v):
    m = v.shape[0]
    for s in range(ROW_TILES):
        ref[pl.ds(s, m, stride=ROW_TILES), :] = v[:, s * LANES:(s + 1) * LANES].astype(ROW_DTYPE)


def _load_rows(ref, first_row, m):
    return jnp.concatenate([ref[pl.ds(first_row * ROW_TILES + s, m, stride=ROW_TILES), :] for s in range(ROW_TILES)],
                           axis=1)


def _row_copy(src, src_row, dst, dst_row, sem):
    return pltpu.make_async_copy(src.at[pl.ds(pl.multiple_of(src_row * ROW_TILES, ROW_TILES), ROW_TILES), :],
                                 dst.at[pl.ds(pl.multiple_of(dst_row * ROW_TILES, ROW_TILES), ROW_TILES), :], sem)


def _dispatch_body(dest_ref, last_ref, used_ref, t_ref, xs_ref, zero_sc, sem, zsem):
    blk_rows = EBLK * ROW_TILES
    nblk = xs_ref.shape[0] // blk_rows

    @pl.when(pl.program_id(0) == 0)
    def _():
        zero_sc[...] = jnp.zeros_like(zero_sc)

        def blk_copy(b):
            return pltpu.make_async_copy(zero_sc, xs_ref.at[pl.ds(pl.multiple_of(b * blk_rows, blk_rows), blk_rows), :],
                                         zsem)

        def for_padded_blocks(act):
            def per_expert(e, carry):
                @pl.when(last_ref[e] >= 0)
                def _():
                    act(blk_copy(last_ref[e]))

                return carry

            def per_tail(b, carry):
                act(blk_copy(b))
                return carry

            lax.fori_loop(0, N_EXPERTS, per_expert, 0)
            lax.fori_loop(used_ref[0], nblk, per_tail, 0)

        for_padded_blocks(lambda cp: cp.start())
        for_padded_blocks(lambda cp: cp.wait())

    def issue(i, carry):
        for k in range(TOP_K):
            _row_copy(t_ref, i, xs_ref, dest_ref[i * SUBLANES + k], sem).start(priority=k % 2)
        return carry

    lax.fori_loop(0, TT, issue, 0)
    for _ in range(TOP_K):
        pltpu.make_async_copy(t_ref, xs_ref.at[pl.ds(0, TT * ROW_TILES), :], sem).wait()


def _dispatch(dest_flat, last_blk, used, trow, total_rows):
    n = trow.shape[0] // ROW_TILES
    smem = pl.BlockSpec(memory_space=pltpu.SMEM)
    return pl.pallas_call(
        _dispatch_body,
        grid=(n // TT,),
        in_specs=[pl.BlockSpec((TT * SUBLANES,), lambda i: (i,), memory_space=pltpu.SMEM), smem, smem,
                  pl.BlockSpec((TT * ROW_TILES, LANES), lambda i: (i, 0))],
        out_specs=pl.BlockSpec(memory_space=pl.ANY),
        out_shape=jax.ShapeDtypeStruct((total_rows * ROW_TILES, LANES), ROW_DTYPE),
        scratch_shapes=[pltpu.VMEM((EBLK * ROW_TILES, LANES), ROW_DTYPE), pltpu.SemaphoreType.DMA(()),
                        pltpu.SemaphoreType.DMA(())],
        compiler_params=_params(1),
        name="dispatch",
    )(dest_flat, last_blk, used, trow)


def _expert_body(blk_ref, used_ref, fill_ref, xs_ref, wg_ref, wu_ref, wd_ref, y_ref, wgu_sc, wd_sc):
    i = pl.program_id(0)
    prev = blk_ref[jnp.maximum(i - 1, 0)]
    fresh = (i == 0) | (blk_ref[i] != prev)

    @pl.when(fresh)
    def _():
        wgu_sc[:, :EXPERT_FF] = wg_ref[...].astype(_BF16)
        wgu_sc[:, EXPERT_FF:] = wu_ref[...].astype(_BF16)
        wd_sc[...] = wd_ref[...].astype(_BF16)

    @pl.when(i < used_ref[0])
    def _():
        live = lax.broadcasted_iota(jnp.int32, (EBLK, 1), 0) < fill_ref[i]
        x = _load_rows(xs_ref, 0, EBLK)
        x = jnp.where(live, x, jnp.zeros_like(x)).astype(_BF16)
        gu = _dot(x, wgu_sc[...])
        mid = (_silu(gu[:, :EXPERT_FF]) * gu[:, EXPERT_FF:]).astype(_BF16)
        _store_rows(y_ref, _dot(mid, wd_sc[...]))

    @pl.when(i >= used_ref[0])
    def _():
        y_ref[...] = jnp.zeros_like(y_ref)


def _experts(blk_e, used, fill, xs, we_gate, we_up, we_down):
    nblk = xs.shape[0] // (EBLK * ROW_TILES)
    last = lambda i, used: jnp.minimum(i, jnp.maximum(used[0] - 1, 0))
    w_spec = lambda shape: pl.BlockSpec((None,) + shape, lambda i, blk, used, fill: (blk[i], 0, 0))
    return pl.pallas_call(
        _expert_body,
        grid_spec=pltpu.PrefetchScalarGridSpec(
            num_scalar_prefetch=3,
            grid=(nblk,),
            in_specs=[pl.BlockSpec((EBLK * ROW_TILES, LANES), lambda i, blk, used, fill: (last(i, used), 0)),
                      w_spec((D_MODEL, EXPERT_FF)), w_spec((D_MODEL, EXPERT_FF)), w_spec((EXPERT_FF, D_MODEL))],
            out_specs=pl.BlockSpec((EBLK * ROW_TILES, LANES), lambda i, blk, used, fill: (i, 0)),
            scratch_shapes=[pltpu.VMEM((D_MODEL, 2 * EXPERT_FF), _BF16), pltpu.VMEM((EXPERT_FF, D_MODEL), _BF16)],
        ),
        out_shape=jax.ShapeDtypeStruct(xs.shape, ROW_DTYPE),
        compiler_params=_params(1),
        name="experts",
    )(blk_e, used, fill, xs, we_gate, we_up, we_down)


def _combine_body(dest_ref, dnext_ref, y_ref, base_ref, gate_ref, g2_ref, b2_ref, o_ref, z_sc, sems):
    i = pl.program_id(0)
    slot = lax.rem(i, 2)
    tile_rows = TT * TOP_K * ROW_TILES

    def gather(dref, into):
        def issue(t, carry):
            for k in range(TOP_K):
                _row_copy(y_ref, dref[t * SUBLANES + k], z_sc.at[into], k * TT + t, sems.at[into]).start(priority=k % 2)
            return carry

        lax.fori_loop(0, TT, issue, 0)

    @pl.when(i == 0)
    def _():
        gather(dest_ref, 0)

    @pl.when(i + 1 < pl.num_programs(0))
    def _():
        gather(dnext_ref, 1 - slot)

    pltpu.make_async_copy(y_ref.at[pl.ds(0, tile_rows), :], z_sc.at[slot], sems.at[slot]).wait()
    gates = gate_ref[...]
    acc = base_ref[...]
    for k in range(TOP_K):
        acc = acc + gates[:, k:k + 1] * _load_rows(z_sc.at[slot], k * TT, TT).astype(_F32)
    o_ref[...] = _ln(acc, g2_ref[...], b2_ref[...])


def _combine(dest_flat, y, base, gates_t, ln2_g, ln2_b):
    n = base.shape[0]
    steps = n // TT
    const = lambda shape: pl.BlockSpec(shape, lambda i: (0, 0))
    dest_spec = lambda ahead: pl.BlockSpec((TT * SUBLANES,), lambda i: (jnp.minimum(i + ahead, steps - 1),),
                                           memory_space=pltpu.SMEM)
    return pl.pallas_call(
        _combine_body,
        grid=(steps,),
        in_specs=[dest_spec(0), dest_spec(1),
                  pl.BlockSpec(memory_space=pl.ANY),
                  pl.BlockSpec((TT, D_MODEL), lambda i: (i, 0)),
                  pl.BlockSpec((TT, SUBLANES), lambda i: (i, 0)),
                  const((1, D_MODEL)), const((1, D_MODEL))],
        out_specs=pl.BlockSpec((TT, D_MODEL), lambda i: (i, 0)),
        out_shape=jax.ShapeDtypeStruct((n, D_MODEL), _F32),
        scratch_shapes=[pltpu.VMEM((2, TT * TOP_K * ROW_TILES, LANES), ROW_DTYPE), pltpu.SemaphoreType.DMA((2,))],
        compiler_params=_params(1),
        name="combine",
    )(dest_flat, dest_flat, y, base, gates_t, ln2_g, ln2_b)


def _rope_tables(pos):
    half = RET_DK // 2
    inv = ROPE_BASE ** (-jnp.arange(half, dtype=_F32) / half)
    ang = pos[:, None] * inv[None, :]
    cos = jnp.cos(ang)
    sin = jnp.sin(ang)
    return jnp.concatenate([cos, cos], -1), jnp.concatenate([-sin, sin], -1)


def _decay_tables():
    lg = jnp.log1p(-jnp.exp2(-5.0 - jnp.arange(RET_HEADS, dtype=_F32)))
    idx = jnp.arange(BLOCK, dtype=_F32)
    rel = idx[:, None] - idx[None, :]
    causal = rel >= 0
    dmask = jnp.where(causal[None], jnp.exp(jnp.where(causal, rel, 0.0)[None] * lg[:, None, None]), 0.0)
    zeta = jnp.exp((BLOCK - 1.0 - idx)[None, :] * lg[:, None])
    xi = jnp.exp((idx + 1.0)[None, :] * lg[:, None])
    return dmask, xi[:, :, None], zeta[:, :, None], jnp.exp(BLOCK * lg)


def kernel(x, meta, ln0_g, ln0_b, w_in, b_forget, w_out, ln1_g, ln1_b, w_router, router_bias, we_gate, we_up,
           we_down, ws_gate, ws_up, ws_down, ln2_g, ln2_b):
    nb, s, d = x.shape
    assert d == D_MODEL and meta.shape == (N_META, D_MODEL) and w_in.shape[0] == 1
    assert s % TM_PROJ == 0 and s % T_FOX == 0 and (nb * s) % TT == 0
    n = nb * s
    x2d = x.reshape(n, d)
    row2 = lambda v: v.reshape(1, -1).astype(_F32)
    main_cols = 7 * HEAD_W
    w_all = jnp.concatenate(
        [w_in[0, :, :main_cols], w_in[0, :, main_cols:], jnp.zeros((d, LANES - FOX_HEADS), w_in.dtype)],
        axis=1).astype(_BF16)
    bf_pad = jnp.concatenate([b_forget[0].astype(_F32), jnp.zeros((LANES - FOX_HEADS,), _F32)]).reshape(1, LANES)
    g0, b0 = row2(ln0_g), row2(ln0_b)

    cos_x, sin_x = _rope_tables(jnp.arange(s, dtype=_F32) + float(N_META))
    cos_m, sin_m = _rope_tables(jnp.arange(BLOCK, dtype=_F32) - float(PAD))
    meta_blk = jnp.concatenate([jnp.zeros((PAD, d), _F32), meta.astype(_F32)], axis=0)

    rq, rk, rv, rg, fq, fk, fv = _inproj(x2d, g0, b0, w_all, bf_pad, cos_x, sin_x, nb=nb, meta=False)
    _, rk_m, rv_m, _, _, fk_m, fv_m = _inproj(meta_blk, g0, b0, w_all, bf_pad, cos_m, sin_m, nb=1, meta=True)

    dmask, xi, zeta, gch = _decay_tables()
    ret = _retention(rq, rk, rv, rg, rk_m, rv_m, dmask, xi, zeta, gch, nb=nb)

    per_batch = lambda a: a.reshape(nb, s, FOX_W)
    fox = _fox(per_batch(fq), per_batch(fk), per_batch(fv), fk_m, fv_m, nb=nb).reshape(n, HEAD_W)

    w_gu = jnp.concatenate([ws_gate[0], ws_up[0]], axis=1).astype(_BF16)
    base, trow, sel, gates, rank, cnt = _post(
        x2d, ret, fox, g0, b0, w_out[0].astype(_BF16), row2(ln1_g[0]), row2(ln1_b[0]),
        jnp.transpose(w_router[0]).astype(_BF16), router_bias[0].astype(_F32).reshape(N_EXPERTS, 1), w_gu,
        ws_down[0].astype(_BF16))

    nblk = n * TOP_K // EBLK + N_EXPERTS
    nblk_pad = -(-nblk // LANES) * LANES
    dest, blk_e, fill, used, last_blk = _plan(sel, rank, cnt, nblk_pad)
    dest_flat = jnp.transpose(dest).reshape(-1)
    used = used.reshape(-1)

    xs = _dispatch(dest_flat, last_blk.reshape(-1), used, trow, nblk * EBLK)
    y = _experts(blk_e.reshape(-1), used, fill.reshape(-1), xs, we_gate[0], we_up[0], we_down[0])
    out = _combine(dest_flat, y, base, jnp.transpose(gates), row2(ln2_g[0]), row2(ln2_b[0]))
    return out.reshape(nb, s, d)
```

```python
import functools

import jax
import jax.numpy as jnp
import numpy as np
from jax import lax
from jax.experimental import pallas as pl
from jax.experimental.pallas import tpu as pltpu

D_MODEL = 1024
N_META = 16
BLOCK = 128
PAD = BLOCK - N_META
RET_HEADS = 4
RET_DK = 128
FOX_HEADS = 8
FOX_HD = 64
N_EXPERTS = 64
TOP_K = 6
N_GROUPS = 8
GROUP_SIZE = N_EXPERTS // N_GROUPS
TOPK_GROUPS = 4
EXPERT_FF = 256
SHARED_FF = 256
ROUTED_SCALE = 2.5
ROPE_BASE = 10000.0
LN_EPS = 1e-5
NEG_INF = -1e30
ALPHA = 2.0 ** 0.25
HEAD_W = 512
LOG2E = 1.4426950408889634

LANES = 128
FOX_W = FOX_HEADS * LANES
SUBLANES = 8
ROW_TILES = D_MODEL // LANES
ROW_DTYPE = jnp.float32

TM_PROJ = 512
T_FOX = 512
FOX_CHUNK = 64
TT = 256
EBLK = 512
ISSUE_UNROLL = 4
VMEM_LIMIT = 48 * 1024 * 1024

_F32 = jnp.float32
_BF16 = jnp.bfloat16


def _ln(x, g, b):
    xc = x - jnp.mean(x, -1, keepdims=True)
    var = jnp.mean(xc * xc, -1, keepdims=True)
    return xc * lax.rsqrt(var + LN_EPS) * g + b


def _dot(a, b):
    return jnp.dot(a, b, preferred_element_type=_F32)


def _dot_nt(a, b):
    return lax.dot_general(a, b, (((1,), (1,)), ((), ())), preferred_element_type=_F32)


def _dot_tn(a, b):
    return lax.dot_general(a, b, (((0,), (0,)), ((), ())), preferred_element_type=_F32)


def _silu(x):
    return x * jax.nn.sigmoid(x)


def _params(n_axes):
    return pltpu.CompilerParams(dimension_semantics=("arbitrary",) * n_axes, vmem_limit_bytes=VMEM_LIMIT)


def _inproj_body(x_ref, g_ref, b_ref, w_ref, bf_ref, cos_ref, sin_ref, own_ref, pq_ref, pk_ref, oq_ref, ok_ref, ov_ref,
                 rq_ref, rk_ref, rv_ref, rg_ref, fq_ref, fk_ref, fv_ref, carry_ref, *, meta):
    tm = x_ref.shape[0]
    h = _ln(x_ref[...], g_ref[...], b_ref[...])
    if meta:
        valid = lax.broadcasted_iota(jnp.int32, (tm, 1), 0) >= PAD
        h = jnp.where(valid, h, 0.0)
    hb = h.astype(_BF16)
    cos = cos_ref[...]
    sin = sin_ref[...]

    def proj(g):
        return _dot(hb, w_ref[:, g * HEAD_W:(g + 1) * HEAD_W])

    def rope_store(p, out_ref, scale):
        for hd in range(RET_HEADS):
            t = p[:, hd * LANES:(hd + 1) * LANES]
            r = t * cos + pltpu.roll(t, LANES // 2, axis=1) * sin
            out_ref[:, hd * LANES:(hd + 1) * LANES] = (r * scale).astype(_BF16)

    rope_store(proj(0), rq_ref, 1.0)
    rope_store(proj(1), rk_ref, RET_DK ** -0.5)
    rv_ref[...] = proj(2).astype(_BF16)
    rg_ref[...] = _silu(proj(3)).astype(_BF16)

    z = _dot(hb, w_ref[:, 7 * HEAD_W:7 * HEAD_W + LANES]) + bf_ref[...]
    logf = jnp.minimum(z, 0.0) - jnp.log1p(jnp.exp(-jnp.abs(z)))
    if meta:
        logf = jnp.where(valid, logf, 0.0)
    l1 = logf.astype(_BF16)
    r1 = logf - l1.astype(_F32)
    l2 = r1.astype(_BF16)
    l3 = (r1 - l2.astype(_F32)).astype(_BF16)
    row = lax.broadcasted_iota(jnp.int32, (tm, tm), 0)
    col = lax.broadcasted_iota(jnp.int32, (tm, tm), 1)
    tri = (col <= row).astype(_BF16)
    c = _dot(tri, l1) + _dot(tri, l2) + _dot(tri, l3)
    if meta:
        c = c - c[tm - 1:tm, :]
    else:
        @pl.when(pl.program_id(1) == 0)
        def _():
            carry_ref[...] = jnp.zeros_like(carry_ref)

        c = c + carry_ref[...]
        carry_ref[...] = c[tm - 1:tm, :]

    head_lane = lax.broadcasted_iota(jnp.int32, (1, LANES), 1) < FOX_HEADS
    cl = jnp.where(head_lane, c, 0.0) * LOG2E
    c1 = cl.astype(_BF16).astype(_F32)
    r1 = cl - c1
    c2 = r1.astype(_BF16).astype(_F32)
    c3 = (r1 - c2).astype(_BF16).astype(_F32)
    csplit = (c1 + pltpu.roll(c2, FOX_HEADS, axis=1) + pltpu.roll(c3, 2 * FOX_HEADS, axis=1)).astype(_BF16)
    own = own_ref[...] > 0.0

    def per_head(p):
        return jnp.concatenate([p[:, (hd // 2) * LANES:(hd // 2 + 1) * LANES] for hd in range(FOX_HEADS)], axis=1)

    fq_ref[...] = jnp.where(own, per_head(proj(4) * (FOX_HD ** -0.5 * LOG2E)),
                            _dot(csplit, pq_ref[...]) + oq_ref[...]).astype(_BF16)
    fk_ref[...] = jnp.where(own, per_head(proj(5)), _dot(csplit, pk_ref[...]) + ok_ref[...]).astype(_BF16)
    fv_ref[...] = jnp.where(own, per_head(proj(6)), ov_ref[...]).astype(_BF16)


def _fox_lane_tables():
    own = np.zeros((1, FOX_W), np.float32)
    pq = np.zeros((LANES, FOX_W), np.float32)
    pk = np.zeros((LANES, FOX_W), np.float32)
    oq = np.zeros((1, FOX_W), np.float32)
    ok = np.zeros((1, FOX_W), np.float32)
    ov = np.zeros((1, FOX_W), np.float32)
    for hd in range(FOX_HEADS):
        data = hd * LANES + (hd % 2) * FOX_HD
        extra = hd * LANES + (1 - hd % 2) * FOX_HD
        own[0, data:data + FOX_HD] = 1.0
        for term in range(3):
            pq[term * FOX_HEADS + hd, extra + term] = 1.0
            ok[0, extra + term] = 1.0
            oq[0, extra + 3 + term] = 1.0
            pk[term * FOX_HEADS + hd, extra + 3 + term] = -1.0
        ov[0, extra] = 1.0
    return (jnp.asarray(own), jnp.asarray(pq, _BF16), jnp.asarray(pk, _BF16), jnp.asarray(oq), jnp.asarray(ok),
            jnp.asarray(ov))


def _inproj(x2d, ln_g, ln_b, w_all, bf_pad, cos_t, sin_t, *, nb, meta):
    n = x2d.shape[0]
    s = n // nb
    tm = min(TM_PROJ, s)
    nj = s // tm
    row_spec = lambda w: pl.BlockSpec((tm, w), lambda b, j: (b * nj + j, 0))
    const = lambda shape: pl.BlockSpec(shape, lambda b, j: (0, 0))
    pos_spec = pl.BlockSpec((tm, LANES), lambda b, j: (j, 0))
    tables = _fox_lane_tables()
    outs = [jax.ShapeDtypeStruct((n, HEAD_W), _BF16)] * 4 + [jax.ShapeDtypeStruct((n, FOX_W), _BF16)] * 3
    return pl.pallas_call(
        functools.partial(_inproj_body, meta=meta),
        grid=(nb, nj),
        in_specs=[row_spec(D_MODEL), const((1, D_MODEL)), const((1, D_MODEL)), const(w_all.shape),
                  const((1, LANES)), pos_spec, pos_spec] + [const(t.shape) for t in tables],
        out_specs=[row_spec(HEAD_W)] * 4 + [row_spec(FOX_W)] * 3,
        out_shape=outs,
        scratch_shapes=[pltpu.VMEM((1, LANES), _F32)],
        compiler_params=_params(2),
        name="inproj_meta" if meta else "inproj",
    )(x2d, ln_g, ln_b, w_all, bf_pad, cos_t, sin_t, *tables)


def _ret_body(q_ref, k_ref, v_ref, g_ref, km_ref, vm_ref, dm_ref, xi_ref, zeta_ref, gch_ref, o_ref, st_ref):
    def kv_update(k, v, hd):
        vz = (v.astype(_F32) * zeta_ref[hd]).astype(_BF16)
        return _dot_tn(k, vz)

    nb = q_ref.shape[0]

    @pl.when(pl.program_id(0) == 0)
    def _():
        for hd in range(RET_HEADS):
            sl = slice(hd * LANES, (hd + 1) * LANES)
            first = kv_update(km_ref[:, sl], vm_ref[:, sl], hd)
            for b in range(nb):
                st_ref[b, hd] = first

    for b in range(nb):
        for hd in range(RET_HEADS):
            sl = slice(hd * LANES, (hd + 1) * LANES)
            q = q_ref[b, :, sl]
            k = k_ref[b, :, sl]
            v = v_ref[b, :, sl]
            st = st_ref[b, hd]
            scores = _dot_nt(q, k) * dm_ref[hd]
            o = _dot(scores.astype(_BF16), v) + _dot(q, st.astype(_BF16)) * xi_ref[hd]
            oc = o - jnp.mean(o, -1, keepdims=True)
            y = oc * lax.rsqrt(jnp.mean(oc * oc, -1, keepdims=True) + LN_EPS)
            o_ref[b, :, sl] = (y * g_ref[b, :, sl].astype(_F32)).astype(_BF16)
            st_ref[b, hd] = gch_ref[hd] * st + kv_update(k, v, hd)


def _retention(rq, rk, rv, rg, rk_m, rv_m, dmask, xi, zeta, gch):
    nb, s, _ = rq.shape
    row_spec = pl.BlockSpec((nb, BLOCK, HEAD_W), lambda j: (0, j, 0))
    meta_spec = pl.BlockSpec((BLOCK, HEAD_W), lambda j: (0, 0))
    tab = pl.BlockSpec((RET_HEADS, BLOCK, BLOCK), lambda j: (0, 0, 0))
    return pl.pallas_call(
        _ret_body,
        grid=(s // BLOCK,),
        in_specs=[row_spec] * 4 + [meta_spec] * 2 + [tab] * 3 + [pl.BlockSpec(memory_space=pltpu.SMEM)],
        out_specs=row_spec,
        out_shape=jax.ShapeDtypeStruct((nb, s, HEAD_W), _BF16),
        scratch_shapes=[pltpu.VMEM((nb, RET_HEADS, RET_DK, LANES), _F32)],
        compiler_params=_params(1),
        name="retention",
    )(rq, rk, rv, rg, rk_m, rv_m, dmask, xi, zeta, gch)


def _fox_body(q_ref, k_ref, v_ref, km_ref, vm_ref, o_ref, m_sc, acc_sc, s_sc, p_sc):
    t = q_ref.shape[0]
    qi = pl.program_id(2)

    heads = [slice(hh * LANES, (hh + 1) * LANES) for hh in range(2)]

    def logits(hh, k, buf, tk):
        s_sc[buf, hh, :, :tk] = _dot_nt(q_ref[:, heads[hh]], k)

    def update(hh, v, buf, tk, mask, first):
        for c in range(t // FOX_CHUNK):
            rows = slice(c * FOX_CHUNK, (c + 1) * FOX_CHUNK)
            s = s_sc[buf, hh, rows, :tk]
            if mask is not None:
                s = mask(s, c)
            mx = jnp.max(s, axis=1, keepdims=True)
            if first:
                m_new = mx
            else:
                m_prev = m_sc[hh, rows, :]
                m_new = jnp.maximum(m_prev, mx)
                acc_sc[hh, rows, :] = jnp.exp2(m_prev - m_new) * acc_sc[hh, rows, :]
            p_sc[hh, rows, :tk] = jnp.exp2(s - m_new).astype(_BF16)
            m_sc[hh, rows, :] = m_new
        pv = _dot(p_sc[hh, :, :tk], v)
        if first:
            acc_sc[hh] = pv
        else:
            acc_sc[hh] += pv

    def meta_mask(s, c):
        key = lax.broadcasted_iota(jnp.int32, s.shape, 1)
        return jnp.where(key >= PAD, s, NEG_INF)

    def causal_mask(s, c):
        key = lax.broadcasted_iota(jnp.int32, s.shape, 1)
        query = lax.broadcasted_iota(jnp.int32, s.shape, 0) + c * FOX_CHUNK
        return jnp.where(key <= query, s, NEG_INF)

    def key_tile(ref, ki, hh):
        return ref[pl.ds(pl.multiple_of(ki * t, t), t), heads[hh]]

    for hh in range(2):
        logits(hh, km_ref[:, heads[hh]], 1, BLOCK)
    for hh in range(2):
        logits(hh, key_tile(k_ref, 0, hh), 0, t)
        update(hh, vm_ref[:, heads[hh]], 1, BLOCK, meta_mask, True)

    def step(ki, buf, mask, more):
        for hh in range(2):
            if more:
                logits(hh, key_tile(k_ref, ki + 1, hh), 1 - buf, t)
            update(hh, key_tile(v_ref, ki, hh), buf, t, mask, False)

    def pair_body(j, carry):
        step(2 * j, 0, None, True)
        step(2 * j + 1, 1, None, True)
        return carry

    lax.fori_loop(0, lax.shift_right_logical(qi, 1), pair_body, 0)
    odd = lax.rem(qi, 2) == 1

    @pl.when(odd)
    def _():
        step(qi - 1, 0, None, True)
        step(qi, 1, causal_mask, False)

    @pl.when(jnp.logical_not(odd))
    def _():
        step(qi, 0, causal_mask, False)

    outs = []
    for hh in range(2):
        acc = acc_sc[hh]
        ones_lane = (1 - hh) * FOX_HD
        outs.append(acc / acc[:, ones_lane:ones_lane + 1])
    lane = lax.broadcasted_iota(jnp.int32, (t, LANES), 1)
    o_ref[...] = jnp.where(lane < FOX_HD, outs[0], outs[1]).astype(_BF16)


def _fox(fq, fk, fv, fk_m, fv_m, *, nb):
    s = fq.shape[1]
    t = T_FOX
    pair_w = 2 * LANES
    return pl.pallas_call(
        _fox_body,
        grid=(nb, FOX_HEADS // 2, s // t),
        in_specs=[
            pl.BlockSpec((None, t, pair_w), lambda b, p, i: (b, i, p)),
            pl.BlockSpec((None, s, pair_w), lambda b, p, i: (b, 0, p)),
            pl.BlockSpec((None, s, pair_w), lambda b, p, i: (b, 0, p)),
            pl.BlockSpec((BLOCK, pair_w), lambda b, p, i: (0, p)),
            pl.BlockSpec((BLOCK, pair_w), lambda b, p, i: (0, p)),
        ],
        out_specs=pl.BlockSpec((None, t, LANES), lambda b, p, i: (b, i, p)),
        out_shape=jax.ShapeDtypeStruct((nb, s, HEAD_W), _BF16),
        scratch_shapes=[pltpu.VMEM((2, t, 1), _F32), pltpu.VMEM((2, t, LANES), _F32), pltpu.VMEM((2, 2, t, t), _F32),
                        pltpu.VMEM((2, t, t), _BF16)],
        compiler_params=_params(3),
        name="fox",
    )(fq, fk, fv, fk_m, fv_m)


def _post_body(x_ref, ret_ref, fox_ref, g0_ref, b0_ref, wo_ref, g1_ref, b1_ref, wr_ref, rb_ref, wgu_ref, wd_ref,
               base_ref, trow_ref, sel_ref, gate_ref, rank_ref, cnt_ref, run_ref):
    tm = x_ref.shape[0]
    i = pl.program_id(0)

    @pl.when(i == 0)
    def _():
        run_ref[...] = jnp.zeros_like(run_ref)

    h0 = _ln(x_ref[...], g0_ref[...], b0_ref[...])
    y = _dot(ret_ref[...], wo_ref[:HEAD_W, :]) + _dot(fox_ref[...], wo_ref[HEAD_W:, :])
    h1 = _ln(ALPHA * h0 + y, g1_ref[...], b1_ref[...])
    tb = h1.astype(_BF16)
    _store_rows(trow_ref, h1)

    gu = _dot(tb, wgu_ref[...])
    mid = (_silu(gu[:, :SHARED_FF]) * gu[:, SHARED_FF:]).astype(_BF16)
    base_ref[...] = ALPHA * h1 + _dot(mid, wd_ref[...])

    scores = jax.nn.sigmoid(_dot_nt(wr_ref[...], tb))
    biased = scores + rb_ref[...]
    sub = lax.broadcasted_iota(jnp.int32, (GROUP_SIZE, tm), 0).astype(_F32)
    gscore = []
    for g in range(N_GROUPS):
        v = biased[g * GROUP_SIZE:(g + 1) * GROUP_SIZE, :]
        m1 = jnp.max(v, axis=0, keepdims=True)
        i1 = jnp.min(jnp.where(v == m1, sub, float(GROUP_SIZE)), axis=0, keepdims=True)
        m2 = jnp.max(jnp.where(sub == i1, -jnp.inf, v), axis=0, keepdims=True)
        gscore.append(m1 + m2)
    masked = []
    for g in range(N_GROUPS):
        beaten = jnp.zeros((1, tm), _F32)
        for o in range(N_GROUPS):
            if o == g:
                continue
            wins = (gscore[o] >= gscore[g]) if o < g else (gscore[o] > gscore[g])
            beaten = beaten + wins.astype(_F32)
        keep = beaten < float(TOPK_GROUPS)
        masked.append(jnp.where(keep, biased[g * GROUP_SIZE:(g + 1) * GROUP_SIZE, :], NEG_INF))
    work = jnp.concatenate(masked, axis=0)
    eid = lax.broadcasted_iota(jnp.int32, (N_EXPERTS, tm), 0).astype(_F32)
    hots, sels, raws = [], [], []
    for _ in range(TOP_K):
        m = jnp.max(work, axis=0, keepdims=True)
        idx = jnp.min(jnp.where(work == m, eid, float(N_EXPERTS)), axis=0, keepdims=True)
        hot = eid == idx
        hots.append(hot)
        sels.append(idx.astype(jnp.int32))
        raws.append(jnp.sum(jnp.where(hot, scores, 0.0), axis=0, keepdims=True))
        work = jnp.where(hot, -jnp.inf, work)
    total = raws[0]
    for r in raws[1:]:
        total = total + r
    inv = ROUTED_SCALE / total
    member = hots[0]
    for hot in hots[1:]:
        member = member | hot
    member_f = member.astype(_F32)
    row = lax.broadcasted_iota(jnp.int32, (tm, tm), 0)
    col = lax.broadcasted_iota(jnp.int32, (tm, tm), 1)
    before = (row < col).astype(_BF16)
    rank_e = _dot(member_f.astype(_BF16), before) + run_ref[...]
    pad_rows = sel_ref.shape[0] - TOP_K
    sel_ref[...] = jnp.concatenate(sels + [jnp.zeros((pad_rows, tm), jnp.int32)], axis=0)
    gate_ref[...] = jnp.concatenate([r * inv for r in raws] + [jnp.zeros((pad_rows, tm), _F32)], axis=0)
    ranks = [jnp.sum(jnp.where(hot, rank_e, 0.0), axis=0, keepdims=True).astype(jnp.int32) for hot in hots]
    rank_ref[...] = jnp.concatenate(ranks + [jnp.zeros((pad_rows, tm), jnp.int32)], axis=0)
    run_ref[...] = run_ref[...] + jnp.sum(member_f, axis=1, keepdims=True)
    cnt_ref[...] = run_ref[...]


def _post(x2d, ret, fox, ln0_g, ln0_b, w_out, ln1_g, ln1_b, w_rt, rbias, w_gu, w_sd):
    n = x2d.shape[0]
    tm = TM_PROJ
    row_spec = lambda w: pl.BlockSpec((tm, w), lambda i: (i, 0))
    col_spec = pl.BlockSpec((SUBLANES, tm), lambda i: (0, i))
    const = lambda shape: pl.BlockSpec(shape, lambda i: (0, 0))
    return pl.pallas_call(
        _post_body,
        grid=(n // tm,),
        in_specs=[row_spec(D_MODEL), row_spec(HEAD_W), row_spec(HEAD_W), const((1, D_MODEL)), const((1, D_MODEL)),
                  const(w_out.shape), const((1, D_MODEL)), const((1, D_MODEL)), const(w_rt.shape),
                  const((N_EXPERTS, 1)), const(w_gu.shape), const(w_sd.shape)],
        out_specs=[row_spec(D_MODEL), pl.BlockSpec((tm * ROW_TILES, LANES), lambda i: (i, 0)),
                   col_spec, col_spec, col_spec, const((N_EXPERTS, 1))],
        out_shape=[jax.ShapeDtypeStruct((n, D_MODEL), _F32), jax.ShapeDtypeStruct((n * ROW_TILES, LANES), ROW_DTYPE),
                   jax.ShapeDtypeStruct((SUBLANES, n), jnp.int32), jax.ShapeDtypeStruct((SUBLANES, n), _F32),
                   jax.ShapeDtypeStruct((SUBLANES, n), jnp.int32), jax.ShapeDtypeStruct((N_EXPERTS, 1), _F32)],
        scratch_shapes=[pltpu.VMEM((N_EXPERTS, 1), _F32)],
        compiler_params=_params(1),
        name="post_mixer",
    )(x2d, ret, fox, ln0_g, ln0_b, w_out, ln1_g, ln1_b, w_rt, rbias, w_gu, w_sd)


def _plan_body(sel_ref, rank_ref, cnt_ref, dest_ref, blk_ref, fill_ref, used_ref, last_ref):
    cnt = cnt_ref[...]
    padded = jnp.ceil(cnt * (1.0 / EBLK)) * EBLK
    er = lax.broadcasted_iota(jnp.int32, (N_EXPERTS, N_EXPERTS), 0)
    ec = lax.broadcasted_iota(jnp.int32, (N_EXPERTS, N_EXPERTS), 1)
    padded_row = jnp.sum(jnp.where(er == ec, padded, 0.0), axis=0, keepdims=True)
    pstart = jnp.sum(jnp.where(ec < er, padded_row, 0.0), axis=1, keepdims=True)
    pend = pstart + padded
    sel = sel_ref[...]
    dest = rank_ref[...]
    for e in range(N_EXPERTS):
        dest = dest + jnp.where(sel == e, pstart[e:e + 1, :].astype(jnp.int32), 0)
    dest_ref[...] = dest
    nblk = blk_ref.shape[1]
    first_row = (lax.broadcasted_iota(jnp.int32, (N_EXPERTS, nblk), 1) * EBLK).astype(_F32)
    owner = jnp.minimum(jnp.sum((pend <= first_row).astype(_F32), axis=0, keepdims=True), N_EXPERTS - 1.0)
    blk_ref[...] = owner.astype(jnp.int32)
    mine = lax.broadcasted_iota(jnp.int32, (N_EXPERTS, nblk), 0).astype(_F32) == owner
    live_end = jnp.sum(jnp.where(mine, pstart + cnt, 0.0), axis=0, keepdims=True)
    fill_ref[...] = jnp.clip(live_end - first_row[:1, :], 0.0, float(EBLK)).astype(jnp.int32)
    used_ref[...] = (pend[N_EXPERTS - 1:, :] * (1.0 / EBLK)).astype(jnp.int32)
    last = jnp.where(padded > 0.0, pend * (1.0 / EBLK) - 1.0, -1.0)
    last_ref[...] = jnp.sum(jnp.where(er == ec, last, 0.0), axis=0, keepdims=True).astype(jnp.int32)


def _plan(sel, rank, cnt, nblk_pad):
    n = sel.shape[1]
    full = lambda shape: pl.BlockSpec(shape, lambda i: (0, 0))
    return pl.pallas_call(
        _plan_body,
        grid=(1,),
        in_specs=[full(sel.shape), full(rank.shape), full(cnt.shape)],
        out_specs=[full(sel.shape), full((1, nblk_pad)), full((1, nblk_pad)), full((1, 1)), full((1, N_EXPERTS))],
        out_shape=[jax.ShapeDtypeStruct((SUBLANES, n), jnp.int32), jax.ShapeDtypeStruct((1, nblk_pad), jnp.int32),
                   jax.ShapeDtypeStruct((1, nblk_pad), jnp.int32), jax.ShapeDtypeStruct((1, 1), jnp.int32),
                   jax.ShapeDtypeStruct((1, N_EXPERTS), jnp.int32)],
        compiler_params=_params(1),
        name="plan",
    )(sel, rank, cnt)


def _store_rows(ref, v):
    m = v.shape[0]
    for s in range(ROW_TILES):
        ref[pl.ds(s, m, stride=ROW_TILES), :] = v[:, s * LANES:(s + 1) * LANES].astype(ROW_DTYPE)


def _load_rows(ref, first_row, m):
    return jnp.concatenate([ref[pl.ds(first_row * ROW_TILES + s, m, stride=ROW_TILES), :] for s in range(ROW_TILES)],
                           axis=1)


def _row_copy(src, src_row, dst, dst_row, sem):
    return pltpu.make_async_copy(src.at[pl.ds(pl.multiple_of(src_row * ROW_TILES, ROW_TILES), ROW_TILES), :],
                                 dst.at[pl.ds(pl.multiple_of(dst_row * ROW_TILES, ROW_TILES), ROW_TILES), :], sem)


def _dispatch_body(dest_ref, last_ref, used_ref, t_ref, xs_ref, zero_sc, sem, zsem):
    blk_rows = EBLK * ROW_TILES
    nblk = xs_ref.shape[0] // blk_rows

    @pl.when(pl.program_id(0) == 0)
    def _():
        zero_sc[...] = jnp.zeros_like(zero_sc)

        def blk_copy(b):
            return pltpu.make_async_copy(zero_sc, xs_ref.at[pl.ds(pl.multiple_of(b * blk_rows, blk_rows), blk_rows), :],
                                         zsem)

        def for_padded_blocks(act):
            def per_expert(e, carry):
                @pl.when(last_ref[e] >= 0)
                def _():
                    act(blk_copy(last_ref[e]))

                return carry

            def per_tail(b, carry):
                act(blk_copy(b))
                return carry

            lax.fori_loop(0, N_EXPERTS, per_expert, 0)
            lax.fori_loop(used_ref[0], nblk, per_tail, 0)

        for_padded_blocks(lambda cp: cp.start())
        for_padded_blocks(lambda cp: cp.wait())

    def issue(j, carry):
        for u in range(ISSUE_UNROLL):
            i = ISSUE_UNROLL * j + u
            for k in range(TOP_K):
                _row_copy(t_ref, i, xs_ref, dest_ref[i * SUBLANES + k], sem).start(priority=k % 2)
        return carry

    lax.fori_loop(0, TT // ISSUE_UNROLL, issue, 0)
    for _ in range(TOP_K):
        pltpu.make_async_copy(t_ref, xs_ref.at[pl.ds(0, TT * ROW_TILES), :], sem).wait()


def _dispatch(dest_flat, last_blk, used, trow, total_rows):
    n = trow.shape[0] // ROW_TILES
    smem = pl.BlockSpec(memory_space=pltpu.SMEM)
    return pl.pallas_call(
        _dispatch_body,
        grid=(n // TT,),
        in_specs=[pl.BlockSpec((TT * SUBLANES,), lambda i: (i,), memory_space=pltpu.SMEM), smem, smem,
                  pl.BlockSpec((TT * ROW_TILES, LANES), lambda i: (i, 0))],
        out_specs=pl.BlockSpec(memory_space=pl.ANY),
        out_shape=jax.ShapeDtypeStruct((total_rows * ROW_TILES, LANES), ROW_DTYPE),
        scratch_shapes=[pltpu.VMEM((EBLK * ROW_TILES, LANES), ROW_DTYPE), pltpu.SemaphoreType.DMA(()),
                        pltpu.SemaphoreType.DMA(())],
        compiler_params=_params(1),
        name="dispatch",
    )(dest_flat, last_blk, used, trow)


def _expert_body(blk_ref, used_ref, fill_ref, xs_ref, wg_ref, wu_ref, wd_ref, y_ref, wgu_sc, wd_sc):
    i = pl.program_id(0)
    prev = blk_ref[jnp.maximum(i - 1, 0)]
    fresh = (i == 0) | (blk_ref[i] != prev)

    @pl.when(fresh)
    def _():
        wgu_sc[:, :EXPERT_FF] = wg_ref[...].astype(_BF16)
        wgu_sc[:, EXPERT_FF:] = wu_ref[...].astype(_BF16)
        wd_sc[...] = wd_ref[...].astype(_BF16)

    @pl.when(i < used_ref[0])
    def _():
        live = lax.broadcasted_iota(jnp.int32, (EBLK, 1), 0) < fill_ref[i]
        x = _load_rows(xs_ref, 0, EBLK)
        x = jnp.where(live, x, jnp.zeros_like(x)).astype(_BF16)
        gu = _dot(x, wgu_sc[...])
        mid = (_silu(gu[:, :EXPERT_FF]) * gu[:, EXPERT_FF:]).astype(_BF16)
        _store_rows(y_ref, _dot(mid, wd_sc[...]))

    @pl.when(i >= used_ref[0])
    def _():
        y_ref[...] = jnp.zeros_like(y_ref)


def _experts(blk_e, used, fill, xs, we_gate, we_up, we_down):
    nblk = xs.shape[0] // (EBLK * ROW_TILES)
    last = lambda i, used: jnp.minimum(i, jnp.maximum(used[0] - 1, 0))
    w_spec = lambda shape: pl.BlockSpec((None,) + shape, lambda i, blk, used, fill: (blk[i], 0, 0))
    return pl.pallas_call(
        _expert_body,
        grid_spec=pltpu.PrefetchScalarGridSpec(
            num_scalar_prefetch=3,
            grid=(nblk,),
            in_specs=[pl.BlockSpec((EBLK * ROW_TILES, LANES), lambda i, blk, used, fill: (last(i, used), 0)),
                      w_spec((D_MODEL, EXPERT_FF)), w_spec((D_MODEL, EXPERT_FF)), w_spec((EXPERT_FF, D_MODEL))],
            out_specs=pl.BlockSpec((EBLK * ROW_TILES, LANES), lambda i, blk, used, fill: (i, 0)),
            scratch_shapes=[pltpu.VMEM((D_MODEL, 2 * EXPERT_FF), _BF16), pltpu.VMEM((EXPERT_FF, D_MODEL), _BF16)],
        ),
        out_shape=jax.ShapeDtypeStruct(xs.shape, ROW_DTYPE),
        compiler_params=_params(1),
        name="experts",
    )(blk_e, used, fill, xs, we_gate, we_up, we_down)


def _combine_body(dest_ref, dnext_ref, y_ref, base_ref, gate_ref, g2_ref, b2_ref, o_ref, z_sc, sems):
    i = pl.program_id(0)
    slot = lax.rem(i, 2)
    tile_rows = TT * TOP_K * ROW_TILES

    def gather(dref, into):
        def issue(j, carry):
            for u in range(ISSUE_UNROLL):
                t = ISSUE_UNROLL * j + u
                for k in range(TOP_K):
                    _row_copy(y_ref, dref[t * SUBLANES + k], z_sc.at[into], k * TT + t,
                              sems.at[into]).start(priority=k % 2)
            return carry

        lax.fori_loop(0, TT // ISSUE_UNROLL, issue, 0)

    @pl.when(i == 0)
    def _():
        gather(dest_ref, 0)

    @pl.when(i + 1 < pl.num_programs(0))
    def _():
        gather(dnext_ref, 1 - slot)

    pltpu.make_async_copy(y_ref.at[pl.ds(0, tile_rows), :], z_sc.at[slot], sems.at[slot]).wait()
    gates = gate_ref[...]
    acc = base_ref[...]
    for k in range(TOP_K):
        acc = acc + gates[:, k:k + 1] * _load_rows(z_sc.at[slot], k * TT, TT).astype(_F32)
    o_ref[...] = _ln(acc, g2_ref[...], b2_ref[...])


def _combine(dest_flat, y, base, gates_t, ln2_g, ln2_b):
    n = base.shape[0]
    steps = n // TT
    const = lambda shape: pl.BlockSpec(shape, lambda i: (0, 0))
    dest_spec = lambda ahead: pl.BlockSpec((TT * SUBLANES,), lambda i: (jnp.minimum(i + ahead, steps - 1),),
                                           memory_space=pltpu.SMEM)
    return pl.pallas_call(
        _combine_body,
        grid=(steps,),
        in_specs=[dest_spec(0), dest_spec(1),
                  pl.BlockSpec(memory_space=pl.ANY),
                  pl.BlockSpec((TT, D_MODEL), lambda i: (i, 0)),
                  pl.BlockSpec((TT, SUBLANES), lambda i: (i, 0)),
                  const((1, D_MODEL)), const((1, D_MODEL))],
        out_specs=pl.BlockSpec((TT, D_MODEL), lambda i: (i, 0)),
        out_shape=jax.ShapeDtypeStruct((n, D_MODEL), _F32),
        scratch_shapes=[pltpu.VMEM((2, TT * TOP_K * ROW_TILES, LANES), ROW_DTYPE), pltpu.SemaphoreType.DMA((2,))],
        compiler_params=_params(1),
        name="combine",
    )(dest_flat, dest_flat, y, base, gates_t, ln2_g, ln2_b)


def _rope_tables(pos):
    half = RET_DK // 2
    inv = ROPE_BASE ** (-jnp.arange(half, dtype=_F32) / half)
    ang = pos[:, None] * inv[None, :]
    cos = jnp.cos(ang)
    sin = jnp.sin(ang)
    return jnp.concatenate([cos, cos], -1), jnp.concatenate([-sin, sin], -1)


def _decay_tables():
    lg = jnp.log1p(-jnp.exp2(-5.0 - jnp.arange(RET_HEADS, dtype=_F32)))
    idx = jnp.arange(BLOCK, dtype=_F32)
    rel = idx[:, None] - idx[None, :]
    causal = rel >= 0
    dmask = jnp.where(causal[None], jnp.exp(jnp.where(causal, rel, 0.0)[None] * lg[:, None, None]), 0.0)
    zeta = jnp.exp((BLOCK - 1.0 - idx)[None, :] * lg[:, None])
    xi = jnp.exp((idx + 1.0)[None, :] * lg[:, None])
    along_lanes = lambda col: jnp.broadcast_to(col[:, :, None], (RET_HEADS, BLOCK, LANES))
    return dmask, along_lanes(xi), along_lanes(zeta), jnp.exp(BLOCK * lg)


def kernel(x, meta, ln0_g, ln0_b, w_in, b_forget, w_out, ln1_g, ln1_b, w_router, router_bias, we_gate, we_up,
           we_down, ws_gate, ws_up, ws_down, ln2_g, ln2_b):
    nb, s, d = x.shape
    assert d == D_MODEL and meta.shape == (N_META, D_MODEL) and w_in.shape[0] == 1
    assert s % TM_PROJ == 0 and s % T_FOX == 0 and (nb * s) % TT == 0
    n = nb * s
    x2d = x.reshape(n, d)
    row2 = lambda v: v.reshape(1, -1).astype(_F32)
    main_cols = 7 * HEAD_W
    w_all = jnp.concatenate(
        [w_in[0, :, :main_cols], w_in[0, :, main_cols:], jnp.zeros((d, LANES - FOX_HEADS), w_in.dtype)],
        axis=1).astype(_BF16)
    bf_pad = jnp.concatenate([b_forget[0].astype(_F32), jnp.zeros((LANES - FOX_HEADS,), _F32)]).reshape(1, LANES)
    g0, b0 = row2(ln0_g), row2(ln0_b)

    cos_x, sin_x = _rope_tables(jnp.arange(s, dtype=_F32) + float(N_META))
    cos_m, sin_m = _rope_tables(jnp.arange(BLOCK, dtype=_F32) - float(PAD))
    meta_blk = jnp.concatenate([jnp.zeros((PAD, d), _F32), meta.astype(_F32)], axis=0)

    rq, rk, rv, rg, fq, fk, fv = _inproj(x2d, g0, b0, w_all, bf_pad, cos_x, sin_x, nb=nb, meta=False)
    _, rk_m, rv_m, _, _, fk_m, fv_m = _inproj(meta_blk, g0, b0, w_all, bf_pad, cos_m, sin_m, nb=1, meta=True)

    dmask, xi, zeta, gch = _decay_tables()
    per_batch = lambda a: a.reshape(nb, s, a.shape[-1])
    ret = _retention(per_batch(rq), per_batch(rk), per_batch(rv), per_batch(rg), rk_m, rv_m, dmask, xi, zeta,
                     gch).reshape(n, HEAD_W)
    fox = _fox(per_batch(fq), per_batch(fk), per_batch(fv), fk_m, fv_m, nb=nb).reshape(n, HEAD_W)

    w_gu = jnp.concatenate([ws_gate[0], ws_up[0]], axis=1).astype(_BF16)
    base, trow, sel, gates, rank, cnt = _post(
        x2d, ret, fox, g0, b0, w_out[0].astype(_BF16), row2(ln1_g[0]), row2(ln1_b[0]),
        jnp.transpose(w_router[0]).astype(_BF16), router_bias[0].astype(_F32).reshape(N_EXPERTS, 1), w_gu,
        ws_down[0].astype(_BF16))

    nblk = n * TOP_K // EBLK + N_EXPERTS
    nblk_pad = -(-nblk // LANES) * LANES
    dest, blk_e, fill, used, last_blk = _plan(sel, rank, cnt, nblk_pad)
    dest_flat = jnp.transpose(dest).reshape(-1)
    used = used.reshape(-1)

    xs = _dispatch(dest_flat, last_blk.reshape(-1), used, trow, nblk * EBLK)
    y = _experts(blk_e.reshape(-1), used, fill.reshape(-1), xs, we_gate[0], we_up[0], we_down[0])
    out = _combine(dest_flat, y, base, jnp.transpose(gates), row2(ln2_g[0]), row2(ln2_b[0]))
    return out.reshape(nb, s, d)
```

```python
import functools

import jax
import jax.numpy as jnp
import numpy as np
from jax import lax
from jax.experimental import pallas as pl
from jax.experimental.pallas import tpu as pltpu

D_MODEL = 1024
N_META = 16
BLOCK = 128
PAD = BLOCK - N_META
RET_HEADS = 4
RET_DK = 128
FOX_HEADS = 8
FOX_HD = 64
N_EXPERTS = 64
TOP_K = 6
N_GROUPS = 8
GROUP_SIZE = N_EXPERTS // N_GROUPS
TOPK_GROUPS = 4
EXPERT_FF = 256
SHARED_FF = 256
ROUTED_SCALE = 2.5
ROPE_BASE = 10000.0
LN_EPS = 1e-5
NEG_INF = -1e30
ALPHA = 2.0 ** 0.25
HEAD_W = 512
LOG2E = 1.4426950408889634

LANES = 128
FOX_W = FOX_HEADS * LANES
SUBLANES = 8
ROW_TILES = D_MODEL // LANES
ROW_DTYPE = jnp.float32

TM_PROJ = 512
T_FOX = 512
FOX_CHUNK = 64
FOX_K_TERMS = 32
TT = 256
TT_DISPATCH = 512
EBLK = 512
ISSUE_UNROLL = 4
VMEM_LIMIT = 48 * 1024 * 1024

_F32 = jnp.float32
_BF16 = jnp.bfloat16


def _ln(x, g, b):
    xc = x - jnp.mean(x, -1, keepdims=True)
    var = jnp.mean(xc * xc, -1, keepdims=True)
    return xc * lax.rsqrt(var + LN_EPS) * g + b


def _dot(a, b):
    return jnp.dot(a, b, preferred_element_type=_F32)


def _dot_nt(a, b):
    return lax.dot_general(a, b, (((1,), (1,)), ((), ())), preferred_element_type=_F32)


def _dot_tn(a, b):
    return lax.dot_general(a, b, (((0,), (0,)), ((), ())), preferred_element_type=_F32)


def _silu(x):
    return x * jax.nn.sigmoid(x)


def _params(n_axes):
    return pltpu.CompilerParams(dimension_semantics=("arbitrary",) * n_axes, vmem_limit_bytes=VMEM_LIMIT)


def _inproj_body(x_ref, g_ref, b_ref, w_ref, bf_ref, cos_ref, sin_ref, own_ref, tq_ref, tk_ref, oq_ref, ok_ref, ov_ref,
                 h_ref, rq_ref, rk_ref, rv_ref, rg_ref, fq_ref, fk_ref, fv_ref, carry_ref, *, meta):
    tm = x_ref.shape[0]
    h = _ln(x_ref[...], g_ref[...], b_ref[...])
    if meta:
        valid = lax.broadcasted_iota(jnp.int32, (tm, 1), 0) >= PAD
        h = jnp.where(valid, h, 0.0)
    h_ref[...] = h
    hb = h.astype(_BF16)
    cos = cos_ref[...]
    sin = sin_ref[...]

    def proj(g):
        return _dot(hb, w_ref[:, g * HEAD_W:(g + 1) * HEAD_W])

    def rope_store(p, out_ref, scale):
        for hd in range(RET_HEADS):
            t = p[:, hd * LANES:(hd + 1) * LANES]
            r = t * cos + pltpu.roll(t, LANES // 2, axis=1) * sin
            out_ref[:, hd * LANES:(hd + 1) * LANES] = (r * scale).astype(_BF16)

    rope_store(proj(0), rq_ref, 1.0)
    rope_store(proj(1), rk_ref, RET_DK ** -0.5)
    rv_ref[...] = proj(2).astype(_BF16)
    rg_ref[...] = _silu(proj(3)).astype(_BF16)

    z = _dot(hb, w_ref[:, 7 * HEAD_W:7 * HEAD_W + LANES]) + bf_ref[...]
    logf = jnp.minimum(z, 0.0) - jnp.log1p(jnp.exp(-jnp.abs(z)))
    if meta:
        logf = jnp.where(valid, logf, 0.0)
    l1 = logf.astype(_BF16)
    r1 = logf - l1.astype(_F32)
    l2 = r1.astype(_BF16)
    l3 = (r1 - l2.astype(_F32)).astype(_BF16)
    row = lax.broadcasted_iota(jnp.int32, (tm, tm), 0)
    col = lax.broadcasted_iota(jnp.int32, (tm, tm), 1)
    tri = (col <= row).astype(_BF16)
    c = _dot(tri, l1) + _dot(tri, l2) + _dot(tri, l3)
    if meta:
        c = c - c[tm - 1:tm, :]
    else:
        @pl.when(pl.program_id(1) == 0)
        def _():
            carry_ref[...] = jnp.zeros_like(carry_ref)

        c = c + carry_ref[...]
        carry_ref[...] = c[tm - 1:tm, :]

    head_lane = lax.broadcasted_iota(jnp.int32, (1, LANES), 1) < FOX_HEADS
    cl = jnp.where(head_lane, c, 0.0) * LOG2E
    c1 = cl.astype(_BF16).astype(_F32)
    r1 = cl - c1
    c2 = r1.astype(_BF16).astype(_F32)
    c3 = (r1 - c2).astype(_BF16).astype(_F32)
    csplit = c1 + pltpu.roll(c2, FOX_HEADS, axis=1) + pltpu.roll(c3, 2 * FOX_HEADS, axis=1)
    half = FOX_HD
    q_even, q_odd = pltpu.roll(csplit, half, axis=1), csplit
    k_even, k_odd = pltpu.roll(csplit, half + FOX_K_TERMS, axis=1), pltpu.roll(csplit, FOX_K_TERMS, axis=1)
    own = own_ref[...] > 0.0

    def per_head(p):
        return jnp.concatenate([p[:, (hd // 2) * LANES:(hd // 2 + 1) * LANES] for hd in range(FOX_HEADS)], axis=1)

    def by_parity(even, odd):
        return jnp.concatenate([even, odd] * (FOX_HEADS // 2), axis=1)

    q_extra = jnp.where(tq_ref[...] > 0.0, by_parity(q_even, q_odd), oq_ref[...])
    k_extra = jnp.where(tk_ref[...] > 0.0, -by_parity(k_even, k_odd), ok_ref[...])
    fq_ref[...] = jnp.where(own, per_head(proj(4) * (FOX_HD ** -0.5 * LOG2E)), q_extra).astype(_BF16)
    fk_ref[...] = jnp.where(own, per_head(proj(5)), k_extra).astype(_BF16)
    fv_ref[...] = jnp.where(own, per_head(proj(6)), ov_ref[...]).astype(_BF16)


def _fox_lane_tables():
    own, tq, tk, oq, ok, ov = (np.zeros((1, FOX_W), np.float32) for _ in range(6))
    for hd in range(FOX_HEADS):
        data = hd * LANES + (hd % 2) * FOX_HD
        extra = hd * LANES + (1 - hd % 2) * FOX_HD
        own[0, data:data + FOX_HD] = 1.0
        for term in range(3):
            lane = extra + term * FOX_HEADS + hd
            tq[0, lane] = 1.0
            ok[0, lane] = 1.0
            tk[0, lane + FOX_K_TERMS] = 1.0
            oq[0, lane + FOX_K_TERMS] = 1.0
        ov[0, extra] = 1.0
    return tuple(jnp.asarray(t) for t in (own, tq, tk, oq, ok, ov))


def _inproj(x2d, ln_g, ln_b, w_all, bf_pad, cos_t, sin_t, *, nb, meta):
    n = x2d.shape[0]
    s = n // nb
    tm = min(TM_PROJ, s)
    nj = s // tm
    row_spec = lambda w: pl.BlockSpec((tm, w), lambda b, j: (b * nj + j, 0))
    const = lambda shape: pl.BlockSpec(shape, lambda b, j: (0, 0))
    pos_spec = pl.BlockSpec((tm, LANES), lambda b, j: (j, 0))
    tables = _fox_lane_tables()
    outs = ([jax.ShapeDtypeStruct((n, D_MODEL), _F32)] + [jax.ShapeDtypeStruct((n, HEAD_W), _BF16)] * 4
            + [jax.ShapeDtypeStruct((n, FOX_W), _BF16)] * 3)
    return pl.pallas_call(
        functools.partial(_inproj_body, meta=meta),
        grid=(nb, nj),
        in_specs=[row_spec(D_MODEL), const((1, D_MODEL)), const((1, D_MODEL)), const(w_all.shape),
                  const((1, LANES)), pos_spec, pos_spec] + [const(t.shape) for t in tables],
        out_specs=[row_spec(D_MODEL)] + [row_spec(HEAD_W)] * 4 + [row_spec(FOX_W)] * 3,
        out_shape=outs,
        scratch_shapes=[pltpu.VMEM((1, LANES), _F32)],
        compiler_params=_params(2),
        name="inproj_meta" if meta else "inproj",
    )(x2d, ln_g, ln_b, w_all, bf_pad, cos_t, sin_t, *tables)


def _ret_body(q_ref, k_ref, v_ref, g_ref, km_ref, vm_ref, dm_ref, xi_ref, zeta_ref, gch_ref, o_ref, st_ref):
    def kv_update(k, v, hd):
        vz = (v.astype(_F32) * zeta_ref[hd]).astype(_BF16)
        return _dot_tn(k, vz)

    nb = q_ref.shape[0]

    @pl.when(pl.program_id(0) == 0)
    def _():
        for hd in range(RET_HEADS):
            sl = slice(hd * LANES, (hd + 1) * LANES)
            first = kv_update(km_ref[:, sl], vm_ref[:, sl], hd)
            for b in range(nb):
                st_ref[b, hd] = first

    for b in range(nb):
        for hd in range(RET_HEADS):
            sl = slice(hd * LANES, (hd + 1) * LANES)
            q = q_ref[b, :, sl]
            k = k_ref[b, :, sl]
            v = v_ref[b, :, sl]
            st = st_ref[b, hd]
            scores = _dot_nt(q, k) * dm_ref[hd]
            o = _dot(scores.astype(_BF16), v) + _dot(q, st.astype(_BF16)) * xi_ref[hd]
            oc = o - jnp.mean(o, -1, keepdims=True)
            y = oc * lax.rsqrt(jnp.mean(oc * oc, -1, keepdims=True) + LN_EPS)
            o_ref[b, :, sl] = (y * g_ref[b, :, sl].astype(_F32)).astype(_BF16)
            st_ref[b, hd] = gch_ref[hd] * st + kv_update(k, v, hd)


def _retention(rq, rk, rv, rg, rk_m, rv_m, dmask, xi, zeta, gch):
    nb, s, _ = rq.shape
    row_spec = pl.BlockSpec((nb, BLOCK, HEAD_W), lambda j: (0, j, 0))
    meta_spec = pl.BlockSpec((BLOCK, HEAD_W), lambda j: (0, 0))
    tab = pl.BlockSpec((RET_HEADS, BLOCK, BLOCK), lambda j: (0, 0, 0))
    return pl.pallas_call(
        _ret_body,
        grid=(s // BLOCK,),
        in_specs=[row_spec] * 4 + [meta_spec] * 2 + [tab] * 3 + [pl.BlockSpec(memory_space=pltpu.SMEM)],
        out_specs=row_spec,
        out_shape=jax.ShapeDtypeStruct((nb, s, HEAD_W), _BF16),
        scratch_shapes=[pltpu.VMEM((nb, RET_HEADS, RET_DK, LANES), _F32)],
        compiler_params=_params(1),
        name="retention",
    )(rq, rk, rv, rg, rk_m, rv_m, dmask, xi, zeta, gch)


def _fox_body(q_ref, k_ref, v_ref, km_ref, vm_ref, o_ref, m_sc, acc_sc, s_sc, p_sc):
    t = q_ref.shape[0]
    qi = pl.program_id(2)

    heads = [slice(hh * LANES, (hh + 1) * LANES) for hh in range(2)]

    def logits(hh, k, buf, tk):
        s_sc[buf, hh, :, :tk] = _dot_nt(q_ref[:, heads[hh]], k)

    def update(hh, v, buf, tk, mask, first):
        for c in range(t // FOX_CHUNK):
            rows = slice(c * FOX_CHUNK, (c + 1) * FOX_CHUNK)
            s = s_sc[buf, hh, rows, :tk]
            if mask is not None:
                s = mask(s, c)
            mx = jnp.max(s, axis=1, keepdims=True)
            if first:
                m_new = mx
            else:
                m_prev = m_sc[hh, rows, :]
                m_new = jnp.maximum(m_prev, mx)
                acc_sc[hh, rows, :] = jnp.exp2(m_prev - m_new) * acc_sc[hh, rows, :]
            p_sc[hh, rows, :tk] = jnp.exp2(s - m_new).astype(_BF16)
            m_sc[hh, rows, :] = m_new
        pv = _dot(p_sc[hh, :, :tk], v)
        if first:
            acc_sc[hh] = pv
        else:
            acc_sc[hh] += pv

    def meta_mask(s, c):
        key = lax.broadcasted_iota(jnp.int32, s.shape, 1)
        return jnp.where(key >= PAD, s, NEG_INF)

    def causal_mask(s, c):
        key = lax.broadcasted_iota(jnp.int32, s.shape, 1)
        query = lax.broadcasted_iota(jnp.int32, s.shape, 0) + c * FOX_CHUNK
        return jnp.where(key <= query, s, NEG_INF)

    def key_tile(ref, ki, hh):
        return ref[pl.ds(pl.multiple_of(ki * t, t), t), heads[hh]]

    for hh in range(2):
        logits(hh, km_ref[:, heads[hh]], 1, BLOCK)
    for hh in range(2):
        logits(hh, key_tile(k_ref, 0, hh), 0, t)
        update(hh, vm_ref[:, heads[hh]], 1, BLOCK, meta_mask, True)

    def step(ki, buf, mask, more):
        for hh in range(2):
            if more:
                logits(hh, key_tile(k_ref, ki + 1, hh), 1 - buf, t)
            update(hh, key_tile(v_ref, ki, hh), buf, t, mask, False)

    def pair_body(j, carry):
        step(2 * j, 0, None, True)
        step(2 * j + 1, 1, None, True)
        return carry

    lax.fori_loop(0, lax.shift_right_logical(qi, 1), pair_body, 0)
    odd = lax.rem(qi, 2) == 1

    @pl.when(odd)
    def _():
        step(qi - 1, 0, None, True)
        step(qi, 1, causal_mask, False)

    @pl.when(jnp.logical_not(odd))
    def _():
        step(qi, 0, causal_mask, False)

    outs = []
    for hh in range(2):
        acc = acc_sc[hh]
        ones_lane = (1 - hh) * FOX_HD
        outs.append(acc / acc[:, ones_lane:ones_lane + 1])
    lane = lax.broadcasted_iota(jnp.int32, (t, LANES), 1)
    o_ref[...] = jnp.where(lane < FOX_HD, outs[0], outs[1]).astype(_BF16)


def _fox(fq, fk, fv, fk_m, fv_m, *, nb):
    s = fq.shape[1]
    t = T_FOX
    pair_w = 2 * LANES
    return pl.pallas_call(
        _fox_body,
        grid=(nb, FOX_HEADS // 2, s // t),
        in_specs=[
            pl.BlockSpec((None, t, pair_w), lambda b, p, i: (b, i, p)),
            pl.BlockSpec((None, s, pair_w), lambda b, p, i: (b, 0, p)),
            pl.BlockSpec((None, s, pair_w), lambda b, p, i: (b, 0, p)),
            pl.BlockSpec((BLOCK, pair_w), lambda b, p, i: (0, p)),
            pl.BlockSpec((BLOCK, pair_w), lambda b, p, i: (0, p)),
        ],
        out_specs=pl.BlockSpec((None, t, LANES), lambda b, p, i: (b, i, p)),
        out_shape=jax.ShapeDtypeStruct((nb, s, HEAD_W), _BF16),
        scratch_shapes=[pltpu.VMEM((2, t, 1), _F32), pltpu.VMEM((2, t, LANES), _F32), pltpu.VMEM((2, 2, t, t), _F32),
                        pltpu.VMEM((2, t, t), _BF16)],
        compiler_params=_params(3),
        name="fox",
    )(fq, fk, fv, fk_m, fv_m)


def _post_body(h0_ref, ret_ref, fox_ref, wo_ref, g1_ref, b1_ref, wr_ref, rb_ref, wgu_ref, wd_ref,
               base_ref, trow_ref, sel_ref, gate_ref, rank_ref, cnt_ref, run_ref):
    tm = h0_ref.shape[0]
    i = pl.program_id(0)

    @pl.when(i == 0)
    def _():
        run_ref[...] = jnp.zeros_like(run_ref)

    y = _dot(ret_ref[...], wo_ref[:HEAD_W, :]) + _dot(fox_ref[...], wo_ref[HEAD_W:, :])
    h1 = _ln(ALPHA * h0_ref[...] + y, g1_ref[...], b1_ref[...])
    tb = h1.astype(_BF16)
    _store_rows(trow_ref, h1)

    gu = _dot(tb, wgu_ref[...])
    mid = (_silu(gu[:, :SHARED_FF]) * gu[:, SHARED_FF:]).astype(_BF16)
    base_ref[...] = ALPHA * h1 + _dot(mid, wd_ref[...])

    scores = jax.nn.sigmoid(_dot_nt(wr_ref[...], tb))
    biased = scores + rb_ref[...]
    sub = lax.broadcasted_iota(jnp.int32, (GROUP_SIZE, tm), 0).astype(_F32)
    gscore = []
    for g in range(N_GROUPS):
        v = biased[g * GROUP_SIZE:(g + 1) * GROUP_SIZE, :]
        m1 = jnp.max(v, axis=0, keepdims=True)
        i1 = jnp.min(jnp.where(v == m1, sub, float(GROUP_SIZE)), axis=0, keepdims=True)
        m2 = jnp.max(jnp.where(sub == i1, -jnp.inf, v), axis=0, keepdims=True)
        gscore.append(m1 + m2)
    masked = []
    for g in range(N_GROUPS):
        beaten = jnp.zeros((1, tm), _F32)
        for o in range(N_GROUPS):
            if o == g:
                continue
            wins = (gscore[o] >= gscore[g]) if o < g else (gscore[o] > gscore[g])
            beaten = beaten + wins.astype(_F32)
        keep = beaten < float(TOPK_GROUPS)
        masked.append(jnp.where(keep, biased[g * GROUP_SIZE:(g + 1) * GROUP_SIZE, :], NEG_INF))
    work = jnp.concatenate(masked, axis=0)
    eid = lax.broadcasted_iota(jnp.int32, (N_EXPERTS, tm), 0).astype(_F32)
    hots, sels, raws = [], [], []
    for _ in range(TOP_K):
        m = jnp.max(work, axis=0, keepdims=True)
        idx = jnp.min(jnp.where(work == m, eid, float(N_EXPERTS)), axis=0, keepdims=True)
        hot = eid == idx
        hots.append(hot)
        sels.append(idx.astype(jnp.int32))
        raws.append(jnp.sum(jnp.where(hot, scores, 0.0), axis=0, keepdims=True))
        work = jnp.where(hot, -jnp.inf, work)
    total = raws[0]
    for r in raws[1:]:
        total = total + r
    inv = ROUTED_SCALE / total
    member = hots[0]
    for hot in hots[1:]:
        member = member | hot
    member_f = member.astype(_F32)
    row = lax.broadcasted_iota(jnp.int32, (tm, tm), 0)
    col = lax.broadcasted_iota(jnp.int32, (tm, tm), 1)
    before = (row < col).astype(_BF16)
    rank_e = _dot(member_f.astype(_BF16), before) + run_ref[...]
    pad_rows = sel_ref.shape[0] - TOP_K
    sel_ref[...] = jnp.concatenate(sels + [jnp.zeros((pad_rows, tm), jnp.int32)], axis=0)
    gate_ref[...] = jnp.concatenate([r * inv for r in raws] + [jnp.zeros((pad_rows, tm), _F32)], axis=0)
    ranks = [jnp.sum(jnp.where(hot, rank_e, 0.0), axis=0, keepdims=True).astype(jnp.int32) for hot in hots]
    rank_ref[...] = jnp.concatenate(ranks + [jnp.zeros((pad_rows, tm), jnp.int32)], axis=0)
    run_ref[...] = run_ref[...] + jnp.sum(member_f, axis=1, keepdims=True)
    cnt_ref[...] = run_ref[...]


def _post(h0, ret, fox, w_out, ln1_g, ln1_b, w_rt, rbias, w_gu, w_sd):
    n = h0.shape[0]
    tm = TM_PROJ
    row_spec = lambda w: pl.BlockSpec((tm, w), lambda i: (i, 0))
    col_spec = pl.BlockSpec((SUBLANES, tm), lambda i: (0, i))
    const = lambda shape: pl.BlockSpec(shape, lambda i: (0, 0))
    return pl.pallas_call(
        _post_body,
        grid=(n // tm,),
        in_specs=[row_spec(D_MODEL), row_spec(HEAD_W), row_spec(HEAD_W),
                  const(w_out.shape), const((1, D_MODEL)), const((1, D_MODEL)), const(w_rt.shape),
                  const((N_EXPERTS, 1)), const(w_gu.shape), const(w_sd.shape)],
        out_specs=[row_spec(D_MODEL), pl.BlockSpec((tm * ROW_TILES, LANES), lambda i: (i, 0)),
                   col_spec, col_spec, col_spec, const((N_EXPERTS, 1))],
        out_shape=[jax.ShapeDtypeStruct((n, D_MODEL), _F32), jax.ShapeDtypeStruct((n * ROW_TILES, LANES), ROW_DTYPE),
                   jax.ShapeDtypeStruct((SUBLANES, n), jnp.int32), jax.ShapeDtypeStruct((SUBLANES, n), _F32),
                   jax.ShapeDtypeStruct((SUBLANES, n), jnp.int32), jax.ShapeDtypeStruct((N_EXPERTS, 1), _F32)],
        scratch_shapes=[pltpu.VMEM((N_EXPERTS, 1), _F32)],
        compiler_params=_params(1),
        name="post_mixer",
    )(h0, ret, fox, w_out, ln1_g, ln1_b, w_rt, rbias, w_gu, w_sd)


def _plan_body(sel_ref, rank_ref, cnt_ref, dest_ref, blk_ref, fill_ref, used_ref, pad0_ref, padn_ref):
    cnt = cnt_ref[...]
    padded = jnp.ceil(cnt * (1.0 / EBLK)) * EBLK
    er = lax.broadcasted_iota(jnp.int32, (N_EXPERTS, N_EXPERTS), 0)
    ec = lax.broadcasted_iota(jnp.int32, (N_EXPERTS, N_EXPERTS), 1)
    padded_row = jnp.sum(jnp.where(er == ec, padded, 0.0), axis=0, keepdims=True)
    pstart = jnp.sum(jnp.where(ec < er, padded_row, 0.0), axis=1, keepdims=True)
    pend = pstart + padded
    sel = sel_ref[...]
    dest = rank_ref[...]
    for e in range(N_EXPERTS):
        dest = dest + jnp.where(sel == e, pstart[e:e + 1, :].astype(jnp.int32), 0)
    dest_ref[...] = dest
    nblk = blk_ref.shape[1]
    first_row = (lax.broadcasted_iota(jnp.int32, (N_EXPERTS, nblk), 1) * EBLK).astype(_F32)
    owner = jnp.minimum(jnp.sum((pend <= first_row).astype(_F32), axis=0, keepdims=True), N_EXPERTS - 1.0)
    blk_ref[...] = owner.astype(jnp.int32)
    mine = lax.broadcasted_iota(jnp.int32, (N_EXPERTS, nblk), 0).astype(_F32) == owner
    live_end = jnp.sum(jnp.where(mine, pstart + cnt, 0.0), axis=0, keepdims=True)
    fill_ref[...] = jnp.clip(live_end - first_row[:1, :], 0.0, float(EBLK)).astype(jnp.int32)
    used_ref[...] = (pend[N_EXPERTS - 1:, :] * (1.0 / EBLK)).astype(jnp.int32)
    as_row = lambda col: jnp.sum(jnp.where(er == ec, col, 0.0), axis=0, keepdims=True).astype(jnp.int32)
    pad0_ref[...] = as_row(pstart + cnt)
    padn_ref[...] = as_row(padded - cnt)


def _plan(sel, rank, cnt, nblk_pad):
    n = sel.shape[1]
    full = lambda shape: pl.BlockSpec(shape, lambda i: (0, 0))
    return pl.pallas_call(
        _plan_body,
        grid=(1,),
        in_specs=[full(sel.shape), full(rank.shape), full(cnt.shape)],
        out_specs=[full(sel.shape), full((1, nblk_pad)), full((1, nblk_pad)), full((1, 1)), full((1, N_EXPERTS)),
                   full((1, N_EXPERTS))],
        out_shape=[jax.ShapeDtypeStruct((SUBLANES, n), jnp.int32), jax.ShapeDtypeStruct((1, nblk_pad), jnp.int32),
                   jax.ShapeDtypeStruct((1, nblk_pad), jnp.int32), jax.ShapeDtypeStruct((1, 1), jnp.int32),
                   jax.ShapeDtypeStruct((1, N_EXPERTS), jnp.int32), jax.ShapeDtypeStruct((1, N_EXPERTS), jnp.int32)],
        compiler_params=_params(1),
        name="plan",
    )(sel, rank, cnt)


def _store_rows(ref, v):
    m = v.shape[0]
    for s in range(ROW_TILES):
        ref[pl.ds(s, m, stride=ROW_TILES), :] = v[:, s * LANES:(s + 1) * LANES].astype(ROW_DTYPE)


def _load_rows(ref, first_row, m):
    return jnp.concatenate([ref[pl.ds(first_row * ROW_TILES + s, m, stride=ROW_TILES), :] for s in range(ROW_TILES)],
                           axis=1)


def _row_copy(src, src_row, dst, dst_row, sem):
    return pltpu.make_async_copy(src.at[pl.ds(pl.multiple_of(src_row * ROW_TILES, ROW_TILES), ROW_TILES), :],
                                 dst.at[pl.ds(pl.multiple_of(dst_row * ROW_TILES, ROW_TILES), ROW_TILES), :], sem)


def _dispatch_body(dest_ref, pad0_ref, padn_ref, used_ref, t_ref, xs_ref, zero_sc, sem, zsem):
    tt = t_ref.shape[0] // ROW_TILES
    half_blk = EBLK // 2
    n_half = xs_ref.shape[0] // (half_blk * ROW_TILES)

    @pl.when(pl.program_id(0) == 0)
    def _():
        zero_sc[...] = jnp.zeros_like(zero_sc)

        def zero_copy(first_row, rows):
            return pltpu.make_async_copy(
                zero_sc.at[pl.ds(0, rows * ROW_TILES), :],
                xs_ref.at[pl.ds(pl.multiple_of(first_row * ROW_TILES, ROW_TILES), rows * ROW_TILES), :], zsem)

        def for_padding(act):
            def per_expert(e, carry):
                row = pad0_ref[e]
                for bit in range(EBLK.bit_length() - 2, -1, -1):
                    take = (padn_ref[e] & (1 << bit)) != 0

                    @pl.when(take)
                    def _():
                        act(zero_copy(row, 1 << bit))

                    row = row + jnp.where(take, 1 << bit, 0)
                return carry

            def per_tail(hb, carry):
                act(zero_copy(hb * half_blk, half_blk))
                return carry

            lax.fori_loop(0, N_EXPERTS, per_expert, 0)
            lax.fori_loop(2 * used_ref[0], n_half, per_tail, 0)

        for_padding(lambda cp: cp.start())
        for_padding(lambda cp: cp.wait())

    def issue(j, carry):
        for u in range(ISSUE_UNROLL):
            i = ISSUE_UNROLL * j + u
            for k in range(TOP_K):
                _row_copy(t_ref, i, xs_ref, dest_ref[i * SUBLANES + k], sem).start(priority=k % 2)
        return carry

    lax.fori_loop(0, tt // ISSUE_UNROLL, issue, 0)
    for _ in range(TOP_K):
        pltpu.make_async_copy(t_ref, xs_ref.at[pl.ds(0, tt * ROW_TILES), :], sem).wait()


def _dispatch(dest_flat, pad0, padn, used, trow, total_rows):
    n = trow.shape[0] // ROW_TILES
    tt = TT_DISPATCH
    smem = pl.BlockSpec(memory_space=pltpu.SMEM)
    return pl.pallas_call(
        _dispatch_body,
        grid=(n // tt,),
        in_specs=[pl.BlockSpec((tt * SUBLANES,), lambda i: (i,), memory_space=pltpu.SMEM), smem, smem, smem,
                  pl.BlockSpec((tt * ROW_TILES, LANES), lambda i: (i, 0))],
        out_specs=pl.BlockSpec(memory_space=pl.ANY),
        out_shape=jax.ShapeDtypeStruct((total_rows * ROW_TILES, LANES), ROW_DTYPE),
        scratch_shapes=[pltpu.VMEM((EBLK // 2 * ROW_TILES, LANES), ROW_DTYPE), pltpu.SemaphoreType.DMA(()),
                        pltpu.SemaphoreType.DMA(())],
        compiler_params=_params(1),
        name="dispatch",
    )(dest_flat, pad0, padn, used, trow)


def _expert_body(blk_ref, used_ref, fill_ref, xs_ref, wg_ref, wu_ref, wd_ref, y_ref, wgu_sc, wd_sc):
    i = pl.program_id(0)
    prev = blk_ref[jnp.maximum(i - 1, 0)]
    fresh = (i == 0) | (blk_ref[i] != prev)

    @pl.when(fresh)
    def _():
        wgu_sc[:, :EXPERT_FF] = wg_ref[...].astype(_BF16)
        wgu_sc[:, EXPERT_FF:] = wu_ref[...].astype(_BF16)
        wd_sc[...] = wd_ref[...].astype(_BF16)

    @pl.when(i < used_ref[0])
    def _():
        live = lax.broadcasted_iota(jnp.int32, (EBLK, 1), 0) < fill_ref[i]
        x = _load_rows(xs_ref, 0, EBLK)
        x = jnp.where(live, x, jnp.zeros_like(x)).astype(_BF16)
        gu = _dot(x, wgu_sc[...])
        mid = (_silu(gu[:, :EXPERT_FF]) * gu[:, EXPERT_FF:]).astype(_BF16)
        _store_rows(y_ref, _dot(mid, wd_sc[...]))

    @pl.when(i >= used_ref[0])
    def _():
        y_ref[...] = jnp.zeros_like(y_ref)


def _experts(blk_e, used, fill, xs, we_gate, we_up, we_down):
    nblk = xs.shape[0] // (EBLK * ROW_TILES)
    last = lambda i, used: jnp.minimum(i, jnp.maximum(used[0] - 1, 0))
    w_spec = lambda shape: pl.BlockSpec((None,) + shape, lambda i, blk, used, fill: (blk[i], 0, 0))
    return pl.pallas_call(
        _expert_body,
        grid_spec=pltpu.PrefetchScalarGridSpec(
            num_scalar_prefetch=3,
            grid=(nblk,),
            in_specs=[pl.BlockSpec((EBLK * ROW_TILES, LANES), lambda i, blk, used, fill: (last(i, used), 0)),
                      w_spec((D_MODEL, EXPERT_FF)), w_spec((D_MODEL, EXPERT_FF)), w_spec((EXPERT_FF, D_MODEL))],
            out_specs=pl.BlockSpec((EBLK * ROW_TILES, LANES), lambda i, blk, used, fill: (i, 0)),
            scratch_shapes=[pltpu.VMEM((D_MODEL, 2 * EXPERT_FF), _BF16), pltpu.VMEM((EXPERT_FF, D_MODEL), _BF16)],
        ),
        out_shape=jax.ShapeDtypeStruct(xs.shape, ROW_DTYPE),
        compiler_params=_params(1),
        name="experts",
    )(blk_e, used, fill, xs, we_gate, we_up, we_down)


def _combine_body(dest_ref, dnext_ref, y_ref, base_ref, gate_ref, g2_ref, b2_ref, o_ref, z_sc, sems):
    i = pl.program_id(0)
    slot = lax.rem(i, 2)
    tile_rows = TT * TOP_K * ROW_TILES

    def gather(dref, into):
        def issue(j, carry):
            for u in range(ISSUE_UNROLL):
                t = ISSUE_UNROLL * j + u
                for k in range(TOP_K):
                    _row_copy(y_ref, dref[t * SUBLANES + k], z_sc.at[into], k * TT + t,
                              sems.at[into]).start(priority=k % 2)
            return carry

        lax.fori_loop(0, TT // ISSUE_UNROLL, issue, 0)

    @pl.when(i == 0)
    def _():
        gather(dest_ref, 0)

    @pl.when(i + 1 < pl.num_programs(0))
    def _():
        gather(dnext_ref, 1 - slot)

    pltpu.make_async_copy(y_ref.at[pl.ds(0, tile_rows), :], z_sc.at[slot], sems.at[slot]).wait()
    gates = gate_ref[...]
    acc = base_ref[...]
    for k in range(TOP_K):
        acc = acc + gates[:, k:k + 1] * _load_rows(z_sc.at[slot], k * TT, TT).astype(_F32)
    o_ref[...] = _ln(acc, g2_ref[...], b2_ref[...])


def _combine(dest_flat, y, base, gates_t, ln2_g, ln2_b):
    n = base.shape[0]
    steps = n // TT
    const = lambda shape: pl.BlockSpec(shape, lambda i: (0, 0))
    dest_spec = lambda ahead: pl.BlockSpec((TT * SUBLANES,), lambda i: (jnp.minimum(i + ahead, steps - 1),),
                                           memory_space=pltpu.SMEM)
    return pl.pallas_call(
        _combine_body,
        grid=(steps,),
        in_specs=[dest_spec(0), dest_spec(1),
                  pl.BlockSpec(memory_space=pl.ANY),
                  pl.BlockSpec((TT, D_MODEL), lambda i: (i, 0)),
                  pl.BlockSpec((TT, SUBLANES), lambda i: (i, 0)),
                  const((1, D_MODEL)), const((1, D_MODEL))],
        out_specs=pl.BlockSpec((TT, D_MODEL), lambda i: (i, 0)),
        out_shape=jax.ShapeDtypeStruct((n, D_MODEL), _F32),
        scratch_shapes=[pltpu.VMEM((2, TT * TOP_K * ROW_TILES, LANES), ROW_DTYPE), pltpu.SemaphoreType.DMA((2,))],
        compiler_params=_params(1),
        name="combine",
    )(dest_flat, dest_flat, y, base, gates_t, ln2_g, ln2_b)


def _rope_tables(pos):
    half = RET_DK // 2
    inv = ROPE_BASE ** (-jnp.arange(half, dtype=_F32) / half)
    ang = pos[:, None] * inv[None, :]
    cos = jnp.cos(ang)
    sin = jnp.sin(ang)
    return jnp.concatenate([cos, cos], -1), jnp.concatenate([-sin, sin], -1)


def _decay_tables():
    lg = jnp.log1p(-jnp.exp2(-5.0 - jnp.arange(RET_HEADS, dtype=_F32)))
    idx = jnp.arange(BLOCK, dtype=_F32)
    rel = idx[:, None] - idx[None, :]
    causal = rel >= 0
    dmask = jnp.where(causal[None], jnp.exp(jnp.where(causal, rel, 0.0)[None] * lg[:, None, None]), 0.0)
    zeta = jnp.exp((BLOCK - 1.0 - idx)[None, :] * lg[:, None])
    xi = jnp.exp((idx + 1.0)[None, :] * lg[:, None])
    along_lanes = lambda col: jnp.broadcast_to(col[:, :, None], (RET_HEADS, BLOCK, LANES))
    return dmask, along_lanes(xi), along_lanes(zeta), jnp.exp(BLOCK * lg)


def kernel(x, meta, ln0_g, ln0_b, w_in, b_forget, w_out, ln1_g, ln1_b, w_router, router_bias, we_gate, we_up,
           we_down, ws_gate, ws_up, ws_down, ln2_g, ln2_b):
    nb, s, d = x.shape
    assert d == D_MODEL and meta.shape == (N_META, D_MODEL) and w_in.shape[0] == 1
    assert s % TM_PROJ == 0 and s % T_FOX == 0 and (nb * s) % TT == 0
    n = nb * s
    x2d = x.reshape(n, d)
    row2 = lambda v: v.reshape(1, -1).astype(_F32)
    main_cols = 7 * HEAD_W
    w_all = jnp.concatenate(
        [w_in[0, :, :main_cols], w_in[0, :, main_cols:], jnp.zeros((d, LANES - FOX_HEADS), w_in.dtype)],
        axis=1).astype(_BF16)
    bf_pad = jnp.concatenate([b_forget[0].astype(_F32), jnp.zeros((LANES - FOX_HEADS,), _F32)]).reshape(1, LANES)
    g0, b0 = row2(ln0_g), row2(ln0_b)

    cos_x, sin_x = _rope_tables(jnp.arange(s, dtype=_F32) + float(N_META))
    cos_m, sin_m = _rope_tables(jnp.arange(BLOCK, dtype=_F32) - float(PAD))
    meta_blk = jnp.concatenate([jnp.zeros((PAD, d), _F32), meta.astype(_F32)], axis=0)

    h0, rq, rk, rv, rg, fq, fk, fv = _inproj(x2d, g0, b0, w_all, bf_pad, cos_x, sin_x, nb=nb, meta=False)
    _, _, rk_m, rv_m, _, _, fk_m, fv_m = _inproj(meta_blk, g0, b0, w_all, bf_pad, cos_m, sin_m, nb=1, meta=True)

    dmask, xi, zeta, gch = _decay_tables()
    per_batch = lambda a: a.reshape(nb, s, a.shape[-1])
    ret = _retention(per_batch(rq), per_batch(rk), per_batch(rv), per_batch(rg), rk_m, rv_m, dmask, xi, zeta,
                     gch).reshape(n, HEAD_W)
    fox = _fox(per_batch(fq), per_batch(fk), per_batch(fv), fk_m, fv_m, nb=nb).reshape(n, HEAD_W)

    w_gu = jnp.concatenate([ws_gate[0], ws_up[0]], axis=1).astype(_BF16)
    base, trow, sel, gates, rank, cnt = _post(
        h0, ret, fox, w_out[0].astype(_BF16), row2(ln1_g[0]), row2(ln1_b[0]),
        jnp.transpose(w_router[0]).astype(_BF16), router_bias[0].astype(_F32).reshape(N_EXPERTS, 1), w_gu,
        ws_down[0].astype(_BF16))

    nblk = n * TOP_K // EBLK + N_EXPERTS
    nblk_pad = -(-nblk // LANES) * LANES
    dest, blk_e, fill, used, pad0, padn = _plan(sel, rank, cnt, nblk_pad)
    dest_flat = jnp.transpose(dest).reshape(-1)
    used = used.reshape(-1)

    xs = _dispatch(dest_flat, pad0.reshape(-1), padn.reshape(-1), used, trow, nblk * EBLK)
    y = _experts(blk_e.reshape(-1), used, fill.reshape(-1), xs, we_gate[0], we_up[0], we_down[0])
    out = _combine(dest_flat, y, base, jnp.transpose(gates), row2(ln2_g[0]), row2(ln2_b[0]))
    return out.reshape(nb, s, d)
```

```python
import functools

import jax
import jax.numpy as jnp
import numpy as np
from jax import lax
from jax.experimental import pallas as pl
from jax.experimental.pallas import tpu as pltpu

D_MODEL = 1024
N_META = 16
BLOCK = 128
PAD = BLOCK - N_META
RET_HEADS = 4
RET_DK = 128
FOX_HEADS = 8
FOX_HD = 64
N_EXPERTS = 64
TOP_K = 6
N_GROUPS = 8
GROUP_SIZE = N_EXPERTS // N_GROUPS
TOPK_GROUPS = 4
EXPERT_FF = 256
SHARED_FF = 256
ROUTED_SCALE = 2.5
ROPE_BASE = 10000.0
LN_EPS = 1e-5
NEG_INF = -1e30
ALPHA = 2.0 ** 0.25
HEAD_W = 512
LOG2E = 1.4426950408889634

LANES = 128
FOX_W = FOX_HEADS * LANES
SUBLANES = 8
ROW_TILES = D_MODEL // LANES
ROW_DTYPE = jnp.float32

TM_PROJ = 512
T_FOX = 512
FOX_CHUNK = 64
FOX_K_TERMS = 32
TT = 256
TT_DISPATCH = 512
EBLK = 512
ISSUE_UNROLL = 4
VMEM_LIMIT = 48 * 1024 * 1024

_F32 = jnp.float32
_BF16 = jnp.bfloat16


def _ln(x, g, b):
    xc = x - jnp.mean(x, -1, keepdims=True)
    var = jnp.mean(xc * xc, -1, keepdims=True)
    return xc * lax.rsqrt(var + LN_EPS) * g + b


def _dot(a, b):
    return jnp.dot(a, b, preferred_element_type=_F32)


def _dot_nt(a, b):
    return lax.dot_general(a, b, (((1,), (1,)), ((), ())), preferred_element_type=_F32)


def _dot_tn(a, b):
    return lax.dot_general(a, b, (((0,), (0,)), ((), ())), preferred_element_type=_F32)


def _silu(x):
    return x * jax.nn.sigmoid(x)


def _params(n_axes):
    return pltpu.CompilerParams(dimension_semantics=("arbitrary",) * n_axes, vmem_limit_bytes=VMEM_LIMIT)


def _inproj_body(x_ref, g_ref, b_ref, w_ref, bf_ref, cos_ref, sin_ref, own_ref, tq_ref, tk_ref, oq_ref, ok_ref, ov_ref,
                 h_ref, rq_ref, rk_ref, rv_ref, rg_ref, fq_ref, fk_ref, fv_ref, carry_ref, *, meta):
    tm = x_ref.shape[0]
    h = _ln(x_ref[...], g_ref[...], b_ref[...])
    if meta:
        valid = lax.broadcasted_iota(jnp.int32, (tm, 1), 0) >= PAD
        h = jnp.where(valid, h, 0.0)
    h_ref[...] = h
    hb = h.astype(_BF16)
    cos = cos_ref[...]
    sin = sin_ref[...]

    def proj(g):
        return _dot(hb, w_ref[:, g * HEAD_W:(g + 1) * HEAD_W])

    def rope_store(p, out_ref, scale):
        for hd in range(RET_HEADS):
            t = p[:, hd * LANES:(hd + 1) * LANES]
            r = t * cos + pltpu.roll(t, LANES // 2, axis=1) * sin
            out_ref[:, hd * LANES:(hd + 1) * LANES] = (r * scale).astype(_BF16)

    rope_store(proj(0), rq_ref, 1.0)
    rope_store(proj(1), rk_ref, RET_DK ** -0.5)
    rv_ref[...] = proj(2).astype(_BF16)
    rg_ref[...] = _silu(proj(3)).astype(_BF16)

    z = _dot(hb, w_ref[:, 7 * HEAD_W:7 * HEAD_W + LANES]) + bf_ref[...]
    logf = jnp.minimum(z, 0.0) - jnp.log1p(jnp.exp(-jnp.abs(z)))
    if meta:
        logf = jnp.where(valid, logf, 0.0)
    l1 = logf.astype(_BF16)
    r1 = logf - l1.astype(_F32)
    l2 = r1.astype(_BF16)
    l3 = (r1 - l2.astype(_F32)).astype(_BF16)
    row = lax.broadcasted_iota(jnp.int32, (tm, tm), 0)
    col = lax.broadcasted_iota(jnp.int32, (tm, tm), 1)
    tri = (col <= row).astype(_BF16)
    c = _dot(tri, l1) + _dot(tri, l2) + _dot(tri, l3)
    if meta:
        c = c - c[tm - 1:tm, :]
    else:
        @pl.when(pl.program_id(1) == 0)
        def _():
            carry_ref[...] = jnp.zeros_like(carry_ref)

        c = c + carry_ref[...]
        carry_ref[...] = c[tm - 1:tm, :]

    head_lane = lax.broadcasted_iota(jnp.int32, (1, LANES), 1) < FOX_HEADS
    cl = jnp.where(head_lane, c, 0.0) * LOG2E
    c1 = cl.astype(_BF16).astype(_F32)
    r1 = cl - c1
    c2 = r1.astype(_BF16).astype(_F32)
    c3 = (r1 - c2).astype(_BF16).astype(_F32)
    csplit = c1 + pltpu.roll(c2, FOX_HEADS, axis=1) + pltpu.roll(c3, 2 * FOX_HEADS, axis=1)
    half = FOX_HD
    q_even, q_odd = pltpu.roll(csplit, half, axis=1), csplit
    k_even, k_odd = pltpu.roll(csplit, half + FOX_K_TERMS, axis=1), pltpu.roll(csplit, FOX_K_TERMS, axis=1)
    own = own_ref[...] > 0.0

    def per_head(p):
        return jnp.concatenate([p[:, (hd // 2) * LANES:(hd // 2 + 1) * LANES] for hd in range(FOX_HEADS)], axis=1)

    def by_parity(even, odd):
        return jnp.concatenate([even, odd] * (FOX_HEADS // 2), axis=1)

    q_extra = jnp.where(tq_ref[...] > 0.0, by_parity(q_even, q_odd), oq_ref[...])
    k_extra = jnp.where(tk_ref[...] > 0.0, -by_parity(k_even, k_odd), ok_ref[...])
    fq_ref[...] = jnp.where(own, per_head(proj(4) * (FOX_HD ** -0.5 * LOG2E)), q_extra).astype(_BF16)
    fk_ref[...] = jnp.where(own, per_head(proj(5)), k_extra).astype(_BF16)
    fv_ref[...] = jnp.where(own, per_head(proj(6)), ov_ref[...]).astype(_BF16)


def _fox_lane_tables():
    own, tq, tk, oq, ok, ov = (np.zeros((1, FOX_W), np.float32) for _ in range(6))
    for hd in range(FOX_HEADS):
        data = hd * LANES + (hd % 2) * FOX_HD
        extra = hd * LANES + (1 - hd % 2) * FOX_HD
        own[0, data:data + FOX_HD] = 1.0
        for term in range(3):
            lane = extra + term * FOX_HEADS + hd
            tq[0, lane] = 1.0
            ok[0, lane] = 1.0
            tk[0, lane + FOX_K_TERMS] = 1.0
            oq[0, lane + FOX_K_TERMS] = 1.0
        ov[0, extra] = 1.0
    return tuple(jnp.asarray(t) for t in (own, tq, tk, oq, ok, ov))


def _inproj(x2d, ln_g, ln_b, w_all, bf_pad, cos_t, sin_t, *, nb, meta):
    n = x2d.shape[0]
    s = n // nb
    tm = min(TM_PROJ, s)
    nj = s // tm
    row_spec = lambda w: pl.BlockSpec((tm, w), lambda b, j: (b * nj + j, 0))
    const = lambda shape: pl.BlockSpec(shape, lambda b, j: (0, 0))
    pos_spec = pl.BlockSpec((tm, LANES), lambda b, j: (j, 0))
    tables = _fox_lane_tables()
    outs = ([jax.ShapeDtypeStruct((n, D_MODEL), _F32)] + [jax.ShapeDtypeStruct((n, HEAD_W), _BF16)] * 4
            + [jax.ShapeDtypeStruct((n, FOX_W), _BF16)] * 3)
    return pl.pallas_call(
        functools.partial(_inproj_body, meta=meta),
        grid=(nb, nj),
        in_specs=[row_spec(D_MODEL), const((1, D_MODEL)), const((1, D_MODEL)), const(w_all.shape),
                  const((1, LANES)), pos_spec, pos_spec] + [const(t.shape) for t in tables],
        out_specs=[row_spec(D_MODEL)] + [row_spec(HEAD_W)] * 4 + [row_spec(FOX_W)] * 3,
        out_shape=outs,
        scratch_shapes=[pltpu.VMEM((1, LANES), _F32)],
        compiler_params=_params(2),
        name="inproj_meta" if meta else "inproj",
    )(x2d, ln_g, ln_b, w_all, bf_pad, cos_t, sin_t, *tables)


def _ret_body(q_ref, k_ref, v_ref, g_ref, km_ref, vm_ref, dm_ref, xi_ref, zeta_ref, gch_ref, o_ref, st_ref):
    def kv_update(k, v, hd):
        vz = (v.astype(_F32) * zeta_ref[hd]).astype(_BF16)
        return _dot_tn(k, vz)

    nb = q_ref.shape[0]

    @pl.when(pl.program_id(0) == 0)
    def _():
        for hd in range(RET_HEADS):
            sl = slice(hd * LANES, (hd + 1) * LANES)
            first = kv_update(km_ref[:, sl], vm_ref[:, sl], hd)
            for b in range(nb):
                st_ref[b, hd] = first

    for b in range(nb):
        for hd in range(RET_HEADS):
            sl = slice(hd * LANES, (hd + 1) * LANES)
            q = q_ref[b, :, sl]
            k = k_ref[b, :, sl]
            v = v_ref[b, :, sl]
            st = st_ref[b, hd]
            scores = _dot_nt(q, k) * dm_ref[hd]
            o = _dot(scores.astype(_BF16), v) + _dot(q, st.astype(_BF16)) * xi_ref[hd]
            oc = o - jnp.mean(o, -1, keepdims=True)
            y = oc * lax.rsqrt(jnp.mean(oc * oc, -1, keepdims=True) + LN_EPS)
            o_ref[b, :, sl] = (y * g_ref[b, :, sl].astype(_F32)).astype(_BF16)
            st_ref[b, hd] = gch_ref[hd] * st + kv_update(k, v, hd)


def _retention(rq, rk, rv, rg, rk_m, rv_m, dmask, xi, zeta, gch):
    nb, s, _ = rq.shape
    row_spec = pl.BlockSpec((nb, BLOCK, HEAD_W), lambda j: (0, j, 0))
    meta_spec = pl.BlockSpec((BLOCK, HEAD_W), lambda j: (0, 0))
    tab = pl.BlockSpec((RET_HEADS, BLOCK, BLOCK), lambda j: (0, 0, 0))
    return pl.pallas_call(
        _ret_body,
        grid=(s // BLOCK,),
        in_specs=[row_spec] * 4 + [meta_spec] * 2 + [tab] * 3 + [pl.BlockSpec(memory_space=pltpu.SMEM)],
        out_specs=row_spec,
        out_shape=jax.ShapeDtypeStruct((nb, s, HEAD_W), _BF16),
        scratch_shapes=[pltpu.VMEM((nb, RET_HEADS, RET_DK, LANES), _F32)],
        compiler_params=_params(1),
        name="retention",
    )(rq, rk, rv, rg, rk_m, rv_m, dmask, xi, zeta, gch)


def _fox_body(q_ref, k_ref, v_ref, km_ref, vm_ref, o_ref, m_sc, acc_sc, s_sc, p_sc):
    t = q_ref.shape[0]
    qi = pl.program_id(2)

    heads = [slice(hh * LANES, (hh + 1) * LANES) for hh in range(2)]

    def logits(hh, k, buf, tk):
        s_sc[buf, hh, :, :tk] = _dot_nt(q_ref[:, heads[hh]], k)

    def update(hh, v, buf, tk, mask, first):
        for c in range(t // FOX_CHUNK):
            rows = slice(c * FOX_CHUNK, (c + 1) * FOX_CHUNK)
            s = s_sc[buf, hh, rows, :tk]
            if mask is not None:
                s = mask(s, c)
            mx = jnp.max(s, axis=1, keepdims=True)
            if first:
                m_new = mx
            else:
                m_prev = m_sc[hh, rows, :]
                m_new = jnp.maximum(m_prev, mx)
                acc_sc[hh, rows, :] = jnp.exp2(m_prev - m_new) * acc_sc[hh, rows, :]
            p_sc[hh, rows, :tk] = jnp.exp2(s - m_new).astype(_BF16)
            m_sc[hh, rows, :] = m_new
        pv = _dot(p_sc[hh, :, :tk], v)
        if first:
            acc_sc[hh] = pv
        else:
            acc_sc[hh] += pv

    def meta_mask(s, c):
        key = lax.broadcasted_iota(jnp.int32, s.shape, 1)
        return jnp.where(key >= PAD, s, NEG_INF)

    def causal_mask(s, c):
        key = lax.broadcasted_iota(jnp.int32, s.shape, 1)
        query = lax.broadcasted_iota(jnp.int32, s.shape, 0) + c * FOX_CHUNK
        return jnp.where(key <= query, s, NEG_INF)

    def key_tile(ref, ki, hh):
        return ref[pl.ds(pl.multiple_of(ki * t, t), t), heads[hh]]

    for hh in range(2):
        logits(hh, km_ref[:, heads[hh]], 1, BLOCK)
    for hh in range(2):
        logits(hh, key_tile(k_ref, 0, hh), 0, t)
        update(hh, vm_ref[:, heads[hh]], 1, BLOCK, meta_mask, True)

    def step(ki, buf, mask, more):
        for hh in range(2):
            if more:
                logits(hh, key_tile(k_ref, ki + 1, hh), 1 - buf, t)
            update(hh, key_tile(v_ref, ki, hh), buf, t, mask, False)

    def pair_body(j, carry):
        step(2 * j, 0, None, True)
        step(2 * j + 1, 1, None, True)
        return carry

    lax.fori_loop(0, lax.shift_right_logical(qi, 1), pair_body, 0)
    odd = lax.rem(qi, 2) == 1

    @pl.when(odd)
    def _():
        step(qi - 1, 0, None, True)
        step(qi, 1, causal_mask, False)

    @pl.when(jnp.logical_not(odd))
    def _():
        step(qi, 0, causal_mask, False)

    outs = []
    for hh in range(2):
        acc = acc_sc[hh]
        ones_lane = (1 - hh) * FOX_HD
        outs.append(acc / acc[:, ones_lane:ones_lane + 1])
    lane = lax.broadcasted_iota(jnp.int32, (t, LANES), 1)
    o_ref[...] = jnp.where(lane < FOX_HD, outs[0], outs[1]).astype(_BF16)


def _fox(fq, fk, fv, fk_m, fv_m, *, nb):
    s = fq.shape[1]
    t = T_FOX
    pair_w = 2 * LANES
    return pl.pallas_call(
        _fox_body,
        grid=(nb, FOX_HEADS // 2, s // t),
        in_specs=[
            pl.BlockSpec((None, t, pair_w), lambda b, p, i: (b, i, p)),
            pl.BlockSpec((None, s, pair_w), lambda b, p, i: (b, 0, p)),
            pl.BlockSpec((None, s, pair_w), lambda b, p, i: (b, 0, p)),
            pl.BlockSpec((BLOCK, pair_w), lambda b, p, i: (0, p)),
            pl.BlockSpec((BLOCK, pair_w), lambda b, p, i: (0, p)),
        ],
        out_specs=pl.BlockSpec((None, t, LANES), lambda b, p, i: (b, i, p)),
        out_shape=jax.ShapeDtypeStruct((nb, s, HEAD_W), _BF16),
        scratch_shapes=[pltpu.VMEM((2, t, 1), _F32), pltpu.VMEM((2, t, LANES), _F32), pltpu.VMEM((2, 2, t, t), _F32),
                        pltpu.VMEM((2, t, t), _BF16)],
        compiler_params=_params(3),
        name="fox",
    )(fq, fk, fv, fk_m, fv_m)


def _route(scores, biased):
    w = scores.shape[1]
    sub = lax.broadcasted_iota(jnp.int32, (GROUP_SIZE, w), 0).astype(_F32)
    groups = [biased[g * GROUP_SIZE:(g + 1) * GROUP_SIZE, :] for g in range(N_GROUPS)]
    gscore = []
    for v in groups:
        m1 = jnp.max(v, axis=0, keepdims=True)
        i1 = jnp.min(jnp.where(v == m1, sub, float(GROUP_SIZE)), axis=0, keepdims=True)
        m2 = jnp.max(jnp.where(sub == i1, -jnp.inf, v), axis=0, keepdims=True)
        gscore.append(m1 + m2)
    masked = []
    for g in range(N_GROUPS):
        beaten = jnp.zeros((1, w), _F32)
        for o in range(N_GROUPS):
            if o == g:
                continue
            wins = (gscore[o] >= gscore[g]) if o < g else (gscore[o] > gscore[g])
            beaten = beaten + wins.astype(_F32)
        masked.append(jnp.where(beaten < float(TOPK_GROUPS), groups[g], NEG_INF))
    work = jnp.concatenate(masked, axis=0)
    eid = lax.broadcasted_iota(jnp.int32, (N_EXPERTS, w), 0).astype(_F32)
    sels, raws = [], []
    member = None
    for _ in range(TOP_K):
        m = jnp.max(work, axis=0, keepdims=True)
        idx = jnp.min(jnp.where(work == m, eid, float(N_EXPERTS)), axis=0, keepdims=True)
        hot = eid == idx
        sels.append(idx)
        raws.append(jnp.sum(jnp.where(hot, scores, 0.0), axis=0, keepdims=True))
        work = jnp.where(hot, -jnp.inf, work)
        member = hot if member is None else (member | hot)
    return sels, raws, member


def _post_body(h0_ref, ret_ref, fox_ref, wo_ref, g1_ref, b1_ref, wr_ref, rb_ref,
               trow_ref, sel_ref, gate_ref, rank_ref, cnt_ref, run_ref):
    tm = h0_ref.shape[0]
    i = pl.program_id(0)

    @pl.when(i == 0)
    def _():
        run_ref[...] = jnp.zeros_like(run_ref)

    y = _dot(ret_ref[...], wo_ref[:HEAD_W, :]) + _dot(fox_ref[...], wo_ref[HEAD_W:, :])
    h1 = _ln(ALPHA * h0_ref[...] + y, g1_ref[...], b1_ref[...])
    _store_rows(trow_ref, h1)

    scores = jax.nn.sigmoid(_dot_nt(wr_ref[...], h1.astype(_BF16)))
    biased = scores + rb_ref[...]
    chunks = [slice(c * LANES, (c + 1) * LANES) for c in range(tm // LANES)]
    routed = [_route(scores[:, cs], biased[:, cs]) for cs in chunks]
    member_f = jnp.concatenate([member.astype(_F32) for _, _, member in routed], axis=1)
    row = lax.broadcasted_iota(jnp.int32, (tm, tm), 0)
    col = lax.broadcasted_iota(jnp.int32, (tm, tm), 1)
    before = (row < col).astype(_BF16)
    rank_e = _dot(member_f.astype(_BF16), before) + run_ref[...]
    eid = lax.broadcasted_iota(jnp.int32, (N_EXPERTS, LANES), 0).astype(_F32)
    sel_rows, gate_rows, rank_rows = [], [], []
    for k in range(TOP_K):
        sel_k, gate_k, rank_k = [], [], []
        for cs, (sels, raws, _) in zip(chunks, routed):
            total = raws[0]
            for r in raws[1:]:
                total = total + r
            sel_k.append(sels[k].astype(jnp.int32))
            gate_k.append(raws[k] * (ROUTED_SCALE / total))
            rank_k.append(jnp.sum(jnp.where(eid == sels[k], rank_e[:, cs], 0.0), axis=0, keepdims=True).astype(jnp.int32))
        sel_rows.append(jnp.concatenate(sel_k, axis=1))
        gate_rows.append(jnp.concatenate(gate_k, axis=1))
        rank_rows.append(jnp.concatenate(rank_k, axis=1))
    pad_rows = sel_ref.shape[0] - TOP_K
    sel_ref[...] = jnp.concatenate(sel_rows + [jnp.zeros((pad_rows, tm), jnp.int32)], axis=0)
    gate_ref[...] = jnp.concatenate(gate_rows + [jnp.zeros((pad_rows, tm), _F32)], axis=0)
    rank_ref[...] = jnp.concatenate(rank_rows + [jnp.zeros((pad_rows, tm), jnp.int32)], axis=0)
    run_ref[...] = run_ref[...] + jnp.sum(member_f, axis=1, keepdims=True)
    cnt_ref[...] = run_ref[...]


def _post(h0, ret, fox, w_out, ln1_g, ln1_b, w_rt, rbias):
    n = h0.shape[0]
    tm = TM_PROJ
    row_spec = lambda w: pl.BlockSpec((tm, w), lambda i: (i, 0))
    col_spec = pl.BlockSpec((SUBLANES, tm), lambda i: (0, i))
    const = lambda shape: pl.BlockSpec(shape, lambda i: (0, 0))
    return pl.pallas_call(
        _post_body,
        grid=(n // tm,),
        in_specs=[row_spec(D_MODEL), row_spec(HEAD_W), row_spec(HEAD_W),
                  const(w_out.shape), const((1, D_MODEL)), const((1, D_MODEL)), const(w_rt.shape),
                  const((N_EXPERTS, 1))],
        out_specs=[pl.BlockSpec((tm * ROW_TILES, LANES), lambda i: (i, 0)),
                   col_spec, col_spec, col_spec, const((N_EXPERTS, 1))],
        out_shape=[jax.ShapeDtypeStruct((n * ROW_TILES, LANES), ROW_DTYPE),
                   jax.ShapeDtypeStruct((SUBLANES, n), jnp.int32), jax.ShapeDtypeStruct((SUBLANES, n), _F32),
                   jax.ShapeDtypeStruct((SUBLANES, n), jnp.int32), jax.ShapeDtypeStruct((N_EXPERTS, 1), _F32)],
        scratch_shapes=[pltpu.VMEM((N_EXPERTS, 1), _F32)],
        compiler_params=_params(1),
        name="post_mixer",
    )(h0, ret, fox, w_out, ln1_g, ln1_b, w_rt, rbias)


def _plan_body(sel_ref, rank_ref, cnt_ref, dest_ref, blk_ref, fill_ref, used_ref, pad0_ref, padn_ref):
    cnt = cnt_ref[...]
    padded = jnp.ceil(cnt * (1.0 / EBLK)) * EBLK
    er = lax.broadcasted_iota(jnp.int32, (N_EXPERTS, N_EXPERTS), 0)
    ec = lax.broadcasted_iota(jnp.int32, (N_EXPERTS, N_EXPERTS), 1)
    padded_row = jnp.sum(jnp.where(er == ec, padded, 0.0), axis=0, keepdims=True)
    pstart = jnp.sum(jnp.where(ec < er, padded_row, 0.0), axis=1, keepdims=True)
    pend = pstart + padded
    sel = sel_ref[...]
    dest = rank_ref[...]
    for e in range(N_EXPERTS):
        dest = dest + jnp.where(sel == e, pstart[e:e + 1, :].astype(jnp.int32), 0)
    dest_ref[...] = dest
    nblk = blk_ref.shape[1]
    first_row = (lax.broadcasted_iota(jnp.int32, (N_EXPERTS, nblk), 1) * EBLK).astype(_F32)
    owner = jnp.minimum(jnp.sum((pend <= first_row).astype(_F32), axis=0, keepdims=True), N_EXPERTS - 1.0)
    blk_ref[...] = owner.astype(jnp.int32)
    mine = lax.broadcasted_iota(jnp.int32, (N_EXPERTS, nblk), 0).astype(_F32) == owner
    live_end = jnp.sum(jnp.where(mine, pstart + cnt, 0.0), axis=0, keepdims=True)
    fill_ref[...] = jnp.clip(live_end - first_row[:1, :], 0.0, float(EBLK)).astype(jnp.int32)
    used_ref[...] = (pend[N_EXPERTS - 1:, :] * (1.0 / EBLK)).astype(jnp.int32)
    as_row = lambda col: jnp.sum(jnp.where(er == ec, col, 0.0), axis=0, keepdims=True).astype(jnp.int32)
    pad0_ref[...] = as_row(pstart + cnt)
    padn_ref[...] = as_row(padded - cnt)


def _plan(sel, rank, cnt, nblk_pad):
    n = sel.shape[1]
    full = lambda shape: pl.BlockSpec(shape, lambda i: (0, 0))
    return pl.pallas_call(
        _plan_body,
        grid=(1,),
        in_specs=[full(sel.shape), full(rank.shape), full(cnt.shape)],
        out_specs=[full(sel.shape), full((1, nblk_pad)), full((1, nblk_pad)), full((1, 1)), full((1, N_EXPERTS)),
                   full((1, N_EXPERTS))],
        out_shape=[jax.ShapeDtypeStruct((SUBLANES, n), jnp.int32), jax.ShapeDtypeStruct((1, nblk_pad), jnp.int32),
                   jax.ShapeDtypeStruct((1, nblk_pad), jnp.int32), jax.ShapeDtypeStruct((1, 1), jnp.int32),
                   jax.ShapeDtypeStruct((1, N_EXPERTS), jnp.int32), jax.ShapeDtypeStruct((1, N_EXPERTS), jnp.int32)],
        compiler_params=_params(1),
        name="plan",
    )(sel, rank, cnt)


def _store_rows(ref, v):
    m = v.shape[0]
    for s in range(ROW_TILES):
        ref[pl.ds(s, m, stride=ROW_TILES), :] = v[:, s * LANES:(s + 1) * LANES].astype(ROW_DTYPE)


def _load_rows(ref, first_row, m):
    return jnp.concatenate([ref[pl.ds(first_row * ROW_TILES + s, m, stride=ROW_TILES), :] for s in range(ROW_TILES)],
                           axis=1)


def _row_copy(src, src_row, dst, dst_row, sem):
    return pltpu.make_async_copy(src.at[pl.ds(pl.multiple_of(src_row * ROW_TILES, ROW_TILES), ROW_TILES), :],
                                 dst.at[pl.ds(pl.multiple_of(dst_row * ROW_TILES, ROW_TILES), ROW_TILES), :], sem)


def _dispatch_body(dest_ref, pad0_ref, padn_ref, used_ref, t_ref, wgu_ref, wd_ref, xs_ref, base_ref, zero_sc, sem,
                   zsem):
    tt = t_ref.shape[0] // ROW_TILES
    half_blk = EBLK // 2
    n_half = xs_ref.shape[0] // (half_blk * ROW_TILES)

    @pl.when(pl.program_id(0) == 0)
    def _():
        zero_sc[...] = jnp.zeros_like(zero_sc)

        def zero_copy(first_row, rows):
            return pltpu.make_async_copy(
                zero_sc.at[pl.ds(0, rows * ROW_TILES), :],
                xs_ref.at[pl.ds(pl.multiple_of(first_row * ROW_TILES, ROW_TILES), rows * ROW_TILES), :], zsem)

        def for_padding(act):
            def per_expert(e, carry):
                row = pad0_ref[e]
                for bit in range(EBLK.bit_length() - 2, -1, -1):
                    take = (padn_ref[e] & (1 << bit)) != 0

                    @pl.when(take)
                    def _():
                        act(zero_copy(row, 1 << bit))

                    row = row + jnp.where(take, 1 << bit, 0)
                return carry

            def per_tail(hb, carry):
                act(zero_copy(hb * half_blk, half_blk))
                return carry

            lax.fori_loop(0, N_EXPERTS, per_expert, 0)
            lax.fori_loop(2 * used_ref[0], n_half, per_tail, 0)

        for_padding(lambda cp: cp.start())
        for_padding(lambda cp: cp.wait())

    def issue(j, carry):
        for u in range(ISSUE_UNROLL):
            i = ISSUE_UNROLL * j + u
            for k in range(TOP_K):
                _row_copy(t_ref, i, xs_ref, dest_ref[i * SUBLANES + k], sem).start(priority=k % 2)
        return carry

    lax.fori_loop(0, tt // ISSUE_UNROLL, issue, 0)

    h1 = _load_rows(t_ref, 0, tt)
    gu = _dot(h1.astype(_BF16), wgu_ref[...])
    mid = (_silu(gu[:, :SHARED_FF]) * gu[:, SHARED_FF:]).astype(_BF16)
    base_ref[...] = ALPHA * h1 + _dot(mid, wd_ref[...])

    for _ in range(TOP_K):
        pltpu.make_async_copy(t_ref, xs_ref.at[pl.ds(0, tt * ROW_TILES), :], sem).wait()


def _dispatch(dest_flat, pad0, padn, used, trow, w_gu, w_sd, total_rows):
    n = trow.shape[0] // ROW_TILES
    tt = TT_DISPATCH
    smem = pl.BlockSpec(memory_space=pltpu.SMEM)
    const = lambda shape: pl.BlockSpec(shape, lambda i: (0, 0))
    return pl.pallas_call(
        _dispatch_body,
        grid=(n // tt,),
        in_specs=[pl.BlockSpec((tt * SUBLANES,), lambda i: (i,), memory_space=pltpu.SMEM), smem, smem, smem,
                  pl.BlockSpec((tt * ROW_TILES, LANES), lambda i: (i, 0)), const(w_gu.shape), const(w_sd.shape)],
        out_specs=[pl.BlockSpec(memory_space=pl.ANY), pl.BlockSpec((tt, D_MODEL), lambda i: (i, 0))],
        out_shape=[jax.ShapeDtypeStruct((total_rows * ROW_TILES, LANES), ROW_DTYPE),
                   jax.ShapeDtypeStruct((n, D_MODEL), _F32)],
        scratch_shapes=[pltpu.VMEM((EBLK // 2 * ROW_TILES, LANES), ROW_DTYPE), pltpu.SemaphoreType.DMA(()),
                        pltpu.SemaphoreType.DMA(())],
        compiler_params=_params(1),
        name="dispatch",
    )(dest_flat, pad0, padn, used, trow, w_gu, w_sd)


def _expert_body(blk_ref, used_ref, fill_ref, xs_ref, wg_ref, wu_ref, wd_ref, y_ref, wgu_sc, wd_sc):
    i = pl.program_id(0)
    prev = blk_ref[jnp.maximum(i - 1, 0)]
    fresh = (i == 0) | (blk_ref[i] != prev)

    @pl.when(fresh)
    def _():
        wgu_sc[:, :EXPERT_FF] = wg_ref[...].astype(_BF16)
        wgu_sc[:, EXPERT_FF:] = wu_ref[...].astype(_BF16)
        wd_sc[...] = wd_ref[...].astype(_BF16)

    @pl.when(i < used_ref[0])
    def _():
        live = lax.broadcasted_iota(jnp.int32, (EBLK, 1), 0) < fill_ref[i]
        x = _load_rows(xs_ref, 0, EBLK)
        x = jnp.where(live, x, jnp.zeros_like(x)).astype(_BF16)
        gu = _dot(x, wgu_sc[...])
        mid = (_silu(gu[:, :EXPERT_FF]) * gu[:, EXPERT_FF:]).astype(_BF16)
        _store_rows(y_ref, _dot(mid, wd_sc[...]))

    @pl.when(i >= used_ref[0])
    def _():
        y_ref[...] = jnp.zeros_like(y_ref)


def _experts(blk_e, used, fill, xs, we_gate, we_up, we_down):
    nblk = xs.shape[0] // (EBLK * ROW_TILES)
    last = lambda i, used: jnp.minimum(i, jnp.maximum(used[0] - 1, 0))
    w_spec = lambda shape: pl.BlockSpec((None,) + shape, lambda i, blk, used, fill: (blk[i], 0, 0))
    return pl.pallas_call(
        _expert_body,
        grid_spec=pltpu.PrefetchScalarGridSpec(
            num_scalar_prefetch=3,
            grid=(nblk,),
            in_specs=[pl.BlockSpec((EBLK * ROW_TILES, LANES), lambda i, blk, used, fill: (last(i, used), 0)),
                      w_spec((D_MODEL, EXPERT_FF)), w_spec((D_MODEL, EXPERT_FF)), w_spec((EXPERT_FF, D_MODEL))],
            out_specs=pl.BlockSpec((EBLK * ROW_TILES, LANES), lambda i, blk, used, fill: (i, 0)),
            scratch_shapes=[pltpu.VMEM((D_MODEL, 2 * EXPERT_FF), _BF16), pltpu.VMEM((EXPERT_FF, D_MODEL), _BF16)],
        ),
        out_shape=jax.ShapeDtypeStruct(xs.shape, ROW_DTYPE),
        compiler_params=_params(1),
        name="experts",
    )(blk_e, used, fill, xs, we_gate, we_up, we_down)


def _combine_body(dest_ref, dnext_ref, y_ref, base_ref, gate_ref, g2_ref, b2_ref, o_ref, z_sc, sems):
    i = pl.program_id(0)
    slot = lax.rem(i, 2)
    tile_rows = TT * TOP_K * ROW_TILES

    def gather(dref, into):
        def issue(j, carry):
            for u in range(ISSUE_UNROLL):
                t = ISSUE_UNROLL * j + u
                for k in range(TOP_K):
                    _row_copy(y_ref, dref[t * SUBLANES + k], z_sc.at[into], k * TT + t,
                              sems.at[into]).start(priority=k % 2)
            return carry

        lax.fori_loop(0, TT // ISSUE_UNROLL, issue, 0)

    @pl.when(i == 0)
    def _():
        gather(dest_ref, 0)

    @pl.when(i + 1 < pl.num_programs(0))
    def _():
        gather(dnext_ref, 1 - slot)

    pltpu.make_async_copy(y_ref.at[pl.ds(0, tile_rows), :], z_sc.at[slot], sems.at[slot]).wait()
    gates = gate_ref[...]
    acc = base_ref[...]
    for k in range(TOP_K):
        acc = acc + gates[:, k:k + 1] * _load_rows(z_sc.at[slot], k * TT, TT).astype(_F32)
    o_ref[...] = _ln(acc, g2_ref[...], b2_ref[...])


def _combine(dest_flat, y, base, gates_t, ln2_g, ln2_b):
    n = base.shape[0]
    steps = n // TT
    const = lambda shape: pl.BlockSpec(shape, lambda i: (0, 0))
    dest_spec = lambda ahead: pl.BlockSpec((TT * SUBLANES,), lambda i: (jnp.minimum(i + ahead, steps - 1),),
                                           memory_space=pltpu.SMEM)
    return pl.pallas_call(
        _combine_body,
        grid=(steps,),
        in_specs=[dest_spec(0), dest_spec(1),
                  pl.BlockSpec(memory_space=pl.ANY),
                  pl.BlockSpec((TT, D_MODEL), lambda i: (i, 0)),
                  pl.BlockSpec((TT, SUBLANES), lambda i: (i, 0)),
                  const((1, D_MODEL)), const((1, D_MODEL))],
        out_specs=pl.BlockSpec((TT, D_MODEL), lambda i: (i, 0)),
        out_shape=jax.ShapeDtypeStruct((n, D_MODEL), _F32),
        scratch_shapes=[pltpu.VMEM((2, TT * TOP_K * ROW_TILES, LANES), ROW_DTYPE), pltpu.SemaphoreType.DMA((2,))],
        compiler_params=_params(1),
        name="combine",
    )(dest_flat, dest_flat, y, base, gates_t, ln2_g, ln2_b)


def _rope_tables(pos):
    half = RET_DK // 2
    inv = ROPE_BASE ** (-jnp.arange(half, dtype=_F32) / half)
    ang = pos[:, None] * inv[None, :]
    cos = jnp.cos(ang)
    sin = jnp.sin(ang)
    return jnp.concatenate([cos, cos], -1), jnp.concatenate([-sin, sin], -1)


def _decay_tables():
    lg = jnp.log1p(-jnp.exp2(-5.0 - jnp.arange(RET_HEADS, dtype=_F32)))
    idx = jnp.arange(BLOCK, dtype=_F32)
    rel = idx[:, None] - idx[None, :]
    causal = rel >= 0
    dmask = jnp.where(causal[None], jnp.exp(jnp.where(causal, rel, 0.0)[None] * lg[:, None, None]), 0.0)
    zeta = jnp.exp((BLOCK - 1.0 - idx)[None, :] * lg[:, None])
    xi = jnp.exp((idx + 1.0)[None, :] * lg[:, None])
    along_lanes = lambda col: jnp.broadcast_to(col[:, :, None], (RET_HEADS, BLOCK, LANES))
    return dmask, along_lanes(xi), along_lanes(zeta), jnp.exp(BLOCK * lg)


def kernel(x, meta, ln0_g, ln0_b, w_in, b_forget, w_out, ln1_g, ln1_b, w_router, router_bias, we_gate, we_up,
           we_down, ws_gate, ws_up, ws_down, ln2_g, ln2_b):
    nb, s, d = x.shape
    assert d == D_MODEL and meta.shape == (N_META, D_MODEL) and w_in.shape[0] == 1
    assert s % TM_PROJ == 0 and s % T_FOX == 0 and (nb * s) % TT == 0
    n = nb * s
    x2d = x.reshape(n, d)
    row2 = lambda v: v.reshape(1, -1).astype(_F32)
    main_cols = 7 * HEAD_W
    w_all = jnp.concatenate(
        [w_in[0, :, :main_cols], w_in[0, :, main_cols:], jnp.zeros((d, LANES - FOX_HEADS), w_in.dtype)],
        axis=1).astype(_BF16)
    bf_pad = jnp.concatenate([b_forget[0].astype(_F32), jnp.zeros((LANES - FOX_HEADS,), _F32)]).reshape(1, LANES)
    g0, b0 = row2(ln0_g), row2(ln0_b)

    cos_x, sin_x = _rope_tables(jnp.arange(s, dtype=_F32) + float(N_META))
    cos_m, sin_m = _rope_tables(jnp.arange(BLOCK, dtype=_F32) - float(PAD))
    meta_blk = jnp.concatenate([jnp.zeros((PAD, d), _F32), meta.astype(_F32)], axis=0)

    h0, rq, rk, rv, rg, fq, fk, fv = _inproj(x2d, g0, b0, w_all, bf_pad, cos_x, sin_x, nb=nb, meta=False)
    _, _, rk_m, rv_m, _, _, fk_m, fv_m = _inproj(meta_blk, g0, b0, w_all, bf_pad, cos_m, sin_m, nb=1, meta=True)

    dmask, xi, zeta, gch = _decay_tables()
    per_batch = lambda a: a.reshape(nb, s, a.shape[-1])
    ret = _retention(per_batch(rq), per_batch(rk), per_batch(rv), per_batch(rg), rk_m, rv_m, dmask, xi, zeta,
                     gch).reshape(n, HEAD_W)
    fox = _fox(per_batch(fq), per_batch(fk), per_batch(fv), fk_m, fv_m, nb=nb).reshape(n, HEAD_W)

    w_gu = jnp.concatenate([ws_gate[0], ws_up[0]], axis=1).astype(_BF16)
    trow, sel, gates, rank, cnt = _post(
        h0, ret, fox, w_out[0].astype(_BF16), row2(ln1_g[0]), row2(ln1_b[0]),
        jnp.transpose(w_router[0]).astype(_BF16), router_bias[0].astype(_F32).reshape(N_EXPERTS, 1))

    nblk = n * TOP_K // EBLK + N_EXPERTS
    nblk_pad = -(-nblk // LANES) * LANES
    dest, blk_e, fill, used, pad0, padn = _plan(sel, rank, cnt, nblk_pad)
    dest_flat = jnp.transpose(dest).reshape(-1)
    used = used.reshape(-1)

    xs, base = _dispatch(dest_flat, pad0.reshape(-1), padn.reshape(-1), used, trow, w_gu, ws_down[0].astype(_BF16),
                         nblk * EBLK)
    y = _experts(blk_e.reshape(-1), used, fill.reshape(-1), xs, we_gate[0], we_up[0], we_down[0])
    out = _combine(dest_flat, y, base, jnp.transpose(gates), row2(ln2_g[0]), row2(ln2_b[0]))
    return out.reshape(nb, s, d)
```

```python
import functools

import jax
import jax.numpy as jnp
import numpy as np
from jax import lax
from jax.experimental import pallas as pl
from jax.experimental.pallas import tpu as pltpu

D_MODEL = 1024
N_META = 16
BLOCK = 128
PAD = BLOCK - N_META
RET_HEADS = 4
RET_DK = 128
FOX_HEADS = 8
FOX_HD = 64
N_EXPERTS = 64
TOP_K = 6
N_GROUPS = 8
GROUP_SIZE = N_EXPERTS // N_GROUPS
TOPK_GROUPS = 4
EXPERT_FF = 256
SHARED_FF = 256
ROUTED_SCALE = 2.5
ROPE_BASE = 10000.0
LN_EPS = 1e-5
NEG_INF = -1e30
ALPHA = 2.0 ** 0.25
HEAD_W = 512
LOG2E = 1.4426950408889634

LANES = 128
FOX_W = FOX_HEADS * LANES
SUBLANES = 8
ROW_TILES = D_MODEL // LANES
ROW_DTYPE = jnp.float32

TM_PROJ = 512
T_FOX = 512
FOX_CHUNK = 64
FOX_K_TERMS = 32
TT = 256
TT_DISPATCH = 512
EBLK = 512
ISSUE_UNROLL = 4
VMEM_LIMIT = 48 * 1024 * 1024

_F32 = jnp.float32
_BF16 = jnp.bfloat16


def _ln(x, g, b):
    xc = x - jnp.mean(x, -1, keepdims=True)
    var = jnp.mean(xc * xc, -1, keepdims=True)
    return xc * lax.rsqrt(var + LN_EPS) * g + b


def _dot(a, b):
    return jnp.dot(a, b, preferred_element_type=_F32)


def _dot_nt(a, b):
    return lax.dot_general(a, b, (((1,), (1,)), ((), ())), preferred_element_type=_F32)


def _dot_tn(a, b):
    return lax.dot_general(a, b, (((0,), (0,)), ((), ())), preferred_element_type=_F32)


def _silu(x):
    return x * jax.nn.sigmoid(x)


def _params(n_axes):
    return pltpu.CompilerParams(dimension_semantics=("arbitrary",) * n_axes, vmem_limit_bytes=VMEM_LIMIT)


def _inproj_body(x_ref, g_ref, b_ref, w_ref, bf_ref, cos_ref, sin_ref, own_ref, tq_ref, tk_ref, oq_ref, ok_ref, ov_ref,
                 h_ref, rq_ref, rk_ref, rv_ref, rg_ref, fq_ref, fk_ref, fv_ref, carry_ref, *, meta):
    tm = x_ref.shape[0]
    h = _ln(x_ref[...], g_ref[...], b_ref[...])
    if meta:
        valid = lax.broadcasted_iota(jnp.int32, (tm, 1), 0) >= PAD
        h = jnp.where(valid, h, 0.0)
    h_ref[...] = h
    hb = h.astype(_BF16)
    cos = cos_ref[...]
    sin = sin_ref[...]

    def proj(g):
        return _dot(hb, w_ref[:, g * HEAD_W:(g + 1) * HEAD_W])

    def rope_store(p, out_ref, scale):
        for hd in range(RET_HEADS):
            t = p[:, hd * LANES:(hd + 1) * LANES]
            r = t * cos + pltpu.roll(t, LANES // 2, axis=1) * sin
            out_ref[:, hd * LANES:(hd + 1) * LANES] = (r * scale).astype(_BF16)

    rope_store(proj(0), rq_ref, 1.0)
    rope_store(proj(1), rk_ref, RET_DK ** -0.5)
    rv_ref[...] = proj(2).astype(_BF16)
    rg_ref[...] = _silu(proj(3)).astype(_BF16)

    z = _dot(hb, w_ref[:, 7 * HEAD_W:7 * HEAD_W + LANES]) + bf_ref[...]
    logf = jnp.minimum(z, 0.0) - jnp.log1p(jnp.exp(-jnp.abs(z)))
    if meta:
        logf = jnp.where(valid, logf, 0.0)
    l1 = logf.astype(_BF16)
    r1 = logf - l1.astype(_F32)
    l2 = r1.astype(_BF16)
    l3 = (r1 - l2.astype(_F32)).astype(_BF16)
    row = lax.broadcasted_iota(jnp.int32, (tm, tm), 0)
    col = lax.broadcasted_iota(jnp.int32, (tm, tm), 1)
    tri = (col <= row).astype(_BF16)
    c = _dot(tri, l1) + _dot(tri, l2) + _dot(tri, l3)
    if meta:
        c = c - c[tm - 1:tm, :]
    else:
        @pl.when(pl.program_id(1) == 0)
        def _():
            carry_ref[...] = jnp.zeros_like(carry_ref)

        c = c + carry_ref[...]
        carry_ref[...] = c[tm - 1:tm, :]

    head_lane = lax.broadcasted_iota(jnp.int32, (1, LANES), 1) < FOX_HEADS
    cl = jnp.where(head_lane, c, 0.0) * LOG2E
    c1 = cl.astype(_BF16).astype(_F32)
    r1 = cl - c1
    c2 = r1.astype(_BF16).astype(_F32)
    c3 = (r1 - c2).astype(_BF16).astype(_F32)
    csplit = c1 + pltpu.roll(c2, FOX_HEADS, axis=1) + pltpu.roll(c3, 2 * FOX_HEADS, axis=1)
    half = FOX_HD
    q_even, q_odd = pltpu.roll(csplit, half, axis=1), csplit
    k_even, k_odd = pltpu.roll(csplit, half + FOX_K_TERMS, axis=1), pltpu.roll(csplit, FOX_K_TERMS, axis=1)
    own = own_ref[...] > 0.0

    def per_head(p):
        return jnp.concatenate([p[:, (hd // 2) * LANES:(hd // 2 + 1) * LANES] for hd in range(FOX_HEADS)], axis=1)

    def by_parity(even, odd):
        return jnp.concatenate([even, odd] * (FOX_HEADS // 2), axis=1)

    q_extra = jnp.where(tq_ref[...] > 0.0, by_parity(q_even, q_odd), oq_ref[...])
    k_extra = jnp.where(tk_ref[...] > 0.0, -by_parity(k_even, k_odd), ok_ref[...])
    fq_ref[...] = jnp.where(own, per_head(proj(4) * (FOX_HD ** -0.5 * LOG2E)), q_extra).astype(_BF16)
    fk_ref[...] = jnp.where(own, per_head(proj(5)), k_extra).astype(_BF16)
    fv_ref[...] = jnp.where(own, per_head(proj(6)), ov_ref[...]).astype(_BF16)


def _fox_lane_tables():
    own, tq, tk, oq, ok, ov = (np.zeros((1, FOX_W), np.float32) for _ in range(6))
    for hd in range(FOX_HEADS):
        data = hd * LANES + (hd % 2) * FOX_HD
        extra = hd * LANES + (1 - hd % 2) * FOX_HD
        own[0, data:data + FOX_HD] = 1.0
        for term in range(3):
            lane = extra + term * FOX_HEADS + hd
            tq[0, lane] = 1.0
            ok[0, lane] = 1.0
            tk[0, lane + FOX_K_TERMS] = 1.0
            oq[0, lane + FOX_K_TERMS] = 1.0
        ov[0, extra] = 1.0
    return tuple(jnp.asarray(t) for t in (own, tq, tk, oq, ok, ov))


def _inproj(x2d, ln_g, ln_b, w_all, bf_pad, cos_t, sin_t, *, nb, meta):
    n = x2d.shape[0]
    s = n // nb
    tm = min(TM_PROJ, s)
    nj = s // tm
    row_spec = lambda w: pl.BlockSpec((tm, w), lambda b, j: (b * nj + j, 0))
    const = lambda shape: pl.BlockSpec(shape, lambda b, j: (0, 0))
    pos_spec = pl.BlockSpec((tm, LANES), lambda b, j: (j, 0))
    tables = _fox_lane_tables()
    outs = ([jax.ShapeDtypeStruct((n, D_MODEL), _F32)] + [jax.ShapeDtypeStruct((n, HEAD_W), _BF16)] * 4
            + [jax.ShapeDtypeStruct((n, FOX_W), _BF16)] * 3)
    return pl.pallas_call(
        functools.partial(_inproj_body, meta=meta),
        grid=(nb, nj),
        in_specs=[row_spec(D_MODEL), const((1, D_MODEL)), const((1, D_MODEL)), const(w_all.shape),
                  const((1, LANES)), pos_spec, pos_spec] + [const(t.shape) for t in tables],
        out_specs=[row_spec(D_MODEL)] + [row_spec(HEAD_W)] * 4 + [row_spec(FOX_W)] * 3,
        out_shape=outs,
        scratch_shapes=[pltpu.VMEM((1, LANES), _F32)],
        compiler_params=_params(2),
        name="inproj_meta" if meta else "inproj",
    )(x2d, ln_g, ln_b, w_all, bf_pad, cos_t, sin_t, *tables)


def _ret_body(q_ref, k_ref, v_ref, g_ref, km_ref, vm_ref, dm_ref, xi_ref, zeta_ref, gch_ref, o_ref, st_ref):
    def kv_update(k, v, hd):
        vz = (v.astype(_F32) * zeta_ref[hd]).astype(_BF16)
        return _dot_tn(k, vz)

    nb = q_ref.shape[0]

    @pl.when(pl.program_id(0) == 0)
    def _():
        for hd in range(RET_HEADS):
            sl = slice(hd * LANES, (hd + 1) * LANES)
            first = kv_update(km_ref[:, sl], vm_ref[:, sl], hd)
            for b in range(nb):
                st_ref[b, hd] = first

    for b in range(nb):
        for hd in range(RET_HEADS):
            sl = slice(hd * LANES, (hd + 1) * LANES)
            q = q_ref[b, :, sl]
            k = k_ref[b, :, sl]
            v = v_ref[b, :, sl]
            st = st_ref[b, hd]
            scores = _dot_nt(q, k) * dm_ref[hd]
            o = _dot(scores.astype(_BF16), v) + _dot(q, st.astype(_BF16)) * xi_ref[hd]
            oc = o - jnp.mean(o, -1, keepdims=True)
            y = oc * lax.rsqrt(jnp.mean(oc * oc, -1, keepdims=True) + LN_EPS)
            o_ref[b, :, sl] = (y * g_ref[b, :, sl].astype(_F32)).astype(_BF16)
            st_ref[b, hd] = gch_ref[hd] * st + kv_update(k, v, hd)


def _retention(rq, rk, rv, rg, rk_m, rv_m, dmask, xi, zeta, gch):
    nb, s, _ = rq.shape
    row_spec = pl.BlockSpec((nb, BLOCK, HEAD_W), lambda j: (0, j, 0))
    meta_spec = pl.BlockSpec((BLOCK, HEAD_W), lambda j: (0, 0))
    tab = pl.BlockSpec((RET_HEADS, BLOCK, BLOCK), lambda j: (0, 0, 0))
    return pl.pallas_call(
        _ret_body,
        grid=(s // BLOCK,),
        in_specs=[row_spec] * 4 + [meta_spec] * 2 + [tab] * 3 + [pl.BlockSpec(memory_space=pltpu.SMEM)],
        out_specs=row_spec,
        out_shape=jax.ShapeDtypeStruct((nb, s, HEAD_W), _BF16),
        scratch_shapes=[pltpu.VMEM((nb, RET_HEADS, RET_DK, LANES), _F32)],
        compiler_params=_params(1),
        name="retention",
    )(rq, rk, rv, rg, rk_m, rv_m, dmask, xi, zeta, gch)


def _fox_body(q_ref, k_ref, v_ref, km_ref, vm_ref, o_ref, m_sc, acc_sc, s_sc, p_sc):
    t = q_ref.shape[0]
    qi = pl.program_id(2)

    heads = [slice(hh * LANES, (hh + 1) * LANES) for hh in range(2)]

    def logits(hh, k, buf, tk):
        s_sc[buf, hh, :, :tk] = _dot_nt(q_ref[:, heads[hh]], k)

    def update(hh, v, buf, tk, mask, first):
        for c in range(t // FOX_CHUNK):
            rows = slice(c * FOX_CHUNK, (c + 1) * FOX_CHUNK)
            live = min(tk, -(-(c + 1) * FOX_CHUNK // LANES) * LANES) if mask is causal_mask else tk
            s = s_sc[buf, hh, rows, :live]
            if mask is not None:
                s = mask(s, c)
            mx = jnp.max(s, axis=1, keepdims=True)
            if first:
                m_new = mx
            else:
                m_prev = m_sc[hh, rows, :]
                m_new = jnp.maximum(m_prev, mx)
                acc_sc[hh, rows, :] = jnp.exp2(m_prev - m_new) * acc_sc[hh, rows, :]
            p_sc[hh, rows, :live] = jnp.exp2(s - m_new).astype(_BF16)
            if live < tk:
                p_sc[hh, rows, live:tk] = jnp.zeros((FOX_CHUNK, tk - live), _BF16)
            m_sc[hh, rows, :] = m_new
        pv = _dot(p_sc[hh, :, :tk], v)
        if first:
            acc_sc[hh] = pv
        else:
            acc_sc[hh] += pv

    def meta_mask(s, c):
        key = lax.broadcasted_iota(jnp.int32, s.shape, 1)
        return jnp.where(key >= PAD, s, NEG_INF)

    def causal_mask(s, c):
        key = lax.broadcasted_iota(jnp.int32, s.shape, 1)
        query = lax.broadcasted_iota(jnp.int32, s.shape, 0) + c * FOX_CHUNK
        return jnp.where(key <= query, s, NEG_INF)

    def key_tile(ref, ki, hh):
        return ref[pl.ds(pl.multiple_of(ki * t, t), t), heads[hh]]

    for hh in range(2):
        logits(hh, km_ref[:, heads[hh]], 1, BLOCK)
    for hh in range(2):
        logits(hh, key_tile(k_ref, 0, hh), 0, t)
        update(hh, vm_ref[:, heads[hh]], 1, BLOCK, meta_mask, True)

    def step(ki, buf, mask, more):
        for hh in range(2):
            if more:
                logits(hh, key_tile(k_ref, ki + 1, hh), 1 - buf, t)
            update(hh, key_tile(v_ref, ki, hh), buf, t, mask, False)

    def pair_body(j, carry):
        step(2 * j, 0, None, True)
        step(2 * j + 1, 1, None, True)
        return carry

    lax.fori_loop(0, lax.shift_right_logical(qi, 1), pair_body, 0)
    odd = lax.rem(qi, 2) == 1

    @pl.when(odd)
    def _():
        step(qi - 1, 0, None, True)
        step(qi, 1, causal_mask, False)

    @pl.when(jnp.logical_not(odd))
    def _():
        step(qi, 0, causal_mask, False)

    outs = []
    for hh in range(2):
        acc = acc_sc[hh]
        ones_lane = (1 - hh) * FOX_HD
        outs.append(acc / acc[:, ones_lane:ones_lane + 1])
    lane = lax.broadcasted_iota(jnp.int32, (t, LANES), 1)
    o_ref[...] = jnp.where(lane < FOX_HD, outs[0], outs[1]).astype(_BF16)


def _fox(fq, fk, fv, fk_m, fv_m, *, nb):
    s = fq.shape[1]
    t = T_FOX
    pair_w = 2 * LANES
    return pl.pallas_call(
        _fox_body,
        grid=(nb, FOX_HEADS // 2, s // t),
        in_specs=[
            pl.BlockSpec((None, t, pair_w), lambda b, p, i: (b, i, p)),
            pl.BlockSpec((None, s, pair_w), lambda b, p, i: (b, 0, p)),
            pl.BlockSpec((None, s, pair_w), lambda b, p, i: (b, 0, p)),
            pl.BlockSpec((BLOCK, pair_w), lambda b, p, i: (0, p)),
            pl.BlockSpec((BLOCK, pair_w), lambda b, p, i: (0, p)),
        ],
        out_specs=pl.BlockSpec((None, t, LANES), lambda b, p, i: (b, i, p)),
        out_shape=jax.ShapeDtypeStruct((nb, s, HEAD_W), _BF16),
        scratch_shapes=[pltpu.VMEM((2, t, 1), _F32), pltpu.VMEM((2, t, LANES), _F32), pltpu.VMEM((2, 2, t, t), _F32),
                        pltpu.VMEM((2, t, t), _BF16)],
        compiler_params=_params(3),
        name="fox",
    )(fq, fk, fv, fk_m, fv_m)


def _route(scores, biased):
    w = scores.shape[1]
    sub = lax.broadcasted_iota(jnp.int32, (GROUP_SIZE, w), 0).astype(_F32)
    groups = [biased[g * GROUP_SIZE:(g + 1) * GROUP_SIZE, :] for g in range(N_GROUPS)]
    gscore = []
    for v in groups:
        m1 = jnp.max(v, axis=0, keepdims=True)
        i1 = jnp.min(jnp.where(v == m1, sub, float(GROUP_SIZE)), axis=0, keepdims=True)
        m2 = jnp.max(jnp.where(sub == i1, -jnp.inf, v), axis=0, keepdims=True)
        gscore.append(m1 + m2)
    masked = []
    for g in range(N_GROUPS):
        beaten = jnp.zeros((1, w), _F32)
        for o in range(N_GROUPS):
            if o == g:
                continue
            wins = (gscore[o] >= gscore[g]) if o < g else (gscore[o] > gscore[g])
            beaten = beaten + wins.astype(_F32)
        masked.append(jnp.where(beaten < float(TOPK_GROUPS), groups[g], NEG_INF))
    work = jnp.concatenate(masked, axis=0)
    eid = lax.broadcasted_iota(jnp.int32, (N_EXPERTS, w), 0).astype(_F32)
    sels, raws = [], []
    member = None
    for _ in range(TOP_K):
        m = jnp.max(work, axis=0, keepdims=True)
        idx = jnp.min(jnp.where(work == m, eid, float(N_EXPERTS)), axis=0, keepdims=True)
        hot = eid == idx
        sels.append(idx)
        raws.append(jnp.sum(jnp.where(hot, scores, 0.0), axis=0, keepdims=True))
        work = jnp.where(hot, -jnp.inf, work)
        member = hot if member is None else (member | hot)
    return sels, raws, member


def _post_body(h0_ref, ret_ref, fox_ref, wo_ref, g1_ref, b1_ref, wr_ref, rb_ref,
               trow_ref, sel_ref, gate_ref, rank_ref, cnt_ref, run_ref):
    tm = h0_ref.shape[0]
    i = pl.program_id(0)

    @pl.when(i == 0)
    def _():
        run_ref[...] = jnp.zeros_like(run_ref)

    y = _dot(ret_ref[...], wo_ref[:HEAD_W, :]) + _dot(fox_ref[...], wo_ref[HEAD_W:, :])
    h1 = _ln(ALPHA * h0_ref[...] + y, g1_ref[...], b1_ref[...])
    _store_rows(trow_ref, h1)

    scores = jax.nn.sigmoid(_dot_nt(wr_ref[...], h1.astype(_BF16)))
    biased = scores + rb_ref[...]
    chunks = [slice(c * LANES, (c + 1) * LANES) for c in range(tm // LANES)]
    routed = [_route(scores[:, cs], biased[:, cs]) for cs in chunks]
    member_f = jnp.concatenate([member.astype(_F32) for _, _, member in routed], axis=1)
    row = lax.broadcasted_iota(jnp.int32, (tm, tm), 0)
    col = lax.broadcasted_iota(jnp.int32, (tm, tm), 1)
    before = (row < col).astype(_BF16)
    rank_e = _dot(member_f.astype(_BF16), before) + run_ref[...]
    eid = lax.broadcasted_iota(jnp.int32, (N_EXPERTS, LANES), 0).astype(_F32)
    sel_rows, gate_rows, rank_rows = [], [], []
    for k in range(TOP_K):
        sel_k, gate_k, rank_k = [], [], []
        for cs, (sels, raws, _) in zip(chunks, routed):
            total = raws[0]
            for r in raws[1:]:
                total = total + r
            sel_k.append(sels[k].astype(jnp.int32))
            gate_k.append(raws[k] * (ROUTED_SCALE / total))
            rank_k.append(jnp.sum(jnp.where(eid == sels[k], rank_e[:, cs], 0.0), axis=0, keepdims=True).astype(jnp.int32))
        sel_rows.append(jnp.concatenate(sel_k, axis=1))
        gate_rows.append(jnp.concatenate(gate_k, axis=1))
        rank_rows.append(jnp.concatenate(rank_k, axis=1))
    pad_rows = sel_ref.shape[0] - TOP_K
    sel_ref[...] = jnp.concatenate(sel_rows + [jnp.zeros((pad_rows, tm), jnp.int32)], axis=0)
    gate_ref[...] = jnp.concatenate(gate_rows + [jnp.zeros((pad_rows, tm), _F32)], axis=0)
    rank_ref[...] = jnp.concatenate(rank_rows + [jnp.zeros((pad_rows, tm), jnp.int32)], axis=0)
    run_ref[...] = run_ref[...] + jnp.sum(member_f, axis=1, keepdims=True)
    cnt_ref[...] = run_ref[...]


def _post(h0, ret, fox, w_out, ln1_g, ln1_b, w_rt, rbias):
    n = h0.shape[0]
    tm = TM_PROJ
    row_spec = lambda w: pl.BlockSpec((tm, w), lambda i: (i, 0))
    col_spec = pl.BlockSpec((SUBLANES, tm), lambda i: (0, i))
    const = lambda shape: pl.BlockSpec(shape, lambda i: (0, 0))
    return pl.pallas_call(
        _post_body,
        grid=(n // tm,),
        in_specs=[row_spec(D_MODEL), row_spec(HEAD_W), row_spec(HEAD_W),
                  const(w_out.shape), const((1, D_MODEL)), const((1, D_MODEL)), const(w_rt.shape),
                  const((N_EXPERTS, 1))],
        out_specs=[pl.BlockSpec((tm * ROW_TILES, LANES), lambda i: (i, 0)),
                   col_spec, col_spec, col_spec, const((N_EXPERTS, 1))],
        out_shape=[jax.ShapeDtypeStruct((n * ROW_TILES, LANES), ROW_DTYPE),
                   jax.ShapeDtypeStruct((SUBLANES, n), jnp.int32), jax.ShapeDtypeStruct((SUBLANES, n), _F32),
                   jax.ShapeDtypeStruct((SUBLANES, n), jnp.int32), jax.ShapeDtypeStruct((N_EXPERTS, 1), _F32)],
        scratch_shapes=[pltpu.VMEM((N_EXPERTS, 1), _F32)],
        compiler_params=_params(1),
        name="post_mixer",
    )(h0, ret, fox, w_out, ln1_g, ln1_b, w_rt, rbias)


def _plan_body(sel_ref, rank_ref, cnt_ref, dest_ref, blk_ref, fill_ref, used_ref, pad0_ref, padn_ref):
    cnt = cnt_ref[...]
    padded = jnp.ceil(cnt * (1.0 / EBLK)) * EBLK
    er = lax.broadcasted_iota(jnp.int32, (N_EXPERTS, N_EXPERTS), 0)
    ec = lax.broadcasted_iota(jnp.int32, (N_EXPERTS, N_EXPERTS), 1)
    padded_row = jnp.sum(jnp.where(er == ec, padded, 0.0), axis=0, keepdims=True)
    pstart = jnp.sum(jnp.where(ec < er, padded_row, 0.0), axis=1, keepdims=True)
    pend = pstart + padded
    sel = sel_ref[...]
    dest = rank_ref[...]
    for e in range(N_EXPERTS):
        dest = dest + jnp.where(sel == e, pstart[e:e + 1, :].astype(jnp.int32), 0)
    dest_ref[...] = dest
    nblk = blk_ref.shape[1]
    first_row = (lax.broadcasted_iota(jnp.int32, (N_EXPERTS, nblk), 1) * EBLK).astype(_F32)
    owner = jnp.minimum(jnp.sum((pend <= first_row).astype(_F32), axis=0, keepdims=True), N_EXPERTS - 1.0)
    blk_ref[...] = owner.astype(jnp.int32)
    mine = lax.broadcasted_iota(jnp.int32, (N_EXPERTS, nblk), 0).astype(_F32) == owner
    live_end = jnp.sum(jnp.where(mine, pstart + cnt, 0.0), axis=0, keepdims=True)
    fill_ref[...] = jnp.clip(live_end - first_row[:1, :], 0.0, float(EBLK)).astype(jnp.int32)
    used_ref[...] = (pend[N_EXPERTS - 1:, :] * (1.0 / EBLK)).astype(jnp.int32)
    as_row = lambda col: jnp.sum(jnp.where(er == ec, col, 0.0), axis=0, keepdims=True).astype(jnp.int32)
    pad0_ref[...] = as_row(pstart + cnt)
    padn_ref[...] = as_row(padded - cnt)


def _plan(sel, rank, cnt, nblk_pad):
    n = sel.shape[1]
    full = lambda shape: pl.BlockSpec(shape, lambda i: (0, 0))
    return pl.pallas_call(
        _plan_body,
        grid=(1,),
        in_specs=[full(sel.shape), full(rank.shape), full(cnt.shape)],
        out_specs=[full(sel.shape), full((1, nblk_pad)), full((1, nblk_pad)), full((1, 1)), full((1, N_EXPERTS)),
                   full((1, N_EXPERTS))],
        out_shape=[jax.ShapeDtypeStruct((SUBLANES, n), jnp.int32), jax.ShapeDtypeStruct((1, nblk_pad), jnp.int32),
                   jax.ShapeDtypeStruct((1, nblk_pad), jnp.int32), jax.ShapeDtypeStruct((1, 1), jnp.int32),
                   jax.ShapeDtypeStruct((1, N_EXPERTS), jnp.int32), jax.ShapeDtypeStruct((1, N_EXPERTS), jnp.int32)],
        compiler_params=_params(1),
        name="plan",
    )(sel, rank, cnt)


def _store_rows(ref, v):
    m = v.shape[0]
    for s in range(ROW_TILES):
        ref[pl.ds(s, m, stride=ROW_TILES), :] = v[:, s * LANES:(s + 1) * LANES].astype(ROW_DTYPE)


def _load_rows(ref, first_row, m):
    return jnp.concatenate([ref[pl.ds(first_row * ROW_TILES + s, m, stride=ROW_TILES), :] for s in range(ROW_TILES)],
                           axis=1)


def _row_copy(src, src_row, dst, dst_row, sem):
    return pltpu.make_async_copy(src.at[pl.ds(pl.multiple_of(src_row * ROW_TILES, ROW_TILES), ROW_TILES), :],
                                 dst.at[pl.ds(pl.multiple_of(dst_row * ROW_TILES, ROW_TILES), ROW_TILES), :], sem)


def _dispatch_body(dest_ref, pad0_ref, padn_ref, used_ref, t_ref, wgu_ref, wd_ref, xs_ref, base_ref, zero_sc, sem,
                   zsem):
    tt = t_ref.shape[0] // ROW_TILES
    half_blk = EBLK // 2
    n_half = xs_ref.shape[0] // (half_blk * ROW_TILES)

    @pl.when(pl.program_id(0) == 0)
    def _():
        zero_sc[...] = jnp.zeros_like(zero_sc)

        def zero_copy(first_row, rows):
            return pltpu.make_async_copy(
                zero_sc.at[pl.ds(0, rows * ROW_TILES), :],
                xs_ref.at[pl.ds(pl.multiple_of(first_row * ROW_TILES, ROW_TILES), rows * ROW_TILES), :], zsem)

        def for_padding(act):
            def per_expert(e, carry):
                row = pad0_ref[e]
                for bit in range(EBLK.bit_length() - 2, -1, -1):
                    take = (padn_ref[e] & (1 << bit)) != 0

                    @pl.when(take)
                    def _():
                        act(zero_copy(row, 1 << bit))

                    row = row + jnp.where(take, 1 << bit, 0)
                return carry

            def per_tail(hb, carry):
                act(zero_copy(hb * half_blk, half_blk))
                return carry

            lax.fori_loop(0, N_EXPERTS, per_expert, 0)
            lax.fori_loop(2 * used_ref[0], n_half, per_tail, 0)

        for_padding(lambda cp: cp.start())
        for_padding(lambda cp: cp.wait())

    def issue(j, carry):
        for u in range(ISSUE_UNROLL):
            i = ISSUE_UNROLL * j + u
            for k in range(TOP_K):
                _row_copy(t_ref, i, xs_ref, dest_ref[i * SUBLANES + k], sem).start(priority=k % 2)
        return carry

    lax.fori_loop(0, tt // ISSUE_UNROLL, issue, 0)

    h1 = _load_rows(t_ref, 0, tt)
    gu = _dot(h1.astype(_BF16), wgu_ref[...])
    mid = (_silu(gu[:, :SHARED_FF]) * gu[:, SHARED_FF:]).astype(_BF16)
    base_ref[...] = ALPHA * h1 + _dot(mid, wd_ref[...])

    for _ in range(TOP_K):
        pltpu.make_async_copy(t_ref, xs_ref.at[pl.ds(0, tt * ROW_TILES), :], sem).wait()


def _dispatch(dest_flat, pad0, padn, used, trow, w_gu, w_sd, total_rows):
    n = trow.shape[0] // ROW_TILES
    tt = TT_DISPATCH
    smem = pl.BlockSpec(memory_space=pltpu.SMEM)
    const = lambda shape: pl.BlockSpec(shape, lambda i: (0, 0))
    return pl.pallas_call(
        _dispatch_body,
        grid=(n // tt,),
        in_specs=[pl.BlockSpec((tt * SUBLANES,), lambda i: (i,), memory_space=pltpu.SMEM), smem, smem, smem,
                  pl.BlockSpec((tt * ROW_TILES, LANES), lambda i: (i, 0)), const(w_gu.shape), const(w_sd.shape)],
        out_specs=[pl.BlockSpec(memory_space=pl.ANY), pl.BlockSpec((tt, D_MODEL), lambda i: (i, 0))],
        out_shape=[jax.ShapeDtypeStruct((total_rows * ROW_TILES, LANES), ROW_DTYPE),
                   jax.ShapeDtypeStruct((n, D_MODEL), _F32)],
        scratch_shapes=[pltpu.VMEM((EBLK // 2 * ROW_TILES, LANES), ROW_DTYPE), pltpu.SemaphoreType.DMA(()),
                        pltpu.SemaphoreType.DMA(())],
        compiler_params=_params(1),
        name="dispatch",
    )(dest_flat, pad0, padn, used, trow, w_gu, w_sd)


def _expert_body(blk_ref, used_ref, fill_ref, xa_ref, xb_ref, wg_ref, wu_ref, wd_ref, y_ref, wgu_sc, wd_sc):
    i = pl.program_id(0)
    prev = blk_ref[jnp.maximum(i - 1, 0)]
    fresh = (i == 0) | (blk_ref[i] != prev)

    @pl.when(fresh)
    def _():
        wgu_sc[:, :EXPERT_FF] = wg_ref[...].astype(_BF16)
        wgu_sc[:, EXPERT_FF:] = wu_ref[...].astype(_BF16)
        wd_sc[...] = wd_ref[...].astype(_BF16)

    @pl.when(i < used_ref[0])
    def _():
        live = lax.broadcasted_iota(jnp.int32, (EBLK, 1), 0) < fill_ref[i]
        x = jnp.concatenate([_load_rows(xa_ref, 0, EBLK // 2), _load_rows(xb_ref, 0, EBLK // 2)], axis=0)
        x = jnp.where(live, x, jnp.zeros_like(x)).astype(_BF16)
        gu = _dot(x, wgu_sc[...])
        mid = (_silu(gu[:, :EXPERT_FF]) * gu[:, EXPERT_FF:]).astype(_BF16)
        _store_rows(y_ref, _dot(mid, wd_sc[...]))

    @pl.when(i >= used_ref[0])
    def _():
        y_ref[...] = jnp.zeros_like(y_ref)


def _experts(blk_e, used, fill, xs, we_gate, we_up, we_down):
    nblk = xs.shape[0] // (EBLK * ROW_TILES)
    half_rows = EBLK // 2 * ROW_TILES
    last = lambda i, used: jnp.minimum(i, jnp.maximum(used[0] - 1, 0))
    w_spec = lambda shape: pl.BlockSpec((None,) + shape, lambda i, blk, used, fill: (blk[i], 0, 0))
    return pl.pallas_call(
        _expert_body,
        grid_spec=pltpu.PrefetchScalarGridSpec(
            num_scalar_prefetch=3,
            grid=(nblk,),
            in_specs=[pl.BlockSpec((half_rows, LANES), lambda i, blk, used, fill: (2 * last(i, used), 0)),
                      pl.BlockSpec((half_rows, LANES), lambda i, blk, used, fill: (2 * last(i, used) + 1, 0)),
                      w_spec((D_MODEL, EXPERT_FF)), w_spec((D_MODEL, EXPERT_FF)), w_spec((EXPERT_FF, D_MODEL))],
            out_specs=pl.BlockSpec((EBLK * ROW_TILES, LANES), lambda i, blk, used, fill: (i, 0)),
            scratch_shapes=[pltpu.VMEM((D_MODEL, 2 * EXPERT_FF), _BF16), pltpu.VMEM((EXPERT_FF, D_MODEL), _BF16)],
        ),
        out_shape=jax.ShapeDtypeStruct(xs.shape, ROW_DTYPE),
        compiler_params=_params(1),
        name="experts",
    )(blk_e, used, fill, xs, xs, we_gate, we_up, we_down)


def _combine_body(dest_ref, dnext_ref, y_ref, base_ref, gate_ref, g2_ref, b2_ref, o_ref, z_sc, sems):
    i = pl.program_id(0)
    slot = lax.rem(i, 2)
    tile_rows = TT * TOP_K * ROW_TILES

    def gather(dref, into):
        def issue(j, carry):
            for u in range(ISSUE_UNROLL):
                t = ISSUE_UNROLL * j + u
                for k in range(TOP_K):
                    _row_copy(y_ref, dref[t * SUBLANES + k], z_sc.at[into], k * TT + t,
                              sems.at[into]).start(priority=k % 2)
            return carry

        lax.fori_loop(0, TT // ISSUE_UNROLL, issue, 0)

    @pl.when(i == 0)
    def _():
        gather(dest_ref, 0)

    @pl.when(i + 1 < pl.num_programs(0))
    def _():
        gather(dnext_ref, 1 - slot)

    pltpu.make_async_copy(y_ref.at[pl.ds(0, tile_rows), :], z_sc.at[slot], sems.at[slot]).wait()
    gates = gate_ref[...]
    acc = base_ref[...]
    for k in range(TOP_K):
        acc = acc + gates[:, k:k + 1] * _load_rows(z_sc.at[slot], k * TT, TT).astype(_F32)
    o_ref[...] = _ln(acc, g2_ref[...], b2_ref[...])


def _combine(dest_flat, y, base, gates_t, ln2_g, ln2_b):
    n = base.shape[0]
    steps = n // TT
    const = lambda shape: pl.BlockSpec(shape, lambda i: (0, 0))
    dest_spec = lambda ahead: pl.BlockSpec((TT * SUBLANES,), lambda i: (jnp.minimum(i + ahead, steps - 1),),
                                           memory_space=pltpu.SMEM)
    return pl.pallas_call(
        _combine_body,
        grid=(steps,),
        in_specs=[dest_spec(0), dest_spec(1),
                  pl.BlockSpec(memory_space=pl.ANY),
                  pl.BlockSpec((TT, D_MODEL), lambda i: (i, 0)),
                  pl.BlockSpec((TT, SUBLANES), lambda i: (i, 0)),
                  const((1, D_MODEL)), const((1, D_MODEL))],
        out_specs=pl.BlockSpec((TT, D_MODEL), lambda i: (i, 0)),
        out_shape=jax.ShapeDtypeStruct((n, D_MODEL), _F32),
        scratch_shapes=[pltpu.VMEM((2, TT * TOP_K * ROW_TILES, LANES), ROW_DTYPE), pltpu.SemaphoreType.DMA((2,))],
        compiler_params=_params(1),
        name="combine",
    )(dest_flat, dest_flat, y, base, gates_t, ln2_g, ln2_b)


def _rope_tables(pos):
    half = RET_DK // 2
    inv = ROPE_BASE ** (-jnp.arange(half, dtype=_F32) / half)
    ang = pos[:, None] * inv[None, :]
    cos = jnp.cos(ang)
    sin = jnp.sin(ang)
    return jnp.concatenate([cos, cos], -1), jnp.concatenate([-sin, sin], -1)


def _decay_tables():
    lg = jnp.log1p(-jnp.exp2(-5.0 - jnp.arange(RET_HEADS, dtype=_F32)))
    idx = jnp.arange(BLOCK, dtype=_F32)
    rel = idx[:, None] - idx[None, :]
    causal = rel >= 0
    dmask = jnp.where(causal[None], jnp.exp(jnp.where(causal, rel, 0.0)[None] * lg[:, None, None]), 0.0)
    zeta = jnp.exp((BLOCK - 1.0 - idx)[None, :] * lg[:, None])
    xi = jnp.exp((idx + 1.0)[None, :] * lg[:, None])
    along_lanes = lambda col: jnp.broadcast_to(col[:, :, None], (RET_HEADS, BLOCK, LANES))
    return dmask, along_lanes(xi), along_lanes(zeta), jnp.exp(BLOCK * lg)


def kernel(x, meta, ln0_g, ln0_b, w_in, b_forget, w_out, ln1_g, ln1_b, w_router, router_bias, we_gate, we_up,
           we_down, ws_gate, ws_up, ws_down, ln2_g, ln2_b):
    nb, s, d = x.shape
    assert d == D_MODEL and meta.shape == (N_META, D_MODEL) and w_in.shape[0] == 1
    assert s % TM_PROJ == 0 and s % T_FOX == 0 and (nb * s) % TT == 0
    n = nb * s
    x2d = x.reshape(n, d)
    row2 = lambda v: v.reshape(1, -1).astype(_F32)
    main_cols = 7 * HEAD_W
    w_all = jnp.concatenate(
        [w_in[0, :, :main_cols], w_in[0, :, main_cols:], jnp.zeros((d, LANES - FOX_HEADS), w_in.dtype)],
        axis=1).astype(_BF16)
    bf_pad = jnp.concatenate([b_forget[0].astype(_F32), jnp.zeros((LANES - FOX_HEADS,), _F32)]).reshape(1, LANES)
    g0, b0 = row2(ln0_g), row2(ln0_b)

    cos_x, sin_x = _rope_tables(jnp.arange(s, dtype=_F32) + float(N_META))
    cos_m, sin_m = _rope_tables(jnp.arange(BLOCK, dtype=_F32) - float(PAD))
    meta_blk = jnp.concatenate([jnp.zeros((PAD, d), _F32), meta.astype(_F32)], axis=0)

    h0, rq, rk, rv, rg, fq, fk, fv = _inproj(x2d, g0, b0, w_all, bf_pad, cos_x, sin_x, nb=nb, meta=False)
    _, _, rk_m, rv_m, _, _, fk_m, fv_m = _inproj(meta_blk, g0, b0, w_all, bf_pad, cos_m, sin_m, nb=1, meta=True)

    dmask, xi, zeta, gch = _decay_tables()
    per_batch = lambda a: a.reshape(nb, s, a.shape[-1])
    ret = _retention(per_batch(rq), per_batch(rk), per_batch(rv), per_batch(rg), rk_m, rv_m, dmask, xi, zeta,
                     gch).reshape(n, HEAD_W)
    fox = _fox(per_batch(fq), per_batch(fk), per_batch(fv), fk_m, fv_m, nb=nb).reshape(n, HEAD_W)

    w_gu = jnp.concatenate([ws_gate[0], ws_up[0]], axis=1).astype(_BF16)
    trow, sel, gates, rank, cnt = _post(
        h0, ret, fox, w_out[0].astype(_BF16), row2(ln1_g[0]), row2(ln1_b[0]),
        jnp.transpose(w_router[0]).astype(_BF16), router_bias[0].astype(_F32).reshape(N_EXPERTS, 1))

    nblk = n * TOP_K // EBLK + N_EXPERTS
    nblk_pad = -(-nblk // LANES) * LANES
    dest, blk_e, fill, used, pad0, padn = _plan(sel, rank, cnt, nblk_pad)
    dest_flat = jnp.transpose(dest).reshape(-1)
    used = used.reshape(-1)

    xs, base = _dispatch(dest_flat, pad0.reshape(-1), padn.reshape(-1), used, trow, w_gu, ws_down[0].astype(_BF16),
                         nblk * EBLK)
    y = _experts(blk_e.reshape(-1), used, fill.reshape(-1), xs, we_gate[0], we_up[0], we_down[0])
    out = _combine(dest_flat, y, base, jnp.transpose(gates), row2(ln2_g[0]), row2(ln2_b[0]))
    return out.reshape(nb, s, d)
```

```python
import functools

import jax
import jax.numpy as jnp
import numpy as np
from jax import lax
from jax.experimental import pallas as pl
from jax.experimental.pallas import tpu as pltpu

D_MODEL = 1024
N_META = 16
BLOCK = 128
PAD = BLOCK - N_META
RET_HEADS = 4
RET_DK = 128
FOX_HEADS = 8
FOX_HD = 64
N_EXPERTS = 64
TOP_K = 6
N_GROUPS = 8
GROUP_SIZE = N_EXPERTS // N_GROUPS
TOPK_GROUPS = 4
EXPERT_FF = 256
SHARED_FF = 256
ROUTED_SCALE = 2.5
ROPE_BASE = 10000.0
LN_EPS = 1e-5
NEG_INF = -1e30
ALPHA = 2.0 ** 0.25
HEAD_W = 512
LOG2E = 1.4426950408889634

LANES = 128
FOX_W = FOX_HEADS * LANES
SUBLANES = 8
ROW_TILES = D_MODEL // LANES
ROW_DTYPE = jnp.float32

TM_PROJ = 512
T_FOX = 512
FOX_CHUNK = 64
FOX_K_TERMS = 32
TT = 256
TT_DISPATCH = 512
EBLK = 512
ISSUE_UNROLL = 4
TAB_PAD0, TAB_PADN, TAB_BLK0, TAB_NBLK, TAB_CNT = range(5)
VMEM_LIMIT = 48 * 1024 * 1024

_F32 = jnp.float32
_BF16 = jnp.bfloat16


def _ln(x, g, b):
    xc = x - jnp.mean(x, -1, keepdims=True)
    var = jnp.mean(xc * xc, -1, keepdims=True)
    return xc * lax.rsqrt(var + LN_EPS) * g + b


def _dot(a, b):
    return jnp.dot(a, b, preferred_element_type=_F32)


def _dot_nt(a, b):
    return lax.dot_general(a, b, (((1,), (1,)), ((), ())), preferred_element_type=_F32)


def _dot_tn(a, b):
    return lax.dot_general(a, b, (((0,), (0,)), ((), ())), preferred_element_type=_F32)


def _silu(x):
    return x * jax.nn.sigmoid(x)


def _params(n_axes):
    return pltpu.CompilerParams(dimension_semantics=("arbitrary",) * n_axes, vmem_limit_bytes=VMEM_LIMIT)


def _inproj_body(x_ref, g_ref, b_ref, w_ref, bf_ref, cos_ref, sin_ref, own_ref, tq_ref, tk_ref, oq_ref, ok_ref, ov_ref,
                 h_ref, rq_ref, rk_ref, rv_ref, rg_ref, fq_ref, fk_ref, fv_ref, carry_ref, *, meta):
    tm = x_ref.shape[0]
    h = _ln(x_ref[...], g_ref[...], b_ref[...])
    if meta:
        valid = lax.broadcasted_iota(jnp.int32, (tm, 1), 0) >= PAD
        h = jnp.where(valid, h, 0.0)
    h_ref[...] = h
    hb = h.astype(_BF16)
    cos = cos_ref[...]
    sin = sin_ref[...]

    def proj(g):
        return _dot(hb, w_ref[:, g * HEAD_W:(g + 1) * HEAD_W])

    def rope_store(p, out_ref, scale):
        for hd in range(RET_HEADS):
            t = p[:, hd * LANES:(hd + 1) * LANES]
            r = t * cos + pltpu.roll(t, LANES // 2, axis=1) * sin
            out_ref[:, hd * LANES:(hd + 1) * LANES] = (r * scale).astype(_BF16)

    rope_store(proj(0), rq_ref, 1.0)
    rope_store(proj(1), rk_ref, RET_DK ** -0.5)
    rv_ref[...] = proj(2).astype(_BF16)
    rg_ref[...] = _silu(proj(3)).astype(_BF16)

    z = _dot(hb, w_ref[:, 7 * HEAD_W:7 * HEAD_W + LANES]) + bf_ref[...]
    logf = jnp.minimum(z, 0.0) - jnp.log1p(jnp.exp(-jnp.abs(z)))
    if meta:
        logf = jnp.where(valid, logf, 0.0)
    l1 = logf.astype(_BF16)
    r1 = logf - l1.astype(_F32)
    l2 = r1.astype(_BF16)
    l3 = (r1 - l2.astype(_F32)).astype(_BF16)
    row = lax.broadcasted_iota(jnp.int32, (tm, tm), 0)
    col = lax.broadcasted_iota(jnp.int32, (tm, tm), 1)
    tri = (col <= row).astype(_BF16)
    c = _dot(tri, l1) + _dot(tri, l2) + _dot(tri, l3)
    if meta:
        c = c - c[tm - 1:tm, :]
    else:
        @pl.when(pl.program_id(1) == 0)
        def _():
            carry_ref[...] = jnp.zeros_like(carry_ref)

        c = c + carry_ref[...]
        carry_ref[...] = c[tm - 1:tm, :]

    head_lane = lax.broadcasted_iota(jnp.int32, (1, LANES), 1) < FOX_HEADS
    cl = jnp.where(head_lane, c, 0.0) * LOG2E
    c1 = cl.astype(_BF16).astype(_F32)
    r1 = cl - c1
    c2 = r1.astype(_BF16).astype(_F32)
    c3 = (r1 - c2).astype(_BF16).astype(_F32)
    csplit = c1 + pltpu.roll(c2, FOX_HEADS, axis=1) + pltpu.roll(c3, 2 * FOX_HEADS, axis=1)
    half = FOX_HD
    q_even, q_odd = pltpu.roll(csplit, half, axis=1), csplit
    k_even, k_odd = pltpu.roll(csplit, half + FOX_K_TERMS, axis=1), pltpu.roll(csplit, FOX_K_TERMS, axis=1)
    own = own_ref[...] > 0.0

    def per_head(p):
        return jnp.concatenate([p[:, (hd // 2) * LANES:(hd // 2 + 1) * LANES] for hd in range(FOX_HEADS)], axis=1)

    def by_parity(even, odd):
        return jnp.concatenate([even, odd] * (FOX_HEADS // 2), axis=1)

    q_extra = jnp.where(tq_ref[...] > 0.0, by_parity(q_even, q_odd), oq_ref[...])
    k_extra = jnp.where(tk_ref[...] > 0.0, -by_parity(k_even, k_odd), ok_ref[...])
    fq_ref[...] = jnp.where(own, per_head(proj(4) * (FOX_HD ** -0.5 * LOG2E)), q_extra).astype(_BF16)
    fk_ref[...] = jnp.where(own, per_head(proj(5)), k_extra).astype(_BF16)
    fv_ref[...] = jnp.where(own, per_head(proj(6)), ov_ref[...]).astype(_BF16)


def _fox_lane_tables():
    own, tq, tk, oq, ok, ov = (np.zeros((1, FOX_W), np.float32) for _ in range(6))
    for hd in range(FOX_HEADS):
        data = hd * LANES + (hd % 2) * FOX_HD
        extra = hd * LANES + (1 - hd % 2) * FOX_HD
        own[0, data:data + FOX_HD] = 1.0
        for term in range(3):
            lane = extra + term * FOX_HEADS + hd
            tq[0, lane] = 1.0
            ok[0, lane] = 1.0
            tk[0, lane + FOX_K_TERMS] = 1.0
            oq[0, lane + FOX_K_TERMS] = 1.0
        ov[0, extra] = 1.0
    return tuple(jnp.asarray(t) for t in (own, tq, tk, oq, ok, ov))


def _inproj(x2d, ln_g, ln_b, w_all, bf_pad, cos_t, sin_t, *, nb, meta):
    n = x2d.shape[0]
    s = n // nb
    tm = min(TM_PROJ, s)
    nj = s // tm
    row_spec = lambda w: pl.BlockSpec((tm, w), lambda b, j: (b * nj + j, 0))
    const = lambda shape: pl.BlockSpec(shape, lambda b, j: (0, 0))
    pos_spec = pl.BlockSpec((tm, LANES), lambda b, j: (j, 0))
    tables = _fox_lane_tables()
    outs = ([jax.ShapeDtypeStruct((n, D_MODEL), _F32)] + [jax.ShapeDtypeStruct((n, HEAD_W), _BF16)] * 4
            + [jax.ShapeDtypeStruct((n, FOX_W), _BF16)] * 3)
    return pl.pallas_call(
        functools.partial(_inproj_body, meta=meta),
        grid=(nb, nj),
        in_specs=[row_spec(D_MODEL), const((1, D_MODEL)), const((1, D_MODEL)), const(w_all.shape),
                  const((1, LANES)), pos_spec, pos_spec] + [const(t.shape) for t in tables],
        out_specs=[row_spec(D_MODEL)] + [row_spec(HEAD_W)] * 4 + [row_spec(FOX_W)] * 3,
        out_shape=outs,
        scratch_shapes=[pltpu.VMEM((1, LANES), _F32)],
        compiler_params=_params(2),
        name="inproj_meta" if meta else "inproj",
    )(x2d, ln_g, ln_b, w_all, bf_pad, cos_t, sin_t, *tables)


def _ret_body(q_ref, k_ref, v_ref, g_ref, km_ref, vm_ref, dm_ref, xi_ref, zeta_ref, gch_ref, o_ref, st_ref):
    def kv_update(k, v, hd):
        vz = (v.astype(_F32) * zeta_ref[hd]).astype(_BF16)
        return _dot_tn(k, vz)

    nb = q_ref.shape[0]

    @pl.when(pl.program_id(0) == 0)
    def _():
        for hd in range(RET_HEADS):
            sl = slice(hd * LANES, (hd + 1) * LANES)
            first = kv_update(km_ref[:, sl], vm_ref[:, sl], hd)
            for b in range(nb):
                st_ref[b, hd] = first

    for b in range(nb):
        for hd in range(RET_HEADS):
            sl = slice(hd * LANES, (hd + 1) * LANES)
            q = q_ref[b, :, sl]
            k = k_ref[b, :, sl]
            v = v_ref[b, :, sl]
            st = st_ref[b, hd]
            scores = _dot_nt(q, k) * dm_ref[hd]
            o = _dot(scores.astype(_BF16), v) + _dot(q, st.astype(_BF16)) * xi_ref[hd]
            oc = o - jnp.mean(o, -1, keepdims=True)
            y = oc * lax.rsqrt(jnp.mean(oc * oc, -1, keepdims=True) + LN_EPS)
            o_ref[b, :, sl] = (y * g_ref[b, :, sl].astype(_F32)).astype(_BF16)
            st_ref[b, hd] = gch_ref[hd] * st + kv_update(k, v, hd)


def _retention(rq, rk, rv, rg, rk_m, rv_m, dmask, xi, zeta, gch):
    nb, s, _ = rq.shape
    row_spec = pl.BlockSpec((nb, BLOCK, HEAD_W), lambda j: (0, j, 0))
    meta_spec = pl.BlockSpec((BLOCK, HEAD_W), lambda j: (0, 0))
    tab = pl.BlockSpec((RET_HEADS, BLOCK, BLOCK), lambda j: (0, 0, 0))
    return pl.pallas_call(
        _ret_body,
        grid=(s // BLOCK,),
        in_specs=[row_spec] * 4 + [meta_spec] * 2 + [tab] * 3 + [pl.BlockSpec(memory_space=pltpu.SMEM)],
        out_specs=row_spec,
        out_shape=jax.ShapeDtypeStruct((nb, s, HEAD_W), _BF16),
        scratch_shapes=[pltpu.VMEM((nb, RET_HEADS, RET_DK, LANES), _F32)],
        compiler_params=_params(1),
        name="retention",
    )(rq, rk, rv, rg, rk_m, rv_m, dmask, xi, zeta, gch)


def _fox_body(q_ref, k_ref, v_ref, km_ref, vm_ref, o_ref, m_sc, acc_sc, s_sc, p_sc):
    t = q_ref.shape[0]
    qi = pl.program_id(2)

    heads = [slice(hh * LANES, (hh + 1) * LANES) for hh in range(2)]

    def logits(hh, k, buf, tk):
        s_sc[buf, hh, :, :tk] = _dot_nt(q_ref[:, heads[hh]], k)

    def update(hh, v, buf, tk, mask, first):
        for c in range(t // FOX_CHUNK):
            rows = slice(c * FOX_CHUNK, (c + 1) * FOX_CHUNK)
            s = s_sc[buf, hh, rows, :tk]
            if mask is not None:
                s = mask(s, c)
            mx = jnp.max(s, axis=1, keepdims=True)
            if first:
                m_new = mx
            else:
                m_prev = m_sc[hh, rows, :]
                m_new = jnp.maximum(m_prev, mx)
                acc_sc[hh, rows, :] = jnp.exp2(m_prev - m_new) * acc_sc[hh, rows, :]
            p_sc[hh, rows, :tk] = jnp.exp2(s - m_new).astype(_BF16)
            m_sc[hh, rows, :] = m_new
        pv = _dot(p_sc[hh, :, :tk], v)
        if first:
            acc_sc[hh] = pv
        else:
            acc_sc[hh] += pv

    def meta_mask(s, c):
        key = lax.broadcasted_iota(jnp.int32, s.shape, 1)
        return jnp.where(key >= PAD, s, NEG_INF)

    def causal_mask(s, c):
        key = lax.broadcasted_iota(jnp.int32, s.shape, 1)
        query = lax.broadcasted_iota(jnp.int32, s.shape, 0) + c * FOX_CHUNK
        return jnp.where(key <= query, s, NEG_INF)

    def key_tile(ref, ki, hh):
        return ref[pl.ds(pl.multiple_of(ki * t, t), t), heads[hh]]

    for hh in range(2):
        logits(hh, km_ref[:, heads[hh]], 1, BLOCK)
    for hh in range(2):
        logits(hh, key_tile(k_ref, 0, hh), 0, t)
        update(hh, vm_ref[:, heads[hh]], 1, BLOCK, meta_mask, True)

    def step(ki, buf, mask, more):
        for hh in range(2):
            if more:
                logits(hh, key_tile(k_ref, ki + 1, hh), 1 - buf, t)
            update(hh, key_tile(v_ref, ki, hh), buf, t, mask, False)

    def pair_body(j, carry):
        step(2 * j, 0, None, True)
        step(2 * j + 1, 1, None, True)
        return carry

    lax.fori_loop(0, lax.shift_right_logical(qi, 1), pair_body, 0)
    odd = lax.rem(qi, 2) == 1

    @pl.when(odd)
    def _():
        step(qi - 1, 0, None, True)
        step(qi, 1, causal_mask, False)

    @pl.when(jnp.logical_not(odd))
    def _():
        step(qi, 0, causal_mask, False)

    outs = []
    for hh in range(2):
        acc = acc_sc[hh]
        ones_lane = (1 - hh) * FOX_HD
        outs.append(acc / acc[:, ones_lane:ones_lane + 1])
    lane = lax.broadcasted_iota(jnp.int32, (t, LANES), 1)
    o_ref[...] = jnp.where(lane < FOX_HD, outs[0], outs[1]).astype(_BF16)


def _fox(fq, fk, fv, fk_m, fv_m, *, nb):
    s = fq.shape[1]
    t = T_FOX
    pair_w = 2 * LANES
    return pl.pallas_call(
        _fox_body,
        grid=(nb, FOX_HEADS // 2, s // t),
        in_specs=[
            pl.BlockSpec((None, t, pair_w), lambda b, p, i: (b, i, p)),
            pl.BlockSpec((None, s, pair_w), lambda b, p, i: (b, 0, p)),
            pl.BlockSpec((None, s, pair_w), lambda b, p, i: (b, 0, p)),
            pl.BlockSpec((BLOCK, pair_w), lambda b, p, i: (0, p)),
            pl.BlockSpec((BLOCK, pair_w), lambda b, p, i: (0, p)),
        ],
        out_specs=pl.BlockSpec((None, t, LANES), lambda b, p, i: (b, i, p)),
        out_shape=jax.ShapeDtypeStruct((nb, s, HEAD_W), _BF16),
        scratch_shapes=[pltpu.VMEM((2, t, 1), _F32), pltpu.VMEM((2, t, LANES), _F32), pltpu.VMEM((2, 2, t, t), _F32),
                        pltpu.VMEM((2, t, t), _BF16)],
        compiler_params=_params(3),
        name="fox",
    )(fq, fk, fv, fk_m, fv_m)


def _route(scores, biased):
    w = scores.shape[1]
    sub = lax.broadcasted_iota(jnp.int32, (GROUP_SIZE, w), 0).astype(_F32)
    groups = [biased[g * GROUP_SIZE:(g + 1) * GROUP_SIZE, :] for g in range(N_GROUPS)]
    gscore = []
    for v in groups:
        m1 = jnp.max(v, axis=0, keepdims=True)
        i1 = jnp.min(jnp.where(v == m1, sub, float(GROUP_SIZE)), axis=0, keepdims=True)
        m2 = jnp.max(jnp.where(sub == i1, -jnp.inf, v), axis=0, keepdims=True)
        gscore.append(m1 + m2)
    masked = []
    for g in range(N_GROUPS):
        beaten = jnp.zeros((1, w), _F32)
        for o in range(N_GROUPS):
            if o == g:
                continue
            wins = (gscore[o] >= gscore[g]) if o < g else (gscore[o] > gscore[g])
            beaten = beaten + wins.astype(_F32)
        masked.append(jnp.where(beaten < float(TOPK_GROUPS), groups[g], NEG_INF))
    work = jnp.concatenate(masked, axis=0)
    eid = lax.broadcasted_iota(jnp.int32, (N_EXPERTS, w), 0).astype(_F32)
    sels, raws = [], []
    member = None
    for _ in range(TOP_K):
        m = jnp.max(work, axis=0, keepdims=True)
        idx = jnp.min(jnp.where(work == m, eid, float(N_EXPERTS)), axis=0, keepdims=True)
        hot = eid == idx
        sels.append(idx)
        raws.append(jnp.sum(jnp.where(hot, scores, 0.0), axis=0, keepdims=True))
        work = jnp.where(hot, -jnp.inf, work)
        member = hot if member is None else (member | hot)
    return sels, raws, member


def _post_body(h0_ref, ret_ref, fox_ref, wo_ref, g1_ref, b1_ref, wr_ref, rb_ref,
               trow_ref, sel_ref, gate_ref, rank_ref, cnt_ref, run_ref):
    tm = h0_ref.shape[0]
    i = pl.program_id(0)

    @pl.when(i == 0)
    def _():
        run_ref[...] = jnp.zeros_like(run_ref)

    y = _dot(ret_ref[...], wo_ref[:HEAD_W, :]) + _dot(fox_ref[...], wo_ref[HEAD_W:, :])
    h1 = _ln(ALPHA * h0_ref[...] + y, g1_ref[...], b1_ref[...])
    _store_rows(trow_ref, h1)

    scores = jax.nn.sigmoid(_dot_nt(wr_ref[...], h1.astype(_BF16)))
    biased = scores + rb_ref[...]
    chunks = [slice(c * LANES, (c + 1) * LANES) for c in range(tm // LANES)]
    routed = [_route(scores[:, cs], biased[:, cs]) for cs in chunks]
    member_f = jnp.concatenate([member.astype(_F32) for _, _, member in routed], axis=1)
    row = lax.broadcasted_iota(jnp.int32, (tm, tm), 0)
    col = lax.broadcasted_iota(jnp.int32, (tm, tm), 1)
    before = (row < col).astype(_BF16)
    rank_e = _dot(member_f.astype(_BF16), before) + run_ref[...]
    eid = lax.broadcasted_iota(jnp.int32, (N_EXPERTS, LANES), 0).astype(_F32)
    sel_rows, gate_rows, rank_rows = [], [], []
    for k in range(TOP_K):
        sel_k, gate_k, rank_k = [], [], []
        for cs, (sels, raws, _) in zip(chunks, routed):
            total = raws[0]
            for r in raws[1:]:
                total = total + r
            sel_k.append(sels[k].astype(jnp.int32))
            gate_k.append(raws[k] * (ROUTED_SCALE / total))
            rank_k.append(jnp.sum(jnp.where(eid == sels[k], rank_e[:, cs], 0.0), axis=0, keepdims=True).astype(jnp.int32))
        sel_rows.append(jnp.concatenate(sel_k, axis=1))
        gate_rows.append(jnp.concatenate(gate_k, axis=1))
        rank_rows.append(jnp.concatenate(rank_k, axis=1))
    pad_rows = sel_ref.shape[0] - TOP_K
    sel_ref[...] = jnp.concatenate(sel_rows + [jnp.zeros((pad_rows, tm), jnp.int32)], axis=0)
    gate_ref[...] = jnp.concatenate(gate_rows + [jnp.zeros((pad_rows, tm), _F32)], axis=0)
    rank_ref[...] = jnp.concatenate(rank_rows + [jnp.zeros((pad_rows, tm), jnp.int32)], axis=0)
    run_ref[...] = run_ref[...] + jnp.sum(member_f, axis=1, keepdims=True)
    cnt_ref[...] = run_ref[...]


def _post(h0, ret, fox, w_out, ln1_g, ln1_b, w_rt, rbias):
    n = h0.shape[0]
    tm = TM_PROJ
    row_spec = lambda w: pl.BlockSpec((tm, w), lambda i: (i, 0))
    col_spec = pl.BlockSpec((SUBLANES, tm), lambda i: (0, i))
    const = lambda shape: pl.BlockSpec(shape, lambda i: (0, 0))
    return pl.pallas_call(
        _post_body,
        grid=(n // tm,),
        in_specs=[row_spec(D_MODEL), row_spec(HEAD_W), row_spec(HEAD_W),
                  const(w_out.shape), const((1, D_MODEL)), const((1, D_MODEL)), const(w_rt.shape),
                  const((N_EXPERTS, 1))],
        out_specs=[pl.BlockSpec((tm * ROW_TILES, LANES), lambda i: (i, 0)),
                   col_spec, col_spec, col_spec, const((N_EXPERTS, 1))],
        out_shape=[jax.ShapeDtypeStruct((n * ROW_TILES, LANES), ROW_DTYPE),
                   jax.ShapeDtypeStruct((SUBLANES, n), jnp.int32), jax.ShapeDtypeStruct((SUBLANES, n), _F32),
                   jax.ShapeDtypeStruct((SUBLANES, n), jnp.int32), jax.ShapeDtypeStruct((N_EXPERTS, 1), _F32)],
        scratch_shapes=[pltpu.VMEM((N_EXPERTS, 1), _F32)],
        compiler_params=_params(1),
        name="post_mixer",
    )(h0, ret, fox, w_out, ln1_g, ln1_b, w_rt, rbias)


def _plan_body(sel_ref, rank_ref, cnt_ref, dest_ref, used_ref, tab_ref):
    cnt = cnt_ref[...]
    padded = jnp.ceil(cnt * (1.0 / EBLK)) * EBLK
    er = lax.broadcasted_iota(jnp.int32, (N_EXPERTS, N_EXPERTS), 0)
    ec = lax.broadcasted_iota(jnp.int32, (N_EXPERTS, N_EXPERTS), 1)
    padded_row = jnp.sum(jnp.where(er == ec, padded, 0.0), axis=0, keepdims=True)
    pstart = jnp.sum(jnp.where(ec < er, padded_row, 0.0), axis=1, keepdims=True)
    pend = pstart + padded
    sel = sel_ref[...]
    dest = rank_ref[...]
    for e in range(N_EXPERTS):
        dest = dest + jnp.where(sel == e, pstart[e:e + 1, :].astype(jnp.int32), 0)
    dest_ref[...] = dest
    used_ref[...] = (pend[N_EXPERTS - 1:, :] * (1.0 / EBLK)).astype(jnp.int32)
    as_row = lambda col: jnp.sum(jnp.where(er == ec, col, 0.0), axis=0, keepdims=True).astype(jnp.int32)
    rows = {TAB_PAD0: pstart + cnt, TAB_PADN: padded - cnt, TAB_BLK0: pstart * (1.0 / EBLK),
            TAB_NBLK: padded * (1.0 / EBLK), TAB_CNT: cnt}
    blank = jnp.zeros((1, N_EXPERTS), jnp.int32)
    tab_ref[...] = jnp.concatenate([as_row(rows[r]) if r in rows else blank for r in range(SUBLANES)], axis=0)


def _plan(sel, rank, cnt):
    n = sel.shape[1]
    full = lambda shape: pl.BlockSpec(shape, lambda i: (0, 0))
    return pl.pallas_call(
        _plan_body,
        grid=(1,),
        in_specs=[full(sel.shape), full(rank.shape), full(cnt.shape)],
        out_specs=[full(sel.shape), full((1, 1)), full((SUBLANES, N_EXPERTS))],
        out_shape=[jax.ShapeDtypeStruct((SUBLANES, n), jnp.int32), jax.ShapeDtypeStruct((1, 1), jnp.int32),
                   jax.ShapeDtypeStruct((SUBLANES, N_EXPERTS), jnp.int32)],
        compiler_params=_params(1),
        name="plan",
    )(sel, rank, cnt)


def _store_rows(ref, v):
    m = v.shape[0]
    for s in range(ROW_TILES):
        ref[pl.ds(s, m, stride=ROW_TILES), :] = v[:, s * LANES:(s + 1) * LANES].astype(ROW_DTYPE)


def _load_rows(ref, first_row, m):
    return jnp.concatenate([ref[pl.ds(first_row * ROW_TILES + s, m, stride=ROW_TILES), :] for s in range(ROW_TILES)],
                           axis=1)


def _row_copy(src, src_row, dst, dst_row, sem):
    return pltpu.make_async_copy(src.at[pl.ds(pl.multiple_of(src_row * ROW_TILES, ROW_TILES), ROW_TILES), :],
                                 dst.at[pl.ds(pl.multiple_of(dst_row * ROW_TILES, ROW_TILES), ROW_TILES), :], sem)


def _dispatch_body(dest_ref, tab_ref, used_ref, t_ref, wgu_ref, wd_ref, xs_ref, base_ref, zero_sc, sem, zsem):
    tt = t_ref.shape[0] // ROW_TILES
    half_blk = EBLK // 2
    n_half = xs_ref.shape[0] // (half_blk * ROW_TILES)

    @pl.when(pl.program_id(0) == 0)
    def _():
        zero_sc[...] = jnp.zeros_like(zero_sc)

        def zero_copy(first_row, rows):
            return pltpu.make_async_copy(
                zero_sc.at[pl.ds(0, rows * ROW_TILES), :],
                xs_ref.at[pl.ds(pl.multiple_of(first_row * ROW_TILES, ROW_TILES), rows * ROW_TILES), :], zsem)

        def for_padding(act):
            def per_expert(e, carry):
                row = tab_ref[TAB_PAD0, e]
                for bit in range(EBLK.bit_length() - 2, -1, -1):
                    take = (tab_ref[TAB_PADN, e] & (1 << bit)) != 0

                    @pl.when(take)
                    def _():
                        act(zero_copy(row, 1 << bit))

                    row = row + jnp.where(take, 1 << bit, 0)
                return carry

            def per_tail(hb, carry):
                act(zero_copy(hb * half_blk, half_blk))
                return carry

            lax.fori_loop(0, N_EXPERTS, per_expert, 0)
            lax.fori_loop(2 * used_ref[0], n_half, per_tail, 0)

        for_padding(lambda cp: cp.start())
        for_padding(lambda cp: cp.wait())

    def issue(j, carry):
        for u in range(ISSUE_UNROLL):
            i = ISSUE_UNROLL * j + u
            for k in range(TOP_K):
                _row_copy(t_ref, i, xs_ref, dest_ref[i * SUBLANES + k], sem).start(priority=k % 2)
        return carry

    lax.fori_loop(0, tt // ISSUE_UNROLL, issue, 0)

    h1 = _load_rows(t_ref, 0, tt)
    gu = _dot(h1.astype(_BF16), wgu_ref[...])
    mid = (_silu(gu[:, :SHARED_FF]) * gu[:, SHARED_FF:]).astype(_BF16)
    base_ref[...] = ALPHA * h1 + _dot(mid, wd_ref[...])

    for _ in range(TOP_K):
        pltpu.make_async_copy(t_ref, xs_ref.at[pl.ds(0, tt * ROW_TILES), :], sem).wait()


def _dispatch(dest_flat, tab, used, trow, w_gu, w_sd, total_rows):
    n = trow.shape[0] // ROW_TILES
    tt = TT_DISPATCH
    smem = pl.BlockSpec(memory_space=pltpu.SMEM)
    const = lambda shape: pl.BlockSpec(shape, lambda i: (0, 0))
    return pl.pallas_call(
        _dispatch_body,
        grid=(n // tt,),
        in_specs=[pl.BlockSpec((tt * SUBLANES,), lambda i: (i,), memory_space=pltpu.SMEM), smem, smem,
                  pl.BlockSpec((tt * ROW_TILES, LANES), lambda i: (i, 0)), const(w_gu.shape), const(w_sd.shape)],
        out_specs=[pl.BlockSpec(memory_space=pl.ANY), pl.BlockSpec((tt, D_MODEL), lambda i: (i, 0))],
        out_shape=[jax.ShapeDtypeStruct((total_rows * ROW_TILES, LANES), ROW_DTYPE),
                   jax.ShapeDtypeStruct((n, D_MODEL), _F32)],
        scratch_shapes=[pltpu.VMEM((EBLK // 2 * ROW_TILES, LANES), ROW_DTYPE), pltpu.SemaphoreType.DMA(()),
                        pltpu.SemaphoreType.DMA(())],
        compiler_params=_params(1),
        name="dispatch",
    )(dest_flat, tab, used, trow, w_gu, w_sd)


def _expert_body(tab_ref, used_ref, xs_ref, wg_ref, wu_ref, wd_ref, y_ref, wgu_sc, wd_sc, x_sc, y_sc, xsem, ysem):
    e = pl.program_id(0)
    blk_rows = EBLK * ROW_TILES
    nblk = y_ref.shape[0] // blk_rows
    used = used_ref[0]

    def rows_of(ref, g):
        return ref.at[pl.ds(pl.multiple_of(g * blk_rows, blk_rows), blk_rows), :]

    def fetch(g, slot):
        return pltpu.make_async_copy(rows_of(xs_ref, g), x_sc.at[slot], xsem.at[slot])

    def writeback(g, slot):
        return pltpu.make_async_copy(y_sc.at[slot], rows_of(y_ref, g), ysem.at[slot])

    @pl.when((e == 0) & (used > 0))
    def _():
        fetch(0, 0).start()

    wgu_sc[:, :EXPERT_FF] = wg_ref[...].astype(_BF16)
    wgu_sc[:, EXPERT_FF:] = wu_ref[...].astype(_BF16)
    wd_sc[...] = wd_ref[...].astype(_BF16)
    first = tab_ref[TAB_BLK0, e]
    count = tab_ref[TAB_CNT, e]

    def block(j, carry):
        g = first + j
        slot = lax.rem(g, 2)
        fetch(g, slot).wait()

        @pl.when(g + 1 < used)
        def _():
            fetch(g + 1, 1 - slot).start()

        @pl.when(g >= 2)
        def _():
            writeback(g - 2, slot).wait()

        live = lax.broadcasted_iota(jnp.int32, (EBLK, 1), 0) < count - j * EBLK
        x = _load_rows(x_sc.at[slot], 0, EBLK)
        x = jnp.where(live, x, jnp.zeros_like(x)).astype(_BF16)
        gu = _dot(x, wgu_sc[...])
        mid = (_silu(gu[:, :EXPERT_FF]) * gu[:, EXPERT_FF:]).astype(_BF16)
        _store_rows(y_sc.at[slot], _dot(mid, wd_sc[...]))
        writeback(g, slot).start()
        return carry

    lax.fori_loop(0, tab_ref[TAB_NBLK, e], block, 0)

    @pl.when(e == pl.num_programs(0) - 1)
    def _():
        for back in (2, 1):
            @pl.when(used >= back)
            def _():
                writeback(used - back, lax.rem(used - back, 2)).wait()

        y_sc[0] = jnp.zeros(y_sc.shape[1:], y_sc.dtype)

        def tail(act):
            def body(g, carry):
                act(writeback(g, 0))
                return carry

            lax.fori_loop(used, nblk, body, 0)

        tail(lambda cp: cp.start())
        tail(lambda cp: cp.wait())


def _experts(tab, used, xs, we_gate, we_up, we_down):
    smem = pl.BlockSpec(memory_space=pltpu.SMEM)
    w_spec = lambda shape: pl.BlockSpec((None,) + shape, lambda e: (e, 0, 0))
    buf = pltpu.VMEM((2, EBLK * ROW_TILES, LANES), ROW_DTYPE)
    return pl.pallas_call(
        _expert_body,
        grid=(N_EXPERTS,),
        in_specs=[smem, smem, pl.BlockSpec(memory_space=pl.ANY),
                  w_spec((D_MODEL, EXPERT_FF)), w_spec((D_MODEL, EXPERT_FF)), w_spec((EXPERT_FF, D_MODEL))],
        out_specs=pl.BlockSpec(memory_space=pl.ANY),
        out_shape=jax.ShapeDtypeStruct(xs.shape, ROW_DTYPE),
        scratch_shapes=[pltpu.VMEM((D_MODEL, 2 * EXPERT_FF), _BF16), pltpu.VMEM((EXPERT_FF, D_MODEL), _BF16), buf, buf,
                        pltpu.SemaphoreType.DMA((2,)), pltpu.SemaphoreType.DMA((2,))],
        compiler_params=_params(1),
        name="experts",
    )(tab, used, xs, we_gate, we_up, we_down)


def _combine_body(dest_ref, dnext_ref, y_ref, base_ref, gate_ref, g2_ref, b2_ref, o_ref, z_sc, sems):
    i = pl.program_id(0)
    slot = lax.rem(i, 2)
    tile_rows = TT * TOP_K * ROW_TILES

    def gather(dref, into):
        def issue(j, carry):
            for u in range(ISSUE_UNROLL):
                t = ISSUE_UNROLL * j + u
                for k in range(TOP_K):
                    _row_copy(y_ref, dref[t * SUBLANES + k], z_sc.at[into], k * TT + t,
                              sems.at[into]).start(priority=k % 2)
            return carry

        lax.fori_loop(0, TT // ISSUE_UNROLL, issue, 0)

    @pl.when(i == 0)
    def _():
        gather(dest_ref, 0)

    @pl.when(i + 1 < pl.num_programs(0))
    def _():
        gather(dnext_ref, 1 - slot)

    pltpu.make_async_copy(y_ref.at[pl.ds(0, tile_rows), :], z_sc.at[slot], sems.at[slot]).wait()
    gates = gate_ref[...]
    acc = base_ref[...]
    for k in range(TOP_K):
        acc = acc + gates[:, k:k + 1] * _load_rows(z_sc.at[slot], k * TT, TT).astype(_F32)
    o_ref[...] = _ln(acc, g2_ref[...], b2_ref[...])


def _combine(dest_flat, y, base, gates_t, ln2_g, ln2_b):
    n = base.shape[0]
    steps = n // TT
    const = lambda shape: pl.BlockSpec(shape, lambda i: (0, 0))
    dest_spec = lambda ahead: pl.BlockSpec((TT * SUBLANES,), lambda i: (jnp.minimum(i + ahead, steps - 1),),
                                           memory_space=pltpu.SMEM)
    return pl.pallas_call(
        _combine_body,
        grid=(steps,),
        in_specs=[dest_spec(0), dest_spec(1),
                  pl.BlockSpec(memory_space=pl.ANY),
                  pl.BlockSpec((TT, D_MODEL), lambda i: (i, 0)),
                  pl.BlockSpec((TT, SUBLANES), lambda i: (i, 0)),
                  const((1, D_MODEL)), const((1, D_MODEL))],
        out_specs=pl.BlockSpec((TT, D_MODEL), lambda i: (i, 0)),
        out_shape=jax.ShapeDtypeStruct((n, D_MODEL), _F32),
        scratch_shapes=[pltpu.VMEM((2, TT * TOP_K * ROW_TILES, LANES), ROW_DTYPE), pltpu.SemaphoreType.DMA((2,))],
        compiler_params=_params(1),
        name="combine",
    )(dest_flat, dest_flat, y, base, gates_t, ln2_g, ln2_b)


def _rope_tables(pos):
    half = RET_DK // 2
    inv = ROPE_BASE ** (-jnp.arange(half, dtype=_F32) / half)
    ang = pos[:, None] * inv[None, :]
    cos = jnp.cos(ang)
    sin = jnp.sin(ang)
    return jnp.concatenate([cos, cos], -1), jnp.concatenate([-sin, sin], -1)


def _decay_tables():
    lg = jnp.log1p(-jnp.exp2(-5.0 - jnp.arange(RET_HEADS, dtype=_F32)))
    idx = jnp.arange(BLOCK, dtype=_F32)
    rel = idx[:, None] - idx[None, :]
    causal = rel >= 0
    dmask = jnp.where(causal[None], jnp.exp(jnp.where(causal, rel, 0.0)[None] * lg[:, None, None]), 0.0)
    zeta = jnp.exp((BLOCK - 1.0 - idx)[None, :] * lg[:, None])
    xi = jnp.exp((idx + 1.0)[None, :] * lg[:, None])
    along_lanes = lambda col: jnp.broadcast_to(col[:, :, None], (RET_HEADS, BLOCK, LANES))
    return dmask, along_lanes(xi), along_lanes(zeta), jnp.exp(BLOCK * lg)


def kernel(x, meta, ln0_g, ln0_b, w_in, b_forget, w_out, ln1_g, ln1_b, w_router, router_bias, we_gate, we_up,
           we_down, ws_gate, ws_up, ws_down, ln2_g, ln2_b):
    nb, s, d = x.shape
    assert d == D_MODEL and meta.shape == (N_META, D_MODEL) and w_in.shape[0] == 1
    assert s % TM_PROJ == 0 and s % T_FOX == 0 and (nb * s) % TT == 0
    n = nb * s
    x2d = x.reshape(n, d)
    row2 = lambda v: v.reshape(1, -1).astype(_F32)
    main_cols = 7 * HEAD_W
    w_all = jnp.concatenate(
        [w_in[0, :, :main_cols], w_in[0, :, main_cols:], jnp.zeros((d, LANES - FOX_HEADS), w_in.dtype)],
        axis=1).astype(_BF16)
    bf_pad = jnp.concatenate([b_forget[0].astype(_F32), jnp.zeros((LANES - FOX_HEADS,), _F32)]).reshape(1, LANES)
    g0, b0 = row2(ln0_g), row2(ln0_b)

    cos_x, sin_x = _rope_tables(jnp.arange(s, dtype=_F32) + float(N_META))
    cos_m, sin_m = _rope_tables(jnp.arange(BLOCK, dtype=_F32) - float(PAD))
    meta_blk = jnp.concatenate([jnp.zeros((PAD, d), _F32), meta.astype(_F32)], axis=0)

    h0, rq, rk, rv, rg, fq, fk, fv = _inproj(x2d, g0, b0, w_all, bf_pad, cos_x, sin_x, nb=nb, meta=False)
    _, _, rk_m, rv_m, _, _, fk_m, fv_m = _inproj(meta_blk, g0, b0, w_all, bf_pad, cos_m, sin_m, nb=1, meta=True)

    dmask, xi, zeta, gch = _decay_tables()
    per_batch = lambda a: a.reshape(nb, s, a.shape[-1])
    ret = _retention(per_batch(rq), per_batch(rk), per_batch(rv), per_batch(rg), rk_m, rv_m, dmask, xi, zeta,
                     gch).reshape(n, HEAD_W)
    fox = _fox(per_batch(fq), per_batch(fk), per_batch(fv), fk_m, fv_m, nb=nb).reshape(n, HEAD_W)

    w_gu = jnp.concatenate([ws_gate[0], ws_up[0]], axis=1).astype(_BF16)
    trow, sel, gates, rank, cnt = _post(
        h0, ret, fox, w_out[0].astype(_BF16), row2(ln1_g[0]), row2(ln1_b[0]),
        jnp.transpose(w_router[0]).astype(_BF16), router_bias[0].astype(_F32).reshape(N_EXPERTS, 1))

    nblk = n * TOP_K // EBLK + N_EXPERTS
    dest, used, tab = _plan(sel, rank, cnt)
    dest_flat = jnp.transpose(dest).reshape(-1)
    used = used.reshape(-1)

    xs, base = _dispatch(dest_flat, tab, used, trow, w_gu, ws_down[0].astype(_BF16), nblk * EBLK)
    y = _experts(tab, used, xs, we_gate[0], we_up[0], we_down[0])
    out = _combine(dest_flat, y, base, jnp.transpose(gates), row2(ln2_g[0]), row2(ln2_b[0]))
    return out.reshape(nb, s, d)
```

```python
import functools

import jax
import jax.numpy as jnp
import numpy as np
from jax import lax
from jax.experimental import pallas as pl
from jax.experimental.pallas import tpu as pltpu

D_MODEL = 1024
N_META = 16
BLOCK = 128
PAD = BLOCK - N_META
RET_HEADS = 4
RET_DK = 128
FOX_HEADS = 8
FOX_HD = 64
N_EXPERTS = 64
TOP_K = 6
N_GROUPS = 8
GROUP_SIZE = N_EXPERTS // N_GROUPS
TOPK_GROUPS = 4
EXPERT_FF = 256
SHARED_FF = 256
ROUTED_SCALE = 2.5
ROPE_BASE = 10000.0
LN_EPS = 1e-5
NEG_INF = -1e30
ALPHA = 2.0 ** 0.25
HEAD_W = 512
LOG2E = 1.4426950408889634

LANES = 128
FOX_W = FOX_HEADS * LANES
SUBLANES = 8
ROW_TILES = D_MODEL // LANES
ROW_DTYPE = jnp.float32

TM_PROJ = 512
T_FOX = 512
FOX_CHUNK = 64
FOX_K_TERMS = 32
TT = 256
TT_DISPATCH = 512
EBLK = 512
ISSUE_UNROLL = 4
TAB_PAD0, TAB_PADN = 0, 1
VMEM_LIMIT = 48 * 1024 * 1024

_F32 = jnp.float32
_BF16 = jnp.bfloat16


def _ln(x, g, b):
    xc = x - jnp.mean(x, -1, keepdims=True)
    var = jnp.mean(xc * xc, -1, keepdims=True)
    return xc * lax.rsqrt(var + LN_EPS) * g + b


def _dot(a, b):
    return jnp.dot(a, b, preferred_element_type=_F32)


def _dot_nt(a, b):
    return lax.dot_general(a, b, (((1,), (1,)), ((), ())), preferred_element_type=_F32)


def _dot_tn(a, b):
    return lax.dot_general(a, b, (((0,), (0,)), ((), ())), preferred_element_type=_F32)


def _silu(x):
    return x * jax.nn.sigmoid(x)


def _params(n_axes):
    return pltpu.CompilerParams(dimension_semantics=("arbitrary",) * n_axes, vmem_limit_bytes=VMEM_LIMIT)


def _inproj_body(x_ref, g_ref, b_ref, w_ref, bf_ref, cos_ref, sin_ref, own_ref, tq_ref, tk_ref, oq_ref, ok_ref, ov_ref,
                 h_ref, rq_ref, rk_ref, rv_ref, rg_ref, fq_ref, fk_ref, fv_ref, carry_ref, *, meta):
    tm = x_ref.shape[0]
    h = _ln(x_ref[...], g_ref[...], b_ref[...])
    if meta:
        valid = lax.broadcasted_iota(jnp.int32, (tm, 1), 0) >= PAD
        h = jnp.where(valid, h, 0.0)
    h_ref[...] = h
    hb = h.astype(_BF16)
    cos = cos_ref[...]
    sin = sin_ref[...]

    def proj(g):
        return _dot(hb, w_ref[:, g * HEAD_W:(g + 1) * HEAD_W])

    def rope_store(p, out_ref, scale):
        for hd in range(RET_HEADS):
            t = p[:, hd * LANES:(hd + 1) * LANES]
            r = t * cos + pltpu.roll(t, LANES // 2, axis=1) * sin
            out_ref[:, hd * LANES:(hd + 1) * LANES] = (r * scale).astype(_BF16)

    rope_store(proj(0), rq_ref, 1.0)
    rope_store(proj(1), rk_ref, RET_DK ** -0.5)
    rv_ref[...] = proj(2).astype(_BF16)
    rg_ref[...] = _silu(proj(3)).astype(_BF16)

    z = _dot(hb, w_ref[:, 7 * HEAD_W:7 * HEAD_W + LANES]) + bf_ref[...]
    logf = jnp.minimum(z, 0.0) - jnp.log1p(jnp.exp(-jnp.abs(z)))
    if meta:
        logf = jnp.where(valid, logf, 0.0)
    l1 = logf.astype(_BF16)
    r1 = logf - l1.astype(_F32)
    l2 = r1.astype(_BF16)
    l3 = (r1 - l2.astype(_F32)).astype(_BF16)
    row = lax.broadcasted_iota(jnp.int32, (tm, tm), 0)
    col = lax.broadcasted_iota(jnp.int32, (tm, tm), 1)
    tri = (col <= row).astype(_BF16)
    c = _dot(tri, l1) + _dot(tri, l2) + _dot(tri, l3)
    if meta:
        c = c - c[tm - 1:tm, :]
    else:
        @pl.when(pl.program_id(1) == 0)
        def _():
            carry_ref[...] = jnp.zeros_like(carry_ref)

        c = c + carry_ref[...]
        carry_ref[...] = c[tm - 1:tm, :]

    head_lane = lax.broadcasted_iota(jnp.int32, (1, LANES), 1) < FOX_HEADS
    cl = jnp.where(head_lane, c, 0.0) * LOG2E
    c1 = cl.astype(_BF16).astype(_F32)
    r1 = cl - c1
    c2 = r1.astype(_BF16).astype(_F32)
    c3 = (r1 - c2).astype(_BF16).astype(_F32)
    csplit = c1 + pltpu.roll(c2, FOX_HEADS, axis=1) + pltpu.roll(c3, 2 * FOX_HEADS, axis=1)
    half = FOX_HD
    q_even, q_odd = pltpu.roll(csplit, half, axis=1), csplit
    k_even, k_odd = pltpu.roll(csplit, half + FOX_K_TERMS, axis=1), pltpu.roll(csplit, FOX_K_TERMS, axis=1)
    own = own_ref[...] > 0.0

    def per_head(p):
        return jnp.concatenate([p[:, (hd // 2) * LANES:(hd // 2 + 1) * LANES] for hd in range(FOX_HEADS)], axis=1)

    def by_parity(even, odd):
        return jnp.concatenate([even, odd] * (FOX_HEADS // 2), axis=1)

    q_extra = jnp.where(tq_ref[...] > 0.0, by_parity(q_even, q_odd), oq_ref[...])
    k_extra = jnp.where(tk_ref[...] > 0.0, -by_parity(k_even, k_odd), ok_ref[...])
    fq_ref[...] = jnp.where(own, per_head(proj(4) * (FOX_HD ** -0.5 * LOG2E)), q_extra).astype(_BF16)
    fk_ref[...] = jnp.where(own, per_head(proj(5)), k_extra).astype(_BF16)
    fv_ref[...] = jnp.where(own, per_head(proj(6)), ov_ref[...]).astype(_BF16)


def _fox_lane_tables():
    own, tq, tk, oq, ok, ov = (np.zeros((1, FOX_W), np.float32) for _ in range(6))
    for hd in range(FOX_HEADS):
        data = hd * LANES + (hd % 2) * FOX_HD
        extra = hd * LANES + (1 - hd % 2) * FOX_HD
        own[0, data:data + FOX_HD] = 1.0
        for term in range(3):
            lane = extra + term * FOX_HEADS + hd
            tq[0, lane] = 1.0
            ok[0, lane] = 1.0
            tk[0, lane + FOX_K_TERMS] = 1.0
            oq[0, lane + FOX_K_TERMS] = 1.0
        ov[0, extra] = 1.0
    return tuple(jnp.asarray(t) for t in (own, tq, tk, oq, ok, ov))


def _inproj(x2d, ln_g, ln_b, w_all, bf_pad, cos_t, sin_t, *, nb, meta):
    n = x2d.shape[0]
    s = n // nb
    tm = min(TM_PROJ, s)
    nj = s // tm
    row_spec = lambda w: pl.BlockSpec((tm, w), lambda b, j: (b * nj + j, 0))
    const = lambda shape: pl.BlockSpec(shape, lambda b, j: (0, 0))
    pos_spec = pl.BlockSpec((tm, LANES), lambda b, j: (j, 0))
    tables = _fox_lane_tables()
    outs = ([jax.ShapeDtypeStruct((n, D_MODEL), _F32)] + [jax.ShapeDtypeStruct((n, HEAD_W), _BF16)] * 4
            + [jax.ShapeDtypeStruct((n, FOX_W), _BF16)] * 3)
    return pl.pallas_call(
        functools.partial(_inproj_body, meta=meta),
        grid=(nb, nj),
        in_specs=[row_spec(D_MODEL), const((1, D_MODEL)), const((1, D_MODEL)), const(w_all.shape),
                  const((1, LANES)), pos_spec, pos_spec] + [const(t.shape) for t in tables],
        out_specs=[row_spec(D_MODEL)] + [row_spec(HEAD_W)] * 4 + [row_spec(FOX_W)] * 3,
        out_shape=outs,
        scratch_shapes=[pltpu.VMEM((1, LANES), _F32)],
        compiler_params=_params(2),
        name="inproj_meta" if meta else "inproj",
    )(x2d, ln_g, ln_b, w_all, bf_pad, cos_t, sin_t, *tables)


def _ret_body(q_ref, k_ref, v_ref, g_ref, km_ref, vm_ref, dm_ref, xi_ref, zeta_ref, gch_ref, o_ref, st_ref):
    def kv_update(k, v, hd):
        vz = (v.astype(_F32) * zeta_ref[hd]).astype(_BF16)
        return _dot_tn(k, vz)

    nb = q_ref.shape[0]

    @pl.when(pl.program_id(0) == 0)
    def _():
        for hd in range(RET_HEADS):
            sl = slice(hd * LANES, (hd + 1) * LANES)
            first = kv_update(km_ref[:, sl], vm_ref[:, sl], hd)
            for b in range(nb):
                st_ref[b, hd] = first

    group = 4
    for b0 in range(0, nb, group):
        chains = [(b, hd, slice(hd * LANES, (hd + 1) * LANES)) for b in range(b0, b0 + group) for hd in range(RET_HEADS)]
        scores = [_dot_nt(q_ref[b, :, sl], k_ref[b, :, sl]) * dm_ref[hd] for b, hd, sl in chains]
        cross = [_dot(q_ref[b, :, sl], st_ref[b, hd].astype(_BF16)) * xi_ref[hd] for b, hd, sl in chains]
        outs = [_dot(s.astype(_BF16), v_ref[b, :, sl]) + c for s, c, (b, hd, sl) in zip(scores, cross, chains)]
        for o, (b, hd, sl) in zip(outs, chains):
            oc = o - jnp.mean(o, -1, keepdims=True)
            y = oc * lax.rsqrt(jnp.mean(oc * oc, -1, keepdims=True) + LN_EPS)
            o_ref[b, :, sl] = (y * g_ref[b, :, sl].astype(_F32)).astype(_BF16)
        for b, hd, sl in chains:
            st_ref[b, hd] = gch_ref[hd] * st_ref[b, hd] + kv_update(k_ref[b, :, sl], v_ref[b, :, sl], hd)


def _retention(rq, rk, rv, rg, rk_m, rv_m, dmask, xi, zeta, gch):
    nb, s, _ = rq.shape
    row_spec = pl.BlockSpec((nb, BLOCK, HEAD_W), lambda j: (0, j, 0))
    meta_spec = pl.BlockSpec((BLOCK, HEAD_W), lambda j: (0, 0))
    tab = pl.BlockSpec((RET_HEADS, BLOCK, BLOCK), lambda j: (0, 0, 0))
    return pl.pallas_call(
        _ret_body,
        grid=(s // BLOCK,),
        in_specs=[row_spec] * 4 + [meta_spec] * 2 + [tab] * 3 + [pl.BlockSpec(memory_space=pltpu.SMEM)],
        out_specs=row_spec,
        out_shape=jax.ShapeDtypeStruct((nb, s, HEAD_W), _BF16),
        scratch_shapes=[pltpu.VMEM((nb, RET_HEADS, RET_DK, LANES), _F32)],
        compiler_params=_params(1),
        name="retention",
    )(rq, rk, rv, rg, rk_m, rv_m, dmask, xi, zeta, gch)


def _fox_body(q_ref, k_ref, v_ref, km_ref, vm_ref, o_ref, m_sc, acc_sc, s_sc, p_sc):
    t = q_ref.shape[0]
    qi = pl.program_id(2)

    heads = [slice(hh * LANES, (hh + 1) * LANES) for hh in range(2)]

    def logits(hh, k, buf, tk):
        s_sc[buf, hh, :, :tk] = _dot_nt(q_ref[:, heads[hh]], k)

    def update(hh, v, buf, tk, mask, first):
        for c in range(t // FOX_CHUNK):
            rows = slice(c * FOX_CHUNK, (c + 1) * FOX_CHUNK)
            s = s_sc[buf, hh, rows, :tk]
            if mask is not None:
                s = mask(s, c)
            mx = jnp.max(s, axis=1, keepdims=True)
            if first:
                m_new = mx
            else:
                m_prev = m_sc[hh, rows, :]
                m_new = jnp.maximum(m_prev, mx)
                acc_sc[hh, rows, :] = jnp.exp2(m_prev - m_new) * acc_sc[hh, rows, :]
            p_sc[hh, rows, :tk] = jnp.exp2(s - m_new).astype(_BF16)
            m_sc[hh, rows, :] = m_new
        pv = _dot(p_sc[hh, :, :tk], v)
        if first:
            acc_sc[hh] = pv
        else:
            acc_sc[hh] += pv

    def meta_mask(s, c):
        key = lax.broadcasted_iota(jnp.int32, s.shape, 1)
        return jnp.where(key >= PAD, s, NEG_INF)

    def causal_mask(s, c):
        key = lax.broadcasted_iota(jnp.int32, s.shape, 1)
        query = lax.broadcasted_iota(jnp.int32, s.shape, 0) + c * FOX_CHUNK
        return jnp.where(key <= query, s, NEG_INF)

    def key_tile(ref, ki, hh):
        return ref[pl.ds(pl.multiple_of(ki * t, t), t), heads[hh]]

    for hh in range(2):
        logits(hh, km_ref[:, heads[hh]], 1, BLOCK)
    for hh in range(2):
        logits(hh, key_tile(k_ref, 0, hh), 0, t)
        update(hh, vm_ref[:, heads[hh]], 1, BLOCK, meta_mask, True)

    def step(ki, buf, mask, more):
        for hh in range(2):
            if more:
                logits(hh, key_tile(k_ref, ki + 1, hh), 1 - buf, t)
            update(hh, key_tile(v_ref, ki, hh), buf, t, mask, False)

    def pair_body(j, carry):
        step(2 * j, 0, None, True)
        step(2 * j + 1, 1, None, True)
        return carry

    lax.fori_loop(0, lax.shift_right_logical(qi, 1), pair_body, 0)
    odd = lax.rem(qi, 2) == 1

    @pl.when(odd)
    def _():
        step(qi - 1, 0, None, True)
        step(qi, 1, causal_mask, False)

    @pl.when(jnp.logical_not(odd))
    def _():
        step(qi, 0, causal_mask, False)

    outs = []
    for hh in range(2):
        acc = acc_sc[hh]
        ones_lane = (1 - hh) * FOX_HD
        outs.append(acc / acc[:, ones_lane:ones_lane + 1])
    lane = lax.broadcasted_iota(jnp.int32, (t, LANES), 1)
    o_ref[...] = jnp.where(lane < FOX_HD, outs[0], outs[1]).astype(_BF16)


def _fox(fq, fk, fv, fk_m, fv_m, *, nb):
    s = fq.shape[1]
    t = T_FOX
    pair_w = 2 * LANES
    return pl.pallas_call(
        _fox_body,
        grid=(nb, FOX_HEADS // 2, s // t),
        in_specs=[
            pl.BlockSpec((None, t, pair_w), lambda b, p, i: (b, i, p)),
            pl.BlockSpec((None, s, pair_w), lambda b, p, i: (b, 0, p)),
            pl.BlockSpec((None, s, pair_w), lambda b, p, i: (b, 0, p)),
            pl.BlockSpec((BLOCK, pair_w), lambda b, p, i: (0, p)),
            pl.BlockSpec((BLOCK, pair_w), lambda b, p, i: (0, p)),
        ],
        out_specs=pl.BlockSpec((None, t, LANES), lambda b, p, i: (b, i, p)),
        out_shape=jax.ShapeDtypeStruct((nb, s, HEAD_W), _BF16),
        scratch_shapes=[pltpu.VMEM((2, t, 1), _F32), pltpu.VMEM((2, t, LANES), _F32), pltpu.VMEM((2, 2, t, t), _F32),
                        pltpu.VMEM((2, t, t), _BF16)],
        compiler_params=_params(3),
        name="fox",
    )(fq, fk, fv, fk_m, fv_m)


def _route(scores, biased):
    w = scores.shape[1]
    sub = lax.broadcasted_iota(jnp.int32, (GROUP_SIZE, w), 0).astype(_F32)
    groups = [biased[g * GROUP_SIZE:(g + 1) * GROUP_SIZE, :] for g in range(N_GROUPS)]
    gscore = []
    for v in groups:
        m1 = jnp.max(v, axis=0, keepdims=True)
        i1 = jnp.min(jnp.where(v == m1, sub, float(GROUP_SIZE)), axis=0, keepdims=True)
        m2 = jnp.max(jnp.where(sub == i1, -jnp.inf, v), axis=0, keepdims=True)
        gscore.append(m1 + m2)
    masked = []
    for g in range(N_GROUPS):
        beaten = jnp.zeros((1, w), _F32)
        for o in range(N_GROUPS):
            if o == g:
                continue
            wins = (gscore[o] >= gscore[g]) if o < g else (gscore[o] > gscore[g])
            beaten = beaten + wins.astype(_F32)
        masked.append(jnp.where(beaten < float(TOPK_GROUPS), groups[g], NEG_INF))
    work = jnp.concatenate(masked, axis=0)
    eid = lax.broadcasted_iota(jnp.int32, (N_EXPERTS, w), 0).astype(_F32)
    sels, raws = [], []
    member = None
    for _ in range(TOP_K):
        m = jnp.max(work, axis=0, keepdims=True)
        idx = jnp.min(jnp.where(work == m, eid, float(N_EXPERTS)), axis=0, keepdims=True)
        hot = eid == idx
        sels.append(idx)
        raws.append(jnp.sum(jnp.where(hot, scores, 0.0), axis=0, keepdims=True))
        work = jnp.where(hot, -jnp.inf, work)
        member = hot if member is None else (member | hot)
    return sels, raws, member


def _post_body(h0_ref, ret_ref, fox_ref, wo_ref, g1_ref, b1_ref, wr_ref, rb_ref,
               trow_ref, sel_ref, gate_ref, rank_ref, cnt_ref, run_ref):
    tm = h0_ref.shape[0]
    i = pl.program_id(0)

    @pl.when(i == 0)
    def _():
        run_ref[...] = jnp.zeros_like(run_ref)

    y = _dot(ret_ref[...], wo_ref[:HEAD_W, :]) + _dot(fox_ref[...], wo_ref[HEAD_W:, :])
    h1 = _ln(ALPHA * h0_ref[...] + y, g1_ref[...], b1_ref[...])
    _store_rows(trow_ref, h1)

    scores = jax.nn.sigmoid(_dot_nt(wr_ref[...], h1.astype(_BF16)))
    biased = scores + rb_ref[...]
    chunks = [slice(c * LANES, (c + 1) * LANES) for c in range(tm // LANES)]
    routed = [_route(scores[:, cs], biased[:, cs]) for cs in chunks]
    member_f = jnp.concatenate([member.astype(_F32) for _, _, member in routed], axis=1)
    row = lax.broadcasted_iota(jnp.int32, (tm, tm), 0)
    col = lax.broadcasted_iota(jnp.int32, (tm, tm), 1)
    before = (row < col).astype(_BF16)
    rank_e = _dot(member_f.astype(_BF16), before) + run_ref[...]
    eid = lax.broadcasted_iota(jnp.int32, (N_EXPERTS, LANES), 0).astype(_F32)
    sel_rows, gate_rows, rank_rows = [], [], []
    for k in range(TOP_K):
        sel_k, gate_k, rank_k = [], [], []
        for cs, (sels, raws, _) in zip(chunks, routed):
            total = raws[0]
            for r in raws[1:]:
                total = total + r
            sel_k.append(sels[k].astype(jnp.int32))
            gate_k.append(raws[k] * (ROUTED_SCALE / total))
            rank_k.append(jnp.sum(jnp.where(eid == sels[k], rank_e[:, cs], 0.0), axis=0, keepdims=True).astype(jnp.int32))
        sel_rows.append(jnp.concatenate(sel_k, axis=1))
        gate_rows.append(jnp.concatenate(gate_k, axis=1))
        rank_rows.append(jnp.concatenate(rank_k, axis=1))
    pad_rows = sel_ref.shape[0] - TOP_K
    sel_ref[...] = jnp.concatenate(sel_rows + [jnp.zeros((pad_rows, tm), jnp.int32)], axis=0)
    gate_ref[...] = jnp.concatenate(gate_rows + [jnp.zeros((pad_rows, tm), _F32)], axis=0)
    rank_ref[...] = jnp.concatenate(rank_rows + [jnp.zeros((pad_rows, tm), jnp.int32)], axis=0)
    run_ref[...] = run_ref[...] + jnp.sum(member_f, axis=1, keepdims=True)
    cnt_ref[...] = run_ref[...]


def _post(h0, ret, fox, w_out, ln1_g, ln1_b, w_rt, rbias):
    n = h0.shape[0]
    tm = TM_PROJ
    row_spec = lambda w: pl.BlockSpec((tm, w), lambda i: (i, 0))
    col_spec = pl.BlockSpec((SUBLANES, tm), lambda i: (0, i))
    const = lambda shape: pl.BlockSpec(shape, lambda i: (0, 0))
    return pl.pallas_call(
        _post_body,
        grid=(n // tm,),
        in_specs=[row_spec(D_MODEL), row_spec(HEAD_W), row_spec(HEAD_W),
                  const(w_out.shape), const((1, D_MODEL)), const((1, D_MODEL)), const(w_rt.shape),
                  const((N_EXPERTS, 1))],
        out_specs=[pl.BlockSpec((tm * ROW_TILES, LANES), lambda i: (i, 0)),
                   col_spec, col_spec, col_spec, const((N_EXPERTS, 1))],
        out_shape=[jax.ShapeDtypeStruct((n * ROW_TILES, LANES), ROW_DTYPE),
                   jax.ShapeDtypeStruct((SUBLANES, n), jnp.int32), jax.ShapeDtypeStruct((SUBLANES, n), _F32),
                   jax.ShapeDtypeStruct((SUBLANES, n), jnp.int32), jax.ShapeDtypeStruct((N_EXPERTS, 1), _F32)],
        scratch_shapes=[pltpu.VMEM((N_EXPERTS, 1), _F32)],
        compiler_params=_params(1),
        name="post_mixer",
    )(h0, ret, fox, w_out, ln1_g, ln1_b, w_rt, rbias)


def _plan_body(sel_ref, rank_ref, cnt_ref, dest_ref, blk_ref, fill_ref, used_ref, tab_ref):
    cnt = cnt_ref[...]
    padded = jnp.ceil(cnt * (1.0 / EBLK)) * EBLK
    er = lax.broadcasted_iota(jnp.int32, (N_EXPERTS, N_EXPERTS), 0)
    ec = lax.broadcasted_iota(jnp.int32, (N_EXPERTS, N_EXPERTS), 1)
    padded_row = jnp.sum(jnp.where(er == ec, padded, 0.0), axis=0, keepdims=True)
    pstart = jnp.sum(jnp.where(ec < er, padded_row, 0.0), axis=1, keepdims=True)
    pend = pstart + padded
    sel = sel_ref[...]
    dest = rank_ref[...]
    for e in range(N_EXPERTS):
        dest = dest + jnp.where(sel == e, pstart[e:e + 1, :].astype(jnp.int32), 0)
    dest_ref[...] = dest
    nblk = blk_ref.shape[1]
    first_row = (lax.broadcasted_iota(jnp.int32, (N_EXPERTS, nblk), 1) * EBLK).astype(_F32)
    owner = jnp.minimum(jnp.sum((pend <= first_row).astype(_F32), axis=0, keepdims=True), N_EXPERTS - 1.0)
    blk_ref[...] = owner.astype(jnp.int32)
    mine = lax.broadcasted_iota(jnp.int32, (N_EXPERTS, nblk), 0).astype(_F32) == owner
    live_end = jnp.sum(jnp.where(mine, pstart + cnt, 0.0), axis=0, keepdims=True)
    fill_ref[...] = jnp.clip(live_end - first_row[:1, :], 0.0, float(EBLK)).astype(jnp.int32)
    used_ref[...] = (pend[N_EXPERTS - 1:, :] * (1.0 / EBLK)).astype(jnp.int32)
    as_row = lambda col: jnp.sum(jnp.where(er == ec, col, 0.0), axis=0, keepdims=True).astype(jnp.int32)
    rows = {TAB_PAD0: pstart + cnt, TAB_PADN: padded - cnt}
    blank = jnp.zeros((1, N_EXPERTS), jnp.int32)
    tab_ref[...] = jnp.concatenate([as_row(rows[r]) if r in rows else blank for r in range(SUBLANES)], axis=0)


def _plan(sel, rank, cnt, nblk_pad):
    n = sel.shape[1]
    full = lambda shape: pl.BlockSpec(shape, lambda i: (0, 0))
    return pl.pallas_call(
        _plan_body,
        grid=(1,),
        in_specs=[full(sel.shape), full(rank.shape), full(cnt.shape)],
        out_specs=[full(sel.shape), full((1, nblk_pad)), full((1, nblk_pad)), full((1, 1)),
                   full((SUBLANES, N_EXPERTS))],
        out_shape=[jax.ShapeDtypeStruct((SUBLANES, n), jnp.int32), jax.ShapeDtypeStruct((1, nblk_pad), jnp.int32),
                   jax.ShapeDtypeStruct((1, nblk_pad), jnp.int32), jax.ShapeDtypeStruct((1, 1), jnp.int32),
                   jax.ShapeDtypeStruct((SUBLANES, N_EXPERTS), jnp.int32)],
        compiler_params=_params(1),
        name="plan",
    )(sel, rank, cnt)


def _store_rows(ref, v):
    m = v.shape[0]
    for s in range(ROW_TILES):
        ref[pl.ds(s, m, stride=ROW_TILES), :] = v[:, s * LANES:(s + 1) * LANES].astype(ROW_DTYPE)


def _load_rows(ref, first_row, m):
    return jnp.concatenate([ref[pl.ds(first_row * ROW_TILES + s, m, stride=ROW_TILES), :] for s in range(ROW_TILES)],
                           axis=1)


def _row_copy(src, src_row, dst, dst_row, sem):
    return pltpu.make_async_copy(src.at[pl.ds(pl.multiple_of(src_row * ROW_TILES, ROW_TILES), ROW_TILES), :],
                                 dst.at[pl.ds(pl.multiple_of(dst_row * ROW_TILES, ROW_TILES), ROW_TILES), :], sem)


def _dispatch_body(dest_ref, tab_ref, used_ref, t_ref, wgu_ref, wd_ref, xs_ref, base_ref, zero_sc, sem, zsem):
    tt = t_ref.shape[0] // ROW_TILES
    half_blk = EBLK // 2
    n_half = xs_ref.shape[0] // (half_blk * ROW_TILES)

    @pl.when(pl.program_id(0) == 0)
    def _():
        zero_sc[...] = jnp.zeros_like(zero_sc)

        def zero_copy(first_row, rows):
            return pltpu.make_async_copy(
                zero_sc.at[pl.ds(0, rows * ROW_TILES), :],
                xs_ref.at[pl.ds(pl.multiple_of(first_row * ROW_TILES, ROW_TILES), rows * ROW_TILES), :], zsem)

        def for_padding(act):
            def per_expert(e, carry):
                row = tab_ref[TAB_PAD0, e]
                for bit in range(EBLK.bit_length() - 2, -1, -1):
                    take = (tab_ref[TAB_PADN, e] & (1 << bit)) != 0

                    @pl.when(take)
                    def _():
                        act(zero_copy(row, 1 << bit))

                    row = row + jnp.where(take, 1 << bit, 0)
                return carry

            def per_tail(hb, carry):
                act(zero_copy(hb * half_blk, half_blk))
                return carry

            lax.fori_loop(0, N_EXPERTS, per_expert, 0)
            lax.fori_loop(2 * used_ref[0], n_half, per_tail, 0)

        for_padding(lambda cp: cp.start())
        for_padding(lambda cp: cp.wait())

    def issue(j, carry):
        for u in range(ISSUE_UNROLL):
            i = ISSUE_UNROLL * j + u
            for k in range(TOP_K):
                _row_copy(t_ref, i, xs_ref, dest_ref[i * SUBLANES + k], sem).start(priority=k % 2)
        return carry

    lax.fori_loop(0, tt // ISSUE_UNROLL, issue, 0)

    h1 = _load_rows(t_ref, 0, tt)
    gu = _dot(h1.astype(_BF16), wgu_ref[...])
    mid = (_silu(gu[:, :SHARED_FF]) * gu[:, SHARED_FF:]).astype(_BF16)
    base_ref[...] = ALPHA * h1 + _dot(mid, wd_ref[...])

    for _ in range(TOP_K):
        pltpu.make_async_copy(t_ref, xs_ref.at[pl.ds(0, tt * ROW_TILES), :], sem).wait()


def _dispatch(dest_flat, tab, used, trow, w_gu, w_sd, total_rows):
    n = trow.shape[0] // ROW_TILES
    tt = TT_DISPATCH
    smem = pl.BlockSpec(memory_space=pltpu.SMEM)
    const = lambda shape: pl.BlockSpec(shape, lambda i: (0, 0))
    return pl.pallas_call(
        _dispatch_body,
        grid=(n // tt,),
        in_specs=[pl.BlockSpec((tt * SUBLANES,), lambda i: (i,), memory_space=pltpu.SMEM), smem, smem,
                  pl.BlockSpec((tt * ROW_TILES, LANES), lambda i: (i, 0)), const(w_gu.shape), const(w_sd.shape)],
        out_specs=[pl.BlockSpec(memory_space=pl.ANY), pl.BlockSpec((tt, D_MODEL), lambda i: (i, 0))],
        out_shape=[jax.ShapeDtypeStruct((total_rows * ROW_TILES, LANES), ROW_DTYPE),
                   jax.ShapeDtypeStruct((n, D_MODEL), _F32)],
        scratch_shapes=[pltpu.VMEM((EBLK // 2 * ROW_TILES, LANES), ROW_DTYPE), pltpu.SemaphoreType.DMA(()),
                        pltpu.SemaphoreType.DMA(())],
        compiler_params=_params(1),
        name="dispatch",
    )(dest_flat, tab, used, trow, w_gu, w_sd)


def _expert_body(blk_ref, used_ref, fill_ref, xs_ref, wg_ref, wu_ref, wd_ref, y_ref, wgu_sc, wd_sc):
    i = pl.program_id(0)
    prev = blk_ref[jnp.maximum(i - 1, 0)]
    fresh = (i == 0) | (blk_ref[i] != prev)

    @pl.when(fresh)
    def _():
        wgu_sc[:, :EXPERT_FF] = wg_ref[...].astype(_BF16)
        wgu_sc[:, EXPERT_FF:] = wu_ref[...].astype(_BF16)
        wd_sc[...] = wd_ref[...].astype(_BF16)

    @pl.when(i < used_ref[0])
    def _():
        live = lax.broadcasted_iota(jnp.int32, (EBLK, 1), 0) < fill_ref[i]
        x = _load_rows(xs_ref, 0, EBLK)
        x = jnp.where(live, x, jnp.zeros_like(x)).astype(_BF16)
        gu = _dot(x, wgu_sc[...])
        mid = (_silu(gu[:, :EXPERT_FF]) * gu[:, EXPERT_FF:]).astype(_BF16)
        _store_rows(y_ref, _dot(mid, wd_sc[...]))

    @pl.when(i >= used_ref[0])
    def _():
        y_ref[...] = jnp.zeros_like(y_ref)


def _experts(blk_e, used, fill, xs, we_gate, we_up, we_down):
    nblk = xs.shape[0] // (EBLK * ROW_TILES)
    last = lambda i, used: jnp.minimum(i, jnp.maximum(used[0] - 1, 0))
    w_spec = lambda shape: pl.BlockSpec((None,) + shape, lambda i, blk, used, fill: (blk[i], 0, 0))
    return pl.pallas_call(
        _expert_body,
        grid_spec=pltpu.PrefetchScalarGridSpec(
            num_scalar_prefetch=3,
            grid=(nblk,),
            in_specs=[pl.BlockSpec((EBLK * ROW_TILES, LANES), lambda i, blk, used, fill: (last(i, used), 0)),
                      w_spec((D_MODEL, EXPERT_FF)), w_spec((D_MODEL, EXPERT_FF)), w_spec((EXPERT_FF, D_MODEL))],
            out_specs=pl.BlockSpec((EBLK * ROW_TILES, LANES), lambda i, blk, used, fill: (i, 0)),
            scratch_shapes=[pltpu.VMEM((D_MODEL, 2 * EXPERT_FF), _BF16), pltpu.VMEM((EXPERT_FF, D_MODEL), _BF16)],
        ),
        out_shape=jax.ShapeDtypeStruct(xs.shape, ROW_DTYPE),
        compiler_params=_params(1),
        name="experts",
    )(blk_e, used, fill, xs, we_gate, we_up, we_down)


def _combine_body(dest_ref, dnext_ref, y_ref, base_ref, gate_ref, g2_ref, b2_ref, o_ref, z_sc, sems):
    i = pl.program_id(0)
    slot = lax.rem(i, 2)
    tile_rows = TT * TOP_K * ROW_TILES

    def gather(dref, into):
        def issue(j, carry):
            for u in range(ISSUE_UNROLL):
                t = ISSUE_UNROLL * j + u
                for k in range(TOP_K):
                    _row_copy(y_ref, dref[t * SUBLANES + k], z_sc.at[into], k * TT + t,
                              sems.at[into]).start(priority=k % 2)
            return carry

        lax.fori_loop(0, TT // ISSUE_UNROLL, issue, 0)

    @pl.when(i == 0)
    def _():
        gather(dest_ref, 0)

    @pl.when(i + 1 < pl.num_programs(0))
    def _():
        gather(dnext_ref, 1 - slot)

    pltpu.make_async_copy(y_ref.at[pl.ds(0, tile_rows), :], z_sc.at[slot], sems.at[slot]).wait()
    gates = gate_ref[...]
    acc = base_ref[...]
    for k in range(TOP_K):
        acc = acc + gates[:, k:k + 1] * _load_rows(z_sc.at[slot], k * TT, TT).astype(_F32)
    o_ref[...] = _ln(acc, g2_ref[...], b2_ref[...])


def _combine(dest_flat, y, base, gates_t, ln2_g, ln2_b):
    n = base.shape[0]
    steps = n // TT
    const = lambda shape: pl.BlockSpec(shape, lambda i: (0, 0))
    dest_spec = lambda ahead: pl.BlockSpec((TT * SUBLANES,), lambda i: (jnp.minimum(i + ahead, steps - 1),),
                                           memory_space=pltpu.SMEM)
    return pl.pallas_call(
        _combine_body,
        grid=(steps,),
        in_specs=[dest_spec(0), dest_spec(1),
                  pl.BlockSpec(memory_space=pl.ANY),
                  pl.BlockSpec((TT, D_MODEL), lambda i: (i, 0)),
                  pl.BlockSpec((TT, SUBLANES), lambda i: (i, 0)),
                  const((1, D_MODEL)), const((1, D_MODEL))],
        out_specs=pl.BlockSpec((TT, D_MODEL), lambda i: (i, 0)),
        out_shape=jax.ShapeDtypeStruct((n, D_MODEL), _F32),
        scratch_shapes=[pltpu.VMEM((2, TT * TOP_K * ROW_TILES, LANES), ROW_DTYPE), pltpu.SemaphoreType.DMA((2,))],
        compiler_params=_params(1),
        name="combine",
    )(dest_flat, dest_flat, y, base, gates_t, ln2_g, ln2_b)


def _rope_tables(pos):
    half = RET_DK // 2
    inv = ROPE_BASE ** (-jnp.arange(half, dtype=_F32) / half)
    ang = pos[:, None] * inv[None, :]
    cos = jnp.cos(ang)
    sin = jnp.sin(ang)
    return jnp.concatenate([cos, cos], -1), jnp.concatenate([-sin, sin], -1)


def _decay_tables():
    lg = jnp.log1p(-jnp.exp2(-5.0 - jnp.arange(RET_HEADS, dtype=_F32)))
    idx = jnp.arange(BLOCK, dtype=_F32)
    rel = idx[:, None] - idx[None, :]
    causal = rel >= 0
    dmask = jnp.where(causal[None], jnp.exp(jnp.where(causal, rel, 0.0)[None] * lg[:, None, None]), 0.0)
    zeta = jnp.exp((BLOCK - 1.0 - idx)[None, :] * lg[:, None])
    xi = jnp.exp((idx + 1.0)[None, :] * lg[:, None])
    along_lanes = lambda col: jnp.broadcast_to(col[:, :, None], (RET_HEADS, BLOCK, LANES))
    return dmask, along_lanes(xi), along_lanes(zeta), jnp.exp(BLOCK * lg)


def kernel(x, meta, ln0_g, ln0_b, w_in, b_forget, w_out, ln1_g, ln1_b, w_router, router_bias, we_gate, we_up,
           we_down, ws_gate, ws_up, ws_down, ln2_g, ln2_b):
    nb, s, d = x.shape
    assert d == D_MODEL and meta.shape == (N_META, D_MODEL) and w_in.shape[0] == 1
    assert s % TM_PROJ == 0 and s % T_FOX == 0 and (nb * s) % TT == 0
    n = nb * s
    x2d = x.reshape(n, d)
    row2 = lambda v: v.reshape(1, -1).astype(_F32)
    main_cols = 7 * HEAD_W
    w_all = jnp.concatenate(
        [w_in[0, :, :main_cols], w_in[0, :, main_cols:], jnp.zeros((d, LANES - FOX_HEADS), w_in.dtype)],
        axis=1).astype(_BF16)
    bf_pad = jnp.concatenate([b_forget[0].astype(_F32), jnp.zeros((LANES - FOX_HEADS,), _F32)]).reshape(1, LANES)
    g0, b0 = row2(ln0_g), row2(ln0_b)

    cos_x, sin_x = _rope_tables(jnp.arange(s, dtype=_F32) + float(N_META))
    cos_m, sin_m = _rope_tables(jnp.arange(BLOCK, dtype=_F32) - float(PAD))
    meta_blk = jnp.concatenate([jnp.zeros((PAD, d), _F32), meta.astype(_F32)], axis=0)

    h0, rq, rk, rv, rg, fq, fk, fv = _inproj(x2d, g0, b0, w_all, bf_pad, cos_x, sin_x, nb=nb, meta=False)
    _, _, rk_m, rv_m, _, _, fk_m, fv_m = _inproj(meta_blk, g0, b0, w_all, bf_pad, cos_m, sin_m, nb=1, meta=True)

    dmask, xi, zeta, gch = _decay_tables()
    per_batch = lambda a: a.reshape(nb, s, a.shape[-1])
    ret = _retention(per_batch(rq), per_batch(rk), per_batch(rv), per_batch(rg), rk_m, rv_m, dmask, xi, zeta,
                     gch).reshape(n, HEAD_W)
    fox = _fox(per_batch(fq), per_batch(fk), per_batch(fv), fk_m, fv_m, nb=nb).reshape(n, HEAD_W)

    w_gu = jnp.concatenate([ws_gate[0], ws_up[0]], axis=1).astype(_BF16)
    trow, sel, gates, rank, cnt = _post(
        h0, ret, fox, w_out[0].astype(_BF16), row2(ln1_g[0]), row2(ln1_b[0]),
        jnp.transpose(w_router[0]).astype(_BF16), router_bias[0].astype(_F32).reshape(N_EXPERTS, 1))

    nblk = n * TOP_K // EBLK + N_EXPERTS
    nblk_pad = -(-nblk // LANES) * LANES
    dest, blk_e, fill, used, tab = _plan(sel, rank, cnt, nblk_pad)
    dest_flat = jnp.transpose(dest).reshape(-1)
    used = used.reshape(-1)

    xs, base = _dispatch(dest_flat, tab, used, trow, w_gu, ws_down[0].astype(_BF16), nblk * EBLK)
    y = _experts(blk_e.reshape(-1), used, fill.reshape(-1), xs, we_gate[0], we_up[0], we_down[0])
    out = _combine(dest_flat, y, base, jnp.transpose(gates), row2(ln2_g[0]), row2(ln2_b[0]))
    return out.reshape(nb, s, d)
```

```python
import functools

import jax
import jax.numpy as jnp
import numpy as np
from jax import lax
from jax.experimental import pallas as pl
from jax.experimental.pallas import tpu as pltpu

D_MODEL = 1024
N_META = 16
BLOCK = 128
PAD = BLOCK - N_META
RET_HEADS = 4
RET_DK = 128
FOX_HEADS = 8
FOX_HD = 64
N_EXPERTS = 64
TOP_K = 6
N_GROUPS = 8
GROUP_SIZE = N_EXPERTS // N_GROUPS
TOPK_GROUPS = 4
EXPERT_FF = 256
SHARED_FF = 256
ROUTED_SCALE = 2.5
ROPE_BASE = 10000.0
LN_EPS = 1e-5
NEG_INF = -1e30
ALPHA = 2.0 ** 0.25
HEAD_W = 512
LOG2E = 1.4426950408889634

LANES = 128
FOX_W = FOX_HEADS * LANES
SUBLANES = 8
ROW_TILES = D_MODEL // LANES
ROW_DTYPE = jnp.float32

TM_PROJ = 512
T_FOX = 512
FOX_CHUNK = 64
FOX_K_TERMS = 32
TT = 256
TT_DISPATCH = 512
EBLK = 512
ISSUE_UNROLL = 4
TAB_PAD0, TAB_PADN = 0, 1
VMEM_LIMIT = 48 * 1024 * 1024

_F32 = jnp.float32
_BF16 = jnp.bfloat16


def _ln(x, g, b):
    xc = x - jnp.mean(x, -1, keepdims=True)
    var = jnp.mean(xc * xc, -1, keepdims=True)
    return xc * lax.rsqrt(var + LN_EPS) * g + b


def _dot(a, b):
    return jnp.dot(a, b, preferred_element_type=_F32)


def _dot_nt(a, b):
    return lax.dot_general(a, b, (((1,), (1,)), ((), ())), preferred_element_type=_F32)


def _dot_tn(a, b):
    return lax.dot_general(a, b, (((0,), (0,)), ((), ())), preferred_element_type=_F32)


def _silu(x):
    return x * jax.nn.sigmoid(x)


def _params(n_axes):
    return pltpu.CompilerParams(dimension_semantics=("arbitrary",) * n_axes, vmem_limit_bytes=VMEM_LIMIT)


def _inproj_body(x_ref, g_ref, b_ref, w_ref, bf_ref, cos_ref, sin_ref, own_ref, tq_ref, tk_ref, oq_ref, ok_ref, ov_ref,
                 h_ref, rq_ref, rk_ref, rv_ref, rg_ref, fq_ref, fk_ref, fv_ref, carry_ref, *, meta):
    tm = x_ref.shape[0]
    if not meta:
        @pl.when(pl.program_id(1) == 0)
        def _():
            carry_ref[...] = jnp.zeros_like(carry_ref)

    h = _ln(x_ref[...], g_ref[...], b_ref[...])
    if meta:
        valid = lax.broadcasted_iota(jnp.int32, (tm, 1), 0) >= PAD
        h = jnp.where(valid, h, 0.0)
    h_ref[...] = h
    hb = h.astype(_BF16)
    cos = cos_ref[...]
    sin = sin_ref[...]

    def proj(g):
        return _dot(hb, w_ref[:, g * HEAD_W:(g + 1) * HEAD_W])

    def rope_store(p, out_ref, scale):
        for hd in range(RET_HEADS):
            t = p[:, hd * LANES:(hd + 1) * LANES]
            r = t * cos + pltpu.roll(t, LANES // 2, axis=1) * sin
            out_ref[:, hd * LANES:(hd + 1) * LANES] = (r * scale).astype(_BF16)

    z = _dot(hb, w_ref[:, 7 * HEAD_W:7 * HEAD_W + LANES]) + bf_ref[...]
    rope_store(proj(0), rq_ref, 1.0)
    logf = jnp.minimum(z, 0.0) - jnp.log1p(jnp.exp(-jnp.abs(z)))
    if meta:
        logf = jnp.where(valid, logf, 0.0)
    l1 = logf.astype(_BF16)
    r1 = logf - l1.astype(_F32)
    l2 = r1.astype(_BF16)
    l3 = (r1 - l2.astype(_F32)).astype(_BF16)
    row = lax.broadcasted_iota(jnp.int32, (tm, tm), 0)
    col = lax.broadcasted_iota(jnp.int32, (tm, tm), 1)
    tri = (col <= row).astype(_BF16)
    rope_store(proj(1), rk_ref, RET_DK ** -0.5)
    c = _dot(tri, l1) + _dot(tri, l2) + _dot(tri, l3)
    rv_ref[...] = proj(2).astype(_BF16)
    rg_ref[...] = _silu(proj(3)).astype(_BF16)
    if meta:
        c = c - c[tm - 1:tm, :]
    else:
        c = c + carry_ref[...]
        carry_ref[...] = c[tm - 1:tm, :]

    head_lane = lax.broadcasted_iota(jnp.int32, (1, LANES), 1) < FOX_HEADS
    cl = jnp.where(head_lane, c, 0.0) * LOG2E
    c1 = cl.astype(_BF16).astype(_F32)
    r1 = cl - c1
    c2 = r1.astype(_BF16).astype(_F32)
    c3 = (r1 - c2).astype(_BF16).astype(_F32)
    csplit = c1 + pltpu.roll(c2, FOX_HEADS, axis=1) + pltpu.roll(c3, 2 * FOX_HEADS, axis=1)
    half = FOX_HD
    q_even, q_odd = pltpu.roll(csplit, half, axis=1), csplit
    k_even, k_odd = pltpu.roll(csplit, half + FOX_K_TERMS, axis=1), pltpu.roll(csplit, FOX_K_TERMS, axis=1)
    own = own_ref[...] > 0.0

    def per_head(p):
        return jnp.concatenate([p[:, (hd // 2) * LANES:(hd // 2 + 1) * LANES] for hd in range(FOX_HEADS)], axis=1)

    def by_parity(even, odd):
        return jnp.concatenate([even, odd] * (FOX_HEADS // 2), axis=1)

    q_extra = jnp.where(tq_ref[...] > 0.0, by_parity(q_even, q_odd), oq_ref[...])
    k_extra = jnp.where(tk_ref[...] > 0.0, -by_parity(k_even, k_odd), ok_ref[...])
    fq_ref[...] = jnp.where(own, per_head(proj(4) * (FOX_HD ** -0.5 * LOG2E)), q_extra).astype(_BF16)
    fk_ref[...] = jnp.where(own, per_head(proj(5)), k_extra).astype(_BF16)
    fv_ref[...] = jnp.where(own, per_head(proj(6)), ov_ref[...]).astype(_BF16)


def _fox_lane_tables():
    own, tq, tk, oq, ok, ov = (np.zeros((1, FOX_W), np.float32) for _ in range(6))
    for hd in range(FOX_HEADS):
        data = hd * LANES + (hd % 2) * FOX_HD
        extra = hd * LANES + (1 - hd % 2) * FOX_HD
        own[0, data:data + FOX_HD] = 1.0
        for term in range(3):
            lane = extra + term * FOX_HEADS + hd
            tq[0, lane] = 1.0
            ok[0, lane] = 1.0
            tk[0, lane + FOX_K_TERMS] = 1.0
            oq[0, lane + FOX_K_TERMS] = 1.0
        ov[0, extra] = 1.0
    return tuple(jnp.asarray(t) for t in (own, tq, tk, oq, ok, ov))


def _inproj(x2d, ln_g, ln_b, w_all, bf_pad, cos_t, sin_t, *, nb, meta):
    n = x2d.shape[0]
    s = n // nb
    tm = min(TM_PROJ, s)
    nj = s // tm
    row_spec = lambda w: pl.BlockSpec((tm, w), lambda b, j: (b * nj + j, 0))
    const = lambda shape: pl.BlockSpec(shape, lambda b, j: (0, 0))
    pos_spec = pl.BlockSpec((tm, LANES), lambda b, j: (j, 0))
    tables = _fox_lane_tables()
    outs = ([jax.ShapeDtypeStruct((n, D_MODEL), _F32)] + [jax.ShapeDtypeStruct((n, HEAD_W), _BF16)] * 4
            + [jax.ShapeDtypeStruct((n, FOX_W), _BF16)] * 3)
    return pl.pallas_call(
        functools.partial(_inproj_body, meta=meta),
        grid=(nb, nj),
        in_specs=[row_spec(D_MODEL), const((1, D_MODEL)), const((1, D_MODEL)), const(w_all.shape),
                  const((1, LANES)), pos_spec, pos_spec] + [const(t.shape) for t in tables],
        out_specs=[row_spec(D_MODEL)] + [row_spec(HEAD_W)] * 4 + [row_spec(FOX_W)] * 3,
        out_shape=outs,
        scratch_shapes=[pltpu.VMEM((1, LANES), _F32)],
        compiler_params=_params(2),
        name="inproj_meta" if meta else "inproj",
    )(x2d, ln_g, ln_b, w_all, bf_pad, cos_t, sin_t, *tables)


def _ret_body(q_ref, k_ref, v_ref, g_ref, km_ref, vm_ref, dm_ref, xi_ref, zeta_ref, gch_ref, o_ref, st_ref):
    def kv_update(k, v, hd):
        vz = (v.astype(_F32) * zeta_ref[hd]).astype(_BF16)
        return _dot_tn(k, vz)

    nb = q_ref.shape[0]

    @pl.when(pl.program_id(0) == 0)
    def _():
        for hd in range(RET_HEADS):
            sl = slice(hd * LANES, (hd + 1) * LANES)
            first = kv_update(km_ref[:, sl], vm_ref[:, sl], hd)
            for b in range(nb):
                st_ref[b, hd] = first

    group = 4
    for b0 in range(0, nb, group):
        chains = [(b, hd, slice(hd * LANES, (hd + 1) * LANES)) for b in range(b0, b0 + group) for hd in range(RET_HEADS)]
        scores = [_dot_nt(q_ref[b, :, sl], k_ref[b, :, sl]) * dm_ref[hd] for b, hd, sl in chains]
        cross = [_dot(q_ref[b, :, sl], st_ref[b, hd].astype(_BF16)) * xi_ref[hd] for b, hd, sl in chains]
        outs = [_dot(s.astype(_BF16), v_ref[b, :, sl]) + c for s, c, (b, hd, sl) in zip(scores, cross, chains)]
        for o, (b, hd, sl) in zip(outs, chains):
            oc = o - jnp.mean(o, -1, keepdims=True)
            y = oc * lax.rsqrt(jnp.mean(oc * oc, -1, keepdims=True) + LN_EPS)
            o_ref[b, :, sl] = (y * g_ref[b, :, sl].astype(_F32)).astype(_BF16)
        for b, hd, sl in chains:
            st_ref[b, hd] = gch_ref[hd] * st_ref[b, hd] + kv_update(k_ref[b, :, sl], v_ref[b, :, sl], hd)


def _retention(rq, rk, rv, rg, rk_m, rv_m, dmask, xi, zeta, gch):
    nb, s, _ = rq.shape
    row_spec = pl.BlockSpec((nb, BLOCK, HEAD_W), lambda j: (0, j, 0))
    meta_spec = pl.BlockSpec((BLOCK, HEAD_W), lambda j: (0, 0))
    tab = pl.BlockSpec((RET_HEADS, BLOCK, BLOCK), lambda j: (0, 0, 0))
    return pl.pallas_call(
        _ret_body,
        grid=(s // BLOCK,),
        in_specs=[row_spec] * 4 + [meta_spec] * 2 + [tab] * 3 + [pl.BlockSpec(memory_space=pltpu.SMEM)],
        out_specs=row_spec,
        out_shape=jax.ShapeDtypeStruct((nb, s, HEAD_W), _BF16),
        scratch_shapes=[pltpu.VMEM((nb, RET_HEADS, RET_DK, LANES), _F32)],
        compiler_params=_params(1),
        name="retention",
    )(rq, rk, rv, rg, rk_m, rv_m, dmask, xi, zeta, gch)


def _fox_body(q_ref, k_ref, v_ref, km_ref, vm_ref, o_ref, m_sc, acc_sc, s_sc, p_sc):
    t = q_ref.shape[0]
    qi = pl.program_id(2)

    heads = [slice(hh * LANES, (hh + 1) * LANES) for hh in range(2)]

    def logits(hh, k, buf, tk):
        s_sc[buf, hh, :, :tk] = _dot_nt(q_ref[:, heads[hh]], k)

    def update(hh, v, buf, tk, mask, first):
        for c in range(t // FOX_CHUNK):
            rows = slice(c * FOX_CHUNK, (c + 1) * FOX_CHUNK)
            s = s_sc[buf, hh, rows, :tk]
            if mask is not None:
                s = mask(s, c)
            mx = jnp.max(s, axis=1, keepdims=True)
            if first:
                m_new = mx
            else:
                m_prev = m_sc[hh, rows, :]
                m_new = jnp.maximum(m_prev, mx)
                acc_sc[hh, rows, :] = jnp.exp2(m_prev - m_new) * acc_sc[hh, rows, :]
            p_sc[hh, rows, :tk] = jnp.exp2(s - m_new).astype(_BF16)
            m_sc[hh, rows, :] = m_new
        pv = _dot(p_sc[hh, :, :tk], v)
        if first:
            acc_sc[hh] = pv
        else:
            acc_sc[hh] += pv

    def meta_mask(s, c):
        key = lax.broadcasted_iota(jnp.int32, s.shape, 1)
        return jnp.where(key >= PAD, s, NEG_INF)

    def causal_mask(s, c):
        key = lax.broadcasted_iota(jnp.int32, s.shape, 1)
        query = lax.broadcasted_iota(jnp.int32, s.shape, 0) + c * FOX_CHUNK
        return jnp.where(key <= query, s, NEG_INF)

    def key_tile(ref, ki, hh):
        return ref[pl.ds(pl.multiple_of(ki * t, t), t), heads[hh]]

    for hh in range(2):
        logits(hh, km_ref[:, heads[hh]], 1, BLOCK)
    for hh in range(2):
        logits(hh, key_tile(k_ref, 0, hh), 0, t)
        update(hh, vm_ref[:, heads[hh]], 1, BLOCK, meta_mask, True)

    def step(ki, buf, mask, more):
        for hh in range(2):
            if more:
                logits(hh, key_tile(k_ref, ki + 1, hh), 1 - buf, t)
            update(hh, key_tile(v_ref, ki, hh), buf, t, mask, False)

    def pair_body(j, carry):
        step(2 * j, 0, None, True)
        step(2 * j + 1, 1, None, True)
        return carry

    lax.fori_loop(0, lax.shift_right_logical(qi, 1), pair_body, 0)
    odd = lax.rem(qi, 2) == 1

    @pl.when(odd)
    def _():
        step(qi - 1, 0, None, True)
        step(qi, 1, causal_mask, False)

    @pl.when(jnp.logical_not(odd))
    def _():
        step(qi, 0, causal_mask, False)

    outs = []
    for hh in range(2):
        acc = acc_sc[hh]
        ones_lane = (1 - hh) * FOX_HD
        outs.append(acc / acc[:, ones_lane:ones_lane + 1])
    lane = lax.broadcasted_iota(jnp.int32, (t, LANES), 1)
    o_ref[...] = jnp.where(lane < FOX_HD, outs[0], outs[1]).astype(_BF16)


def _fox(fq, fk, fv, fk_m, fv_m, *, nb):
    s = fq.shape[1]
    t = T_FOX
    pair_w = 2 * LANES
    return pl.pallas_call(
        _fox_body,
        grid=(nb, FOX_HEADS // 2, s // t),
        in_specs=[
            pl.BlockSpec((None, t, pair_w), lambda b, p, i: (b, i, p)),
            pl.BlockSpec((None, s, pair_w), lambda b, p, i: (b, 0, p)),
            pl.BlockSpec((None, s, pair_w), lambda b, p, i: (b, 0, p)),
            pl.BlockSpec((BLOCK, pair_w), lambda b, p, i: (0, p)),
            pl.BlockSpec((BLOCK, pair_w), lambda b, p, i: (0, p)),
        ],
        out_specs=pl.BlockSpec((None, t, LANES), lambda b, p, i: (b, i, p)),
        out_shape=jax.ShapeDtypeStruct((nb, s, HEAD_W), _BF16),
        scratch_shapes=[pltpu.VMEM((2, t, 1), _F32), pltpu.VMEM((2, t, LANES), _F32), pltpu.VMEM((2, 2, t, t), _F32),
                        pltpu.VMEM((2, t, t), _BF16)],
        compiler_params=_params(3),
        name="fox",
    )(fq, fk, fv, fk_m, fv_m)


def _route(scores, biased):
    w = scores.shape[1]
    sub = lax.broadcasted_iota(jnp.int32, (GROUP_SIZE, w), 0).astype(_F32)
    groups = [biased[g * GROUP_SIZE:(g + 1) * GROUP_SIZE, :] for g in range(N_GROUPS)]
    gscore = []
    for v in groups:
        m1 = jnp.max(v, axis=0, keepdims=True)
        i1 = jnp.min(jnp.where(v == m1, sub, float(GROUP_SIZE)), axis=0, keepdims=True)
        m2 = jnp.max(jnp.where(sub == i1, -jnp.inf, v), axis=0, keepdims=True)
        gscore.append(m1 + m2)
    masked = []
    for g in range(N_GROUPS):
        beaten = jnp.zeros((1, w), _F32)
        for o in range(N_GROUPS):
            if o == g:
                continue
            wins = (gscore[o] >= gscore[g]) if o < g else (gscore[o] > gscore[g])
            beaten = beaten + wins.astype(_F32)
        masked.append(jnp.where(beaten < float(TOPK_GROUPS), groups[g], NEG_INF))
    work = jnp.concatenate(masked, axis=0)
    eid = lax.broadcasted_iota(jnp.int32, (N_EXPERTS, w), 0).astype(_F32)
    sels, raws = [], []
    member = None
    for _ in range(TOP_K):
        m = jnp.max(work, axis=0, keepdims=True)
        idx = jnp.min(jnp.where(work == m, eid, float(N_EXPERTS)), axis=0, keepdims=True)
        hot = eid == idx
        sels.append(idx)
        raws.append(jnp.sum(jnp.where(hot, scores, 0.0), axis=0, keepdims=True))
        work = jnp.where(hot, -jnp.inf, work)
        member = hot if member is None else (member | hot)
    return sels, raws, member


def _post_body(h0_ref, ret_ref, fox_ref, wo_ref, g1_ref, b1_ref, wr_ref, rb_ref,
               trow_ref, sel_ref, gate_ref, rank_ref, cnt_ref, run_ref):
    tm = h0_ref.shape[0]
    i = pl.program_id(0)

    @pl.when(i == 0)
    def _():
        run_ref[...] = jnp.zeros_like(run_ref)

    halves = [slice(hf * tm // 2, (hf + 1) * tm // 2) for hf in range(2)]
    ys = [_dot(ret_ref[hs, :], wo_ref[:HEAD_W, :]) + _dot(fox_ref[hs, :], wo_ref[HEAD_W:, :]) for hs in halves]
    h1s = [_ln(ALPHA * h0_ref[hs, :] + y, g1_ref[...], b1_ref[...]) for hs, y in zip(halves, ys)]
    for hs, h1 in zip(halves, h1s):
        _store_rows(trow_ref, h1, hs.start)

    scores = jnp.concatenate([jax.nn.sigmoid(_dot_nt(wr_ref[...], h1.astype(_BF16))) for h1 in h1s], axis=1)
    biased = scores + rb_ref[...]
    chunks = [slice(c * LANES, (c + 1) * LANES) for c in range(tm // LANES)]
    routed = [_route(scores[:, cs], biased[:, cs]) for cs in chunks]
    member_f = jnp.concatenate([member.astype(_F32) for _, _, member in routed], axis=1)
    row = lax.broadcasted_iota(jnp.int32, (tm, tm), 0)
    col = lax.broadcasted_iota(jnp.int32, (tm, tm), 1)
    before = (row < col).astype(_BF16)
    rank_e = _dot(member_f.astype(_BF16), before) + run_ref[...]
    eid = lax.broadcasted_iota(jnp.int32, (N_EXPERTS, LANES), 0).astype(_F32)
    sel_rows, gate_rows, rank_rows = [], [], []
    for k in range(TOP_K):
        sel_k, gate_k, rank_k = [], [], []
        for cs, (sels, raws, _) in zip(chunks, routed):
            total = raws[0]
            for r in raws[1:]:
                total = total + r
            sel_k.append(sels[k].astype(jnp.int32))
            gate_k.append(raws[k] * (ROUTED_SCALE / total))
            rank_k.append(jnp.sum(jnp.where(eid == sels[k], rank_e[:, cs], 0.0), axis=0, keepdims=True).astype(jnp.int32))
        sel_rows.append(jnp.concatenate(sel_k, axis=1))
        gate_rows.append(jnp.concatenate(gate_k, axis=1))
        rank_rows.append(jnp.concatenate(rank_k, axis=1))
    pad_rows = sel_ref.shape[0] - TOP_K
    sel_ref[...] = jnp.concatenate(sel_rows + [jnp.zeros((pad_rows, tm), jnp.int32)], axis=0)
    gate_ref[...] = jnp.concatenate(gate_rows + [jnp.zeros((pad_rows, tm), _F32)], axis=0)
    rank_ref[...] = jnp.concatenate(rank_rows + [jnp.zeros((pad_rows, tm), jnp.int32)], axis=0)
    run_ref[...] = run_ref[...] + jnp.sum(member_f, axis=1, keepdims=True)
    cnt_ref[...] = run_ref[...]


def _post(h0, ret, fox, w_out, ln1_g, ln1_b, w_rt, rbias):
    n = h0.shape[0]
    tm = TM_PROJ
    row_spec = lambda w: pl.BlockSpec((tm, w), lambda i: (i, 0))
    col_spec = pl.BlockSpec((SUBLANES, tm), lambda i: (0, i))
    const = lambda shape: pl.BlockSpec(shape, lambda i: (0, 0))
    return pl.pallas_call(
        _post_body,
        grid=(n // tm,),
        in_specs=[row_spec(D_MODEL), row_spec(HEAD_W), row_spec(HEAD_W),
                  const(w_out.shape), const((1, D_MODEL)), const((1, D_MODEL)), const(w_rt.shape),
                  const((N_EXPERTS, 1))],
        out_specs=[pl.BlockSpec((tm * ROW_TILES, LANES), lambda i: (i, 0)),
                   col_spec, col_spec, col_spec, const((N_EXPERTS, 1))],
        out_shape=[jax.ShapeDtypeStruct((n * ROW_TILES, LANES), ROW_DTYPE),
                   jax.ShapeDtypeStruct((SUBLANES, n), jnp.int32), jax.ShapeDtypeStruct((SUBLANES, n), _F32),
                   jax.ShapeDtypeStruct((SUBLANES, n), jnp.int32), jax.ShapeDtypeStruct((N_EXPERTS, 1), _F32)],
        scratch_shapes=[pltpu.VMEM((N_EXPERTS, 1), _F32)],
        compiler_params=_params(1),
        name="post_mixer",
    )(h0, ret, fox, w_out, ln1_g, ln1_b, w_rt, rbias)


def _plan_body(sel_ref, rank_ref, cnt_ref, dest_ref, blk_ref, fill_ref, used_ref, tab_ref):
    cnt = cnt_ref[...]
    padded = jnp.ceil(cnt * (1.0 / EBLK)) * EBLK
    er = lax.broadcasted_iota(jnp.int32, (N_EXPERTS, N_EXPERTS), 0)
    ec = lax.broadcasted_iota(jnp.int32, (N_EXPERTS, N_EXPERTS), 1)
    padded_row = jnp.sum(jnp.where(er == ec, padded, 0.0), axis=0, keepdims=True)
    pstart = jnp.sum(jnp.where(ec < er, padded_row, 0.0), axis=1, keepdims=True)
    pend = pstart + padded
    sel = sel_ref[...]
    dest = rank_ref[...]
    for e in range(N_EXPERTS):
        dest = dest + jnp.where(sel == e, pstart[e:e + 1, :].astype(jnp.int32), 0)
    dest_ref[...] = dest
    nblk = blk_ref.shape[1]
    first_row = (lax.broadcasted_iota(jnp.int32, (N_EXPERTS, nblk), 1) * EBLK).astype(_F32)
    owner = jnp.minimum(jnp.sum((pend <= first_row).astype(_F32), axis=0, keepdims=True), N_EXPERTS - 1.0)
    blk_ref[...] = owner.astype(jnp.int32)
    mine = lax.broadcasted_iota(jnp.int32, (N_EXPERTS, nblk), 0).astype(_F32) == owner
    live_end = jnp.sum(jnp.where(mine, pstart + cnt, 0.0), axis=0, keepdims=True)
    fill_ref[...] = jnp.clip(live_end - first_row[:1, :], 0.0, float(EBLK)).astype(jnp.int32)
    used_ref[...] = (pend[N_EXPERTS - 1:, :] * (1.0 / EBLK)).astype(jnp.int32)
    as_row = lambda col: jnp.sum(jnp.where(er == ec, col, 0.0), axis=0, keepdims=True).astype(jnp.int32)
    rows = {TAB_PAD0: pstart + cnt, TAB_PADN: padded - cnt}
    blank = jnp.zeros((1, N_EXPERTS), jnp.int32)
    tab_ref[...] = jnp.concatenate([as_row(rows[r]) if r in rows else blank for r in range(SUBLANES)], axis=0)


def _plan(sel, rank, cnt, nblk_pad):
    n = sel.shape[1]
    full = lambda shape: pl.BlockSpec(shape, lambda i: (0, 0))
    return pl.pallas_call(
        _plan_body,
        grid=(1,),
        in_specs=[full(sel.shape), full(rank.shape), full(cnt.shape)],
        out_specs=[full(sel.shape), full((1, nblk_pad)), full((1, nblk_pad)), full((1, 1)),
                   full((SUBLANES, N_EXPERTS))],
        out_shape=[jax.ShapeDtypeStruct((SUBLANES, n), jnp.int32), jax.ShapeDtypeStruct((1, nblk_pad), jnp.int32),
                   jax.ShapeDtypeStruct((1, nblk_pad), jnp.int32), jax.ShapeDtypeStruct((1, 1), jnp.int32),
                   jax.ShapeDtypeStruct((SUBLANES, N_EXPERTS), jnp.int32)],
        compiler_params=_params(1),
        name="plan",
    )(sel, rank, cnt)


def _store_rows(ref, v, first_row=0):
    m = v.shape[0]
    for s in range(ROW_TILES):
        ref[pl.ds(first_row * ROW_TILES + s, m, stride=ROW_TILES), :] = v[:, s * LANES:(s + 1) * LANES].astype(ROW_DTYPE)


def _load_rows(ref, first_row, m):
    return jnp.concatenate([ref[pl.ds(first_row * ROW_TILES + s, m, stride=ROW_TILES), :] for s in range(ROW_TILES)],
                           axis=1)


def _row_copy(src, src_row, dst, dst_row, sem):
    return pltpu.make_async_copy(src.at[pl.ds(pl.multiple_of(src_row * ROW_TILES, ROW_TILES), ROW_TILES), :],
                                 dst.at[pl.ds(pl.multiple_of(dst_row * ROW_TILES, ROW_TILES), ROW_TILES), :], sem)


def _dispatch_body(dest_ref, tab_ref, used_ref, t_ref, wgu_ref, wd_ref, xs_ref, base_ref, zero_sc, sem, zsem):
    tt = t_ref.shape[0] // ROW_TILES
    half_blk = EBLK // 2
    n_half = xs_ref.shape[0] // (half_blk * ROW_TILES)

    @pl.when(pl.program_id(0) == 0)
    def _():
        zero_sc[...] = jnp.zeros_like(zero_sc)

        def zero_copy(first_row, rows):
            return pltpu.make_async_copy(
                zero_sc.at[pl.ds(0, rows * ROW_TILES), :],
                xs_ref.at[pl.ds(pl.multiple_of(first_row * ROW_TILES, ROW_TILES), rows * ROW_TILES), :], zsem)

        def for_padding(act):
            def per_expert(e, carry):
                row = tab_ref[TAB_PAD0, e]
                for bit in range(EBLK.bit_length() - 2, -1, -1):
                    take = (tab_ref[TAB_PADN, e] & (1 << bit)) != 0

                    @pl.when(take)
                    def _():
                        act(zero_copy(row, 1 << bit))

                    row = row + jnp.where(take, 1 << bit, 0)
                return carry

            def per_tail(hb, carry):
                act(zero_copy(hb * half_blk, half_blk))
                return carry

            lax.fori_loop(0, N_EXPERTS, per_expert, 0)
            lax.fori_loop(2 * used_ref[0], n_half, per_tail, 0)

        for_padding(lambda cp: cp.start())
        for_padding(lambda cp: cp.wait())

    def issue(j, carry):
        for u in range(ISSUE_UNROLL):
            i = ISSUE_UNROLL * j + u
            for k in range(TOP_K):
                _row_copy(t_ref, i, xs_ref, dest_ref[i * SUBLANES + k], sem).start(priority=k % 2)
        return carry

    lax.fori_loop(0, tt // ISSUE_UNROLL, issue, 0)

    h1 = _load_rows(t_ref, 0, tt)
    gu = _dot(h1.astype(_BF16), wgu_ref[...])
    mid = (_silu(gu[:, :SHARED_FF]) * gu[:, SHARED_FF:]).astype(_BF16)
    base_ref[...] = ALPHA * h1 + _dot(mid, wd_ref[...])

    for _ in range(TOP_K):
        pltpu.make_async_copy(t_ref, xs_ref.at[pl.ds(0, tt * ROW_TILES), :], sem).wait()


def _dispatch(dest_flat, tab, used, trow, w_gu, w_sd, total_rows):
    n = trow.shape[0] // ROW_TILES
    tt = TT_DISPATCH
    smem = pl.BlockSpec(memory_space=pltpu.SMEM)
    const = lambda shape: pl.BlockSpec(shape, lambda i: (0, 0))
    return pl.pallas_call(
        _dispatch_body,
        grid=(n // tt,),
        in_specs=[pl.BlockSpec((tt * SUBLANES,), lambda i: (i,), memory_space=pltpu.SMEM), smem, smem,
                  pl.BlockSpec((tt * ROW_TILES, LANES), lambda i: (i, 0)), const(w_gu.shape), const(w_sd.shape)],
        out_specs=[pl.BlockSpec(memory_space=pl.ANY), pl.BlockSpec((tt, D_MODEL), lambda i: (i, 0))],
        out_shape=[jax.ShapeDtypeStruct((total_rows * ROW_TILES, LANES), ROW_DTYPE),
                   jax.ShapeDtypeStruct((n, D_MODEL), _F32)],
        scratch_shapes=[pltpu.VMEM((EBLK // 2 * ROW_TILES, LANES), ROW_DTYPE), pltpu.SemaphoreType.DMA(()),
                        pltpu.SemaphoreType.DMA(())],
        compiler_params=_params(1),
        name="dispatch",
    )(dest_flat, tab, used, trow, w_gu, w_sd)


def _expert_body(blk_ref, used_ref, fill_ref, xs_ref, wg_ref, wu_ref, wd_ref, y_ref, wgu_sc, wd_sc):
    i = pl.program_id(0)
    prev = blk_ref[jnp.maximum(i - 1, 0)]
    fresh = (i == 0) | (blk_ref[i] != prev)

    @pl.when(fresh)
    def _():
        wgu_sc[:, :EXPERT_FF] = wg_ref[...].astype(_BF16)
        wgu_sc[:, EXPERT_FF:] = wu_ref[...].astype(_BF16)
        wd_sc[...] = wd_ref[...].astype(_BF16)

    @pl.when(i < used_ref[0])
    def _():
        live = lax.broadcasted_iota(jnp.int32, (EBLK, 1), 0) < fill_ref[i]
        x = _load_rows(xs_ref, 0, EBLK)
        x = jnp.where(live, x, jnp.zeros_like(x)).astype(_BF16)
        gu = _dot(x, wgu_sc[...])
        mid = (_silu(gu[:, :EXPERT_FF]) * gu[:, EXPERT_FF:]).astype(_BF16)
        _store_rows(y_ref, _dot(mid, wd_sc[...]))

    @pl.when(i >= used_ref[0])
    def _():
        y_ref[...] = jnp.zeros_like(y_ref)


def _experts(blk_e, used, fill, xs, we_gate, we_up, we_down):
    nblk = xs.shape[0] // (EBLK * ROW_TILES)
    last = lambda i, used: jnp.minimum(i, jnp.maximum(used[0] - 1, 0))
    w_spec = lambda shape: pl.BlockSpec((None,) + shape, lambda i, blk, used, fill: (blk[i], 0, 0))
    return pl.pallas_call(
        _expert_body,
        grid_spec=pltpu.PrefetchScalarGridSpec(
            num_scalar_prefetch=3,
            grid=(nblk,),
            in_specs=[pl.BlockSpec((EBLK * ROW_TILES, LANES), lambda i, blk, used, fill: (last(i, used), 0)),
                      w_spec((D_MODEL, EXPERT_FF)), w_spec((D_MODEL, EXPERT_FF)), w_spec((EXPERT_FF, D_MODEL))],
            out_specs=pl.BlockSpec((EBLK * ROW_TILES, LANES), lambda i, blk, used, fill: (i, 0)),
            scratch_shapes=[pltpu.VMEM((D_MODEL, 2 * EXPERT_FF), _BF16), pltpu.VMEM((EXPERT_FF, D_MODEL), _BF16)],
        ),
        out_shape=jax.ShapeDtypeStruct(xs.shape, ROW_DTYPE),
        compiler_params=_params(1),
        name="experts",
    )(blk_e, used, fill, xs, we_gate, we_up, we_down)


def _combine_body(dest_ref, dnext_ref, y_ref, base_ref, gate_ref, g2_ref, b2_ref, o_ref, z_sc, sems):
    i = pl.program_id(0)
    slot = lax.rem(i, 2)
    tile_rows = TT * TOP_K * ROW_TILES

    def gather(dref, into):
        def issue(j, carry):
            for u in range(ISSUE_UNROLL):
                t = ISSUE_UNROLL * j + u
                for k in range(TOP_K):
                    _row_copy(y_ref, dref[t * SUBLANES + k], z_sc.at[into], k * TT + t,
                              sems.at[into]).start(priority=k % 2)
            return carry

        lax.fori_loop(0, TT // ISSUE_UNROLL, issue, 0)

    @pl.when(i == 0)
    def _():
        gather(dest_ref, 0)

    @pl.when(i + 1 < pl.num_programs(0))
    def _():
        gather(dnext_ref, 1 - slot)

    pltpu.make_async_copy(y_ref.at[pl.ds(0, tile_rows), :], z_sc.at[slot], sems.at[slot]).wait()
    gates = gate_ref[...]
    acc = base_ref[...]
    for k in range(TOP_K):
        acc = acc + gates[:, k:k + 1] * _load_rows(z_sc.at[slot], k * TT, TT).astype(_F32)
    o_ref[...] = _ln(acc, g2_ref[...], b2_ref[...])


def _combine(dest_flat, y, base, gates_t, ln2_g, ln2_b):
    n = base.shape[0]
    steps = n // TT
    const = lambda shape: pl.BlockSpec(shape, lambda i: (0, 0))
    dest_spec = lambda ahead: pl.BlockSpec((TT * SUBLANES,), lambda i: (jnp.minimum(i + ahead, steps - 1),),
                                           memory_space=pltpu.SMEM)
    return pl.pallas_call(
        _combine_body,
        grid=(steps,),
        in_specs=[dest_spec(0), dest_spec(1),
                  pl.BlockSpec(memory_space=pl.ANY),
                  pl.BlockSpec((TT, D_MODEL), lambda i: (i, 0)),
                  pl.BlockSpec((TT, SUBLANES), lambda i: (i, 0)),
                  const((1, D_MODEL)), const((1, D_MODEL))],
        out_specs=pl.BlockSpec((TT, D_MODEL), lambda i: (i, 0)),
        out_shape=jax.ShapeDtypeStruct((n, D_MODEL), _F32),
        scratch_shapes=[pltpu.VMEM((2, TT * TOP_K * ROW_TILES, LANES), ROW_DTYPE), pltpu.SemaphoreType.DMA((2,))],
        compiler_params=_params(1),
        name="combine",
    )(dest_flat, dest_flat, y, base, gates_t, ln2_g, ln2_b)


def _rope_tables(pos):
    half = RET_DK // 2
    inv = ROPE_BASE ** (-jnp.arange(half, dtype=_F32) / half)
    ang = pos[:, None] * inv[None, :]
    cos = jnp.cos(ang)
    sin = jnp.sin(ang)
    return jnp.concatenate([cos, cos], -1), jnp.concatenate([-sin, sin], -1)


def _decay_tables():
    lg = jnp.log1p(-jnp.exp2(-5.0 - jnp.arange(RET_HEADS, dtype=_F32)))
    idx = jnp.arange(BLOCK, dtype=_F32)
    rel = idx[:, None] - idx[None, :]
    causal = rel >= 0
    dmask = jnp.where(causal[None], jnp.exp(jnp.where(causal, rel, 0.0)[None] * lg[:, None, None]), 0.0)
    zeta = jnp.exp((BLOCK - 1.0 - idx)[None, :] * lg[:, None])
    xi = jnp.exp((idx + 1.0)[None, :] * lg[:, None])
    along_lanes = lambda col: jnp.broadcast_to(col[:, :, None], (RET_HEADS, BLOCK, LANES))
    return dmask, along_lanes(xi), along_lanes(zeta), jnp.exp(BLOCK * lg)


def kernel(x, meta, ln0_g, ln0_b, w_in, b_forget, w_out, ln1_g, ln1_b, w_router, router_bias, we_gate, we_up,
           we_down, ws_gate, ws_up, ws_down, ln2_g, ln2_b):
    nb, s, d = x.shape
    assert d == D_MODEL and meta.shape == (N_META, D_MODEL) and w_in.shape[0] == 1
    assert s % TM_PROJ == 0 and s % T_FOX == 0 and (nb * s) % TT == 0
    n = nb * s
    x2d = x.reshape(n, d)
    row2 = lambda v: v.reshape(1, -1).astype(_F32)
    main_cols = 7 * HEAD_W
    w_all = jnp.concatenate(
        [w_in[0, :, :main_cols], w_in[0, :, main_cols:], jnp.zeros((d, LANES - FOX_HEADS), w_in.dtype)],
        axis=1).astype(_BF16)
    bf_pad = jnp.concatenate([b_forget[0].astype(_F32), jnp.zeros((LANES - FOX_HEADS,), _F32)]).reshape(1, LANES)
    g0, b0 = row2(ln0_g), row2(ln0_b)

    cos_x, sin_x = _rope_tables(jnp.arange(s, dtype=_F32) + float(N_META))
    cos_m, sin_m = _rope_tables(jnp.arange(BLOCK, dtype=_F32) - float(PAD))
    meta_blk = jnp.concatenate([jnp.zeros((PAD, d), _F32), meta.astype(_F32)], axis=0)

    h0, rq, rk, rv, rg, fq, fk, fv = _inproj(x2d, g0, b0, w_all, bf_pad, cos_x, sin_x, nb=nb, meta=False)
    _, _, rk_m, rv_m, _, _, fk_m, fv_m = _inproj(meta_blk, g0, b0, w_all, bf_pad, cos_m, sin_m, nb=1, meta=True)

    dmask, xi, zeta, gch = _decay_tables()
    per_batch = lambda a: a.reshape(nb, s, a.shape[-1])
    ret = _retention(per_batch(rq), per_batch(rk), per_batch(rv), per_batch(rg), rk_m, rv_m, dmask, xi, zeta,
                     gch).reshape(n, HEAD_W)
    fox = _fox(per_batch(fq), per_batch(fk), per_batch(fv), fk_m, fv_m, nb=nb).reshape(n, HEAD_W)

    w_gu = jnp.concatenate([ws_gate[0], ws_up[0]], axis=1).astype(_BF16)
    trow, sel, gates, rank, cnt = _post(
        h0, ret, fox, w_out[0].astype(_BF16), row2(ln1_g[0]), row2(ln1_b[0]),
        jnp.transpose(w_router[0]).astype(_BF16), router_bias[0].astype(_F32).reshape(N_EXPERTS, 1))

    nblk = n * TOP_K // EBLK + N_EXPERTS
    nblk_pad = -(-nblk // LANES) * LANES
    dest, blk_e, fill, used, tab = _plan(sel, rank, cnt, nblk_pad)
    dest_flat = jnp.transpose(dest).reshape(-1)
    used = used.reshape(-1)

    xs, base = _dispatch(dest_flat, tab, used, trow, w_gu, ws_down[0].astype(_BF16), nblk * EBLK)
    y = _experts(blk_e.reshape(-1), used, fill.reshape(-1), xs, we_gate[0], we_up[0], we_down[0])
    out = _combine(dest_flat, y, base, jnp.transpose(gates), row2(ln2_g[0]), row2(ln2_b[0]))
    return out.reshape(nb, s, d)
```

```python
import functools

import jax
import jax.numpy as jnp
import numpy as np
from jax import lax
from jax.experimental import pallas as pl
from jax.experimental.pallas import tpu as pltpu

D_MODEL = 1024
N_META = 16
BLOCK = 128
PAD = BLOCK - N_META
RET_HEADS = 4
RET_DK = 128
FOX_HEADS = 8
FOX_HD = 64
N_EXPERTS = 64
TOP_K = 6
N_GROUPS = 8
GROUP_SIZE = N_EXPERTS // N_GROUPS
TOPK_GROUPS = 4
EXPERT_FF = 256
SHARED_FF = 256
ROUTED_SCALE = 2.5
ROPE_BASE = 10000.0
LN_EPS = 1e-5
NEG_INF = -1e30
ALPHA = 2.0 ** 0.25
HEAD_W = 512
LOG2E = 1.4426950408889634

LANES = 128
FOX_W = FOX_HEADS * LANES
SUBLANES = 8
ROW_TILES = D_MODEL // LANES
ROW_DTYPE = jnp.float32

TM_PROJ = 512
T_FOX = 512
FOX_CHUNK = 64
FOX_K_TERMS = 32
TT = 512
TT_DISPATCH = 1024
EBLK = 512
ISSUE_UNROLL = 4
TAB_PAD0, TAB_PADN = 0, 1
VMEM_LIMIT = 48 * 1024 * 1024

_F32 = jnp.float32
_BF16 = jnp.bfloat16


def _ln(x, g, b):
    xc = x - jnp.mean(x, -1, keepdims=True)
    var = jnp.mean(xc * xc, -1, keepdims=True)
    return xc * lax.rsqrt(var + LN_EPS) * g + b


def _dot(a, b):
    return jnp.dot(a, b, preferred_element_type=_F32)


def _dot_nt(a, b):
    return lax.dot_general(a, b, (((1,), (1,)), ((), ())), preferred_element_type=_F32)


def _dot_tn(a, b):
    return lax.dot_general(a, b, (((0,), (0,)), ((), ())), preferred_element_type=_F32)


def _silu(x):
    return x * jax.nn.sigmoid(x)


def _params(n_axes):
    return pltpu.CompilerParams(dimension_semantics=("arbitrary",) * n_axes, vmem_limit_bytes=VMEM_LIMIT)


def _inproj_body(x_ref, g_ref, b_ref, w_ref, bf_ref, cos_ref, sin_ref, own_ref, tq_ref, tk_ref, oq_ref, ok_ref, ov_ref,
                 h_ref, rq_ref, rk_ref, rv_ref, rg_ref, fq_ref, fk_ref, fv_ref, carry_ref, *, meta):
    tm = x_ref.shape[0]
    if not meta:
        @pl.when(pl.program_id(1) == 0)
        def _():
            carry_ref[...] = jnp.zeros_like(carry_ref)

    h = _ln(x_ref[...], g_ref[...], b_ref[...])
    if meta:
        valid = lax.broadcasted_iota(jnp.int32, (tm, 1), 0) >= PAD
        h = jnp.where(valid, h, 0.0)
    h_ref[...] = h
    hb = h.astype(_BF16)
    cos = cos_ref[...]
    sin = sin_ref[...]

    def proj(g):
        return _dot(hb, w_ref[:, g * HEAD_W:(g + 1) * HEAD_W])

    def rope_store(p, out_ref, scale):
        for hd in range(RET_HEADS):
            t = p[:, hd * LANES:(hd + 1) * LANES]
            r = t * cos + pltpu.roll(t, LANES // 2, axis=1) * sin
            out_ref[:, hd * LANES:(hd + 1) * LANES] = (r * scale).astype(_BF16)

    z = _dot(hb, w_ref[:, 7 * HEAD_W:7 * HEAD_W + LANES]) + bf_ref[...]
    rope_store(proj(0), rq_ref, 1.0)
    logf = jnp.minimum(z, 0.0) - jnp.log1p(jnp.exp(-jnp.abs(z)))
    if meta:
        logf = jnp.where(valid, logf, 0.0)
    l1 = logf.astype(_BF16)
    r1 = logf - l1.astype(_F32)
    l2 = r1.astype(_BF16)
    l3 = (r1 - l2.astype(_F32)).astype(_BF16)
    row = lax.broadcasted_iota(jnp.int32, (tm, tm), 0)
    col = lax.broadcasted_iota(jnp.int32, (tm, tm), 1)
    tri = (col <= row).astype(_BF16)
    rope_store(proj(1), rk_ref, RET_DK ** -0.5)
    c = _dot(tri, l1) + _dot(tri, l2) + _dot(tri, l3)
    rv_ref[...] = proj(2).astype(_BF16)
    rg_ref[...] = _silu(proj(3)).astype(_BF16)
    if meta:
        c = c - c[tm - 1:tm, :]
    else:
        c = c + carry_ref[...]
        carry_ref[...] = c[tm - 1:tm, :]

    head_lane = lax.broadcasted_iota(jnp.int32, (1, LANES), 1) < FOX_HEADS
    cl = jnp.where(head_lane, c, 0.0) * LOG2E
    c1 = cl.astype(_BF16).astype(_F32)
    r1 = cl - c1
    c2 = r1.astype(_BF16).astype(_F32)
    c3 = (r1 - c2).astype(_BF16).astype(_F32)
    csplit = c1 + pltpu.roll(c2, FOX_HEADS, axis=1) + pltpu.roll(c3, 2 * FOX_HEADS, axis=1)
    half = FOX_HD
    q_even, q_odd = pltpu.roll(csplit, half, axis=1), csplit
    k_even, k_odd = pltpu.roll(csplit, half + FOX_K_TERMS, axis=1), pltpu.roll(csplit, FOX_K_TERMS, axis=1)
    own = own_ref[...] > 0.0

    def per_head(p):
        return jnp.concatenate([p[:, (hd // 2) * LANES:(hd // 2 + 1) * LANES] for hd in range(FOX_HEADS)], axis=1)

    def by_parity(even, odd):
        return jnp.concatenate([even, odd] * (FOX_HEADS // 2), axis=1)

    q_extra = jnp.where(tq_ref[...] > 0.0, by_parity(q_even, q_odd), oq_ref[...])
    k_extra = jnp.where(tk_ref[...] > 0.0, -by_parity(k_even, k_odd), ok_ref[...])
    fq_ref[...] = jnp.where(own, per_head(proj(4) * (FOX_HD ** -0.5 * LOG2E)), q_extra).astype(_BF16)
    fk_ref[...] = jnp.where(own, per_head(proj(5)), k_extra).astype(_BF16)
    fv_ref[...] = jnp.where(own, per_head(proj(6)), ov_ref[...]).astype(_BF16)


def _fox_lane_tables():
    own, tq, tk, oq, ok, ov = (np.zeros((1, FOX_W), np.float32) for _ in range(6))
    for hd in range(FOX_HEADS):
        data = hd * LANES + (hd % 2) * FOX_HD
        extra = hd * LANES + (1 - hd % 2) * FOX_HD
        own[0, data:data + FOX_HD] = 1.0
        for term in range(3):
            lane = extra + term * FOX_HEADS + hd
            tq[0, lane] = 1.0
            ok[0, lane] = 1.0
            tk[0, lane + FOX_K_TERMS] = 1.0
            oq[0, lane + FOX_K_TERMS] = 1.0
        ov[0, extra] = 1.0
    return tuple(jnp.asarray(t) for t in (own, tq, tk, oq, ok, ov))


def _inproj(x2d, ln_g, ln_b, w_all, bf_pad, cos_t, sin_t, *, nb, meta):
    n = x2d.shape[0]
    s = n // nb
    tm = min(TM_PROJ, s)
    nj = s // tm
    row_spec = lambda w: pl.BlockSpec((tm, w), lambda b, j: (b * nj + j, 0))
    const = lambda shape: pl.BlockSpec(shape, lambda b, j: (0, 0))
    pos_spec = pl.BlockSpec((tm, LANES), lambda b, j: (j, 0))
    tables = _fox_lane_tables()
    outs = ([jax.ShapeDtypeStruct((n, D_MODEL), _F32)] + [jax.ShapeDtypeStruct((n, HEAD_W), _BF16)] * 4
            + [jax.ShapeDtypeStruct((n, FOX_W), _BF16)] * 3)
    return pl.pallas_call(
        functools.partial(_inproj_body, meta=meta),
        grid=(nb, nj),
        in_specs=[row_spec(D_MODEL), const((1, D_MODEL)), const((1, D_MODEL)), const(w_all.shape),
                  const((1, LANES)), pos_spec, pos_spec] + [const(t.shape) for t in tables],
        out_specs=[row_spec(D_MODEL)] + [row_spec(HEAD_W)] * 4 + [row_spec(FOX_W)] * 3,
        out_shape=outs,
        scratch_shapes=[pltpu.VMEM((1, LANES), _F32)],
        compiler_params=_params(2),
        name="inproj_meta" if meta else "inproj",
    )(x2d, ln_g, ln_b, w_all, bf_pad, cos_t, sin_t, *tables)


def _ret_body(q_ref, k_ref, v_ref, g_ref, km_ref, vm_ref, dm_ref, xi_ref, zeta_ref, gch_ref, o_ref, st_ref):
    def kv_update(k, v, hd):
        vz = (v.astype(_F32) * zeta_ref[hd]).astype(_BF16)
        return _dot_tn(k, vz)

    nb = q_ref.shape[0]

    @pl.when(pl.program_id(0) == 0)
    def _():
        for hd in range(RET_HEADS):
            sl = slice(hd * LANES, (hd + 1) * LANES)
            first = kv_update(km_ref[:, sl], vm_ref[:, sl], hd)
            for b in range(nb):
                st_ref[b, hd] = first

    group = 4
    for b0 in range(0, nb, group):
        chains = [(b, hd, slice(hd * LANES, (hd + 1) * LANES)) for b in range(b0, b0 + group) for hd in range(RET_HEADS)]
        scores = [_dot_nt(q_ref[b, :, sl], k_ref[b, :, sl]) * dm_ref[hd] for b, hd, sl in chains]
        cross = [_dot(q_ref[b, :, sl], st_ref[b, hd].astype(_BF16)) * xi_ref[hd] for b, hd, sl in chains]
        outs = [_dot(s.astype(_BF16), v_ref[b, :, sl]) + c for s, c, (b, hd, sl) in zip(scores, cross, chains)]
        for o, (b, hd, sl) in zip(outs, chains):
            oc = o - jnp.mean(o, -1, keepdims=True)
            y = oc * lax.rsqrt(jnp.mean(oc * oc, -1, keepdims=True) + LN_EPS)
            o_ref[b, :, sl] = (y * g_ref[b, :, sl].astype(_F32)).astype(_BF16)
        for b, hd, sl in chains:
            st_ref[b, hd] = gch_ref[hd] * st_ref[b, hd] + kv_update(k_ref[b, :, sl], v_ref[b, :, sl], hd)


def _retention(rq, rk, rv, rg, rk_m, rv_m, dmask, xi, zeta, gch):
    nb, s, _ = rq.shape
    row_spec = pl.BlockSpec((nb, BLOCK, HEAD_W), lambda j: (0, j, 0))
    meta_spec = pl.BlockSpec((BLOCK, HEAD_W), lambda j: (0, 0))
    tab = pl.BlockSpec((RET_HEADS, BLOCK, BLOCK), lambda j: (0, 0, 0))
    return pl.pallas_call(
        _ret_body,
        grid=(s // BLOCK,),
        in_specs=[row_spec] * 4 + [meta_spec] * 2 + [tab] * 3 + [pl.BlockSpec(memory_space=pltpu.SMEM)],
        out_specs=row_spec,
        out_shape=jax.ShapeDtypeStruct((nb, s, HEAD_W), _BF16),
        scratch_shapes=[pltpu.VMEM((nb, RET_HEADS, RET_DK, LANES), _F32)],
        compiler_params=_params(1),
        name="retention",
    )(rq, rk, rv, rg, rk_m, rv_m, dmask, xi, zeta, gch)


def _fox_body(q_ref, k_ref, v_ref, km_ref, vm_ref, o_ref, m_sc, acc_sc, s_sc, p_sc):
    t = q_ref.shape[0]
    qi = pl.program_id(2)

    heads = [slice(hh * LANES, (hh + 1) * LANES) for hh in range(2)]

    def logits(hh, k, buf, tk):
        s_sc[buf, hh, :, :tk] = _dot_nt(q_ref[:, heads[hh]], k)

    def update(hh, v, buf, tk, mask, first):
        for c in range(t // FOX_CHUNK):
            rows = slice(c * FOX_CHUNK, (c + 1) * FOX_CHUNK)
            s = s_sc[buf, hh, rows, :tk]
            if mask is not None:
                s = mask(s, c)
            mx = jnp.max(s, axis=1, keepdims=True)
            if first:
                m_new = mx
            else:
                m_prev = m_sc[hh, rows, :]
                m_new = jnp.maximum(m_prev, mx)
                acc_sc[hh, rows, :] = jnp.exp2(m_prev - m_new) * acc_sc[hh, rows, :]
            p_sc[hh, rows, :tk] = jnp.exp2(s - m_new).astype(_BF16)
            m_sc[hh, rows, :] = m_new
        pv = _dot(p_sc[hh, :, :tk], v)
        if first:
            acc_sc[hh] = pv
        else:
            acc_sc[hh] += pv

    def meta_mask(s, c):
        key = lax.broadcasted_iota(jnp.int32, s.shape, 1)
        return jnp.where(key >= PAD, s, NEG_INF)

    def causal_mask(s, c):
        key = lax.broadcasted_iota(jnp.int32, s.shape, 1)
        query = lax.broadcasted_iota(jnp.int32, s.shape, 0) + c * FOX_CHUNK
        return jnp.where(key <= query, s, NEG_INF)

    def key_tile(ref, ki, hh):
        return ref[pl.ds(pl.multiple_of(ki * t, t), t), heads[hh]]

    for hh in range(2):
        logits(hh, km_ref[:, heads[hh]], 1, BLOCK)
    for hh in range(2):
        logits(hh, key_tile(k_ref, 0, hh), 0, t)
        update(hh, vm_ref[:, heads[hh]], 1, BLOCK, meta_mask, True)

    def step(ki, buf, mask, more):
        for hh in range(2):
            if more:
                logits(hh, key_tile(k_ref, ki + 1, hh), 1 - buf, t)
            update(hh, key_tile(v_ref, ki, hh), buf, t, mask, False)

    def pair_body(j, carry):
        step(2 * j, 0, None, True)
        step(2 * j + 1, 1, None, True)
        return carry

    lax.fori_loop(0, lax.shift_right_logical(qi, 1), pair_body, 0)
    odd = lax.rem(qi, 2) == 1

    @pl.when(odd)
    def _():
        step(qi - 1, 0, None, True)
        step(qi, 1, causal_mask, False)

    @pl.when(jnp.logical_not(odd))
    def _():
        step(qi, 0, causal_mask, False)

    outs = []
    for hh in range(2):
        acc = acc_sc[hh]
        ones_lane = (1 - hh) * FOX_HD
        outs.append(acc / acc[:, ones_lane:ones_lane + 1])
    lane = lax.broadcasted_iota(jnp.int32, (t, LANES), 1)
    o_ref[...] = jnp.where(lane < FOX_HD, outs[0], outs[1]).astype(_BF16)


def _fox(fq, fk, fv, fk_m, fv_m, *, nb):
    s = fq.shape[1]
    t = T_FOX
    pair_w = 2 * LANES
    return pl.pallas_call(
        _fox_body,
        grid=(nb, FOX_HEADS // 2, s // t),
        in_specs=[
            pl.BlockSpec((None, t, pair_w), lambda b, p, i: (b, i, p)),
            pl.BlockSpec((None, s, pair_w), lambda b, p, i: (b, 0, p)),
            pl.BlockSpec((None, s, pair_w), lambda b, p, i: (b, 0, p)),
            pl.BlockSpec((BLOCK, pair_w), lambda b, p, i: (0, p)),
            pl.BlockSpec((BLOCK, pair_w), lambda b, p, i: (0, p)),
        ],
        out_specs=pl.BlockSpec((None, t, LANES), lambda b, p, i: (b, i, p)),
        out_shape=jax.ShapeDtypeStruct((nb, s, HEAD_W), _BF16),
        scratch_shapes=[pltpu.VMEM((2, t, 1), _F32), pltpu.VMEM((2, t, LANES), _F32), pltpu.VMEM((2, 2, t, t), _F32),
                        pltpu.VMEM((2, t, t), _BF16)],
        compiler_params=_params(3),
        name="fox",
    )(fq, fk, fv, fk_m, fv_m)


def _route(scores, biased):
    w = scores.shape[1]
    sub = lax.broadcasted_iota(jnp.int32, (GROUP_SIZE, w), 0).astype(_F32)
    groups = [biased[g * GROUP_SIZE:(g + 1) * GROUP_SIZE, :] for g in range(N_GROUPS)]
    gscore = []
    for v in groups:
        m1 = jnp.max(v, axis=0, keepdims=True)
        i1 = jnp.min(jnp.where(v == m1, sub, float(GROUP_SIZE)), axis=0, keepdims=True)
        m2 = jnp.max(jnp.where(sub == i1, -jnp.inf, v), axis=0, keepdims=True)
        gscore.append(m1 + m2)
    masked = []
    for g in range(N_GROUPS):
        beaten = jnp.zeros((1, w), _F32)
        for o in range(N_GROUPS):
            if o == g:
                continue
            wins = (gscore[o] >= gscore[g]) if o < g else (gscore[o] > gscore[g])
            beaten = beaten + wins.astype(_F32)
        masked.append(jnp.where(beaten < float(TOPK_GROUPS), groups[g], NEG_INF))
    work = jnp.concatenate(masked, axis=0)
    eid = lax.broadcasted_iota(jnp.int32, (N_EXPERTS, w), 0).astype(_F32)
    sels, raws = [], []
    member = None
    for _ in range(TOP_K):
        m = jnp.max(work, axis=0, keepdims=True)
        idx = jnp.min(jnp.where(work == m, eid, float(N_EXPERTS)), axis=0, keepdims=True)
        hot = eid == idx
        sels.append(idx)
        raws.append(jnp.sum(jnp.where(hot, scores, 0.0), axis=0, keepdims=True))
        work = jnp.where(hot, -jnp.inf, work)
        member = hot if member is None else (member | hot)
    return sels, raws, member


def _post_body(h0_ref, ret_ref, fox_ref, wo_ref, g1_ref, b1_ref, wr_ref, rb_ref,
               trow_ref, sel_ref, gate_ref, rank_ref, cnt_ref, run_ref):
    tm = h0_ref.shape[0]
    i = pl.program_id(0)

    @pl.when(i == 0)
    def _():
        run_ref[...] = jnp.zeros_like(run_ref)

    halves = [slice(hf * tm // 2, (hf + 1) * tm // 2) for hf in range(2)]
    ys = [_dot(ret_ref[hs, :], wo_ref[:HEAD_W, :]) + _dot(fox_ref[hs, :], wo_ref[HEAD_W:, :]) for hs in halves]
    h1s = [_ln(ALPHA * h0_ref[hs, :] + y, g1_ref[...], b1_ref[...]) for hs, y in zip(halves, ys)]
    for hs, h1 in zip(halves, h1s):
        _store_rows(trow_ref, h1, hs.start)

    scores = jnp.concatenate([jax.nn.sigmoid(_dot_nt(wr_ref[...], h1.astype(_BF16))) for h1 in h1s], axis=1)
    biased = scores + rb_ref[...]
    chunks = [slice(c * LANES, (c + 1) * LANES) for c in range(tm // LANES)]
    routed = [_route(scores[:, cs], biased[:, cs]) for cs in chunks]
    member_f = jnp.concatenate([member.astype(_F32) for _, _, member in routed], axis=1)
    row = lax.broadcasted_iota(jnp.int32, (tm, tm), 0)
    col = lax.broadcasted_iota(jnp.int32, (tm, tm), 1)
    before = (row < col).astype(_BF16)
    rank_e = _dot(member_f.astype(_BF16), before) + run_ref[...]
    eid = lax.broadcasted_iota(jnp.int32, (N_EXPERTS, LANES), 0).astype(_F32)
    sel_rows, gate_rows, rank_rows = [], [], []
    for k in range(TOP_K):
        sel_k, gate_k, rank_k = [], [], []
        for cs, (sels, raws, _) in zip(chunks, routed):
            total = raws[0]
            for r in raws[1:]:
                total = total + r
            sel_k.append(sels[k].astype(jnp.int32))
            gate_k.append(raws[k] * (ROUTED_SCALE / total))
            rank_k.append(jnp.sum(jnp.where(eid == sels[k], rank_e[:, cs], 0.0), axis=0, keepdims=True).astype(jnp.int32))
        sel_rows.append(jnp.concatenate(sel_k, axis=1))
        gate_rows.append(jnp.concatenate(gate_k, axis=1))
        rank_rows.append(jnp.concatenate(rank_k, axis=1))
    pad_rows = sel_ref.shape[0] - TOP_K
    sel_ref[...] = jnp.concatenate(sel_rows + [jnp.zeros((pad_rows, tm), jnp.int32)], axis=0)
    gate_ref[...] = jnp.concatenate(gate_rows + [jnp.zeros((pad_rows, tm), _F32)], axis=0)
    rank_ref[...] = jnp.concatenate(rank_rows + [jnp.zeros((pad_rows, tm), jnp.int32)], axis=0)
    run_ref[...] = run_ref[...] + jnp.sum(member_f, axis=1, keepdims=True)
    cnt_ref[...] = run_ref[...]


def _post(h0, ret, fox, w_out, ln1_g, ln1_b, w_rt, rbias):
    n = h0.shape[0]
    tm = TM_PROJ
    row_spec = lambda w: pl.BlockSpec((tm, w), lambda i: (i, 0))
    col_spec = pl.BlockSpec((SUBLANES, tm), lambda i: (0, i))
    const = lambda shape: pl.BlockSpec(shape, lambda i: (0, 0))
    return pl.pallas_call(
        _post_body,
        grid=(n // tm,),
        in_specs=[row_spec(D_MODEL), row_spec(HEAD_W), row_spec(HEAD_W),
                  const(w_out.shape), const((1, D_MODEL)), const((1, D_MODEL)), const(w_rt.shape),
                  const((N_EXPERTS, 1))],
        out_specs=[pl.BlockSpec((tm * ROW_TILES, LANES), lambda i: (i, 0)),
                   col_spec, col_spec, col_spec, const((N_EXPERTS, 1))],
        out_shape=[jax.ShapeDtypeStruct((n * ROW_TILES, LANES), ROW_DTYPE),
                   jax.ShapeDtypeStruct((SUBLANES, n), jnp.int32), jax.ShapeDtypeStruct((SUBLANES, n), _F32),
                   jax.ShapeDtypeStruct((SUBLANES, n), jnp.int32), jax.ShapeDtypeStruct((N_EXPERTS, 1), _F32)],
        scratch_shapes=[pltpu.VMEM((N_EXPERTS, 1), _F32)],
        compiler_params=_params(1),
        name="post_mixer",
    )(h0, ret, fox, w_out, ln1_g, ln1_b, w_rt, rbias)


def _plan_body(sel_ref, rank_ref, cnt_ref, dest_ref, blk_ref, fill_ref, used_ref, tab_ref):
    cnt = cnt_ref[...]
    padded = jnp.ceil(cnt * (1.0 / EBLK)) * EBLK
    er = lax.broadcasted_iota(jnp.int32, (N_EXPERTS, N_EXPERTS), 0)
    ec = lax.broadcasted_iota(jnp.int32, (N_EXPERTS, N_EXPERTS), 1)
    padded_row = jnp.sum(jnp.where(er == ec, padded, 0.0), axis=0, keepdims=True)
    pstart = jnp.sum(jnp.where(ec < er, padded_row, 0.0), axis=1, keepdims=True)
    pend = pstart + padded
    sel = sel_ref[...]
    dest = rank_ref[...]
    for e in range(N_EXPERTS):
        dest = dest + jnp.where(sel == e, pstart[e:e + 1, :].astype(jnp.int32), 0)
    dest_ref[...] = dest
    nblk = blk_ref.shape[1]
    first_row = (lax.broadcasted_iota(jnp.int32, (N_EXPERTS, nblk), 1) * EBLK).astype(_F32)
    owner = jnp.minimum(jnp.sum((pend <= first_row).astype(_F32), axis=0, keepdims=True), N_EXPERTS - 1.0)
    blk_ref[...] = owner.astype(jnp.int32)
    mine = lax.broadcasted_iota(jnp.int32, (N_EXPERTS, nblk), 0).astype(_F32) == owner
    live_end = jnp.sum(jnp.where(mine, pstart + cnt, 0.0), axis=0, keepdims=True)
    fill_ref[...] = jnp.clip(live_end - first_row[:1, :], 0.0, float(EBLK)).astype(jnp.int32)
    used_ref[...] = (pend[N_EXPERTS - 1:, :] * (1.0 / EBLK)).astype(jnp.int32)
    as_row = lambda col: jnp.sum(jnp.where(er == ec, col, 0.0), axis=0, keepdims=True).astype(jnp.int32)
    rows = {TAB_PAD0: pstart + cnt, TAB_PADN: padded - cnt}
    blank = jnp.zeros((1, N_EXPERTS), jnp.int32)
    tab_ref[...] = jnp.concatenate([as_row(rows[r]) if r in rows else blank for r in range(SUBLANES)], axis=0)


def _plan(sel, rank, cnt, nblk_pad):
    n = sel.shape[1]
    full = lambda shape: pl.BlockSpec(shape, lambda i: (0, 0))
    return pl.pallas_call(
        _plan_body,
        grid=(1,),
        in_specs=[full(sel.shape), full(rank.shape), full(cnt.shape)],
        out_specs=[full(sel.shape), full((1, nblk_pad)), full((1, nblk_pad)), full((1, 1)),
                   full((SUBLANES, N_EXPERTS))],
        out_shape=[jax.ShapeDtypeStruct((SUBLANES, n), jnp.int32), jax.ShapeDtypeStruct((1, nblk_pad), jnp.int32),
                   jax.ShapeDtypeStruct((1, nblk_pad), jnp.int32), jax.ShapeDtypeStruct((1, 1), jnp.int32),
                   jax.ShapeDtypeStruct((SUBLANES, N_EXPERTS), jnp.int32)],
        compiler_params=_params(1),
        name="plan",
    )(sel, rank, cnt)


def _store_rows(ref, v, first_row=0):
    m = v.shape[0]
    for s in range(ROW_TILES):
        ref[pl.ds(first_row * ROW_TILES + s, m, stride=ROW_TILES), :] = v[:, s * LANES:(s + 1) * LANES].astype(ROW_DTYPE)


def _load_rows(ref, first_row, m):
    return jnp.concatenate([ref[pl.ds(first_row * ROW_TILES + s, m, stride=ROW_TILES), :] for s in range(ROW_TILES)],
                           axis=1)


def _row_copy(src, src_row, dst, dst_row, sem):
    return pltpu.make_async_copy(src.at[pl.ds(pl.multiple_of(src_row * ROW_TILES, ROW_TILES), ROW_TILES), :],
                                 dst.at[pl.ds(pl.multiple_of(dst_row * ROW_TILES, ROW_TILES), ROW_TILES), :], sem)


def _dispatch_body(dest_ref, tab_ref, used_ref, t_ref, wgu_ref, wd_ref, xs_ref, base_ref, zero_sc, sem, zsem):
    tt = t_ref.shape[0] // ROW_TILES
    half_blk = EBLK // 2
    n_half = xs_ref.shape[0] // (half_blk * ROW_TILES)

    @pl.when(pl.program_id(0) == 0)
    def _():
        zero_sc[...] = jnp.zeros_like(zero_sc)

        def zero_copy(first_row, rows):
            return pltpu.make_async_copy(
                zero_sc.at[pl.ds(0, rows * ROW_TILES), :],
                xs_ref.at[pl.ds(pl.multiple_of(first_row * ROW_TILES, ROW_TILES), rows * ROW_TILES), :], zsem)

        def for_padding(act):
            def per_expert(e, carry):
                row = tab_ref[TAB_PAD0, e]
                for bit in range(EBLK.bit_length() - 2, -1, -1):
                    take = (tab_ref[TAB_PADN, e] & (1 << bit)) != 0

                    @pl.when(take)
                    def _():
                        act(zero_copy(row, 1 << bit))

                    row = row + jnp.where(take, 1 << bit, 0)
                return carry

            def per_tail(hb, carry):
                act(zero_copy(hb * half_blk, half_blk))
                return carry

            lax.fori_loop(0, N_EXPERTS, per_expert, 0)
            lax.fori_loop(2 * used_ref[0], n_half, per_tail, 0)

        for_padding(lambda cp: cp.start())
        for_padding(lambda cp: cp.wait())

    def issue(j, carry):
        for u in range(ISSUE_UNROLL):
            i = ISSUE_UNROLL * j + u
            for k in range(TOP_K):
                _row_copy(t_ref, i, xs_ref, dest_ref[i * SUBLANES + k], sem).start(priority=k % 2)
        return carry

    lax.fori_loop(0, tt // ISSUE_UNROLL, issue, 0)

    h1 = _load_rows(t_ref, 0, tt)
    gu = _dot(h1.astype(_BF16), wgu_ref[...])
    mid = (_silu(gu[:, :SHARED_FF]) * gu[:, SHARED_FF:]).astype(_BF16)
    base_ref[...] = ALPHA * h1 + _dot(mid, wd_ref[...])

    for _ in range(TOP_K):
        pltpu.make_async_copy(t_ref, xs_ref.at[pl.ds(0, tt * ROW_TILES), :], sem).wait()


def _dispatch(dest_flat, tab, used, trow, w_gu, w_sd, total_rows):
    n = trow.shape[0] // ROW_TILES
    tt = TT_DISPATCH
    smem = pl.BlockSpec(memory_space=pltpu.SMEM)
    const = lambda shape: pl.BlockSpec(shape, lambda i: (0, 0))
    return pl.pallas_call(
        _dispatch_body,
        grid=(n // tt,),
        in_specs=[pl.BlockSpec((tt * SUBLANES,), lambda i: (i,), memory_space=pltpu.SMEM), smem, smem,
                  pl.BlockSpec((tt * ROW_TILES, LANES), lambda i: (i, 0)), const(w_gu.shape), const(w_sd.shape)],
        out_specs=[pl.BlockSpec(memory_space=pl.ANY), pl.BlockSpec((tt, D_MODEL), lambda i: (i, 0))],
        out_shape=[jax.ShapeDtypeStruct((total_rows * ROW_TILES, LANES), ROW_DTYPE),
                   jax.ShapeDtypeStruct((n, D_MODEL), _F32)],
        scratch_shapes=[pltpu.VMEM((EBLK // 2 * ROW_TILES, LANES), ROW_DTYPE), pltpu.SemaphoreType.DMA(()),
                        pltpu.SemaphoreType.DMA(())],
        compiler_params=_params(1),
        name="dispatch",
    )(dest_flat, tab, used, trow, w_gu, w_sd)


def _expert_body(blk_ref, used_ref, fill_ref, xs_ref, wg_ref, wu_ref, wd_ref, y_ref, wgu_sc, wd_sc):
    i = pl.program_id(0)
    prev = blk_ref[jnp.maximum(i - 1, 0)]
    fresh = (i == 0) | (blk_ref[i] != prev)

    @pl.when(fresh)
    def _():
        wgu_sc[:, :EXPERT_FF] = wg_ref[...].astype(_BF16)
        wgu_sc[:, EXPERT_FF:] = wu_ref[...].astype(_BF16)
        wd_sc[...] = wd_ref[...].astype(_BF16)

    @pl.when(i < used_ref[0])
    def _():
        live = lax.broadcasted_iota(jnp.int32, (EBLK, 1), 0) < fill_ref[i]
        x = _load_rows(xs_ref, 0, EBLK)
        x = jnp.where(live, x, jnp.zeros_like(x)).astype(_BF16)
        gu = _dot(x, wgu_sc[...])
        mid = (_silu(gu[:, :EXPERT_FF]) * gu[:, EXPERT_FF:]).astype(_BF16)
        _store_rows(y_ref, _dot(mid, wd_sc[...]))

    @pl.when(i >= used_ref[0])
    def _():
        y_ref[...] = jnp.zeros_like(y_ref)


def _experts(blk_e, used, fill, xs, we_gate, we_up, we_down):
    nblk = xs.shape[0] // (EBLK * ROW_TILES)
    last = lambda i, used: jnp.minimum(i, jnp.maximum(used[0] - 1, 0))
    w_spec = lambda shape: pl.BlockSpec((None,) + shape, lambda i, blk, used, fill: (blk[i], 0, 0))
    return pl.pallas_call(
        _expert_body,
        grid_spec=pltpu.PrefetchScalarGridSpec(
            num_scalar_prefetch=3,
            grid=(nblk,),
            in_specs=[pl.BlockSpec((EBLK * ROW_TILES, LANES), lambda i, blk, used, fill: (last(i, used), 0)),
                      w_spec((D_MODEL, EXPERT_FF)), w_spec((D_MODEL, EXPERT_FF)), w_spec((EXPERT_FF, D_MODEL))],
            out_specs=pl.BlockSpec((EBLK * ROW_TILES, LANES), lambda i, blk, used, fill: (i, 0)),
            scratch_shapes=[pltpu.VMEM((D_MODEL, 2 * EXPERT_FF), _BF16), pltpu.VMEM((EXPERT_FF, D_MODEL), _BF16)],
        ),
        out_shape=jax.ShapeDtypeStruct(xs.shape, ROW_DTYPE),
        compiler_params=_params(1),
        name="experts",
    )(blk_e, used, fill, xs, we_gate, we_up, we_down)


def _combine_body(dest_ref, dnext_ref, y_ref, base_ref, gate_ref, g2_ref, b2_ref, o_ref, z_sc, sems):
    i = pl.program_id(0)
    slot = lax.rem(i, 2)
    tile_rows = TT * TOP_K * ROW_TILES

    def gather(dref, into):
        def issue(j, carry):
            for u in range(ISSUE_UNROLL):
                t = ISSUE_UNROLL * j + u
                for k in range(TOP_K):
                    _row_copy(y_ref, dref[t * SUBLANES + k], z_sc.at[into], k * TT + t,
                              sems.at[into]).start(priority=k % 2)
            return carry

        lax.fori_loop(0, TT // ISSUE_UNROLL, issue, 0)

    @pl.when(i == 0)
    def _():
        gather(dest_ref, 0)

    @pl.when(i + 1 < pl.num_programs(0))
    def _():
        gather(dnext_ref, 1 - slot)

    pltpu.make_async_copy(y_ref.at[pl.ds(0, tile_rows), :], z_sc.at[slot], sems.at[slot]).wait()
    gates = gate_ref[...]
    acc = base_ref[...]
    for k in range(TOP_K):
        acc = acc + gates[:, k:k + 1] * _load_rows(z_sc.at[slot], k * TT, TT).astype(_F32)
    o_ref[...] = _ln(acc, g2_ref[...], b2_ref[...])


def _combine(dest_flat, y, base, gates_t, ln2_g, ln2_b):
    n = base.shape[0]
    steps = n // TT
    const = lambda shape: pl.BlockSpec(shape, lambda i: (0, 0))
    dest_spec = lambda ahead: pl.BlockSpec((TT * SUBLANES,), lambda i: (jnp.minimum(i + ahead, steps - 1),),
                                           memory_space=pltpu.SMEM)
    return pl.pallas_call(
        _combine_body,
        grid=(steps,),
        in_specs=[dest_spec(0), dest_spec(1),
                  pl.BlockSpec(memory_space=pl.ANY),
                  pl.BlockSpec((TT, D_MODEL), lambda i: (i, 0)),
                  pl.BlockSpec((TT, SUBLANES), lambda i: (i, 0)),
                  const((1, D_MODEL)), const((1, D_MODEL))],
        out_specs=pl.BlockSpec((TT, D_MODEL), lambda i: (i, 0)),
        out_shape=jax.ShapeDtypeStruct((n, D_MODEL), _F32),
        scratch_shapes=[pltpu.VMEM((2, TT * TOP_K * ROW_TILES, LANES), ROW_DTYPE), pltpu.SemaphoreType.DMA((2,))],
        compiler_params=_params(1),
        name="combine",
    )(dest_flat, dest_flat, y, base, gates_t, ln2_g, ln2_b)


def _rope_tables(pos):
    half = RET_DK // 2
    inv = ROPE_BASE ** (-jnp.arange(half, dtype=_F32) / half)
    ang = pos[:, None] * inv[None, :]
    cos = jnp.cos(ang)
    sin = jnp.sin(ang)
    return jnp.concatenate([cos, cos], -1), jnp.concatenate([-sin, sin], -1)


def _decay_tables():
    lg = jnp.log1p(-jnp.exp2(-5.0 - jnp.arange(RET_HEADS, dtype=_F32)))
    idx = jnp.arange(BLOCK, dtype=_F32)
    rel = idx[:, None] - idx[None, :]
    causal = rel >= 0
    dmask = jnp.where(causal[None], jnp.exp(jnp.where(causal, rel, 0.0)[None] * lg[:, None, None]), 0.0)
    zeta = jnp.exp((BLOCK - 1.0 - idx)[None, :] * lg[:, None])
    xi = jnp.exp((idx + 1.0)[None, :] * lg[:, None])
    along_lanes = lambda col: jnp.broadcast_to(col[:, :, None], (RET_HEADS, BLOCK, LANES))
    return dmask, along_lanes(xi), along_lanes(zeta), jnp.exp(BLOCK * lg)


def kernel(x, meta, ln0_g, ln0_b, w_in, b_forget, w_out, ln1_g, ln1_b, w_router, router_bias, we_gate, we_up,
           we_down, ws_gate, ws_up, ws_down, ln2_g, ln2_b):
    nb, s, d = x.shape
    assert d == D_MODEL and meta.shape == (N_META, D_MODEL) and w_in.shape[0] == 1
    assert s % TM_PROJ == 0 and s % T_FOX == 0 and (nb * s) % TT == 0
    n = nb * s
    x2d = x.reshape(n, d)
    row2 = lambda v: v.reshape(1, -1).astype(_F32)
    main_cols = 7 * HEAD_W
    w_all = jnp.concatenate(
        [w_in[0, :, :main_cols], w_in[0, :, main_cols:], jnp.zeros((d, LANES - FOX_HEADS), w_in.dtype)],
        axis=1).astype(_BF16)
    bf_pad = jnp.concatenate([b_forget[0].astype(_F32), jnp.zeros((LANES - FOX_HEADS,), _F32)]).reshape(1, LANES)
    g0, b0 = row2(ln0_g), row2(ln0_b)

    cos_x, sin_x = _rope_tables(jnp.arange(s, dtype=_F32) + float(N_META))
    cos_m, sin_m = _rope_tables(jnp.arange(BLOCK, dtype=_F32) - float(PAD))
    meta_blk = jnp.concatenate([jnp.zeros((PAD, d), _F32), meta.astype(_F32)], axis=0)

    h0, rq, rk, rv, rg, fq, fk, fv = _inproj(x2d, g0, b0, w_all, bf_pad, cos_x, sin_x, nb=nb, meta=False)
    _, _, rk_m, rv_m, _, _, fk_m, fv_m = _inproj(meta_blk, g0, b0, w_all, bf_pad, cos_m, sin_m, nb=1, meta=True)

    dmask, xi, zeta, gch = _decay_tables()
    per_batch = lambda a: a.reshape(nb, s, a.shape[-1])
    ret = _retention(per_batch(rq), per_batch(rk), per_batch(rv), per_batch(rg), rk_m, rv_m, dmask, xi, zeta,
                     gch).reshape(n, HEAD_W)
    fox = _fox(per_batch(fq), per_batch(fk), per_batch(fv), fk_m, fv_m, nb=nb).reshape(n, HEAD_W)

    w_gu = jnp.concatenate([ws_gate[0], ws_up[0]], axis=1).astype(_BF16)
    trow, sel, gates, rank, cnt = _post(
        h0, ret, fox, w_out[0].astype(_BF16), row2(ln1_g[0]), row2(ln1_b[0]),
        jnp.transpose(w_router[0]).astype(_BF16), router_bias[0].astype(_F32).reshape(N_EXPERTS, 1))

    nblk = n * TOP_K // EBLK + N_EXPERTS
    nblk_pad = -(-nblk // LANES) * LANES
    dest, blk_e, fill, used, tab = _plan(sel, rank, cnt, nblk_pad)
    dest_flat = jnp.transpose(dest).reshape(-1)
    used = used.reshape(-1)

    xs, base = _dispatch(dest_flat, tab, used, trow, w_gu, ws_down[0].astype(_BF16), nblk * EBLK)
    y = _experts(blk_e.reshape(-1), used, fill.reshape(-1), xs, we_gate[0], we_up[0], we_down[0])
    out = _combine(dest_flat, y, base, jnp.transpose(gates), row2(ln2_g[0]), row2(ln2_b[0]))
    return out.reshape(nb, s, d)
```

```python
import functools

import jax
import jax.numpy as jnp
import numpy as np
from jax import lax
from jax.experimental import pallas as pl
from jax.experimental.pallas import tpu as pltpu

D_MODEL = 1024
N_META = 16
BLOCK = 128
PAD = BLOCK - N_META
RET_HEADS = 4
RET_DK = 128
FOX_HEADS = 8
FOX_HD = 64
N_EXPERTS = 64
TOP_K = 6
N_GROUPS = 8
GROUP_SIZE = N_EXPERTS // N_GROUPS
TOPK_GROUPS = 4
EXPERT_FF = 256
SHARED_FF = 256
ROUTED_SCALE = 2.5
ROPE_BASE = 10000.0
LN_EPS = 1e-5
NEG_INF = -1e30
ALPHA = 2.0 ** 0.25
HEAD_W = 512
LOG2E = 1.4426950408889634

LANES = 128
FOX_W = FOX_HEADS * LANES
SUBLANES = 8
ROW_TILES = D_MODEL // LANES
ROW_DTYPE = jnp.float32

TM_PROJ = 512
T_FOX = 512
FOX_CHUNK = 64
FOX_K_TERMS = 32
TT = 256
TT_DISPATCH = 512
EBLK = 1024
ISSUE_UNROLL = 4
TAB_PAD0, TAB_PADN = 0, 1
V7X_VMEM_BYTES = 64 * 1024 * 1024
VMEM_LIMIT = V7X_VMEM_BYTES * 3 // 4

_F32 = jnp.float32
_BF16 = jnp.bfloat16


def _ln(x, g, b):
    xc = x - jnp.mean(x, -1, keepdims=True)
    var = jnp.mean(xc * xc, -1, keepdims=True)
    return xc * lax.rsqrt(var + LN_EPS) * g + b


def _dot(a, b):
    return jnp.dot(a, b, preferred_element_type=_F32)


def _dot_nt(a, b):
    return lax.dot_general(a, b, (((1,), (1,)), ((), ())), preferred_element_type=_F32)


def _dot_tn(a, b):
    return lax.dot_general(a, b, (((0,), (0,)), ((), ())), preferred_element_type=_F32)


def _silu(x):
    return x * jax.nn.sigmoid(x)


def _params(n_axes):
    return pltpu.CompilerParams(dimension_semantics=("arbitrary",) * n_axes, vmem_limit_bytes=VMEM_LIMIT)


def _inproj_body(x_ref, g_ref, b_ref, w_ref, bf_ref, cos_ref, sin_ref, own_ref, tq_ref, tk_ref, oq_ref, ok_ref, ov_ref,
                 h_ref, rq_ref, rk_ref, rv_ref, rg_ref, fq_ref, fk_ref, fv_ref, carry_ref, *, meta):
    tm = x_ref.shape[0]
    if not meta:
        @pl.when(pl.program_id(1) == 0)
        def _():
            carry_ref[...] = jnp.zeros_like(carry_ref)

    h = _ln(x_ref[...], g_ref[...], b_ref[...])
    if meta:
        valid = lax.broadcasted_iota(jnp.int32, (tm, 1), 0) >= PAD
        h = jnp.where(valid, h, 0.0)
    h_ref[...] = h
    hb = h.astype(_BF16)
    cos = cos_ref[...]
    sin = sin_ref[...]

    def proj(g):
        return _dot(hb, w_ref[:, g * HEAD_W:(g + 1) * HEAD_W])

    def rope_store(p, out_ref, scale):
        for hd in range(RET_HEADS):
            t = p[:, hd * LANES:(hd + 1) * LANES]
            r = t * cos + pltpu.roll(t, LANES // 2, axis=1) * sin
            out_ref[:, hd * LANES:(hd + 1) * LANES] = (r * scale).astype(_BF16)

    z = _dot(hb, w_ref[:, 7 * HEAD_W:7 * HEAD_W + LANES]) + bf_ref[...]
    rope_store(proj(0), rq_ref, 1.0)
    logf = jnp.minimum(z, 0.0) - jnp.log1p(jnp.exp(-jnp.abs(z)))
    if meta:
        logf = jnp.where(valid, logf, 0.0)
    l1 = logf.astype(_BF16)
    r1 = logf - l1.astype(_F32)
    l2 = r1.astype(_BF16)
    l3 = (r1 - l2.astype(_F32)).astype(_BF16)
    row = lax.broadcasted_iota(jnp.int32, (tm, tm), 0)
    col = lax.broadcasted_iota(jnp.int32, (tm, tm), 1)
    tri = (col <= row).astype(_BF16)
    rope_store(proj(1), rk_ref, RET_DK ** -0.5)
    c = _dot(tri, l1) + _dot(tri, l2) + _dot(tri, l3)
    rv_ref[...] = proj(2).astype(_BF16)
    rg_ref[...] = _silu(proj(3)).astype(_BF16)
    if meta:
        c = c - c[tm - 1:tm, :]
    else:
        c = c + carry_ref[...]
        carry_ref[...] = c[tm - 1:tm, :]

    head_lane = lax.broadcasted_iota(jnp.int32, (1, LANES), 1) < FOX_HEADS
    cl = jnp.where(head_lane, c, 0.0) * LOG2E
    c1 = cl.astype(_BF16).astype(_F32)
    r1 = cl - c1
    c2 = r1.astype(_BF16).astype(_F32)
    c3 = (r1 - c2).astype(_BF16).astype(_F32)
    csplit = c1 + pltpu.roll(c2, FOX_HEADS, axis=1) + pltpu.roll(c3, 2 * FOX_HEADS, axis=1)
    half = FOX_HD
    q_even, q_odd = pltpu.roll(csplit, half, axis=1), csplit
    k_even, k_odd = pltpu.roll(csplit, half + FOX_K_TERMS, axis=1), pltpu.roll(csplit, FOX_K_TERMS, axis=1)
    own = own_ref[...] > 0.0

    def per_head(p):
        return jnp.concatenate([p[:, (hd // 2) * LANES:(hd // 2 + 1) * LANES] for hd in range(FOX_HEADS)], axis=1)

    def by_parity(even, odd):
        return jnp.concatenate([even, odd] * (FOX_HEADS // 2), axis=1)

    q_extra = jnp.where(tq_ref[...] > 0.0, by_parity(q_even, q_odd), oq_ref[...])
    k_extra = jnp.where(tk_ref[...] > 0.0, -by_parity(k_even, k_odd), ok_ref[...])
    fq_ref[...] = jnp.where(own, per_head(proj(4) * (FOX_HD ** -0.5 * LOG2E)), q_extra).astype(_BF16)
    fk_ref[...] = jnp.where(own, per_head(proj(5)), k_extra).astype(_BF16)
    fv_ref[...] = jnp.where(own, per_head(proj(6)), ov_ref[...]).astype(_BF16)


def _fox_lane_tables():
    own, tq, tk, oq, ok, ov = (np.zeros((1, FOX_W), np.float32) for _ in range(6))
    for hd in range(FOX_HEADS):
        data = hd * LANES + (hd % 2) * FOX_HD
        extra = hd * LANES + (1 - hd % 2) * FOX_HD
        own[0, data:data + FOX_HD] = 1.0
        for term in range(3):
            lane = extra + term * FOX_HEADS + hd
            tq[0, lane] = 1.0
            ok[0, lane] = 1.0
            tk[0, lane + FOX_K_TERMS] = 1.0
            oq[0, lane + FOX_K_TERMS] = 1.0
        ov[0, extra] = 1.0
    return tuple(jnp.asarray(t) for t in (own, tq, tk, oq, ok, ov))


def _inproj(x2d, ln_g, ln_b, w_all, bf_pad, cos_t, sin_t, *, nb, meta):
    n = x2d.shape[0]
    s = n // nb
    tm = min(TM_PROJ, s)
    nj = s // tm
    row_spec = lambda w: pl.BlockSpec((tm, w), lambda b, j: (b * nj + j, 0))
    const = lambda shape: pl.BlockSpec(shape, lambda b, j: (0, 0))
    pos_spec = pl.BlockSpec((tm, LANES), lambda b, j: (j, 0))
    tables = _fox_lane_tables()
    outs = ([jax.ShapeDtypeStruct((n, D_MODEL), _F32)] + [jax.ShapeDtypeStruct((n, HEAD_W), _BF16)] * 4
            + [jax.ShapeDtypeStruct((n, FOX_W), _BF16)] * 3)
    return pl.pallas_call(
        functools.partial(_inproj_body, meta=meta),
        grid=(nb, nj),
        in_specs=[row_spec(D_MODEL), const((1, D_MODEL)), const((1, D_MODEL)), const(w_all.shape),
                  const((1, LANES)), pos_spec, pos_spec] + [const(t.shape) for t in tables],
        out_specs=[row_spec(D_MODEL)] + [row_spec(HEAD_W)] * 4 + [row_spec(FOX_W)] * 3,
        out_shape=outs,
        scratch_shapes=[pltpu.VMEM((1, LANES), _F32)],
        compiler_params=_params(2),
        name="inproj_meta" if meta else "inproj",
    )(x2d, ln_g, ln_b, w_all, bf_pad, cos_t, sin_t, *tables)


def _ret_body(q_ref, k_ref, v_ref, g_ref, km_ref, vm_ref, dm_ref, xi_ref, zeta_ref, gch_ref, o_ref, st_ref):
    def kv_update(k, v, hd):
        vz = (v.astype(_F32) * zeta_ref[hd]).astype(_BF16)
        return _dot_tn(k, vz)

    nb = q_ref.shape[0]

    @pl.when(pl.program_id(0) == 0)
    def _():
        for hd in range(RET_HEADS):
            sl = slice(hd * LANES, (hd + 1) * LANES)
            first = kv_update(km_ref[:, sl], vm_ref[:, sl], hd)
            for b in range(nb):
                st_ref[b, hd] = first

    group = 4
    for b0 in range(0, nb, group):
        chains = [(b, hd, slice(hd * LANES, (hd + 1) * LANES)) for b in range(b0, b0 + group) for hd in range(RET_HEADS)]
        scores = [_dot_nt(q_ref[b, :, sl], k_ref[b, :, sl]) * dm_ref[hd] for b, hd, sl in chains]
        cross = [_dot(q_ref[b, :, sl], st_ref[b, hd].astype(_BF16)) * xi_ref[hd] for b, hd, sl in chains]
        outs = [_dot(s.astype(_BF16), v_ref[b, :, sl]) + c for s, c, (b, hd, sl) in zip(scores, cross, chains)]
        for o, (b, hd, sl) in zip(outs, chains):
            oc = o - jnp.mean(o, -1, keepdims=True)
            y = oc * lax.rsqrt(jnp.mean(oc * oc, -1, keepdims=True) + LN_EPS)
            o_ref[b, :, sl] = (y * g_ref[b, :, sl].astype(_F32)).astype(_BF16)
        for b, hd, sl in chains:
            st_ref[b, hd] = gch_ref[hd] * st_ref[b, hd] + kv_update(k_ref[b, :, sl], v_ref[b, :, sl], hd)


def _retention(rq, rk, rv, rg, rk_m, rv_m, dmask, xi, zeta, gch):
    nb, s, _ = rq.shape
    row_spec = pl.BlockSpec((nb, BLOCK, HEAD_W), lambda j: (0, j, 0))
    meta_spec = pl.BlockSpec((BLOCK, HEAD_W), lambda j: (0, 0))
    tab = pl.BlockSpec((RET_HEADS, BLOCK, BLOCK), lambda j: (0, 0, 0))
    return pl.pallas_call(
        _ret_body,
        grid=(s // BLOCK,),
        in_specs=[row_spec] * 4 + [meta_spec] * 2 + [tab] * 3 + [pl.BlockSpec(memory_space=pltpu.SMEM)],
        out_specs=row_spec,
        out_shape=jax.ShapeDtypeStruct((nb, s, HEAD_W), _BF16),
        scratch_shapes=[pltpu.VMEM((nb, RET_HEADS, RET_DK, LANES), _F32)],
        compiler_params=_params(1),
        name="retention",
    )(rq, rk, rv, rg, rk_m, rv_m, dmask, xi, zeta, gch)


def _fox_body(q_ref, k_ref, v_ref, km_ref, vm_ref, o_ref, m_sc, acc_sc, s_sc, p_sc):
    t = q_ref.shape[0]
    qi = pl.program_id(2)

    heads = [slice(hh * LANES, (hh + 1) * LANES) for hh in range(2)]

    def logits(hh, k, buf, tk):
        s_sc[buf, hh, :, :tk] = _dot_nt(q_ref[:, heads[hh]], k)

    def update(hh, v, buf, tk, mask, first):
        for c in range(t // FOX_CHUNK):
            rows = slice(c * FOX_CHUNK, (c + 1) * FOX_CHUNK)
            s = s_sc[buf, hh, rows, :tk]
            if mask is not None:
                s = mask(s, c)
            mx = jnp.max(s, axis=1, keepdims=True)
            if first:
                m_new = mx
            else:
                m_prev = m_sc[hh, rows, :]
                m_new = jnp.maximum(m_prev, mx)
                acc_sc[hh, rows, :] = jnp.exp2(m_prev - m_new) * acc_sc[hh, rows, :]
            p_sc[hh, rows, :tk] = jnp.exp2(s - m_new).astype(_BF16)
            m_sc[hh, rows, :] = m_new
        pv = _dot(p_sc[hh, :, :tk], v)
        if first:
            acc_sc[hh] = pv
        else:
            acc_sc[hh] += pv

    def meta_mask(s, c):
        key = lax.broadcasted_iota(jnp.int32, s.shape, 1)
        return jnp.where(key >= PAD, s, NEG_INF)

    def causal_mask(s, c):
        key = lax.broadcasted_iota(jnp.int32, s.shape, 1)
        query = lax.broadcasted_iota(jnp.int32, s.shape, 0) + c * FOX_CHUNK
        return jnp.where(key <= query, s, NEG_INF)

    def key_tile(ref, ki, hh):
        return ref[pl.ds(pl.multiple_of(ki * t, t), t), heads[hh]]

    for hh in range(2):
        logits(hh, km_ref[:, heads[hh]], 1, BLOCK)
    for hh in range(2):
        logits(hh, key_tile(k_ref, 0, hh), 0, t)
        update(hh, vm_ref[:, heads[hh]], 1, BLOCK, meta_mask, True)

    def step(ki, buf, mask, more):
        for hh in range(2):
            if more:
                logits(hh, key_tile(k_ref, ki + 1, hh), 1 - buf, t)
            update(hh, key_tile(v_ref, ki, hh), buf, t, mask, False)

    def pair_body(j, carry):
        step(2 * j, 0, None, True)
        step(2 * j + 1, 1, None, True)
        return carry

    lax.fori_loop(0, lax.shift_right_logical(qi, 1), pair_body, 0)
    odd = lax.rem(qi, 2) == 1

    @pl.when(odd)
    def _():
        step(qi - 1, 0, None, True)
        step(qi, 1, causal_mask, False)

    @pl.when(jnp.logical_not(odd))
    def _():
        step(qi, 0, causal_mask, False)

    outs = []
    for hh in range(2):
        acc = acc_sc[hh]
        ones_lane = (1 - hh) * FOX_HD
        outs.append(acc / acc[:, ones_lane:ones_lane + 1])
    lane = lax.broadcasted_iota(jnp.int32, (t, LANES), 1)
    o_ref[...] = jnp.where(lane < FOX_HD, outs[0], outs[1]).astype(_BF16)


def _fox(fq, fk, fv, fk_m, fv_m, *, nb):
    s = fq.shape[1]
    t = T_FOX
    pair_w = 2 * LANES
    return pl.pallas_call(
        _fox_body,
        grid=(nb, FOX_HEADS // 2, s // t),
        in_specs=[
            pl.BlockSpec((None, t, pair_w), lambda b, p, i: (b, i, p)),
            pl.BlockSpec((None, s, pair_w), lambda b, p, i: (b, 0, p)),
            pl.BlockSpec((None, s, pair_w), lambda b, p, i: (b, 0, p)),
            pl.BlockSpec((BLOCK, pair_w), lambda b, p, i: (0, p)),
            pl.BlockSpec((BLOCK, pair_w), lambda b, p, i: (0, p)),
        ],
        out_specs=pl.BlockSpec((None, t, LANES), lambda b, p, i: (b, i, p)),
        out_shape=jax.ShapeDtypeStruct((nb, s, HEAD_W), _BF16),
        scratch_shapes=[pltpu.VMEM((2, t, 1), _F32), pltpu.VMEM((2, t, LANES), _F32), pltpu.VMEM((2, 2, t, t), _F32),
                        pltpu.VMEM((2, t, t), _BF16)],
        compiler_params=_params(3),
        name="fox",
    )(fq, fk, fv, fk_m, fv_m)


def _route(scores, biased):
    w = scores.shape[1]
    sub = lax.broadcasted_iota(jnp.int32, (GROUP_SIZE, w), 0).astype(_F32)
    groups = [biased[g * GROUP_SIZE:(g + 1) * GROUP_SIZE, :] for g in range(N_GROUPS)]
    gscore = []
    for v in groups:
        m1 = jnp.max(v, axis=0, keepdims=True)
        i1 = jnp.min(jnp.where(v == m1, sub, float(GROUP_SIZE)), axis=0, keepdims=True)
        m2 = jnp.max(jnp.where(sub == i1, -jnp.inf, v), axis=0, keepdims=True)
        gscore.append(m1 + m2)
    masked = []
    for g in range(N_GROUPS):
        beaten = jnp.zeros((1, w), _F32)
        for o in range(N_GROUPS):
            if o == g:
                continue
            wins = (gscore[o] >= gscore[g]) if o < g else (gscore[o] > gscore[g])
            beaten = beaten + wins.astype(_F32)
        masked.append(jnp.where(beaten < float(TOPK_GROUPS), groups[g], NEG_INF))
    work = jnp.concatenate(masked, axis=0)
    eid = lax.broadcasted_iota(jnp.int32, (N_EXPERTS, w), 0).astype(_F32)
    sels, raws = [], []
    member = None
    for _ in range(TOP_K):
        m = jnp.max(work, axis=0, keepdims=True)
        idx = jnp.min(jnp.where(work == m, eid, float(N_EXPERTS)), axis=0, keepdims=True)
        hot = eid == idx
        sels.append(idx)
        raws.append(jnp.sum(jnp.where(hot, scores, 0.0), axis=0, keepdims=True))
        work = jnp.where(hot, -jnp.inf, work)
        member = hot if member is None else (member | hot)
    return sels, raws, member


def _post_body(h0_ref, ret_ref, fox_ref, wo_ref, g1_ref, b1_ref, wr_ref, rb_ref,
               trow_ref, sel_ref, gate_ref, rank_ref, cnt_ref, run_ref):
    tm = h0_ref.shape[0]
    i = pl.program_id(0)

    @pl.when(i == 0)
    def _():
        run_ref[...] = jnp.zeros_like(run_ref)

    halves = [slice(hf * tm // 2, (hf + 1) * tm // 2) for hf in range(2)]
    ys = [_dot(ret_ref[hs, :], wo_ref[:HEAD_W, :]) + _dot(fox_ref[hs, :], wo_ref[HEAD_W:, :]) for hs in halves]
    h1s = [_ln(ALPHA * h0_ref[hs, :] + y, g1_ref[...], b1_ref[...]) for hs, y in zip(halves, ys)]
    for hs, h1 in zip(halves, h1s):
        _store_rows(trow_ref, h1, hs.start)

    scores = jnp.concatenate([jax.nn.sigmoid(_dot_nt(wr_ref[...], h1.astype(_BF16))) for h1 in h1s], axis=1)
    biased = scores + rb_ref[...]
    chunks = [slice(c * LANES, (c + 1) * LANES) for c in range(tm // LANES)]
    routed = [_route(scores[:, cs], biased[:, cs]) for cs in chunks]
    member_f = jnp.concatenate([member.astype(_F32) for _, _, member in routed], axis=1)
    row = lax.broadcasted_iota(jnp.int32, (tm, tm), 0)
    col = lax.broadcasted_iota(jnp.int32, (tm, tm), 1)
    before = (row < col).astype(_BF16)
    rank_e = _dot(member_f.astype(_BF16), before) + run_ref[...]
    eid = lax.broadcasted_iota(jnp.int32, (N_EXPERTS, LANES), 0).astype(_F32)
    sel_rows, gate_rows, rank_rows = [], [], []
    for k in range(TOP_K):
        sel_k, gate_k, rank_k = [], [], []
        for cs, (sels, raws, _) in zip(chunks, routed):
            total = raws[0]
            for r in raws[1:]:
                total = total + r
            sel_k.append(sels[k].astype(jnp.int32))
            gate_k.append(raws[k] * (ROUTED_SCALE / total))
            rank_k.append(jnp.sum(jnp.where(eid == sels[k], rank_e[:, cs], 0.0), axis=0, keepdims=True).astype(jnp.int32))
        sel_rows.append(jnp.concatenate(sel_k, axis=1))
        gate_rows.append(jnp.concatenate(gate_k, axis=1))
        rank_rows.append(jnp.concatenate(rank_k, axis=1))
    pad_rows = sel_ref.shape[0] - TOP_K
    sel_ref[...] = jnp.concatenate(sel_rows + [jnp.zeros((pad_rows, tm), jnp.int32)], axis=0)
    gate_ref[...] = jnp.concatenate(gate_rows + [jnp.zeros((pad_rows, tm), _F32)], axis=0)
    rank_ref[...] = jnp.concatenate(rank_rows + [jnp.zeros((pad_rows, tm), jnp.int32)], axis=0)
    run_ref[...] = run_ref[...] + jnp.sum(member_f, axis=1, keepdims=True)
    cnt_ref[...] = run_ref[...]


def _post(h0, ret, fox, w_out, ln1_g, ln1_b, w_rt, rbias):
    n = h0.shape[0]
    tm = TM_PROJ
    row_spec = lambda w: pl.BlockSpec((tm, w), lambda i: (i, 0))
    col_spec = pl.BlockSpec((SUBLANES, tm), lambda i: (0, i))
    const = lambda shape: pl.BlockSpec(shape, lambda i: (0, 0))
    return pl.pallas_call(
        _post_body,
        grid=(n // tm,),
        in_specs=[row_spec(D_MODEL), row_spec(HEAD_W), row_spec(HEAD_W),
                  const(w_out.shape), const((1, D_MODEL)), const((1, D_MODEL)), const(w_rt.shape),
                  const((N_EXPERTS, 1))],
        out_specs=[pl.BlockSpec((tm * ROW_TILES, LANES), lambda i: (i, 0)),
                   col_spec, col_spec, col_spec, const((N_EXPERTS, 1))],
        out_shape=[jax.ShapeDtypeStruct((n * ROW_TILES, LANES), ROW_DTYPE),
                   jax.ShapeDtypeStruct((SUBLANES, n), jnp.int32), jax.ShapeDtypeStruct((SUBLANES, n), _F32),
                   jax.ShapeDtypeStruct((SUBLANES, n), jnp.int32), jax.ShapeDtypeStruct((N_EXPERTS, 1), _F32)],
        scratch_shapes=[pltpu.VMEM((N_EXPERTS, 1), _F32)],
        compiler_params=_params(1),
        name="post_mixer",
    )(h0, ret, fox, w_out, ln1_g, ln1_b, w_rt, rbias)


def _plan_body(sel_ref, rank_ref, cnt_ref, dest_ref, blk_ref, fill_ref, used_ref, tab_ref):
    cnt = cnt_ref[...]
    padded = jnp.ceil(cnt * (1.0 / EBLK)) * EBLK
    er = lax.broadcasted_iota(jnp.int32, (N_EXPERTS, N_EXPERTS), 0)
    ec = lax.broadcasted_iota(jnp.int32, (N_EXPERTS, N_EXPERTS), 1)
    padded_row = jnp.sum(jnp.where(er == ec, padded, 0.0), axis=0, keepdims=True)
    pstart = jnp.sum(jnp.where(ec < er, padded_row, 0.0), axis=1, keepdims=True)
    pend = pstart + padded
    sel = sel_ref[...]
    dest = rank_ref[...]
    for e in range(N_EXPERTS):
        dest = dest + jnp.where(sel == e, pstart[e:e + 1, :].astype(jnp.int32), 0)
    dest_ref[...] = dest
    nblk = blk_ref.shape[1]
    first_row = (lax.broadcasted_iota(jnp.int32, (N_EXPERTS, nblk), 1) * EBLK).astype(_F32)
    owner = jnp.minimum(jnp.sum((pend <= first_row).astype(_F32), axis=0, keepdims=True), N_EXPERTS - 1.0)
    blk_ref[...] = owner.astype(jnp.int32)
    mine = lax.broadcasted_iota(jnp.int32, (N_EXPERTS, nblk), 0).astype(_F32) == owner
    live_end = jnp.sum(jnp.where(mine, pstart + cnt, 0.0), axis=0, keepdims=True)
    fill_ref[...] = jnp.clip(live_end - first_row[:1, :], 0.0, float(EBLK)).astype(jnp.int32)
    used_ref[...] = (pend[N_EXPERTS - 1:, :] * (1.0 / EBLK)).astype(jnp.int32)
    as_row = lambda col: jnp.sum(jnp.where(er == ec, col, 0.0), axis=0, keepdims=True).astype(jnp.int32)
    rows = {TAB_PAD0: pstart + cnt, TAB_PADN: padded - cnt}
    blank = jnp.zeros((1, N_EXPERTS), jnp.int32)
    tab_ref[...] = jnp.concatenate([as_row(rows[r]) if r in rows else blank for r in range(SUBLANES)], axis=0)


def _plan(sel, rank, cnt, nblk_pad):
    n = sel.shape[1]
    full = lambda shape: pl.BlockSpec(shape, lambda i: (0, 0))
    return pl.pallas_call(
        _plan_body,
        grid=(1,),
        in_specs=[full(sel.shape), full(rank.shape), full(cnt.shape)],
        out_specs=[full(sel.shape), full((1, nblk_pad)), full((1, nblk_pad)), full((1, 1)),
                   full((SUBLANES, N_EXPERTS))],
        out_shape=[jax.ShapeDtypeStruct((SUBLANES, n), jnp.int32), jax.ShapeDtypeStruct((1, nblk_pad), jnp.int32),
                   jax.ShapeDtypeStruct((1, nblk_pad), jnp.int32), jax.ShapeDtypeStruct((1, 1), jnp.int32),
                   jax.ShapeDtypeStruct((SUBLANES, N_EXPERTS), jnp.int32)],
        compiler_params=_params(1),
        name="plan",
    )(sel, rank, cnt)


def _store_rows(ref, v, first_row=0):
    m = v.shape[0]
    for s in range(ROW_TILES):
        ref[pl.ds(first_row * ROW_TILES + s, m, stride=ROW_TILES), :] = v[:, s * LANES:(s + 1) * LANES].astype(ROW_DTYPE)


def _load_rows(ref, first_row, m):
    return jnp.concatenate([ref[pl.ds(first_row * ROW_TILES + s, m, stride=ROW_TILES), :] for s in range(ROW_TILES)],
                           axis=1)


def _row_copy(src, src_row, dst, dst_row, sem):
    return pltpu.make_async_copy(src.at[pl.ds(pl.multiple_of(src_row * ROW_TILES, ROW_TILES), ROW_TILES), :],
                                 dst.at[pl.ds(pl.multiple_of(dst_row * ROW_TILES, ROW_TILES), ROW_TILES), :], sem)


def _dispatch_body(dest_ref, tab_ref, used_ref, t_ref, wgu_ref, wd_ref, xs_ref, base_ref, zero_sc, sem, zsem):
    tt = t_ref.shape[0] // ROW_TILES
    half_blk = EBLK // 2
    n_half = xs_ref.shape[0] // (half_blk * ROW_TILES)

    @pl.when(pl.program_id(0) == 0)
    def _():
        zero_sc[...] = jnp.zeros_like(zero_sc)

        def zero_copy(first_row, rows):
            return pltpu.make_async_copy(
                zero_sc.at[pl.ds(0, rows * ROW_TILES), :],
                xs_ref.at[pl.ds(pl.multiple_of(first_row * ROW_TILES, ROW_TILES), rows * ROW_TILES), :], zsem)

        def for_padding(act):
            def per_expert(e, carry):
                row = tab_ref[TAB_PAD0, e]
                for bit in range(EBLK.bit_length() - 2, -1, -1):
                    take = (tab_ref[TAB_PADN, e] & (1 << bit)) != 0

                    @pl.when(take)
                    def _():
                        act(zero_copy(row, 1 << bit))

                    row = row + jnp.where(take, 1 << bit, 0)
                return carry

            def per_tail(hb, carry):
                act(zero_copy(hb * half_blk, half_blk))
                return carry

            lax.fori_loop(0, N_EXPERTS, per_expert, 0)
            lax.fori_loop(2 * used_ref[0], n_half, per_tail, 0)

        for_padding(lambda cp: cp.start())
        for_padding(lambda cp: cp.wait())

    def issue(j, carry):
        for u in range(ISSUE_UNROLL):
            i = ISSUE_UNROLL * j + u
            for k in range(TOP_K):
                _row_copy(t_ref, i, xs_ref, dest_ref[i * SUBLANES + k], sem).start(priority=k % 2)
        return carry

    lax.fori_loop(0, tt // ISSUE_UNROLL, issue, 0)

    h1 = _load_rows(t_ref, 0, tt)
    gu = _dot(h1.astype(_BF16), wgu_ref[...])
    mid = (_silu(gu[:, :SHARED_FF]) * gu[:, SHARED_FF:]).astype(_BF16)
    base_ref[...] = ALPHA * h1 + _dot(mid, wd_ref[...])

    for _ in range(TOP_K):
        pltpu.make_async_copy(t_ref, xs_ref.at[pl.ds(0, tt * ROW_TILES), :], sem).wait()


def _dispatch(dest_flat, tab, used, trow, w_gu, w_sd, total_rows):
    n = trow.shape[0] // ROW_TILES
    tt = TT_DISPATCH
    smem = pl.BlockSpec(memory_space=pltpu.SMEM)
    const = lambda shape: pl.BlockSpec(shape, lambda i: (0, 0))
    return pl.pallas_call(
        _dispatch_body,
        grid=(n // tt,),
        in_specs=[pl.BlockSpec((tt * SUBLANES,), lambda i: (i,), memory_space=pltpu.SMEM), smem, smem,
                  pl.BlockSpec((tt * ROW_TILES, LANES), lambda i: (i, 0)), const(w_gu.shape), const(w_sd.shape)],
        out_specs=[pl.BlockSpec(memory_space=pl.ANY), pl.BlockSpec((tt, D_MODEL), lambda i: (i, 0))],
        out_shape=[jax.ShapeDtypeStruct((total_rows * ROW_TILES, LANES), ROW_DTYPE),
                   jax.ShapeDtypeStruct((n, D_MODEL), _F32)],
        scratch_shapes=[pltpu.VMEM((EBLK // 2 * ROW_TILES, LANES), ROW_DTYPE), pltpu.SemaphoreType.DMA(()),
                        pltpu.SemaphoreType.DMA(())],
        compiler_params=_params(1),
        name="dispatch",
    )(dest_flat, tab, used, trow, w_gu, w_sd)


def _expert_body(blk_ref, used_ref, fill_ref, xs_ref, wg_ref, wu_ref, wd_ref, y_ref, wgu_sc, wd_sc):
    i = pl.program_id(0)
    prev = blk_ref[jnp.maximum(i - 1, 0)]
    fresh = (i == 0) | (blk_ref[i] != prev)

    @pl.when(fresh)
    def _():
        wgu_sc[:, :EXPERT_FF] = wg_ref[...].astype(_BF16)
        wgu_sc[:, EXPERT_FF:] = wu_ref[...].astype(_BF16)
        wd_sc[...] = wd_ref[...].astype(_BF16)

    @pl.when(i < used_ref[0])
    def _():
        live = lax.broadcasted_iota(jnp.int32, (EBLK, 1), 0) < fill_ref[i]
        x = _load_rows(xs_ref, 0, EBLK)
        x = jnp.where(live, x, jnp.zeros_like(x)).astype(_BF16)
        gu = _dot(x, wgu_sc[...])
        mid = (_silu(gu[:, :EXPERT_FF]) * gu[:, EXPERT_FF:]).astype(_BF16)
        _store_rows(y_ref, _dot(mid, wd_sc[...]))

    @pl.when(i >= used_ref[0])
    def _():
        y_ref[...] = jnp.zeros_like(y_ref)


def _experts(blk_e, used, fill, xs, we_gate, we_up, we_down):
    nblk = xs.shape[0] // (EBLK * ROW_TILES)
    last = lambda i, used: jnp.minimum(i, jnp.maximum(used[0] - 1, 0))
    w_spec = lambda shape: pl.BlockSpec((None,) + shape, lambda i, blk, used, fill: (blk[i], 0, 0))
    return pl.pallas_call(
        _expert_body,
        grid_spec=pltpu.PrefetchScalarGridSpec(
            num_scalar_prefetch=3,
            grid=(nblk,),
            in_specs=[pl.BlockSpec((EBLK * ROW_TILES, LANES), lambda i, blk, used, fill: (last(i, used), 0)),
                      w_spec((D_MODEL, EXPERT_FF)), w_spec((D_MODEL, EXPERT_FF)), w_spec((EXPERT_FF, D_MODEL))],
            out_specs=pl.BlockSpec((EBLK * ROW_TILES, LANES), lambda i, blk, used, fill: (i, 0)),
            scratch_shapes=[pltpu.VMEM((D_MODEL, 2 * EXPERT_FF), _BF16), pltpu.VMEM((EXPERT_FF, D_MODEL), _BF16)],
        ),
        out_shape=jax.ShapeDtypeStruct(xs.shape, ROW_DTYPE),
        compiler_params=_params(1),
        name="experts",
    )(blk_e, used, fill, xs, we_gate, we_up, we_down)


def _combine_body(dest_ref, dnext_ref, y_ref, base_ref, gate_ref, g2_ref, b2_ref, o_ref, z_sc, sems):
    i = pl.program_id(0)
    slot = lax.rem(i, 2)
    tile_rows = TT * TOP_K * ROW_TILES

    def gather(dref, into):
        def issue(j, carry):
            for u in range(ISSUE_UNROLL):
                t = ISSUE_UNROLL * j + u
                for k in range(TOP_K):
                    _row_copy(y_ref, dref[t * SUBLANES + k], z_sc.at[into], k * TT + t,
                              sems.at[into]).start(priority=k % 2)
            return carry

        lax.fori_loop(0, TT // ISSUE_UNROLL, issue, 0)

    @pl.when(i == 0)
    def _():
        gather(dest_ref, 0)

    @pl.when(i + 1 < pl.num_programs(0))
    def _():
        gather(dnext_ref, 1 - slot)

    pltpu.make_async_copy(y_ref.at[pl.ds(0, tile_rows), :], z_sc.at[slot], sems.at[slot]).wait()
    gates = gate_ref[...]
    acc = base_ref[...]
    for k in range(TOP_K):
        acc = acc + gates[:, k:k + 1] * _load_rows(z_sc.at[slot], k * TT, TT).astype(_F32)
    o_ref[...] = _ln(acc, g2_ref[...], b2_ref[...])


def _combine(dest_flat, y, base, gates_t, ln2_g, ln2_b):
    n = base.shape[0]
    steps = n // TT
    const = lambda shape: pl.BlockSpec(shape, lambda i: (0, 0))
    dest_spec = lambda ahead: pl.BlockSpec((TT * SUBLANES,), lambda i: (jnp.minimum(i + ahead, steps - 1),),
                                           memory_space=pltpu.SMEM)
    return pl.pallas_call(
        _combine_body,
        grid=(steps,),
        in_specs=[dest_spec(0), dest_spec(1),
                  pl.BlockSpec(memory_space=pl.ANY),
                  pl.BlockSpec((TT, D_MODEL), lambda i: (i, 0)),
                  pl.BlockSpec((TT, SUBLANES), lambda i: (i, 0)),
                  const((1, D_MODEL)), const((1, D_MODEL))],
        out_specs=pl.BlockSpec((TT, D_MODEL), lambda i: (i, 0)),
        out_shape=jax.ShapeDtypeStruct((n, D_MODEL), _F32),
        scratch_shapes=[pltpu.VMEM((2, TT * TOP_K * ROW_TILES, LANES), ROW_DTYPE), pltpu.SemaphoreType.DMA((2,))],
        compiler_params=_params(1),
        name="combine",
    )(dest_flat, dest_flat, y, base, gates_t, ln2_g, ln2_b)


def _rope_tables(pos):
    half = RET_DK // 2
    inv = ROPE_BASE ** (-jnp.arange(half, dtype=_F32) / half)
    ang = pos[:, None] * inv[None, :]
    cos = jnp.cos(ang)
    sin = jnp.sin(ang)
    return jnp.concatenate([cos, cos], -1), jnp.concatenate([-sin, sin], -1)


def _decay_tables():
    lg = jnp.log1p(-jnp.exp2(-5.0 - jnp.arange(RET_HEADS, dtype=_F32)))
    idx = jnp.arange(BLOCK, dtype=_F32)
    rel = idx[:, None] - idx[None, :]
    causal = rel >= 0
    dmask = jnp.where(causal[None], jnp.exp(jnp.where(causal, rel, 0.0)[None] * lg[:, None, None]), 0.0)
    zeta = jnp.exp((BLOCK - 1.0 - idx)[None, :] * lg[:, None])
    xi = jnp.exp((idx + 1.0)[None, :] * lg[:, None])
    along_lanes = lambda col: jnp.broadcast_to(col[:, :, None], (RET_HEADS, BLOCK, LANES))
    return dmask, along_lanes(xi), along_lanes(zeta), jnp.exp(BLOCK * lg)


def kernel(x, meta, ln0_g, ln0_b, w_in, b_forget, w_out, ln1_g, ln1_b, w_router, router_bias, we_gate, we_up,
           we_down, ws_gate, ws_up, ws_down, ln2_g, ln2_b):
    nb, s, d = x.shape
    assert d == D_MODEL and meta.shape == (N_META, D_MODEL) and w_in.shape[0] == 1
    assert s % TM_PROJ == 0 and s % T_FOX == 0 and (nb * s) % TT == 0
    n = nb * s
    x2d = x.reshape(n, d)
    row2 = lambda v: v.reshape(1, -1).astype(_F32)
    main_cols = 7 * HEAD_W
    w_all = jnp.concatenate(
        [w_in[0, :, :main_cols], w_in[0, :, main_cols:], jnp.zeros((d, LANES - FOX_HEADS), w_in.dtype)],
        axis=1).astype(_BF16)
    bf_pad = jnp.concatenate([b_forget[0].astype(_F32), jnp.zeros((LANES - FOX_HEADS,), _F32)]).reshape(1, LANES)
    g0, b0 = row2(ln0_g), row2(ln0_b)

    cos_x, sin_x = _rope_tables(jnp.arange(s, dtype=_F32) + float(N_META))
    cos_m, sin_m = _rope_tables(jnp.arange(BLOCK, dtype=_F32) - float(PAD))
    meta_blk = jnp.concatenate([jnp.zeros((PAD, d), _F32), meta.astype(_F32)], axis=0)

    h0, rq, rk, rv, rg, fq, fk, fv = _inproj(x2d, g0, b0, w_all, bf_pad, cos_x, sin_x, nb=nb, meta=False)
    _, _, rk_m, rv_m, _, _, fk_m, fv_m = _inproj(meta_blk, g0, b0, w_all, bf_pad, cos_m, sin_m, nb=1, meta=True)

    dmask, xi, zeta, gch = _decay_tables()
    per_batch = lambda a: a.reshape(nb, s, a.shape[-1])
    ret = _retention(per_batch(rq), per_batch(rk), per_batch(rv), per_batch(rg), rk_m, rv_m, dmask, xi, zeta,
                     gch).reshape(n, HEAD_W)
    fox = _fox(per_batch(fq), per_batch(fk), per_batch(fv), fk_m, fv_m, nb=nb).reshape(n, HEAD_W)

    w_gu = jnp.concatenate([ws_gate[0], ws_up[0]], axis=1).astype(_BF16)
    trow, sel, gates, rank, cnt = _post(
        h0, ret, fox, w_out[0].astype(_BF16), row2(ln1_g[0]), row2(ln1_b[0]),
        jnp.transpose(w_router[0]).astype(_BF16), router_bias[0].astype(_F32).reshape(N_EXPERTS, 1))

    nblk = n * TOP_K // EBLK + N_EXPERTS
    nblk_pad = -(-nblk // LANES) * LANES
    dest, blk_e, fill, used, tab = _plan(sel, rank, cnt, nblk_pad)
    dest_flat = jnp.transpose(dest).reshape(-1)
    used = used.reshape(-1)

    xs, base = _dispatch(dest_flat, tab, used, trow, w_gu, ws_down[0].astype(_BF16), nblk * EBLK)
    y = _experts(blk_e.reshape(-1), used, fill.reshape(-1), xs, we_gate[0], we_up[0], we_down[0])
    out = _combine(dest_flat, y, base, jnp.transpose(gates), row2(ln2_g[0]), row2(ln2_b[0]))
    return out.reshape(nb, s, d)
```

```python
import functools

import jax
import jax.numpy as jnp
import numpy as np
from jax import lax
from jax.experimental import pallas as pl
from jax.experimental.pallas import tpu as pltpu

D_MODEL = 1024
N_META = 16
BLOCK = 128
PAD = BLOCK - N_META
RET_HEADS = 4
RET_DK = 128
FOX_HEADS = 8
FOX_HD = 64
N_EXPERTS = 64
TOP_K = 6
N_GROUPS = 8
GROUP_SIZE = N_EXPERTS // N_GROUPS
TOPK_GROUPS = 4
EXPERT_FF = 256
SHARED_FF = 256
ROUTED_SCALE = 2.5
ROPE_BASE = 10000.0
LN_EPS = 1e-5
NEG_INF = -1e30
ALPHA = 2.0 ** 0.25
HEAD_W = 512
LOG2E = 1.4426950408889634

LANES = 128
FOX_W = FOX_HEADS * LANES
SUBLANES = 8
ROW_TILES = D_MODEL // LANES
ROW_DTYPE = jnp.float32

TM_PROJ = 512
T_FOX = 512
FOX_CHUNK = 64
FOX_K_TERMS = 32
TT = 256
TT_DISPATCH = 512
EBLK = 1024
ISSUE_UNROLL = 4
TAB_PAD0, TAB_PADN = 0, 1
V7X_VMEM_BYTES = 64 * 1024 * 1024
VMEM_LIMIT = V7X_VMEM_BYTES * 3 // 4

_F32 = jnp.float32
_BF16 = jnp.bfloat16


def _ln(x, g, b):
    xc = x - jnp.mean(x, -1, keepdims=True)
    var = jnp.mean(xc * xc, -1, keepdims=True)
    return xc * lax.rsqrt(var + LN_EPS) * g + b


def _dot(a, b):
    return jnp.dot(a, b, preferred_element_type=_F32)


def _dot_nt(a, b):
    return lax.dot_general(a, b, (((1,), (1,)), ((), ())), preferred_element_type=_F32)


def _dot_tn(a, b):
    return lax.dot_general(a, b, (((0,), (0,)), ((), ())), preferred_element_type=_F32)


def _silu(x):
    return x * jax.nn.sigmoid(x)


def _params(n_axes):
    return pltpu.CompilerParams(dimension_semantics=("arbitrary",) * n_axes, vmem_limit_bytes=VMEM_LIMIT)


def _inproj_body(x_ref, g_ref, b_ref, w_ref, bf_ref, cos_ref, sin_ref, own_ref, tq_ref, tk_ref, oq_ref, ok_ref, ov_ref,
                 h_ref, rq_ref, rk_ref, rv_ref, rg_ref, fq_ref, fk_ref, fv_ref, carry_ref, *, meta):
    tm = x_ref.shape[0]
    if not meta:
        @pl.when(pl.program_id(1) == 0)
        def _():
            carry_ref[...] = jnp.zeros_like(carry_ref)

    h = _ln(x_ref[...], g_ref[...], b_ref[...])
    if meta:
        valid = lax.broadcasted_iota(jnp.int32, (tm, 1), 0) >= PAD
        h = jnp.where(valid, h, 0.0)
    h_ref[...] = h
    hb = h.astype(_BF16)
    cos = cos_ref[...]
    sin = sin_ref[...]

    def proj(g):
        return _dot(hb, w_ref[:, g * HEAD_W:(g + 1) * HEAD_W])

    def rope_store(p, out_ref, scale):
        for hd in range(RET_HEADS):
            t = p[:, hd * LANES:(hd + 1) * LANES]
            r = t * cos + pltpu.roll(t, LANES // 2, axis=1) * sin
            out_ref[:, hd * LANES:(hd + 1) * LANES] = (r * scale).astype(_BF16)

    z = _dot(hb, w_ref[:, 7 * HEAD_W:7 * HEAD_W + LANES]) + bf_ref[...]
    rope_store(proj(0), rq_ref, 1.0)
    logf = jnp.minimum(z, 0.0) - jnp.log1p(jnp.exp(-jnp.abs(z)))
    if meta:
        logf = jnp.where(valid, logf, 0.0)
    l1 = logf.astype(_BF16)
    r1 = logf - l1.astype(_F32)
    l2 = r1.astype(_BF16)
    l3 = (r1 - l2.astype(_F32)).astype(_BF16)
    row = lax.broadcasted_iota(jnp.int32, (tm, tm), 0)
    col = lax.broadcasted_iota(jnp.int32, (tm, tm), 1)
    tri = (col <= row).astype(_BF16)
    rope_store(proj(1), rk_ref, RET_DK ** -0.5)
    c = _dot(tri, l1) + _dot(tri, l2) + _dot(tri, l3)
    rv_ref[...] = proj(2).astype(_BF16)
    rg_ref[...] = _silu(proj(3)).astype(_BF16)
    if meta:
        c = c - c[tm - 1:tm, :]
    else:
        c = c + carry_ref[...]
        carry_ref[...] = c[tm - 1:tm, :]

    head_lane = lax.broadcasted_iota(jnp.int32, (1, LANES), 1) < FOX_HEADS
    cl = jnp.where(head_lane, c, 0.0) * LOG2E
    c1 = cl.astype(_BF16).astype(_F32)
    r1 = cl - c1
    c2 = r1.astype(_BF16).astype(_F32)
    c3 = (r1 - c2).astype(_BF16).astype(_F32)
    csplit = c1 + pltpu.roll(c2, FOX_HEADS, axis=1) + pltpu.roll(c3, 2 * FOX_HEADS, axis=1)
    half = FOX_HD
    q_even, q_odd = pltpu.roll(csplit, half, axis=1), csplit
    k_even, k_odd = pltpu.roll(csplit, half + FOX_K_TERMS, axis=1), pltpu.roll(csplit, FOX_K_TERMS, axis=1)
    own = own_ref[...] > 0.0

    def per_head(p):
        return jnp.concatenate([p[:, (hd // 2) * LANES:(hd // 2 + 1) * LANES] for hd in range(FOX_HEADS)], axis=1)

    def by_parity(even, odd):
        return jnp.concatenate([even, odd] * (FOX_HEADS // 2), axis=1)

    q_extra = jnp.where(tq_ref[...] > 0.0, by_parity(q_even, q_odd), oq_ref[...])
    k_extra = jnp.where(tk_ref[...] > 0.0, -by_parity(k_even, k_odd), ok_ref[...])
    fq_ref[...] = jnp.where(own, per_head(proj(4) * (FOX_HD ** -0.5 * LOG2E)), q_extra).astype(_BF16)
    fk_ref[...] = jnp.where(own, per_head(proj(5)), k_extra).astype(_BF16)
    fv_ref[...] = jnp.where(own, per_head(proj(6)), ov_ref[...]).astype(_BF16)


def _fox_lane_tables():
    own, tq, tk, oq, ok, ov = (np.zeros((1, FOX_W), np.float32) for _ in range(6))
    for hd in range(FOX_HEADS):
        data = hd * LANES + (hd % 2) * FOX_HD
        extra = hd * LANES + (1 - hd % 2) * FOX_HD
        own[0, data:data + FOX_HD] = 1.0
        for term in range(3):
            lane = extra + term * FOX_HEADS + hd
            tq[0, lane] = 1.0
            ok[0, lane] = 1.0
            tk[0, lane + FOX_K_TERMS] = 1.0
            oq[0, lane + FOX_K_TERMS] = 1.0
        ov[0, extra] = 1.0
    return tuple(jnp.asarray(t) for t in (own, tq, tk, oq, ok, ov))


def _inproj(x2d, ln_g, ln_b, w_all, bf_pad, cos_t, sin_t, *, nb, meta):
    n = x2d.shape[0]
    s = n // nb
    tm = min(TM_PROJ, s)
    nj = s // tm
    row_spec = lambda w: pl.BlockSpec((tm, w), lambda b, j: (b * nj + j, 0))
    const = lambda shape: pl.BlockSpec(shape, lambda b, j: (0, 0))
    pos_spec = pl.BlockSpec((tm, LANES), lambda b, j: (j, 0))
    tables = _fox_lane_tables()
    outs = ([jax.ShapeDtypeStruct((n, D_MODEL), _F32)] + [jax.ShapeDtypeStruct((n, HEAD_W), _BF16)] * 4
            + [jax.ShapeDtypeStruct((n, FOX_W), _BF16)] * 3)
    return pl.pallas_call(
        functools.partial(_inproj_body, meta=meta),
        grid=(nb, nj),
        in_specs=[row_spec(D_MODEL), const((1, D_MODEL)), const((1, D_MODEL)), const(w_all.shape),
                  const((1, LANES)), pos_spec, pos_spec] + [const(t.shape) for t in tables],
        out_specs=[row_spec(D_MODEL)] + [row_spec(HEAD_W)] * 4 + [row_spec(FOX_W)] * 3,
        out_shape=outs,
        scratch_shapes=[pltpu.VMEM((1, LANES), _F32)],
        compiler_params=_params(2),
        name="inproj_meta" if meta else "inproj",
    )(x2d, ln_g, ln_b, w_all, bf_pad, cos_t, sin_t, *tables)


def _ret_body(q_ref, k_ref, v_ref, g_ref, km_ref, vm_ref, dm_ref, xi_ref, zeta_ref, gch_ref, o_ref, st_ref):
    def kv_update(k, v, hd):
        vz = (v.astype(_F32) * zeta_ref[hd]).astype(_BF16)
        return _dot_tn(k, vz)

    nb = q_ref.shape[0]

    @pl.when(pl.program_id(0) == 0)
    def _():
        for hd in range(RET_HEADS):
            sl = slice(hd * LANES, (hd + 1) * LANES)
            first = kv_update(km_ref[:, sl], vm_ref[:, sl], hd)
            for b in range(nb):
                st_ref[b, hd] = first

    group = 4
    for b0 in range(0, nb, group):
        chains = [(b, hd, slice(hd * LANES, (hd + 1) * LANES)) for b in range(b0, b0 + group) for hd in range(RET_HEADS)]
        scores = [_dot_nt(q_ref[b, :, sl], k_ref[b, :, sl]) * dm_ref[hd] for b, hd, sl in chains]
        cross = [_dot(q_ref[b, :, sl], st_ref[b, hd].astype(_BF16)) * xi_ref[hd] for b, hd, sl in chains]
        outs = [_dot(s.astype(_BF16), v_ref[b, :, sl]) + c for s, c, (b, hd, sl) in zip(scores, cross, chains)]
        for o, (b, hd, sl) in zip(outs, chains):
            oc = o - jnp.mean(o, -1, keepdims=True)
            y = oc * lax.rsqrt(jnp.mean(oc * oc, -1, keepdims=True) + LN_EPS)
            o_ref[b, :, sl] = (y * g_ref[b, :, sl].astype(_F32)).astype(_BF16)
        for b, hd, sl in chains:
            st_ref[b, hd] = gch_ref[hd] * st_ref[b, hd] + kv_update(k_ref[b, :, sl], v_ref[b, :, sl], hd)


def _retention(rq, rk, rv, rg, rk_m, rv_m, dmask, xi, zeta, gch):
    nb, s, _ = rq.shape
    row_spec = pl.BlockSpec((nb, BLOCK, HEAD_W), lambda j: (0, j, 0))
    meta_spec = pl.BlockSpec((BLOCK, HEAD_W), lambda j: (0, 0))
    tab = pl.BlockSpec((RET_HEADS, BLOCK, BLOCK), lambda j: (0, 0, 0))
    return pl.pallas_call(
        _ret_body,
        grid=(s // BLOCK,),
        in_specs=[row_spec] * 4 + [meta_spec] * 2 + [tab] * 3 + [pl.BlockSpec(memory_space=pltpu.SMEM)],
        out_specs=row_spec,
        out_shape=jax.ShapeDtypeStruct((nb, s, HEAD_W), _BF16),
        scratch_shapes=[pltpu.VMEM((nb, RET_HEADS, RET_DK, LANES), _F32)],
        compiler_params=_params(1),
        name="retention",
    )(rq, rk, rv, rg, rk_m, rv_m, dmask, xi, zeta, gch)


def _fox_body(q_ref, k_ref, v_ref, km_ref, vm_ref, o_ref, m_sc, acc_sc, s_sc, p_sc):
    t = q_ref.shape[0]
    qi = pl.program_id(2)

    heads = [slice(hh * LANES, (hh + 1) * LANES) for hh in range(2)]

    def logits(hh, k, buf, tk):
        s_sc[buf, hh, :, :tk] = _dot_nt(q_ref[:, heads[hh]], k)

    def update(hh, v, buf, tk, mask, first):
        for c in range(t // FOX_CHUNK):
            rows = slice(c * FOX_CHUNK, (c + 1) * FOX_CHUNK)
            s = s_sc[buf, hh, rows, :tk]
            if mask is not None:
                s = mask(s, c)
            mx = jnp.max(s, axis=1, keepdims=True)
            if first:
                m_new = mx
            else:
                m_prev = m_sc[hh, rows, :]
                m_new = jnp.maximum(m_prev, mx)
                acc_sc[hh, rows, :] = jnp.exp2(m_prev - m_new) * acc_sc[hh, rows, :]
            p_sc[hh, rows, :tk] = jnp.exp2(s - m_new).astype(_BF16)
            m_sc[hh, rows, :] = m_new
        pv = _dot(p_sc[hh, :, :tk], v)
        if first:
            acc_sc[hh] = pv
        else:
            acc_sc[hh] += pv

    def meta_mask(s, c):
        key = lax.broadcasted_iota(jnp.int32, s.shape, 1)
        return jnp.where(key >= PAD, s, NEG_INF)

    def causal_mask(s, c):
        key = lax.broadcasted_iota(jnp.int32, s.shape, 1)
        query = lax.broadcasted_iota(jnp.int32, s.shape, 0) + c * FOX_CHUNK
        return jnp.where(key <= query, s, NEG_INF)

    def key_tile(ref, ki, hh):
        return ref[pl.ds(pl.multiple_of(ki * t, t), t), heads[hh]]

    for hh in range(2):
        logits(hh, km_ref[:, heads[hh]], 1, BLOCK)
    for hh in range(2):
        logits(hh, key_tile(k_ref, 0, hh), 0, t)
        update(hh, vm_ref[:, heads[hh]], 1, BLOCK, meta_mask, True)

    def step(ki, buf, mask, more):
        for hh in range(2):
            if more:
                logits(hh, key_tile(k_ref, ki + 1, hh), 1 - buf, t)
            update(hh, key_tile(v_ref, ki, hh), buf, t, mask, False)

    def pair_body(j, carry):
        step(2 * j, 0, None, True)
        step(2 * j + 1, 1, None, True)
        return carry

    lax.fori_loop(0, lax.shift_right_logical(qi, 1), pair_body, 0)
    odd = lax.rem(qi, 2) == 1

    @pl.when(odd)
    def _():
        step(qi - 1, 0, None, True)
        step(qi, 1, causal_mask, False)

    @pl.when(jnp.logical_not(odd))
    def _():
        step(qi, 0, causal_mask, False)

    outs = []
    for hh in range(2):
        acc = acc_sc[hh]
        ones_lane = (1 - hh) * FOX_HD
        outs.append(acc / acc[:, ones_lane:ones_lane + 1])
    lane = lax.broadcasted_iota(jnp.int32, (t, LANES), 1)
    o_ref[...] = jnp.where(lane < FOX_HD, outs[0], outs[1]).astype(_BF16)


def _fox(fq, fk, fv, fk_m, fv_m, *, nb):
    s = fq.shape[1]
    t = T_FOX
    pair_w = 2 * LANES
    return pl.pallas_call(
        _fox_body,
        grid=(nb, FOX_HEADS // 2, s // t),
        in_specs=[
            pl.BlockSpec((None, t, pair_w), lambda b, p, i: (b, i, p)),
            pl.BlockSpec((None, s, pair_w), lambda b, p, i: (b, 0, p)),
            pl.BlockSpec((None, s, pair_w), lambda b, p, i: (b, 0, p)),
            pl.BlockSpec((BLOCK, pair_w), lambda b, p, i: (0, p)),
            pl.BlockSpec((BLOCK, pair_w), lambda b, p, i: (0, p)),
        ],
        out_specs=pl.BlockSpec((None, t, LANES), lambda b, p, i: (b, i, p)),
        out_shape=jax.ShapeDtypeStruct((nb, s, HEAD_W), _BF16),
        scratch_shapes=[pltpu.VMEM((2, t, 1), _F32), pltpu.VMEM((2, t, LANES), _F32), pltpu.VMEM((2, 2, t, t), _F32),
                        pltpu.VMEM((2, t, t), _BF16)],
        compiler_params=_params(3),
        name="fox",
    )(fq, fk, fv, fk_m, fv_m)


def _route(scores, biased):
    w = scores.shape[1]
    sub = lax.broadcasted_iota(jnp.int32, (GROUP_SIZE, w), 0).astype(_F32)
    groups = [biased[g * GROUP_SIZE:(g + 1) * GROUP_SIZE, :] for g in range(N_GROUPS)]
    gscore = []
    for v in groups:
        m1 = jnp.max(v, axis=0, keepdims=True)
        i1 = jnp.min(jnp.where(v == m1, sub, float(GROUP_SIZE)), axis=0, keepdims=True)
        m2 = jnp.max(jnp.where(sub == i1, -jnp.inf, v), axis=0, keepdims=True)
        gscore.append(m1 + m2)
    masked = []
    for g in range(N_GROUPS):
        beaten = jnp.zeros((1, w), _F32)
        for o in range(N_GROUPS):
            if o == g:
                continue
            wins = (gscore[o] >= gscore[g]) if o < g else (gscore[o] > gscore[g])
            beaten = beaten + wins.astype(_F32)
        masked.append(jnp.where(beaten < float(TOPK_GROUPS), groups[g], NEG_INF))
    work = jnp.concatenate(masked, axis=0)
    eid = lax.broadcasted_iota(jnp.int32, (N_EXPERTS, w), 0).astype(_F32)
    sels, raws = [], []
    member = None
    for _ in range(TOP_K):
        m = jnp.max(work, axis=0, keepdims=True)
        idx = jnp.min(jnp.where(work == m, eid, float(N_EXPERTS)), axis=0, keepdims=True)
        hot = eid == idx
        sels.append(idx)
        raws.append(jnp.sum(jnp.where(hot, scores, 0.0), axis=0, keepdims=True))
        work = jnp.where(hot, -jnp.inf, work)
        member = hot if member is None else (member | hot)
    return sels, raws, member


def _post_body(h0_ref, ret_ref, fox_ref, wo_ref, g1_ref, b1_ref, wr_ref, rb_ref,
               trow_ref, sel_ref, gate_ref, rank_ref, cnt_ref, run_ref):
    tm = h0_ref.shape[0]
    i = pl.program_id(0)

    @pl.when(i == 0)
    def _():
        run_ref[...] = jnp.zeros_like(run_ref)

    halves = [slice(hf * tm // 2, (hf + 1) * tm // 2) for hf in range(2)]
    ys = [_dot(ret_ref[hs, :], wo_ref[:HEAD_W, :]) + _dot(fox_ref[hs, :], wo_ref[HEAD_W:, :]) for hs in halves]
    h1s = [_ln(ALPHA * h0_ref[hs, :] + y, g1_ref[...], b1_ref[...]) for hs, y in zip(halves, ys)]
    for hs, h1 in zip(halves, h1s):
        _store_rows(trow_ref, h1, hs.start)

    scores = jnp.concatenate([jax.nn.sigmoid(_dot_nt(wr_ref[...], h1.astype(_BF16))) for h1 in h1s], axis=1)
    biased = scores + rb_ref[...]
    chunks = [slice(c * LANES, (c + 1) * LANES) for c in range(tm // LANES)]
    routed = [_route(scores[:, cs], biased[:, cs]) for cs in chunks]
    member_f = jnp.concatenate([member.astype(_F32) for _, _, member in routed], axis=1)
    row = lax.broadcasted_iota(jnp.int32, (tm, tm), 0)
    col = lax.broadcasted_iota(jnp.int32, (tm, tm), 1)
    before = (row < col).astype(_BF16)
    rank_e = _dot(member_f.astype(_BF16), before) + run_ref[...]
    eid = lax.broadcasted_iota(jnp.int32, (N_EXPERTS, LANES), 0).astype(_F32)
    sel_rows, gate_rows, rank_rows = [], [], []
    for k in range(TOP_K):
        sel_k, gate_k, rank_k = [], [], []
        for cs, (sels, raws, _) in zip(chunks, routed):
            total = raws[0]
            for r in raws[1:]:
                total = total + r
            sel_k.append(sels[k].astype(jnp.int32))
            gate_k.append(raws[k] * (ROUTED_SCALE / total))
            rank_k.append(jnp.sum(jnp.where(eid == sels[k], rank_e[:, cs], 0.0), axis=0, keepdims=True).astype(jnp.int32))
        sel_rows.append(jnp.concatenate(sel_k, axis=1))
        gate_rows.append(jnp.concatenate(gate_k, axis=1))
        rank_rows.append(jnp.concatenate(rank_k, axis=1))
    pad_rows = sel_ref.shape[0] - TOP_K
    sel_ref[...] = jnp.concatenate(sel_rows + [jnp.zeros((pad_rows, tm), jnp.int32)], axis=0)
    gate_ref[...] = jnp.concatenate(gate_rows + [jnp.zeros((pad_rows, tm), _F32)], axis=0)
    rank_ref[...] = jnp.concatenate(rank_rows + [jnp.zeros((pad_rows, tm), jnp.int32)], axis=0)
    run_ref[...] = run_ref[...] + jnp.sum(member_f, axis=1, keepdims=True)
    cnt_ref[...] = run_ref[...]


def _post(h0, ret, fox, w_out, ln1_g, ln1_b, w_rt, rbias):
    n = h0.shape[0]
    tm = TM_PROJ
    row_spec = lambda w: pl.BlockSpec((tm, w), lambda i: (i, 0))
    col_spec = pl.BlockSpec((SUBLANES, tm), lambda i: (0, i))
    const = lambda shape: pl.BlockSpec(shape, lambda i: (0, 0))
    return pl.pallas_call(
        _post_body,
        grid=(n // tm,),
        in_specs=[row_spec(D_MODEL), row_spec(HEAD_W), row_spec(HEAD_W),
                  const(w_out.shape), const((1, D_MODEL)), const((1, D_MODEL)), const(w_rt.shape),
                  const((N_EXPERTS, 1))],
        out_specs=[pl.BlockSpec((tm * ROW_TILES, LANES), lambda i: (i, 0)),
                   col_spec, col_spec, col_spec, const((N_EXPERTS, 1))],
        out_shape=[jax.ShapeDtypeStruct((n * ROW_TILES, LANES), ROW_DTYPE),
                   jax.ShapeDtypeStruct((SUBLANES, n), jnp.int32), jax.ShapeDtypeStruct((SUBLANES, n), _F32),
                   jax.ShapeDtypeStruct((SUBLANES, n), jnp.int32), jax.ShapeDtypeStruct((N_EXPERTS, 1), _F32)],
        scratch_shapes=[pltpu.VMEM((N_EXPERTS, 1), _F32)],
        compiler_params=_params(1),
        name="post_mixer",
    )(h0, ret, fox, w_out, ln1_g, ln1_b, w_rt, rbias)


def _plan_body(sel_ref, rank_ref, cnt_ref, dest_ref, blk_ref, fill_ref, used_ref, tab_ref):
    cnt = cnt_ref[...]
    padded = jnp.ceil(cnt * (1.0 / EBLK)) * EBLK
    er = lax.broadcasted_iota(jnp.int32, (N_EXPERTS, N_EXPERTS), 0)
    ec = lax.broadcasted_iota(jnp.int32, (N_EXPERTS, N_EXPERTS), 1)
    padded_row = jnp.sum(jnp.where(er == ec, padded, 0.0), axis=0, keepdims=True)
    pstart = jnp.sum(jnp.where(ec < er, padded_row, 0.0), axis=1, keepdims=True)
    pend = pstart + padded
    sel = sel_ref[...]
    dest = rank_ref[...]
    for e in range(N_EXPERTS):
        dest = dest + jnp.where(sel == e, pstart[e:e + 1, :].astype(jnp.int32), 0)
    dest_ref[...] = dest
    nblk = blk_ref.shape[1]
    first_row = (lax.broadcasted_iota(jnp.int32, (N_EXPERTS, nblk), 1) * EBLK).astype(_F32)
    owner = jnp.minimum(jnp.sum((pend <= first_row).astype(_F32), axis=0, keepdims=True), N_EXPERTS - 1.0)
    blk_ref[...] = owner.astype(jnp.int32)
    mine = lax.broadcasted_iota(jnp.int32, (N_EXPERTS, nblk), 0).astype(_F32) == owner
    live_end = jnp.sum(jnp.where(mine, pstart + cnt, 0.0), axis=0, keepdims=True)
    fill_ref[...] = jnp.clip(live_end - first_row[:1, :], 0.0, float(EBLK)).astype(jnp.int32)
    used_ref[...] = (pend[N_EXPERTS - 1:, :] * (1.0 / EBLK)).astype(jnp.int32)
    as_row = lambda col: jnp.sum(jnp.where(er == ec, col, 0.0), axis=0, keepdims=True).astype(jnp.int32)
    rows = {TAB_PAD0: pstart + cnt, TAB_PADN: padded - cnt}
    blank = jnp.zeros((1, N_EXPERTS), jnp.int32)
    tab_ref[...] = jnp.concatenate([as_row(rows[r]) if r in rows else blank for r in range(SUBLANES)], axis=0)


def _plan(sel, rank, cnt, nblk_pad):
    n = sel.shape[1]
    full = lambda shape: pl.BlockSpec(shape, lambda i: (0, 0))
    return pl.pallas_call(
        _plan_body,
        grid=(1,),
        in_specs=[full(sel.shape), full(rank.shape), full(cnt.shape)],
        out_specs=[full(sel.shape), full((1, nblk_pad)), full((1, nblk_pad)), full((1, 1)),
                   full((SUBLANES, N_EXPERTS))],
        out_shape=[jax.ShapeDtypeStruct((SUBLANES, n), jnp.int32), jax.ShapeDtypeStruct((1, nblk_pad), jnp.int32),
                   jax.ShapeDtypeStruct((1, nblk_pad), jnp.int32), jax.ShapeDtypeStruct((1, 1), jnp.int32),
                   jax.ShapeDtypeStruct((SUBLANES, N_EXPERTS), jnp.int32)],
        compiler_params=_params(1),
        name="plan",
    )(sel, rank, cnt)


def _store_rows(ref, v, first_row=0):
    m = v.shape[0]
    for s in range(ROW_TILES):
        ref[pl.ds(first_row * ROW_TILES + s, m, stride=ROW_TILES), :] = v[:, s * LANES:(s + 1) * LANES].astype(ROW_DTYPE)


def _load_rows(ref, first_row, m):
    return jnp.concatenate([ref[pl.ds(first_row * ROW_TILES + s, m, stride=ROW_TILES), :] for s in range(ROW_TILES)],
                           axis=1)


def _row_copy(src, src_row, dst, dst_row, sem):
    return pltpu.make_async_copy(src.at[pl.ds(pl.multiple_of(src_row * ROW_TILES, ROW_TILES), ROW_TILES), :],
                                 dst.at[pl.ds(pl.multiple_of(dst_row * ROW_TILES, ROW_TILES), ROW_TILES), :], sem)


def _dispatch_body(dest_ref, tab_ref, used_ref, t_ref, wgu_ref, wd_ref, xs_ref, base_ref, zero_sc, sem, zsem):
    tt = t_ref.shape[0] // ROW_TILES
    half_blk = EBLK // 2
    n_half = xs_ref.shape[0] // (half_blk * ROW_TILES)

    def zero_copy(first_row, rows):
        return pltpu.make_async_copy(
            zero_sc.at[pl.ds(0, rows * ROW_TILES), :],
            xs_ref.at[pl.ds(pl.multiple_of(first_row * ROW_TILES, ROW_TILES), rows * ROW_TILES), :], zsem)

    def for_padding(act):
        def per_expert(e, carry):
            row = tab_ref[TAB_PAD0, e]
            for bit in range(EBLK.bit_length() - 2, -1, -1):
                take = (tab_ref[TAB_PADN, e] & (1 << bit)) != 0

                @pl.when(take)
                def _():
                    act(zero_copy(row, 1 << bit))

                row = row + jnp.where(take, 1 << bit, 0)
            return carry

        def per_tail(hb, carry):
            act(zero_copy(hb * half_blk, half_blk))
            return carry

        lax.fori_loop(0, N_EXPERTS, per_expert, 0)
        lax.fori_loop(2 * used_ref[0], n_half, per_tail, 0)

    @pl.when(pl.program_id(0) == 0)
    def _():
        zero_sc[...] = jnp.zeros_like(zero_sc)
        for_padding(lambda cp: cp.start())

    def issue(j, carry):
        for u in range(ISSUE_UNROLL):
            i = ISSUE_UNROLL * j + u
            for k in range(TOP_K):
                _row_copy(t_ref, i, xs_ref, dest_ref[i * SUBLANES + k], sem).start(priority=k % 2)
        return carry

    lax.fori_loop(0, tt // ISSUE_UNROLL, issue, 0)

    h1 = _load_rows(t_ref, 0, tt)
    gu = _dot(h1.astype(_BF16), wgu_ref[...])
    mid = (_silu(gu[:, :SHARED_FF]) * gu[:, SHARED_FF:]).astype(_BF16)
    base_ref[...] = ALPHA * h1 + _dot(mid, wd_ref[...])

    for _ in range(TOP_K):
        pltpu.make_async_copy(t_ref, xs_ref.at[pl.ds(0, tt * ROW_TILES), :], sem).wait()

    @pl.when(pl.program_id(0) == pl.num_programs(0) - 1)
    def _():
        for_padding(lambda cp: cp.wait())


def _dispatch(dest_flat, tab, used, trow, w_gu, w_sd, total_rows):
    n = trow.shape[0] // ROW_TILES
    tt = TT_DISPATCH
    smem = pl.BlockSpec(memory_space=pltpu.SMEM)
    const = lambda shape: pl.BlockSpec(shape, lambda i: (0, 0))
    return pl.pallas_call(
        _dispatch_body,
        grid=(n // tt,),
        in_specs=[pl.BlockSpec((tt * SUBLANES,), lambda i: (i,), memory_space=pltpu.SMEM), smem, smem,
                  pl.BlockSpec((tt * ROW_TILES, LANES), lambda i: (i, 0)), const(w_gu.shape), const(w_sd.shape)],
        out_specs=[pl.BlockSpec(memory_space=pl.ANY), pl.BlockSpec((tt, D_MODEL), lambda i: (i, 0))],
        out_shape=[jax.ShapeDtypeStruct((total_rows * ROW_TILES, LANES), ROW_DTYPE),
                   jax.ShapeDtypeStruct((n, D_MODEL), _F32)],
        scratch_shapes=[pltpu.VMEM((EBLK // 2 * ROW_TILES, LANES), ROW_DTYPE), pltpu.SemaphoreType.DMA(()),
                        pltpu.SemaphoreType.DMA(())],
        compiler_params=_params(1),
        name="dispatch",
    )(dest_flat, tab, used, trow, w_gu, w_sd)


def _expert_body(blk_ref, used_ref, fill_ref, xs_ref, wg_ref, wu_ref, wd_ref, y_ref, wgu_sc, wd_sc):
    i = pl.program_id(0)
    prev = blk_ref[jnp.maximum(i - 1, 0)]
    fresh = (i == 0) | (blk_ref[i] != prev)

    @pl.when(fresh)
    def _():
        wgu_sc[:, :EXPERT_FF] = wg_ref[...].astype(_BF16)
        wgu_sc[:, EXPERT_FF:] = wu_ref[...].astype(_BF16)
        wd_sc[...] = wd_ref[...].astype(_BF16)

    @pl.when(i < used_ref[0])
    def _():
        live = lax.broadcasted_iota(jnp.int32, (EBLK, 1), 0) < fill_ref[i]
        x = _load_rows(xs_ref, 0, EBLK)
        x = jnp.where(live, x, jnp.zeros_like(x)).astype(_BF16)
        gu = _dot(x, wgu_sc[...])
        mid = (_silu(gu[:, :EXPERT_FF]) * gu[:, EXPERT_FF:]).astype(_BF16)
        _store_rows(y_ref, _dot(mid, wd_sc[...]))

    @pl.when(i >= used_ref[0])
    def _():
        y_ref[...] = jnp.zeros_like(y_ref)


def _experts(blk_e, used, fill, xs, we_gate, we_up, we_down):
    nblk = xs.shape[0] // (EBLK * ROW_TILES)
    last = lambda i, used: jnp.minimum(i, jnp.maximum(used[0] - 1, 0))
    w_spec = lambda shape: pl.BlockSpec((None,) + shape, lambda i, blk, used, fill: (blk[i], 0, 0))
    return pl.pallas_call(
        _expert_body,
        grid_spec=pltpu.PrefetchScalarGridSpec(
            num_scalar_prefetch=3,
            grid=(nblk,),
            in_specs=[pl.BlockSpec((EBLK * ROW_TILES, LANES), lambda i, blk, used, fill: (last(i, used), 0)),
                      w_spec((D_MODEL, EXPERT_FF)), w_spec((D_MODEL, EXPERT_FF)), w_spec((EXPERT_FF, D_MODEL))],
            out_specs=pl.BlockSpec((EBLK * ROW_TILES, LANES), lambda i, blk, used, fill: (i, 0)),
            scratch_shapes=[pltpu.VMEM((D_MODEL, 2 * EXPERT_FF), _BF16), pltpu.VMEM((EXPERT_FF, D_MODEL), _BF16)],
        ),
        out_shape=jax.ShapeDtypeStruct(xs.shape, ROW_DTYPE),
        compiler_params=_params(1),
        name="experts",
    )(blk_e, used, fill, xs, we_gate, we_up, we_down)


def _combine_body(dest_ref, dnext_ref, y_ref, base_ref, gate_ref, g2_ref, b2_ref, o_ref, z_sc, sems):
    i = pl.program_id(0)
    slot = lax.rem(i, 2)
    tile_rows = TT * TOP_K * ROW_TILES

    def gather(dref, into):
        def issue(j, carry):
            for u in range(ISSUE_UNROLL):
                t = ISSUE_UNROLL * j + u
                for k in range(TOP_K):
                    _row_copy(y_ref, dref[t * SUBLANES + k], z_sc.at[into], k * TT + t,
                              sems.at[into]).start(priority=k % 2)
            return carry

        lax.fori_loop(0, TT // ISSUE_UNROLL, issue, 0)

    @pl.when(i == 0)
    def _():
        gather(dest_ref, 0)

    @pl.when(i + 1 < pl.num_programs(0))
    def _():
        gather(dnext_ref, 1 - slot)

    pltpu.make_async_copy(y_ref.at[pl.ds(0, tile_rows), :], z_sc.at[slot], sems.at[slot]).wait()
    gates = gate_ref[...]
    acc = base_ref[...]
    for k in range(TOP_K):
        acc = acc + gates[:, k:k + 1] * _load_rows(z_sc.at[slot], k * TT, TT).astype(_F32)
    o_ref[...] = _ln(acc, g2_ref[...], b2_ref[...])


def _combine(dest_flat, y, base, gates_t, ln2_g, ln2_b):
    n = base.shape[0]
    steps = n // TT
    const = lambda shape: pl.BlockSpec(shape, lambda i: (0, 0))
    dest_spec = lambda ahead: pl.BlockSpec((TT * SUBLANES,), lambda i: (jnp.minimum(i + ahead, steps - 1),),
                                           memory_space=pltpu.SMEM)
    return pl.pallas_call(
        _combine_body,
        grid=(steps,),
        in_specs=[dest_spec(0), dest_spec(1),
                  pl.BlockSpec(memory_space=pl.ANY),
                  pl.BlockSpec((TT, D_MODEL), lambda i: (i, 0)),
                  pl.BlockSpec((TT, SUBLANES), lambda i: (i, 0)),
                  const((1, D_MODEL)), const((1, D_MODEL))],
        out_specs=pl.BlockSpec((TT, D_MODEL), lambda i: (i, 0)),
        out_shape=jax.ShapeDtypeStruct((n, D_MODEL), _F32),
        scratch_shapes=[pltpu.VMEM((2, TT * TOP_K * ROW_TILES, LANES), ROW_DTYPE), pltpu.SemaphoreType.DMA((2,))],
        compiler_params=_params(1),
        name="combine",
    )(dest_flat, dest_flat, y, base, gates_t, ln2_g, ln2_b)


def _rope_tables(pos):
    half = RET_DK // 2
    inv = ROPE_BASE ** (-jnp.arange(half, dtype=_F32) / half)
    ang = pos[:, None] * inv[None, :]
    cos = jnp.cos(ang)
    sin = jnp.sin(ang)
    return jnp.concatenate([cos, cos], -1), jnp.concatenate([-sin, sin], -1)


def _decay_tables():
    lg = jnp.log1p(-jnp.exp2(-5.0 - jnp.arange(RET_HEADS, dtype=_F32)))
    idx = jnp.arange(BLOCK, dtype=_F32)
    rel = idx[:, None] - idx[None, :]
    causal = rel >= 0
    dmask = jnp.where(causal[None], jnp.exp(jnp.where(causal, rel, 0.0)[None] * lg[:, None, None]), 0.0)
    zeta = jnp.exp((BLOCK - 1.0 - idx)[None, :] * lg[:, None])
    xi = jnp.exp((idx + 1.0)[None, :] * lg[:, None])
    along_lanes = lambda col: jnp.broadcast_to(col[:, :, None], (RET_HEADS, BLOCK, LANES))
    return dmask, along_lanes(xi), along_lanes(zeta), jnp.exp(BLOCK * lg)


def kernel(x, meta, ln0_g, ln0_b, w_in, b_forget, w_out, ln1_g, ln1_b, w_router, router_bias, we_gate, we_up,
           we_down, ws_gate, ws_up, ws_down, ln2_g, ln2_b):
    nb, s, d = x.shape
    assert d == D_MODEL and meta.shape == (N_META, D_MODEL) and w_in.shape[0] == 1
    assert s % TM_PROJ == 0 and s % T_FOX == 0 and (nb * s) % TT == 0
    n = nb * s
    x2d = x.reshape(n, d)
    row2 = lambda v: v.reshape(1, -1).astype(_F32)
    main_cols = 7 * HEAD_W
    w_all = jnp.concatenate(
        [w_in[0, :, :main_cols], w_in[0, :, main_cols:], jnp.zeros((d, LANES - FOX_HEADS), w_in.dtype)],
        axis=1).astype(_BF16)
    bf_pad = jnp.concatenate([b_forget[0].astype(_F32), jnp.zeros((LANES - FOX_HEADS,), _F32)]).reshape(1, LANES)
    g0, b0 = row2(ln0_g), row2(ln0_b)

    cos_x, sin_x = _rope_tables(jnp.arange(s, dtype=_F32) + float(N_META))
    cos_m, sin_m = _rope_tables(jnp.arange(BLOCK, dtype=_F32) - float(PAD))
    meta_blk = jnp.concatenate([jnp.zeros((PAD, d), _F32), meta.astype(_F32)], axis=0)

    h0, rq, rk, rv, rg, fq, fk, fv = _inproj(x2d, g0, b0, w_all, bf_pad, cos_x, sin_x, nb=nb, meta=False)
    _, _, rk_m, rv_m, _, _, fk_m, fv_m = _inproj(meta_blk, g0, b0, w_all, bf_pad, cos_m, sin_m, nb=1, meta=True)

    dmask, xi, zeta, gch = _decay_tables()
    per_batch = lambda a: a.reshape(nb, s, a.shape[-1])
    ret = _retention(per_batch(rq), per_batch(rk), per_batch(rv), per_batch(rg), rk_m, rv_m, dmask, xi, zeta,
                     gch).reshape(n, HEAD_W)
    fox = _fox(per_batch(fq), per_batch(fk), per_batch(fv), fk_m, fv_m, nb=nb).reshape(n, HEAD_W)

    w_gu = jnp.concatenate([ws_gate[0], ws_up[0]], axis=1).astype(_BF16)
    trow, sel, gates, rank, cnt = _post(
        h0, ret, fox, w_out[0].astype(_BF16), row2(ln1_g[0]), row2(ln1_b[0]),
        jnp.transpose(w_router[0]).astype(_BF16), router_bias[0].astype(_F32).reshape(N_EXPERTS, 1))

    nblk = n * TOP_K // EBLK + N_EXPERTS
    nblk_pad = -(-nblk // LANES) * LANES
    dest, blk_e, fill, used, tab = _plan(sel, rank, cnt, nblk_pad)
    dest_flat = jnp.transpose(dest).reshape(-1)
    used = used.reshape(-1)

    xs, base = _dispatch(dest_flat, tab, used, trow, w_gu, ws_down[0].astype(_BF16), nblk * EBLK)
    y = _experts(blk_e.reshape(-1), used, fill.reshape(-1), xs, we_gate[0], we_up[0], we_down[0])
    out = _combine(dest_flat, y, base, jnp.transpose(gates), row2(ln2_g[0]), row2(ln2_b[0]))
    return out.reshape(nb, s, d)
```

```python
import functools

import jax
import jax.numpy as jnp
import numpy as np
from jax import lax
from jax.experimental import pallas as pl
from jax.experimental.pallas import tpu as pltpu

D_MODEL = 1024
N_META = 16
BLOCK = 128
PAD = BLOCK - N_META
RET_HEADS = 4
RET_DK = 128
FOX_HEADS = 8
FOX_HD = 64
N_EXPERTS = 64
TOP_K = 6
N_GROUPS = 8
GROUP_SIZE = N_EXPERTS // N_GROUPS
TOPK_GROUPS = 4
EXPERT_FF = 256
SHARED_FF = 256
ROUTED_SCALE = 2.5
ROPE_BASE = 10000.0
LN_EPS = 1e-5
NEG_INF = -1e30
ALPHA = 2.0 ** 0.25
HEAD_W = 512
LOG2E = 1.4426950408889634

LANES = 128
FOX_W = FOX_HEADS * LANES
SUBLANES = 8
ROW_TILES = D_MODEL // LANES
ROW_DTYPE = jnp.float32

TM_PROJ = 512
T_FOX = 512
FOX_CHUNK = 64
FOX_K_TERMS = 32
TT = 256
TT_DISPATCH = 512
EBLK = 1024
ISSUE_UNROLL = 4
TAB_PAD0, TAB_PADN = 0, 1
V7X_VMEM_BYTES = 64 * 1024 * 1024
VMEM_LIMIT = V7X_VMEM_BYTES * 3 // 4

_F32 = jnp.float32
_BF16 = jnp.bfloat16


def _ln(x, g, b):
    xc = x - jnp.mean(x, -1, keepdims=True)
    var = jnp.mean(xc * xc, -1, keepdims=True)
    return xc * lax.rsqrt(var + LN_EPS) * g + b


def _dot(a, b):
    return jnp.dot(a, b, preferred_element_type=_F32)


def _dot_nt(a, b):
    return lax.dot_general(a, b, (((1,), (1,)), ((), ())), preferred_element_type=_F32)


def _dot_tn(a, b):
    return lax.dot_general(a, b, (((0,), (0,)), ((), ())), preferred_element_type=_F32)


def _silu(x):
    return x * jax.nn.sigmoid(x)


def _params(n_axes):
    return pltpu.CompilerParams(dimension_semantics=("arbitrary",) * n_axes, vmem_limit_bytes=VMEM_LIMIT)


def _inproj_body(x_ref, g_ref, b_ref, w_ref, bf_ref, cos_ref, sin_ref, own_ref, tq_ref, tk_ref, oq_ref, ok_ref, ov_ref,
                 h_ref, rq_ref, rk_ref, rv_ref, rg_ref, fq_ref, fk_ref, fv_ref, carry_ref, *, meta):
    tm = x_ref.shape[0]
    if not meta:
        @pl.when(pl.program_id(1) == 0)
        def _():
            carry_ref[...] = jnp.zeros_like(carry_ref)

    h = _ln(x_ref[...], g_ref[...], b_ref[...])
    if meta:
        valid = lax.broadcasted_iota(jnp.int32, (tm, 1), 0) >= PAD
        h = jnp.where(valid, h, 0.0)
    h_ref[...] = h
    hb = h.astype(_BF16)
    cos = cos_ref[...]
    sin = sin_ref[...]

    def proj(g):
        return _dot(hb, w_ref[:, g * HEAD_W:(g + 1) * HEAD_W])

    def rope_store(p, out_ref, scale):
        for hd in range(RET_HEADS):
            t = p[:, hd * LANES:(hd + 1) * LANES]
            r = t * cos + pltpu.roll(t, LANES // 2, axis=1) * sin
            out_ref[:, hd * LANES:(hd + 1) * LANES] = (r * scale).astype(_BF16)

    z = _dot(hb, w_ref[:, 7 * HEAD_W:7 * HEAD_W + LANES]) + bf_ref[...]
    rope_store(proj(0), rq_ref, 1.0)
    logf = jnp.minimum(z, 0.0) - jnp.log1p(jnp.exp(-jnp.abs(z)))
    if meta:
        logf = jnp.where(valid, logf, 0.0)
    l1 = logf.astype(_BF16)
    r1 = logf - l1.astype(_F32)
    l2 = r1.astype(_BF16)
    l3 = (r1 - l2.astype(_F32)).astype(_BF16)
    row = lax.broadcasted_iota(jnp.int32, (tm, tm), 0)
    col = lax.broadcasted_iota(jnp.int32, (tm, tm), 1)
    tri = (col <= row).astype(_BF16)
    rope_store(proj(1), rk_ref, RET_DK ** -0.5)
    c = _dot(tri, l1) + _dot(tri, l2) + _dot(tri, l3)
    rv_ref[...] = proj(2).astype(_BF16)
    rg_ref[...] = _silu(proj(3)).astype(_BF16)
    if meta:
        c = c - c[tm - 1:tm, :]
    else:
        c = c + carry_ref[...]
        carry_ref[...] = c[tm - 1:tm, :]

    head_lane = lax.broadcasted_iota(jnp.int32, (1, LANES), 1) < FOX_HEADS
    cl = jnp.where(head_lane, c, 0.0) * LOG2E
    c1 = cl.astype(_BF16).astype(_F32)
    r1 = cl - c1
    c2 = r1.astype(_BF16).astype(_F32)
    c3 = (r1 - c2).astype(_BF16).astype(_F32)
    csplit = c1 + pltpu.roll(c2, FOX_HEADS, axis=1) + pltpu.roll(c3, 2 * FOX_HEADS, axis=1)
    half = FOX_HD
    q_even, q_odd = pltpu.roll(csplit, half, axis=1), csplit
    k_even, k_odd = pltpu.roll(csplit, half + FOX_K_TERMS, axis=1), pltpu.roll(csplit, FOX_K_TERMS, axis=1)
    own = own_ref[...] > 0.0

    def per_head(p):
        return jnp.concatenate([p[:, (hd // 2) * LANES:(hd // 2 + 1) * LANES] for hd in range(FOX_HEADS)], axis=1)

    def by_parity(even, odd):
        return jnp.concatenate([even, odd] * (FOX_HEADS // 2), axis=1)

    q_extra = jnp.where(tq_ref[...] > 0.0, by_parity(q_even, q_odd), oq_ref[...])
    k_extra = jnp.where(tk_ref[...] > 0.0, -by_parity(k_even, k_odd), ok_ref[...])
    fq_ref[...] = jnp.where(own, per_head(proj(4) * (FOX_HD ** -0.5 * LOG2E)), q_extra).astype(_BF16)
    fk_ref[...] = jnp.where(own, per_head(proj(5)), k_extra).astype(_BF16)
    fv_ref[...] = jnp.where(own, per_head(proj(6)), ov_ref[...]).astype(_BF16)


def _fox_lane_tables():
    own, tq, tk, oq, ok, ov = (np.zeros((1, FOX_W), np.float32) for _ in range(6))
    for hd in range(FOX_HEADS):
        data = hd * LANES + (hd % 2) * FOX_HD
        extra = hd * LANES + (1 - hd % 2) * FOX_HD
        own[0, data:data + FOX_HD] = 1.0
        for term in range(3):
            lane = extra + term * FOX_HEADS + hd
            tq[0, lane] = 1.0
            ok[0, lane] = 1.0
            tk[0, lane + FOX_K_TERMS] = 1.0
            oq[0, lane + FOX_K_TERMS] = 1.0
        ov[0, extra] = 1.0
    return tuple(jnp.asarray(t) for t in (own, tq, tk, oq, ok, ov))


def _inproj(x2d, ln_g, ln_b, w_all, bf_pad, cos_t, sin_t, *, nb, meta):
    n = x2d.shape[0]
    s = n // nb
    tm = min(TM_PROJ, s)
    nj = s // tm
    row_spec = lambda w: pl.BlockSpec((tm, w), lambda b, j: (b * nj + j, 0))
    const = lambda shape: pl.BlockSpec(shape, lambda b, j: (0, 0))
    pos_spec = pl.BlockSpec((tm, LANES), lambda b, j: (j, 0))
    tables = _fox_lane_tables()
    outs = ([jax.ShapeDtypeStruct((n, D_MODEL), _F32)] + [jax.ShapeDtypeStruct((n, HEAD_W), _BF16)] * 4
            + [jax.ShapeDtypeStruct((n, FOX_W), _BF16)] * 3)
    return pl.pallas_call(
        functools.partial(_inproj_body, meta=meta),
        grid=(nb, nj),
        in_specs=[row_spec(D_MODEL), const((1, D_MODEL)), const((1, D_MODEL)), const(w_all.shape),
                  const((1, LANES)), pos_spec, pos_spec] + [const(t.shape) for t in tables],
        out_specs=[row_spec(D_MODEL)] + [row_spec(HEAD_W)] * 4 + [row_spec(FOX_W)] * 3,
        out_shape=outs,
        scratch_shapes=[pltpu.VMEM((1, LANES), _F32)],
        compiler_params=_params(2),
        name="inproj_meta" if meta else "inproj",
    )(x2d, ln_g, ln_b, w_all, bf_pad, cos_t, sin_t, *tables)


def _ret_body(q_ref, k_ref, v_ref, g_ref, km_ref, vm_ref, dm_ref, xi_ref, zeta_ref, gch_ref, o_ref, st_ref):
    def kv_update(k, v, hd):
        vz = (v.astype(_F32) * zeta_ref[hd]).astype(_BF16)
        return _dot_tn(k, vz)

    nb = q_ref.shape[0]

    @pl.when(pl.program_id(0) == 0)
    def _():
        for hd in range(RET_HEADS):
            sl = slice(hd * LANES, (hd + 1) * LANES)
            first = kv_update(km_ref[:, sl], vm_ref[:, sl], hd)
            for b in range(nb):
                st_ref[b, hd] = first

    group = 4
    for b0 in range(0, nb, group):
        chains = [(b, hd, slice(hd * LANES, (hd + 1) * LANES)) for b in range(b0, b0 + group) for hd in range(RET_HEADS)]
        scores = [_dot_nt(q_ref[b, :, sl], k_ref[b, :, sl]) * dm_ref[hd] for b, hd, sl in chains]
        cross = [_dot(q_ref[b, :, sl], st_ref[b, hd].astype(_BF16)) * xi_ref[hd] for b, hd, sl in chains]
        outs = [_dot(s.astype(_BF16), v_ref[b, :, sl]) + c for s, c, (b, hd, sl) in zip(scores, cross, chains)]
        for o, (b, hd, sl) in zip(outs, chains):
            oc = o - jnp.mean(o, -1, keepdims=True)
            y = oc * lax.rsqrt(jnp.mean(oc * oc, -1, keepdims=True) + LN_EPS)
            o_ref[b, :, sl] = (y * g_ref[b, :, sl].astype(_F32)).astype(_BF16)
        for b, hd, sl in chains:
            st_ref[b, hd] = gch_ref[hd] * st_ref[b, hd] + kv_update(k_ref[b, :, sl], v_ref[b, :, sl], hd)


def _retention(rq, rk, rv, rg, rk_m, rv_m, dmask, xi, zeta, gch):
    nb, s, _ = rq.shape
    row_spec = pl.BlockSpec((nb, BLOCK, HEAD_W), lambda j: (0, j, 0))
    meta_spec = pl.BlockSpec((BLOCK, HEAD_W), lambda j: (0, 0))
    tab = pl.BlockSpec((RET_HEADS, BLOCK, BLOCK), lambda j: (0, 0, 0))
    return pl.pallas_call(
        _ret_body,
        grid=(s // BLOCK,),
        in_specs=[row_spec] * 4 + [meta_spec] * 2 + [tab] * 3 + [pl.BlockSpec(memory_space=pltpu.SMEM)],
        out_specs=row_spec,
        out_shape=jax.ShapeDtypeStruct((nb, s, HEAD_W), _BF16),
        scratch_shapes=[pltpu.VMEM((nb, RET_HEADS, RET_DK, LANES), _F32)],
        compiler_params=_params(1),
        name="retention",
    )(rq, rk, rv, rg, rk_m, rv_m, dmask, xi, zeta, gch)


def _fox_body(q_ref, k_ref, v_ref, km_ref, vm_ref, o_ref, m_sc, acc_sc, s_sc, p_sc):
    t = T_FOX
    first_tile = 2 * pl.program_id(2)

    heads = [slice(hh * LANES, (hh + 1) * LANES) for hh in range(2)]
    chains = [(qt, hh) for qt in range(2) for hh in range(2)]
    chain_id = lambda qt, hh: 2 * qt + hh

    def logits(qt, hh, k, buf, tk):
        s_sc[buf, chain_id(qt, hh), :, :tk] = _dot_nt(q_ref[qt * t:(qt + 1) * t, heads[hh]], k)

    def update(qt, hh, v, buf, tk, mask, first):
        ch = chain_id(qt, hh)
        for c in range(t // FOX_CHUNK):
            rows = slice(c * FOX_CHUNK, (c + 1) * FOX_CHUNK)
            s = s_sc[buf, ch, rows, :tk]
            if mask is not None:
                s = mask(s, c)
            mx = jnp.max(s, axis=1, keepdims=True)
            if first:
                m_new = mx
            else:
                m_prev = m_sc[ch, rows, :]
                m_new = jnp.maximum(m_prev, mx)
                acc_sc[ch, rows, :] = jnp.exp2(m_prev - m_new) * acc_sc[ch, rows, :]
            p_sc[ch, rows, :tk] = jnp.exp2(s - m_new).astype(_BF16)
            m_sc[ch, rows, :] = m_new
        pv = _dot(p_sc[ch, :, :tk], v)
        if first:
            acc_sc[ch] = pv
        else:
            acc_sc[ch] += pv

    def meta_mask(s, c):
        key = lax.broadcasted_iota(jnp.int32, s.shape, 1)
        return jnp.where(key >= PAD, s, NEG_INF)

    def causal_mask(s, c):
        key = lax.broadcasted_iota(jnp.int32, s.shape, 1)
        query = lax.broadcasted_iota(jnp.int32, s.shape, 0) + c * FOX_CHUNK
        return jnp.where(key <= query, s, NEG_INF)

    def key_tile(ref, ki, hh):
        return ref[pl.ds(pl.multiple_of(ki * t, t), t), heads[hh]]

    for qt, hh in chains:
        logits(qt, hh, km_ref[:, heads[hh]], 1, BLOCK)
    for qt, hh in chains:
        logits(qt, hh, key_tile(k_ref, 0, hh), 0, t)
        update(qt, hh, vm_ref[:, heads[hh]], 1, BLOCK, meta_mask, True)

    def step(ki, buf, active, mask_of, prefetch):
        for qt, hh in active:
            if (qt, hh) in prefetch:
                logits(qt, hh, key_tile(k_ref, ki + 1, hh), 1 - buf, t)
            update(qt, hh, key_tile(v_ref, ki, hh), buf, t, mask_of(qt), False)

    def pair_body(j, carry):
        step(2 * j, 0, chains, lambda qt: None, chains)
        step(2 * j + 1, 1, chains, lambda qt: None, chains)
        return carry

    lax.fori_loop(0, pl.program_id(2), pair_body, 0)
    later = [(1, hh) for hh in range(2)]
    step(first_tile, 0, chains, lambda qt: causal_mask if qt == 0 else None, later)
    step(first_tile + 1, 1, later, lambda qt: causal_mask, [])

    lane = lax.broadcasted_iota(jnp.int32, (t, LANES), 1)
    for qt in range(2):
        outs = []
        for hh in range(2):
            acc = acc_sc[chain_id(qt, hh)]
            ones_lane = (1 - hh) * FOX_HD
            outs.append(acc / acc[:, ones_lane:ones_lane + 1])
        o_ref[qt * t:(qt + 1) * t, :] = jnp.where(lane < FOX_HD, outs[0], outs[1]).astype(_BF16)


def _fox(fq, fk, fv, fk_m, fv_m, *, nb):
    s = fq.shape[1]
    t = T_FOX
    pair_w = 2 * LANES
    n_chain = 4
    return pl.pallas_call(
        _fox_body,
        grid=(nb, FOX_HEADS // 2, s // (2 * t)),
        in_specs=[
            pl.BlockSpec((None, 2 * t, pair_w), lambda b, p, i: (b, i, p)),
            pl.BlockSpec((None, s, pair_w), lambda b, p, i: (b, 0, p)),
            pl.BlockSpec((None, s, pair_w), lambda b, p, i: (b, 0, p)),
            pl.BlockSpec((BLOCK, pair_w), lambda b, p, i: (0, p)),
            pl.BlockSpec((BLOCK, pair_w), lambda b, p, i: (0, p)),
        ],
        out_specs=pl.BlockSpec((None, 2 * t, LANES), lambda b, p, i: (b, i, p)),
        out_shape=jax.ShapeDtypeStruct((nb, s, HEAD_W), _BF16),
        scratch_shapes=[pltpu.VMEM((n_chain, t, 1), _F32), pltpu.VMEM((n_chain, t, LANES), _F32),
                        pltpu.VMEM((2, n_chain, t, t), _F32), pltpu.VMEM((n_chain, t, t), _BF16)],
        compiler_params=_params(3),
        name="fox",
    )(fq, fk, fv, fk_m, fv_m)


def _route(scores, biased):
    w = scores.shape[1]
    sub = lax.broadcasted_iota(jnp.int32, (GROUP_SIZE, w), 0).astype(_F32)
    groups = [biased[g * GROUP_SIZE:(g + 1) * GROUP_SIZE, :] for g in range(N_GROUPS)]
    gscore = []
    for v in groups:
        m1 = jnp.max(v, axis=0, keepdims=True)
        i1 = jnp.min(jnp.where(v == m1, sub, float(GROUP_SIZE)), axis=0, keepdims=True)
        m2 = jnp.max(jnp.where(sub == i1, -jnp.inf, v), axis=0, keepdims=True)
        gscore.append(m1 + m2)
    masked = []
    for g in range(N_GROUPS):
        beaten = jnp.zeros((1, w), _F32)
        for o in range(N_GROUPS):
            if o == g:
                continue
            wins = (gscore[o] >= gscore[g]) if o < g else (gscore[o] > gscore[g])
            beaten = beaten + wins.astype(_F32)
        masked.append(jnp.where(beaten < float(TOPK_GROUPS), groups[g], NEG_INF))
    work = jnp.concatenate(masked, axis=0)
    eid = lax.broadcasted_iota(jnp.int32, (N_EXPERTS, w), 0).astype(_F32)
    sels, raws = [], []
    member = None
    for _ in range(TOP_K):
        m = jnp.max(work, axis=0, keepdims=True)
        idx = jnp.min(jnp.where(work == m, eid, float(N_EXPERTS)), axis=0, keepdims=True)
        hot = eid == idx
        sels.append(idx)
        raws.append(jnp.sum(jnp.where(hot, scores, 0.0), axis=0, keepdims=True))
        work = jnp.where(hot, -jnp.inf, work)
        member = hot if member is None else (member | hot)
    return sels, raws, member


def _post_body(h0_ref, ret_ref, fox_ref, wo_ref, g1_ref, b1_ref, wr_ref, rb_ref,
               trow_ref, sel_ref, gate_ref, rank_ref, cnt_ref, run_ref):
    tm = h0_ref.shape[0]
    i = pl.program_id(0)

    @pl.when(i == 0)
    def _():
        run_ref[...] = jnp.zeros_like(run_ref)

    halves = [slice(hf * tm // 2, (hf + 1) * tm // 2) for hf in range(2)]
    ys = [_dot(ret_ref[hs, :], wo_ref[:HEAD_W, :]) + _dot(fox_ref[hs, :], wo_ref[HEAD_W:, :]) for hs in halves]
    h1s = [_ln(ALPHA * h0_ref[hs, :] + y, g1_ref[...], b1_ref[...]) for hs, y in zip(halves, ys)]
    for hs, h1 in zip(halves, h1s):
        _store_rows(trow_ref, h1, hs.start)

    scores = jnp.concatenate([jax.nn.sigmoid(_dot_nt(wr_ref[...], h1.astype(_BF16))) for h1 in h1s], axis=1)
    biased = scores + rb_ref[...]
    chunks = [slice(c * LANES, (c + 1) * LANES) for c in range(tm // LANES)]
    routed = [_route(scores[:, cs], biased[:, cs]) for cs in chunks]
    member_f = jnp.concatenate([member.astype(_F32) for _, _, member in routed], axis=1)
    row = lax.broadcasted_iota(jnp.int32, (tm, tm), 0)
    col = lax.broadcasted_iota(jnp.int32, (tm, tm), 1)
    before = (row < col).astype(_BF16)
    rank_e = _dot(member_f.astype(_BF16), before) + run_ref[...]
    eid = lax.broadcasted_iota(jnp.int32, (N_EXPERTS, LANES), 0).astype(_F32)
    sel_rows, gate_rows, rank_rows = [], [], []
    for k in range(TOP_K):
        sel_k, gate_k, rank_k = [], [], []
        for cs, (sels, raws, _) in zip(chunks, routed):
            total = raws[0]
            for r in raws[1:]:
                total = total + r
            sel_k.append(sels[k].astype(jnp.int32))
            gate_k.append(raws[k] * (ROUTED_SCALE / total))
            rank_k.append(jnp.sum(jnp.where(eid == sels[k], rank_e[:, cs], 0.0), axis=0, keepdims=True).astype(jnp.int32))
        sel_rows.append(jnp.concatenate(sel_k, axis=1))
        gate_rows.append(jnp.concatenate(gate_k, axis=1))
        rank_rows.append(jnp.concatenate(rank_k, axis=1))
    pad_rows = sel_ref.shape[0] - TOP_K
    sel_ref[...] = jnp.concatenate(sel_rows + [jnp.zeros((pad_rows, tm), jnp.int32)], axis=0)
    gate_ref[...] = jnp.concatenate(gate_rows + [jnp.zeros((pad_rows, tm), _F32)], axis=0)
    rank_ref[...] = jnp.concatenate(rank_rows + [jnp.zeros((pad_rows, tm), jnp.int32)], axis=0)
    run_ref[...] = run_ref[...] + jnp.sum(member_f, axis=1, keepdims=True)
    cnt_ref[...] = run_ref[...]


def _post(h0, ret, fox, w_out, ln1_g, ln1_b, w_rt, rbias):
    n = h0.shape[0]
    tm = TM_PROJ
    row_spec = lambda w: pl.BlockSpec((tm, w), lambda i: (i, 0))
    col_spec = pl.BlockSpec((SUBLANES, tm), lambda i: (0, i))
    const = lambda shape: pl.BlockSpec(shape, lambda i: (0, 0))
    return pl.pallas_call(
        _post_body,
        grid=(n // tm,),
        in_specs=[row_spec(D_MODEL), row_spec(HEAD_W), row_spec(HEAD_W),
                  const(w_out.shape), const((1, D_MODEL)), const((1, D_MODEL)), const(w_rt.shape),
                  const((N_EXPERTS, 1))],
        out_specs=[pl.BlockSpec((tm * ROW_TILES, LANES), lambda i: (i, 0)),
                   col_spec, col_spec, col_spec, const((N_EXPERTS, 1))],
        out_shape=[jax.ShapeDtypeStruct((n * ROW_TILES, LANES), ROW_DTYPE),
                   jax.ShapeDtypeStruct((SUBLANES, n), jnp.int32), jax.ShapeDtypeStruct((SUBLANES, n), _F32),
                   jax.ShapeDtypeStruct((SUBLANES, n), jnp.int32), jax.ShapeDtypeStruct((N_EXPERTS, 1), _F32)],
        scratch_shapes=[pltpu.VMEM((N_EXPERTS, 1), _F32)],
        compiler_params=_params(1),
        name="post_mixer",
    )(h0, ret, fox, w_out, ln1_g, ln1_b, w_rt, rbias)


def _plan_body(sel_ref, rank_ref, cnt_ref, dest_ref, blk_ref, fill_ref, used_ref, tab_ref):
    cnt = cnt_ref[...]
    padded = jnp.ceil(cnt * (1.0 / EBLK)) * EBLK
    er = lax.broadcasted_iota(jnp.int32, (N_EXPERTS, N_EXPERTS), 0)
    ec = lax.broadcasted_iota(jnp.int32, (N_EXPERTS, N_EXPERTS), 1)
    padded_row = jnp.sum(jnp.where(er == ec, padded, 0.0), axis=0, keepdims=True)
    pstart = jnp.sum(jnp.where(ec < er, padded_row, 0.0), axis=1, keepdims=True)
    pend = pstart + padded
    sel = sel_ref[...]
    dest = rank_ref[...]
    for e in range(N_EXPERTS):
        dest = dest + jnp.where(sel == e, pstart[e:e + 1, :].astype(jnp.int32), 0)
    dest_ref[...] = dest
    nblk = blk_ref.shape[1]
    first_row = (lax.broadcasted_iota(jnp.int32, (N_EXPERTS, nblk), 1) * EBLK).astype(_F32)
    owner = jnp.minimum(jnp.sum((pend <= first_row).astype(_F32), axis=0, keepdims=True), N_EXPERTS - 1.0)
    blk_ref[...] = owner.astype(jnp.int32)
    mine = lax.broadcasted_iota(jnp.int32, (N_EXPERTS, nblk), 0).astype(_F32) == owner
    live_end = jnp.sum(jnp.where(mine, pstart + cnt, 0.0), axis=0, keepdims=True)
    fill_ref[...] = jnp.clip(live_end - first_row[:1, :], 0.0, float(EBLK)).astype(jnp.int32)
    used_ref[...] = (pend[N_EXPERTS - 1:, :] * (1.0 / EBLK)).astype(jnp.int32)
    as_row = lambda col: jnp.sum(jnp.where(er == ec, col, 0.0), axis=0, keepdims=True).astype(jnp.int32)
    rows = {TAB_PAD0: pstart + cnt, TAB_PADN: padded - cnt}
    blank = jnp.zeros((1, N_EXPERTS), jnp.int32)
    tab_ref[...] = jnp.concatenate([as_row(rows[r]) if r in rows else blank for r in range(SUBLANES)], axis=0)


def _plan(sel, rank, cnt, nblk_pad):
    n = sel.shape[1]
    full = lambda shape: pl.BlockSpec(shape, lambda i: (0, 0))
    return pl.pallas_call(
        _plan_body,
        grid=(1,),
        in_specs=[full(sel.shape), full(rank.shape), full(cnt.shape)],
        out_specs=[full(sel.shape), full((1, nblk_pad)), full((1, nblk_pad)), full((1, 1)),
                   full((SUBLANES, N_EXPERTS))],
        out_shape=[jax.ShapeDtypeStruct((SUBLANES, n), jnp.int32), jax.ShapeDtypeStruct((1, nblk_pad), jnp.int32),
                   jax.ShapeDtypeStruct((1, nblk_pad), jnp.int32), jax.ShapeDtypeStruct((1, 1), jnp.int32),
                   jax.ShapeDtypeStruct((SUBLANES, N_EXPERTS), jnp.int32)],
        compiler_params=_params(1),
        name="plan",
    )(sel, rank, cnt)


def _store_rows(ref, v, first_row=0):
    m = v.shape[0]
    for s in range(ROW_TILES):
        ref[pl.ds(first_row * ROW_TILES + s, m, stride=ROW_TILES), :] = v[:, s * LANES:(s + 1) * LANES].astype(ROW_DTYPE)


def _load_rows(ref, first_row, m):
    return jnp.concatenate([ref[pl.ds(first_row * ROW_TILES + s, m, stride=ROW_TILES), :] for s in range(ROW_TILES)],
                           axis=1)


def _row_copy(src, src_row, dst, dst_row, sem):
    return pltpu.make_async_copy(src.at[pl.ds(pl.multiple_of(src_row * ROW_TILES, ROW_TILES), ROW_TILES), :],
                                 dst.at[pl.ds(pl.multiple_of(dst_row * ROW_TILES, ROW_TILES), ROW_TILES), :], sem)


def _dispatch_body(dest_ref, tab_ref, used_ref, t_ref, wgu_ref, wd_ref, xs_ref, base_ref, zero_sc, sem, zsem):
    tt = t_ref.shape[0] // ROW_TILES
    half_blk = EBLK // 2
    n_half = xs_ref.shape[0] // (half_blk * ROW_TILES)

    @pl.when(pl.program_id(0) == 0)
    def _():
        zero_sc[...] = jnp.zeros_like(zero_sc)

        def zero_copy(first_row, rows):
            return pltpu.make_async_copy(
                zero_sc.at[pl.ds(0, rows * ROW_TILES), :],
                xs_ref.at[pl.ds(pl.multiple_of(first_row * ROW_TILES, ROW_TILES), rows * ROW_TILES), :], zsem)

        def for_padding(act):
            def per_expert(e, carry):
                row = tab_ref[TAB_PAD0, e]
                for bit in range(EBLK.bit_length() - 2, -1, -1):
                    take = (tab_ref[TAB_PADN, e] & (1 << bit)) != 0

                    @pl.when(take)
                    def _():
                        act(zero_copy(row, 1 << bit))

                    row = row + jnp.where(take, 1 << bit, 0)
                return carry

            def per_tail(hb, carry):
                act(zero_copy(hb * half_blk, half_blk))
                return carry

            lax.fori_loop(0, N_EXPERTS, per_expert, 0)
            lax.fori_loop(2 * used_ref[0], n_half, per_tail, 0)

        for_padding(lambda cp: cp.start())
        for_padding(lambda cp: cp.wait())

    def issue(j, carry):
        for u in range(ISSUE_UNROLL):
            i = ISSUE_UNROLL * j + u
            for k in range(TOP_K):
                _row_copy(t_ref, i, xs_ref, dest_ref[i * SUBLANES + k], sem).start(priority=k % 2)
        return carry

    lax.fori_loop(0, tt // ISSUE_UNROLL, issue, 0)

    h1 = _load_rows(t_ref, 0, tt)
    gu = _dot(h1.astype(_BF16), wgu_ref[...])
    mid = (_silu(gu[:, :SHARED_FF]) * gu[:, SHARED_FF:]).astype(_BF16)
    base_ref[...] = ALPHA * h1 + _dot(mid, wd_ref[...])

    for _ in range(TOP_K):
        pltpu.make_async_copy(t_ref, xs_ref.at[pl.ds(0, tt * ROW_TILES), :], sem).wait()


def _dispatch(dest_flat, tab, used, trow, w_gu, w_sd, total_rows):
    n = trow.shape[0] // ROW_TILES
    tt = TT_DISPATCH
    smem = pl.BlockSpec(memory_space=pltpu.SMEM)
    const = lambda shape: pl.BlockSpec(shape, lambda i: (0, 0))
    return pl.pallas_call(
        _dispatch_body,
        grid=(n // tt,),
        in_specs=[pl.BlockSpec((tt * SUBLANES,), lambda i: (i,), memory_space=pltpu.SMEM), smem, smem,
                  pl.BlockSpec((tt * ROW_TILES, LANES), lambda i: (i, 0)), const(w_gu.shape), const(w_sd.shape)],
        out_specs=[pl.BlockSpec(memory_space=pl.ANY), pl.BlockSpec((tt, D_MODEL), lambda i: (i, 0))],
        out_shape=[jax.ShapeDtypeStruct((total_rows * ROW_TILES, LANES), ROW_DTYPE),
                   jax.ShapeDtypeStruct((n, D_MODEL), _F32)],
        scratch_shapes=[pltpu.VMEM((EBLK // 2 * ROW_TILES, LANES), ROW_DTYPE), pltpu.SemaphoreType.DMA(()),
                        pltpu.SemaphoreType.DMA(())],
        compiler_params=_params(1),
        name="dispatch",
    )(dest_flat, tab, used, trow, w_gu, w_sd)


def _expert_body(blk_ref, used_ref, fill_ref, xs_ref, wg_ref, wu_ref, wd_ref, y_ref, wgu_sc, wd_sc):
    i = pl.program_id(0)
    prev = blk_ref[jnp.maximum(i - 1, 0)]
    fresh = (i == 0) | (blk_ref[i] != prev)

    @pl.when(fresh)
    def _():
        wgu_sc[:, :EXPERT_FF] = wg_ref[...].astype(_BF16)
        wgu_sc[:, EXPERT_FF:] = wu_ref[...].astype(_BF16)
        wd_sc[...] = wd_ref[...].astype(_BF16)

    @pl.when(i < used_ref[0])
    def _():
        live = lax.broadcasted_iota(jnp.int32, (EBLK, 1), 0) < fill_ref[i]
        x = _load_rows(xs_ref, 0, EBLK)
        x = jnp.where(live, x, jnp.zeros_like(x)).astype(_BF16)
        gu = _dot(x, wgu_sc[...])
        mid = (_silu(gu[:, :EXPERT_FF]) * gu[:, EXPERT_FF:]).astype(_BF16)
        _store_rows(y_ref, _dot(mid, wd_sc[...]))

    @pl.when(i >= used_ref[0])
    def _():
        y_ref[...] = jnp.zeros_like(y_ref)


def _experts(blk_e, used, fill, xs, we_gate, we_up, we_down):
    nblk = xs.shape[0] // (EBLK * ROW_TILES)
    last = lambda i, used: jnp.minimum(i, jnp.maximum(used[0] - 1, 0))
    w_spec = lambda shape: pl.BlockSpec((None,) + shape, lambda i, blk, used, fill: (blk[i], 0, 0))
    return pl.pallas_call(
        _expert_body,
        grid_spec=pltpu.PrefetchScalarGridSpec(
            num_scalar_prefetch=3,
            grid=(nblk,),
            in_specs=[pl.BlockSpec((EBLK * ROW_TILES, LANES), lambda i, blk, used, fill: (last(i, used), 0)),
                      w_spec((D_MODEL, EXPERT_FF)), w_spec((D_MODEL, EXPERT_FF)), w_spec((EXPERT_FF, D_MODEL))],
            out_specs=pl.BlockSpec((EBLK * ROW_TILES, LANES), lambda i, blk, used, fill: (i, 0)),
            scratch_shapes=[pltpu.VMEM((D_MODEL, 2 * EXPERT_FF), _BF16), pltpu.VMEM((EXPERT_FF, D_MODEL), _BF16)],
        ),
        out_shape=jax.ShapeDtypeStruct(xs.shape, ROW_DTYPE),
        compiler_params=_params(1),
        name="experts",
    )(blk_e, used, fill, xs, we_gate, we_up, we_down)


def _combine_body(dest_ref, dnext_ref, y_ref, base_ref, gate_ref, g2_ref, b2_ref, o_ref, z_sc, sems):
    i = pl.program_id(0)
    slot = lax.rem(i, 2)
    tile_rows = TT * TOP_K * ROW_TILES

    def gather(dref, into):
        def issue(j, carry):
            for u in range(ISSUE_UNROLL):
                t = ISSUE_UNROLL * j + u
                for k in range(TOP_K):
                    _row_copy(y_ref, dref[t * SUBLANES + k], z_sc.at[into], k * TT + t,
                              sems.at[into]).start(priority=k % 2)
            return carry

        lax.fori_loop(0, TT // ISSUE_UNROLL, issue, 0)

    @pl.when(i == 0)
    def _():
        gather(dest_ref, 0)

    @pl.when(i + 1 < pl.num_programs(0))
    def _():
        gather(dnext_ref, 1 - slot)

    pltpu.make_async_copy(y_ref.at[pl.ds(0, tile_rows), :], z_sc.at[slot], sems.at[slot]).wait()
    gates = gate_ref[...]
    acc = base_ref[...]
    for k in range(TOP_K):
        acc = acc + gates[:, k:k + 1] * _load_rows(z_sc.at[slot], k * TT, TT).astype(_F32)
    o_ref[...] = _ln(acc, g2_ref[...], b2_ref[...])


def _combine(dest_flat, y, base, gates_t, ln2_g, ln2_b):
    n = base.shape[0]
    steps = n // TT
    const = lambda shape: pl.BlockSpec(shape, lambda i: (0, 0))
    dest_spec = lambda ahead: pl.BlockSpec((TT * SUBLANES,), lambda i: (jnp.minimum(i + ahead, steps - 1),),
                                           memory_space=pltpu.SMEM)
    return pl.pallas_call(
        _combine_body,
        grid=(steps,),
        in_specs=[dest_spec(0), dest_spec(1),
                  pl.BlockSpec(memory_space=pl.ANY),
                  pl.BlockSpec((TT, D_MODEL), lambda i: (i, 0)),
                  pl.BlockSpec((TT, SUBLANES), lambda i: (i, 0)),
                  const((1, D_MODEL)), const((1, D_MODEL))],
        out_specs=pl.BlockSpec((TT, D_MODEL), lambda i: (i, 0)),
        out_shape=jax.ShapeDtypeStruct((n, D_MODEL), _F32),
        scratch_shapes=[pltpu.VMEM((2, TT * TOP_K * ROW_TILES, LANES), ROW_DTYPE), pltpu.SemaphoreType.DMA((2,))],
        compiler_params=_params(1),
        name="combine",
    )(dest_flat, dest_flat, y, base, gates_t, ln2_g, ln2_b)


def _rope_tables(pos):
    half = RET_DK // 2
    inv = ROPE_BASE ** (-jnp.arange(half, dtype=_F32) / half)
    ang = pos[:, None] * inv[None, :]
    cos = jnp.cos(ang)
    sin = jnp.sin(ang)
    return jnp.concatenate([cos, cos], -1), jnp.concatenate([-sin, sin], -1)


def _decay_tables():
    lg = jnp.log1p(-jnp.exp2(-5.0 - jnp.arange(RET_HEADS, dtype=_F32)))
    idx = jnp.arange(BLOCK, dtype=_F32)
    rel = idx[:, None] - idx[None, :]
    causal = rel >= 0
    dmask = jnp.where(causal[None], jnp.exp(jnp.where(causal, rel, 0.0)[None] * lg[:, None, None]), 0.0)
    zeta = jnp.exp((BLOCK - 1.0 - idx)[None, :] * lg[:, None])
    xi = jnp.exp((idx + 1.0)[None, :] * lg[:, None])
    along_lanes = lambda col: jnp.broadcast_to(col[:, :, None], (RET_HEADS, BLOCK, LANES))
    return dmask, along_lanes(xi), along_lanes(zeta), jnp.exp(BLOCK * lg)


def kernel(x, meta, ln0_g, ln0_b, w_in, b_forget, w_out, ln1_g, ln1_b, w_router, router_bias, we_gate, we_up,
           we_down, ws_gate, ws_up, ws_down, ln2_g, ln2_b):
    nb, s, d = x.shape
    assert d == D_MODEL and meta.shape == (N_META, D_MODEL) and w_in.shape[0] == 1
    assert s % TM_PROJ == 0 and s % (2 * T_FOX) == 0 and (nb * s) % TT == 0
    n = nb * s
    x2d = x.reshape(n, d)
    row2 = lambda v: v.reshape(1, -1).astype(_F32)
    main_cols = 7 * HEAD_W
    w_all = jnp.concatenate(
        [w_in[0, :, :main_cols], w_in[0, :, main_cols:], jnp.zeros((d, LANES - FOX_HEADS), w_in.dtype)],
        axis=1).astype(_BF16)
    bf_pad = jnp.concatenate([b_forget[0].astype(_F32), jnp.zeros((LANES - FOX_HEADS,), _F32)]).reshape(1, LANES)
    g0, b0 = row2(ln0_g), row2(ln0_b)

    cos_x, sin_x = _rope_tables(jnp.arange(s, dtype=_F32) + float(N_META))
    cos_m, sin_m = _rope_tables(jnp.arange(BLOCK, dtype=_F32) - float(PAD))
    meta_blk = jnp.concatenate([jnp.zeros((PAD, d), _F32), meta.astype(_F32)], axis=0)

    h0, rq, rk, rv, rg, fq, fk, fv = _inproj(x2d, g0, b0, w_all, bf_pad, cos_x, sin_x, nb=nb, meta=False)
    _, _, rk_m, rv_m, _, _, fk_m, fv_m = _inproj(meta_blk, g0, b0, w_all, bf_pad, cos_m, sin_m, nb=1, meta=True)

    dmask, xi, zeta, gch = _decay_tables()
    per_batch = lambda a: a.reshape(nb, s, a.shape[-1])
    ret = _retention(per_batch(rq), per_batch(rk), per_batch(rv), per_batch(rg), rk_m, rv_m, dmask, xi, zeta,
                     gch).reshape(n, HEAD_W)
    fox = _fox(per_batch(fq), per_batch(fk), per_batch(fv), fk_m, fv_m, nb=nb).reshape(n, HEAD_W)

    w_gu = jnp.concatenate([ws_gate[0], ws_up[0]], axis=1).astype(_BF16)
    trow, sel, gates, rank, cnt = _post(
        h0, ret, fox, w_out[0].astype(_BF16), row2(ln1_g[0]), row2(ln1_b[0]),
        jnp.transpose(w_router[0]).astype(_BF16), router_bias[0].astype(_F32).reshape(N_EXPERTS, 1))

    nblk = n * TOP_K // EBLK + N_EXPERTS
    nblk_pad = -(-nblk // LANES) * LANES
    dest, blk_e, fill, used, tab = _plan(sel, rank, cnt, nblk_pad)
    dest_flat = jnp.transpose(dest).reshape(-1)
    used = used.reshape(-1)

    xs, base = _dispatch(dest_flat, tab, used, trow, w_gu, ws_down[0].astype(_BF16), nblk * EBLK)
    y = _experts(blk_e.reshape(-1), used, fill.reshape(-1), xs, we_gate[0], we_up[0], we_down[0])
    out = _combine(dest_flat, y, base, jnp.transpose(gates), row2(ln2_g[0]), row2(ln2_b[0]))
    return out.reshape(nb, s, d)
```

```python
import functools

import jax
import jax.numpy as jnp
import numpy as np
from jax import lax
from jax.experimental import pallas as pl
from jax.experimental.pallas import tpu as pltpu

D_MODEL = 1024
N_META = 16
BLOCK = 128
PAD = BLOCK - N_META
RET_HEADS = 4
RET_DK = 128
FOX_HEADS = 8
FOX_HD = 64
N_EXPERTS = 64
TOP_K = 6
N_GROUPS = 8
GROUP_SIZE = N_EXPERTS // N_GROUPS
TOPK_GROUPS = 4
EXPERT_FF = 256
SHARED_FF = 256
ROUTED_SCALE = 2.5
ROPE_BASE = 10000.0
LN_EPS = 1e-5
NEG_INF = -1e30
ALPHA = 2.0 ** 0.25
HEAD_W = 512
LOG2E = 1.4426950408889634

LANES = 128
FOX_W = FOX_HEADS * LANES
SUBLANES = 8
ROW_TILES = D_MODEL // LANES
ROW_DTYPE = jnp.float32

TM_PROJ = 512
T_FOX = 512
FOX_Q_TILES = 4
FOX_CHUNK = 64
FOX_K_TERMS = 32
TT = 256
TT_DISPATCH = 512
EBLK = 1024
ISSUE_UNROLL = 4
TAB_PAD0, TAB_PADN = 0, 1
V7X_VMEM_BYTES = 64 * 1024 * 1024
VMEM_LIMIT = V7X_VMEM_BYTES * 3 // 4

_F32 = jnp.float32
_BF16 = jnp.bfloat16


def _ln(x, g, b):
    xc = x - jnp.mean(x, -1, keepdims=True)
    var = jnp.mean(xc * xc, -1, keepdims=True)
    return xc * lax.rsqrt(var + LN_EPS) * g + b


def _dot(a, b):
    return jnp.dot(a, b, preferred_element_type=_F32)


def _dot_nt(a, b):
    return lax.dot_general(a, b, (((1,), (1,)), ((), ())), preferred_element_type=_F32)


def _dot_tn(a, b):
    return lax.dot_general(a, b, (((0,), (0,)), ((), ())), preferred_element_type=_F32)


def _silu(x):
    return x * jax.nn.sigmoid(x)


def _params(n_axes):
    return pltpu.CompilerParams(dimension_semantics=("arbitrary",) * n_axes, vmem_limit_bytes=VMEM_LIMIT)


def _inproj_body(x_ref, g_ref, b_ref, w_ref, bf_ref, cos_ref, sin_ref, own_ref, tq_ref, tk_ref, oq_ref, ok_ref, ov_ref,
                 h_ref, rq_ref, rk_ref, rv_ref, rg_ref, fq_ref, fk_ref, fv_ref, carry_ref, *, meta):
    tm = x_ref.shape[0]
    if not meta:
        @pl.when(pl.program_id(1) == 0)
        def _():
            carry_ref[...] = jnp.zeros_like(carry_ref)

    h = _ln(x_ref[...], g_ref[...], b_ref[...])
    if meta:
        valid = lax.broadcasted_iota(jnp.int32, (tm, 1), 0) >= PAD
        h = jnp.where(valid, h, 0.0)
    h_ref[...] = h
    hb = h.astype(_BF16)
    cos = cos_ref[...]
    sin = sin_ref[...]

    def proj(g):
        return _dot(hb, w_ref[:, g * HEAD_W:(g + 1) * HEAD_W])

    def rope_store(p, out_ref, scale):
        for hd in range(RET_HEADS):
            t = p[:, hd * LANES:(hd + 1) * LANES]
            r = t * cos + pltpu.roll(t, LANES // 2, axis=1) * sin
            out_ref[:, hd * LANES:(hd + 1) * LANES] = (r * scale).astype(_BF16)

    z = _dot(hb, w_ref[:, 7 * HEAD_W:7 * HEAD_W + LANES]) + bf_ref[...]
    rope_store(proj(0), rq_ref, 1.0)
    logf = jnp.minimum(z, 0.0) - jnp.log1p(jnp.exp(-jnp.abs(z)))
    if meta:
        logf = jnp.where(valid, logf, 0.0)
    l1 = logf.astype(_BF16)
    r1 = logf - l1.astype(_F32)
    l2 = r1.astype(_BF16)
    l3 = (r1 - l2.astype(_F32)).astype(_BF16)
    row = lax.broadcasted_iota(jnp.int32, (tm, tm), 0)
    col = lax.broadcasted_iota(jnp.int32, (tm, tm), 1)
    tri = (col <= row).astype(_BF16)
    rope_store(proj(1), rk_ref, RET_DK ** -0.5)
    c = _dot(tri, l1) + _dot(tri, l2) + _dot(tri, l3)
    rv_ref[...] = proj(2).astype(_BF16)
    rg_ref[...] = _silu(proj(3)).astype(_BF16)
    if meta:
        c = c - c[tm - 1:tm, :]
    else:
        c = c + carry_ref[...]
        carry_ref[...] = c[tm - 1:tm, :]

    head_lane = lax.broadcasted_iota(jnp.int32, (1, LANES), 1) < FOX_HEADS
    cl = jnp.where(head_lane, c, 0.0) * LOG2E
    c1 = cl.astype(_BF16).astype(_F32)
    r1 = cl - c1
    c2 = r1.astype(_BF16).astype(_F32)
    c3 = (r1 - c2).astype(_BF16).astype(_F32)
    csplit = c1 + pltpu.roll(c2, FOX_HEADS, axis=1) + pltpu.roll(c3, 2 * FOX_HEADS, axis=1)
    half = FOX_HD
    q_even, q_odd = pltpu.roll(csplit, half, axis=1), csplit
    k_even, k_odd = pltpu.roll(csplit, half + FOX_K_TERMS, axis=1), pltpu.roll(csplit, FOX_K_TERMS, axis=1)
    own = own_ref[...] > 0.0

    def per_head(p):
        return jnp.concatenate([p[:, (hd // 2) * LANES:(hd // 2 + 1) * LANES] for hd in range(FOX_HEADS)], axis=1)

    def by_parity(even, odd):
        return jnp.concatenate([even, odd] * (FOX_HEADS // 2), axis=1)

    q_extra = jnp.where(tq_ref[...] > 0.0, by_parity(q_even, q_odd), oq_ref[...])
    k_extra = jnp.where(tk_ref[...] > 0.0, -by_parity(k_even, k_odd), ok_ref[...])
    fq_ref[...] = jnp.where(own, per_head(proj(4) * (FOX_HD ** -0.5 * LOG2E)), q_extra).astype(_BF16)
    fk_ref[...] = jnp.where(own, per_head(proj(5)), k_extra).astype(_BF16)
    fv_ref[...] = jnp.where(own, per_head(proj(6)), ov_ref[...]).astype(_BF16)


def _fox_lane_tables():
    own, tq, tk, oq, ok, ov = (np.zeros((1, FOX_W), np.float32) for _ in range(6))
    for hd in range(FOX_HEADS):
        data = hd * LANES + (hd % 2) * FOX_HD
        extra = hd * LANES + (1 - hd % 2) * FOX_HD
        own[0, data:data + FOX_HD] = 1.0
        for term in range(3):
            lane = extra + term * FOX_HEADS + hd
            tq[0, lane] = 1.0
            ok[0, lane] = 1.0
            tk[0, lane + FOX_K_TERMS] = 1.0
            oq[0, lane + FOX_K_TERMS] = 1.0
        ov[0, extra] = 1.0
    return tuple(jnp.asarray(t) for t in (own, tq, tk, oq, ok, ov))


def _inproj(x2d, ln_g, ln_b, w_all, bf_pad, cos_t, sin_t, *, nb, meta):
    n = x2d.shape[0]
    s = n // nb
    tm = min(TM_PROJ, s)
    nj = s // tm
    row_spec = lambda w: pl.BlockSpec((tm, w), lambda b, j: (b * nj + j, 0))
    const = lambda shape: pl.BlockSpec(shape, lambda b, j: (0, 0))
    pos_spec = pl.BlockSpec((tm, LANES), lambda b, j: (j, 0))
    tables = _fox_lane_tables()
    outs = ([jax.ShapeDtypeStruct((n, D_MODEL), _F32)] + [jax.ShapeDtypeStruct((n, HEAD_W), _BF16)] * 4
            + [jax.ShapeDtypeStruct((n, FOX_W), _BF16)] * 3)
    return pl.pallas_call(
        functools.partial(_inproj_body, meta=meta),
        grid=(nb, nj),
        in_specs=[row_spec(D_MODEL), const((1, D_MODEL)), const((1, D_MODEL)), const(w_all.shape),
                  const((1, LANES)), pos_spec, pos_spec] + [const(t.shape) for t in tables],
        out_specs=[row_spec(D_MODEL)] + [row_spec(HEAD_W)] * 4 + [row_spec(FOX_W)] * 3,
        out_shape=outs,
        scratch_shapes=[pltpu.VMEM((1, LANES), _F32)],
        compiler_params=_params(2),
        name="inproj_meta" if meta else "inproj",
    )(x2d, ln_g, ln_b, w_all, bf_pad, cos_t, sin_t, *tables)


def _ret_body(q_ref, k_ref, v_ref, g_ref, km_ref, vm_ref, dm_ref, xi_ref, zeta_ref, gch_ref, o_ref, st_ref):
    def kv_update(k, v, hd):
        vz = (v.astype(_F32) * zeta_ref[hd]).astype(_BF16)
        return _dot_tn(k, vz)

    nb = q_ref.shape[0]

    @pl.when(pl.program_id(0) == 0)
    def _():
        for hd in range(RET_HEADS):
            sl = slice(hd * LANES, (hd + 1) * LANES)
            first = kv_update(km_ref[:, sl], vm_ref[:, sl], hd)
            for b in range(nb):
                st_ref[b, hd] = first

    group = 4
    for b0 in range(0, nb, group):
        chains = [(b, hd, slice(hd * LANES, (hd + 1) * LANES)) for b in range(b0, b0 + group) for hd in range(RET_HEADS)]
        scores = [_dot_nt(q_ref[b, :, sl], k_ref[b, :, sl]) * dm_ref[hd] for b, hd, sl in chains]
        cross = [_dot(q_ref[b, :, sl], st_ref[b, hd].astype(_BF16)) * xi_ref[hd] for b, hd, sl in chains]
        outs = [_dot(s.astype(_BF16), v_ref[b, :, sl]) + c for s, c, (b, hd, sl) in zip(scores, cross, chains)]
        for o, (b, hd, sl) in zip(outs, chains):
            oc = o - jnp.mean(o, -1, keepdims=True)
            y = oc * lax.rsqrt(jnp.mean(oc * oc, -1, keepdims=True) + LN_EPS)
            o_ref[b, :, sl] = (y * g_ref[b, :, sl].astype(_F32)).astype(_BF16)
        for b, hd, sl in chains:
            st_ref[b, hd] = gch_ref[hd] * st_ref[b, hd] + kv_update(k_ref[b, :, sl], v_ref[b, :, sl], hd)


def _retention(rq, rk, rv, rg, rk_m, rv_m, dmask, xi, zeta, gch):
    nb, s, _ = rq.shape
    row_spec = pl.BlockSpec((nb, BLOCK, HEAD_W), lambda j: (0, j, 0))
    meta_spec = pl.BlockSpec((BLOCK, HEAD_W), lambda j: (0, 0))
    tab = pl.BlockSpec((RET_HEADS, BLOCK, BLOCK), lambda j: (0, 0, 0))
    return pl.pallas_call(
        _ret_body,
        grid=(s // BLOCK,),
        in_specs=[row_spec] * 4 + [meta_spec] * 2 + [tab] * 3 + [pl.BlockSpec(memory_space=pltpu.SMEM)],
        out_specs=row_spec,
        out_shape=jax.ShapeDtypeStruct((nb, s, HEAD_W), _BF16),
        scratch_shapes=[pltpu.VMEM((nb, RET_HEADS, RET_DK, LANES), _F32)],
        compiler_params=_params(1),
        name="retention",
    )(rq, rk, rv, rg, rk_m, rv_m, dmask, xi, zeta, gch)


def _fox_body(q_ref, k_ref, v_ref, km_ref, vm_ref, o_ref, m_sc, acc_sc, s_sc, p_sc):
    t = T_FOX
    nq = q_ref.shape[0] // t
    first_tile = nq * pl.program_id(2)

    heads = [slice(hh * LANES, (hh + 1) * LANES) for hh in range(2)]
    chains = [(qt, hh) for qt in range(nq) for hh in range(2)]
    chain_id = lambda qt, hh: 2 * qt + hh

    def logits(qt, hh, k, buf, tk):
        s_sc[buf, chain_id(qt, hh), :, :tk] = _dot_nt(q_ref[qt * t:(qt + 1) * t, heads[hh]], k)

    def update(qt, hh, v, buf, tk, mask, first):
        ch = chain_id(qt, hh)
        for c in range(t // FOX_CHUNK):
            rows = slice(c * FOX_CHUNK, (c + 1) * FOX_CHUNK)
            s = s_sc[buf, ch, rows, :tk]
            if mask is not None:
                s = mask(s, c)
            mx = jnp.max(s, axis=1, keepdims=True)
            if first:
                m_new = mx
            else:
                m_prev = m_sc[ch, rows, :]
                m_new = jnp.maximum(m_prev, mx)
                acc_sc[ch, rows, :] = jnp.exp2(m_prev - m_new) * acc_sc[ch, rows, :]
            p_sc[ch, rows, :tk] = jnp.exp2(s - m_new).astype(_BF16)
            m_sc[ch, rows, :] = m_new
        pv = _dot(p_sc[ch, :, :tk], v)
        if first:
            acc_sc[ch] = pv
        else:
            acc_sc[ch] += pv

    def meta_mask(s, c):
        key = lax.broadcasted_iota(jnp.int32, s.shape, 1)
        return jnp.where(key >= PAD, s, NEG_INF)

    def causal_mask(s, c):
        key = lax.broadcasted_iota(jnp.int32, s.shape, 1)
        query = lax.broadcasted_iota(jnp.int32, s.shape, 0) + c * FOX_CHUNK
        return jnp.where(key <= query, s, NEG_INF)

    def key_tile(ref, ki, hh):
        return ref[pl.ds(pl.multiple_of(ki * t, t), t), heads[hh]]

    for qt, hh in chains:
        logits(qt, hh, km_ref[:, heads[hh]], 1, BLOCK)
    for qt, hh in chains:
        logits(qt, hh, key_tile(k_ref, 0, hh), 0, t)
        update(qt, hh, vm_ref[:, heads[hh]], 1, BLOCK, meta_mask, True)

    def step(ki, buf, active, mask_of, prefetch):
        for qt, hh in active:
            if (qt, hh) in prefetch:
                logits(qt, hh, key_tile(k_ref, ki + 1, hh), 1 - buf, t)
            update(qt, hh, key_tile(v_ref, ki, hh), buf, t, mask_of(qt), False)

    def pair_body(j, carry):
        step(2 * j, 0, chains, lambda qt: None, chains)
        step(2 * j + 1, 1, chains, lambda qt: None, chains)
        return carry

    lax.fori_loop(0, (nq // 2) * pl.program_id(2), pair_body, 0)
    for r in range(nq):
        active = [(qt, hh) for qt, hh in chains if qt >= r]
        step(first_tile + r, r % 2, active, lambda qt, r=r: causal_mask if qt == r else None,
             [(qt, hh) for qt, hh in active if qt > r])

    lane = lax.broadcasted_iota(jnp.int32, (t, LANES), 1)
    for qt in range(nq):
        outs = []
        for hh in range(2):
            acc = acc_sc[chain_id(qt, hh)]
            ones_lane = (1 - hh) * FOX_HD
            outs.append(acc / acc[:, ones_lane:ones_lane + 1])
        o_ref[qt * t:(qt + 1) * t, :] = jnp.where(lane < FOX_HD, outs[0], outs[1]).astype(_BF16)


def _fox(fq, fk, fv, fk_m, fv_m, *, nb):
    s = fq.shape[1]
    t = T_FOX
    pair_w = 2 * LANES
    n_chain = 2 * FOX_Q_TILES
    return pl.pallas_call(
        _fox_body,
        grid=(nb, FOX_HEADS // 2, s // (FOX_Q_TILES * t)),
        in_specs=[
            pl.BlockSpec((None, FOX_Q_TILES * t, pair_w), lambda b, p, i: (b, i, p)),
            pl.BlockSpec((None, s, pair_w), lambda b, p, i: (b, 0, p)),
            pl.BlockSpec((None, s, pair_w), lambda b, p, i: (b, 0, p)),
            pl.BlockSpec((BLOCK, pair_w), lambda b, p, i: (0, p)),
            pl.BlockSpec((BLOCK, pair_w), lambda b, p, i: (0, p)),
        ],
        out_specs=pl.BlockSpec((None, FOX_Q_TILES * t, LANES), lambda b, p, i: (b, i, p)),
        out_shape=jax.ShapeDtypeStruct((nb, s, HEAD_W), _BF16),
        scratch_shapes=[pltpu.VMEM((n_chain, t, 1), _F32), pltpu.VMEM((n_chain, t, LANES), _F32),
                        pltpu.VMEM((2, n_chain, t, t), _F32), pltpu.VMEM((n_chain, t, t), _BF16)],
        compiler_params=_params(3),
        name="fox",
    )(fq, fk, fv, fk_m, fv_m)


def _route(scores, biased):
    w = scores.shape[1]
    sub = lax.broadcasted_iota(jnp.int32, (GROUP_SIZE, w), 0).astype(_F32)
    groups = [biased[g * GROUP_SIZE:(g + 1) * GROUP_SIZE, :] for g in range(N_GROUPS)]
    gscore = []
    for v in groups:
        m1 = jnp.max(v, axis=0, keepdims=True)
        i1 = jnp.min(jnp.where(v == m1, sub, float(GROUP_SIZE)), axis=0, keepdims=True)
        m2 = jnp.max(jnp.where(sub == i1, -jnp.inf, v), axis=0, keepdims=True)
        gscore.append(m1 + m2)
    masked = []
    for g in range(N_GROUPS):
        beaten = jnp.zeros((1, w), _F32)
        for o in range(N_GROUPS):
            if o == g:
                continue
            wins = (gscore[o] >= gscore[g]) if o < g else (gscore[o] > gscore[g])
            beaten = beaten + wins.astype(_F32)
        masked.append(jnp.where(beaten < float(TOPK_GROUPS), groups[g], NEG_INF))
    work = jnp.concatenate(masked, axis=0)
    eid = lax.broadcasted_iota(jnp.int32, (N_EXPERTS, w), 0).astype(_F32)
    sels, raws = [], []
    member = None
    for _ in range(TOP_K):
        m = jnp.max(work, axis=0, keepdims=True)
        idx = jnp.min(jnp.where(work == m, eid, float(N_EXPERTS)), axis=0, keepdims=True)
        hot = eid == idx
        sels.append(idx)
        raws.append(jnp.sum(jnp.where(hot, scores, 0.0), axis=0, keepdims=True))
        work = jnp.where(hot, -jnp.inf, work)
        member = hot if member is None else (member | hot)
    return sels, raws, member


def _post_body(h0_ref, ret_ref, fox_ref, wo_ref, g1_ref, b1_ref, wr_ref, rb_ref,
               trow_ref, sel_ref, gate_ref, rank_ref, cnt_ref, run_ref):
    tm = h0_ref.shape[0]
    i = pl.program_id(0)

    @pl.when(i == 0)
    def _():
        run_ref[...] = jnp.zeros_like(run_ref)

    halves = [slice(hf * tm // 2, (hf + 1) * tm // 2) for hf in range(2)]
    ys = [_dot(ret_ref[hs, :], wo_ref[:HEAD_W, :]) + _dot(fox_ref[hs, :], wo_ref[HEAD_W:, :]) for hs in halves]
    h1s = [_ln(ALPHA * h0_ref[hs, :] + y, g1_ref[...], b1_ref[...]) for hs, y in zip(halves, ys)]
    for hs, h1 in zip(halves, h1s):
        _store_rows(trow_ref, h1, hs.start)

    scores = jnp.concatenate([jax.nn.sigmoid(_dot_nt(wr_ref[...], h1.astype(_BF16))) for h1 in h1s], axis=1)
    biased = scores + rb_ref[...]
    chunks = [slice(c * LANES, (c + 1) * LANES) for c in range(tm // LANES)]
    routed = [_route(scores[:, cs], biased[:, cs]) for cs in chunks]
    member_f = jnp.concatenate([member.astype(_F32) for _, _, member in routed], axis=1)
    row = lax.broadcasted_iota(jnp.int32, (tm, tm), 0)
    col = lax.broadcasted_iota(jnp.int32, (tm, tm), 1)
    before = (row < col).astype(_BF16)
    rank_e = _dot(member_f.astype(_BF16), before) + run_ref[...]
    eid = lax.broadcasted_iota(jnp.int32, (N_EXPERTS, LANES), 0).astype(_F32)
    sel_rows, gate_rows, rank_rows = [], [], []
    for k in range(TOP_K):
        sel_k, gate_k, rank_k = [], [], []
        for cs, (sels, raws, _) in zip(chunks, routed):
            total = raws[0]
            for r in raws[1:]:
                total = total + r
            sel_k.append(sels[k].astype(jnp.int32))
            gate_k.append(raws[k] * (ROUTED_SCALE / total))
            rank_k.append(jnp.sum(jnp.where(eid == sels[k], rank_e[:, cs], 0.0), axis=0, keepdims=True).astype(jnp.int32))
        sel_rows.append(jnp.concatenate(sel_k, axis=1))
        gate_rows.append(jnp.concatenate(gate_k, axis=1))
        rank_rows.append(jnp.concatenate(rank_k, axis=1))
    pad_rows = sel_ref.shape[0] - TOP_K
    sel_ref[...] = jnp.concatenate(sel_rows + [jnp.zeros((pad_rows, tm), jnp.int32)], axis=0)
    gate_ref[...] = jnp.concatenate(gate_rows + [jnp.zeros((pad_rows, tm), _F32)], axis=0)
    rank_ref[...] = jnp.concatenate(rank_rows + [jnp.zeros((pad_rows, tm), jnp.int32)], axis=0)
    run_ref[...] = run_ref[...] + jnp.sum(member_f, axis=1, keepdims=True)
    cnt_ref[...] = run_ref[...]


def _post(h0, ret, fox, w_out, ln1_g, ln1_b, w_rt, rbias):
    n = h0.shape[0]
    tm = TM_PROJ
    row_spec = lambda w: pl.BlockSpec((tm, w), lambda i: (i, 0))
    col_spec = pl.BlockSpec((SUBLANES, tm), lambda i: (0, i))
    const = lambda shape: pl.BlockSpec(shape, lambda i: (0, 0))
    return pl.pallas_call(
        _post_body,
        grid=(n // tm,),
        in_specs=[row_spec(D_MODEL), row_spec(HEAD_W), row_spec(HEAD_W),
                  const(w_out.shape), const((1, D_MODEL)), const((1, D_MODEL)), const(w_rt.shape),
                  const((N_EXPERTS, 1))],
        out_specs=[pl.BlockSpec((tm * ROW_TILES, LANES), lambda i: (i, 0)),
                   col_spec, col_spec, col_spec, const((N_EXPERTS, 1))],
        out_shape=[jax.ShapeDtypeStruct((n * ROW_TILES, LANES), ROW_DTYPE),
                   jax.ShapeDtypeStruct((SUBLANES, n), jnp.int32), jax.ShapeDtypeStruct((SUBLANES, n), _F32),
                   jax.ShapeDtypeStruct((SUBLANES, n), jnp.int32), jax.ShapeDtypeStruct((N_EXPERTS, 1), _F32)],
        scratch_shapes=[pltpu.VMEM((N_EXPERTS, 1), _F32)],
        compiler_params=_params(1),
        name="post_mixer",
    )(h0, ret, fox, w_out, ln1_g, ln1_b, w_rt, rbias)


def _plan_body(sel_ref, rank_ref, cnt_ref, dest_ref, blk_ref, fill_ref, used_ref, tab_ref):
    cnt = cnt_ref[...]
    padded = jnp.ceil(cnt * (1.0 / EBLK)) * EBLK
    er = lax.broadcasted_iota(jnp.int32, (N_EXPERTS, N_EXPERTS), 0)
    ec = lax.broadcasted_iota(jnp.int32, (N_EXPERTS, N_EXPERTS), 1)
    padded_row = jnp.sum(jnp.where(er == ec, padded, 0.0), axis=0, keepdims=True)
    pstart = jnp.sum(jnp.where(ec < er, padded_row, 0.0), axis=1, keepdims=True)
    pend = pstart + padded
    sel = sel_ref[...]
    dest = rank_ref[...]
    for e in range(N_EXPERTS):
        dest = dest + jnp.where(sel == e, pstart[e:e + 1, :].astype(jnp.int32), 0)
    dest_ref[...] = dest
    nblk = blk_ref.shape[1]
    first_row = (lax.broadcasted_iota(jnp.int32, (N_EXPERTS, nblk), 1) * EBLK).astype(_F32)
    owner = jnp.minimum(jnp.sum((pend <= first_row).astype(_F32), axis=0, keepdims=True), N_EXPERTS - 1.0)
    blk_ref[...] = owner.astype(jnp.int32)
    mine = lax.broadcasted_iota(jnp.int32, (N_EXPERTS, nblk), 0).astype(_F32) == owner
    live_end = jnp.sum(jnp.where(mine, pstart + cnt, 0.0), axis=0, keepdims=True)
    fill_ref[...] = jnp.clip(live_end - first_row[:1, :], 0.0, float(EBLK)).astype(jnp.int32)
    used_ref[...] = (pend[N_EXPERTS - 1:, :] * (1.0 / EBLK)).astype(jnp.int32)
    as_row = lambda col: jnp.sum(jnp.where(er == ec, col, 0.0), axis=0, keepdims=True).astype(jnp.int32)
    rows = {TAB_PAD0: pstart + cnt, TAB_PADN: padded - cnt}
    blank = jnp.zeros((1, N_EXPERTS), jnp.int32)
    tab_ref[...] = jnp.concatenate([as_row(rows[r]) if r in rows else blank for r in range(SUBLANES)], axis=0)


def _plan(sel, rank, cnt, nblk_pad):
    n = sel.shape[1]
    full = lambda shape: pl.BlockSpec(shape, lambda i: (0, 0))
    return pl.pallas_call(
        _plan_body,
        grid=(1,),
        in_specs=[full(sel.shape), full(rank.shape), full(cnt.shape)],
        out_specs=[full(sel.shape), full((1, nblk_pad)), full((1, nblk_pad)), full((1, 1)),
                   full((SUBLANES, N_EXPERTS))],
        out_shape=[jax.ShapeDtypeStruct((SUBLANES, n), jnp.int32), jax.ShapeDtypeStruct((1, nblk_pad), jnp.int32),
                   jax.ShapeDtypeStruct((1, nblk_pad), jnp.int32), jax.ShapeDtypeStruct((1, 1), jnp.int32),
                   jax.ShapeDtypeStruct((SUBLANES, N_EXPERTS), jnp.int32)],
        compiler_params=_params(1),
        name="plan",
    )(sel, rank, cnt)


def _store_rows(ref, v, first_row=0):
    m = v.shape[0]
    for s in range(ROW_TILES):
        ref[pl.ds(first_row * ROW_TILES + s, m, stride=ROW_TILES), :] = v[:, s * LANES:(s + 1) * LANES].astype(ROW_DTYPE)


def _load_rows(ref, first_row, m):
    return jnp.concatenate([ref[pl.ds(first_row * ROW_TILES + s, m, stride=ROW_TILES), :] for s in range(ROW_TILES)],
                           axis=1)


def _row_copy(src, src_row, dst, dst_row, sem):
    return pltpu.make_async_copy(src.at[pl.ds(pl.multiple_of(src_row * ROW_TILES, ROW_TILES), ROW_TILES), :],
                                 dst.at[pl.ds(pl.multiple_of(dst_row * ROW_TILES, ROW_TILES), ROW_TILES), :], sem)


def _dispatch_body(dest_ref, tab_ref, used_ref, t_ref, wgu_ref, wd_ref, xs_ref, base_ref, zero_sc, sem, zsem):
    tt = t_ref.shape[0] // ROW_TILES
    half_blk = EBLK // 2
    n_half = xs_ref.shape[0] // (half_blk * ROW_TILES)

    @pl.when(pl.program_id(0) == 0)
    def _():
        zero_sc[...] = jnp.zeros_like(zero_sc)

        def zero_copy(first_row, rows):
            return pltpu.make_async_copy(
                zero_sc.at[pl.ds(0, rows * ROW_TILES), :],
                xs_ref.at[pl.ds(pl.multiple_of(first_row * ROW_TILES, ROW_TILES), rows * ROW_TILES), :], zsem)

        def for_padding(act):
            def per_expert(e, carry):
                row = tab_ref[TAB_PAD0, e]
                for bit in range(EBLK.bit_length() - 2, -1, -1):
                    take = (tab_ref[TAB_PADN, e] & (1 << bit)) != 0

                    @pl.when(take)
                    def _():
                        act(zero_copy(row, 1 << bit))

                    row = row + jnp.where(take, 1 << bit, 0)
                return carry

            def per_tail(hb, carry):
                act(zero_copy(hb * half_blk, half_blk))
                return carry

            lax.fori_loop(0, N_EXPERTS, per_expert, 0)
            lax.fori_loop(2 * used_ref[0], n_half, per_tail, 0)

        for_padding(lambda cp: cp.start())
        for_padding(lambda cp: cp.wait())

    def issue(j, carry):
        for u in range(ISSUE_UNROLL):
            i = ISSUE_UNROLL * j + u
            for k in range(TOP_K):
                _row_copy(t_ref, i, xs_ref, dest_ref[i * SUBLANES + k], sem).start(priority=k % 2)
        return carry

    lax.fori_loop(0, tt // ISSUE_UNROLL, issue, 0)

    h1 = _load_rows(t_ref, 0, tt)
    gu = _dot(h1.astype(_BF16), wgu_ref[...])
    mid = (_silu(gu[:, :SHARED_FF]) * gu[:, SHARED_FF:]).astype(_BF16)
    base_ref[...] = ALPHA * h1 + _dot(mid, wd_ref[...])

    for _ in range(TOP_K):
        pltpu.make_async_copy(t_ref, xs_ref.at[pl.ds(0, tt * ROW_TILES), :], sem).wait()


def _dispatch(dest_flat, tab, used, trow, w_gu, w_sd, total_rows):
    n = trow.shape[0] // ROW_TILES
    tt = TT_DISPATCH
    smem = pl.BlockSpec(memory_space=pltpu.SMEM)
    const = lambda shape: pl.BlockSpec(shape, lambda i: (0, 0))
    return pl.pallas_call(
        _dispatch_body,
        grid=(n // tt,),
        in_specs=[pl.BlockSpec((tt * SUBLANES,), lambda i: (i,), memory_space=pltpu.SMEM), smem, smem,
                  pl.BlockSpec((tt * ROW_TILES, LANES), lambda i: (i, 0)), const(w_gu.shape), const(w_sd.shape)],
        out_specs=[pl.BlockSpec(memory_space=pl.ANY), pl.BlockSpec((tt, D_MODEL), lambda i: (i, 0))],
        out_shape=[jax.ShapeDtypeStruct((total_rows * ROW_TILES, LANES), ROW_DTYPE),
                   jax.ShapeDtypeStruct((n, D_MODEL), _F32)],
        scratch_shapes=[pltpu.VMEM((EBLK // 2 * ROW_TILES, LANES), ROW_DTYPE), pltpu.SemaphoreType.DMA(()),
                        pltpu.SemaphoreType.DMA(())],
        compiler_params=_params(1),
        name="dispatch",
    )(dest_flat, tab, used, trow, w_gu, w_sd)


def _expert_body(blk_ref, used_ref, fill_ref, xs_ref, wg_ref, wu_ref, wd_ref, y_ref, wgu_sc, wd_sc):
    i = pl.program_id(0)
    prev = blk_ref[jnp.maximum(i - 1, 0)]
    fresh = (i == 0) | (blk_ref[i] != prev)

    @pl.when(fresh)
    def _():
        wgu_sc[:, :EXPERT_FF] = wg_ref[...].astype(_BF16)
        wgu_sc[:, EXPERT_FF:] = wu_ref[...].astype(_BF16)
        wd_sc[...] = wd_ref[...].astype(_BF16)

    @pl.when(i < used_ref[0])
    def _():
        live = lax.broadcasted_iota(jnp.int32, (EBLK, 1), 0) < fill_ref[i]
        x = _load_rows(xs_ref, 0, EBLK)
        x = jnp.where(live, x, jnp.zeros_like(x)).astype(_BF16)
        gu = _dot(x, wgu_sc[...])
        mid = (_silu(gu[:, :EXPERT_FF]) * gu[:, EXPERT_FF:]).astype(_BF16)
        _store_rows(y_ref, _dot(mid, wd_sc[...]))

    @pl.when(i >= used_ref[0])
    def _():
        y_ref[...] = jnp.zeros_like(y_ref)


def _experts(blk_e, used, fill, xs, we_gate, we_up, we_down):
    nblk = xs.shape[0] // (EBLK * ROW_TILES)
    last = lambda i, used: jnp.minimum(i, jnp.maximum(used[0] - 1, 0))
    w_spec = lambda shape: pl.BlockSpec((None,) + shape, lambda i, blk, used, fill: (blk[i], 0, 0))
    return pl.pallas_call(
        _expert_body,
        grid_spec=pltpu.PrefetchScalarGridSpec(
            num_scalar_prefetch=3,
            grid=(nblk,),
            in_specs=[pl.BlockSpec((EBLK * ROW_TILES, LANES), lambda i, blk, used, fill: (last(i, used), 0)),
                      w_spec((D_MODEL, EXPERT_FF)), w_spec((D_MODEL, EXPERT_FF)), w_spec((EXPERT_FF, D_MODEL))],
            out_specs=pl.BlockSpec((EBLK * ROW_TILES, LANES), lambda i, blk, used, fill: (i, 0)),
            scratch_shapes=[pltpu.VMEM((D_MODEL, 2 * EXPERT_FF), _BF16), pltpu.VMEM((EXPERT_FF, D_MODEL), _BF16)],
        ),
        out_shape=jax.ShapeDtypeStruct(xs.shape, ROW_DTYPE),
        compiler_params=_params(1),
        name="experts",
    )(blk_e, used, fill, xs, we_gate, we_up, we_down)


def _combine_body(dest_ref, dnext_ref, y_ref, base_ref, gate_ref, g2_ref, b2_ref, o_ref, z_sc, sems):
    i = pl.program_id(0)
    slot = lax.rem(i, 2)
    tile_rows = TT * TOP_K * ROW_TILES

    def gather(dref, into):
        def issue(j, carry):
            for u in range(ISSUE_UNROLL):
                t = ISSUE_UNROLL * j + u
                for k in range(TOP_K):
                    _row_copy(y_ref, dref[t * SUBLANES + k], z_sc.at[into], k * TT + t,
                              sems.at[into]).start(priority=k % 2)
            return carry

        lax.fori_loop(0, TT // ISSUE_UNROLL, issue, 0)

    @pl.when(i == 0)
    def _():
        gather(dest_ref, 0)

    @pl.when(i + 1 < pl.num_programs(0))
    def _():
        gather(dnext_ref, 1 - slot)

    pltpu.make_async_copy(y_ref.at[pl.ds(0, tile_rows), :], z_sc.at[slot], sems.at[slot]).wait()
    gates = gate_ref[...]
    acc = base_ref[...]
    for k in range(TOP_K):
        acc = acc + gates[:, k:k + 1] * _load_rows(z_sc.at[slot], k * TT, TT).astype(_F32)
    o_ref[...] = _ln(acc, g2_ref[...], b2_ref[...])


def _combine(dest_flat, y, base, gates_t, ln2_g, ln2_b):
    n = base.shape[0]
    steps = n // TT
    const = lambda shape: pl.BlockSpec(shape, lambda i: (0, 0))
    dest_spec = lambda ahead: pl.BlockSpec((TT * SUBLANES,), lambda i: (jnp.minimum(i + ahead, steps - 1),),
                                           memory_space=pltpu.SMEM)
    return pl.pallas_call(
        _combine_body,
        grid=(steps,),
        in_specs=[dest_spec(0), dest_spec(1),
                  pl.BlockSpec(memory_space=pl.ANY),
                  pl.BlockSpec((TT, D_MODEL), lambda i: (i, 0)),
                  pl.BlockSpec((TT, SUBLANES), lambda i: (i, 0)),
                  const((1, D_MODEL)), const((1, D_MODEL))],
        out_specs=pl.BlockSpec((TT, D_MODEL), lambda i: (i, 0)),
        out_shape=jax.ShapeDtypeStruct((n, D_MODEL), _F32),
        scratch_shapes=[pltpu.VMEM((2, TT * TOP_K * ROW_TILES, LANES), ROW_DTYPE), pltpu.SemaphoreType.DMA((2,))],
        compiler_params=_params(1),
        name="combine",
    )(dest_flat, dest_flat, y, base, gates_t, ln2_g, ln2_b)


def _rope_tables(pos):
    half = RET_DK // 2
    inv = ROPE_BASE ** (-jnp.arange(half, dtype=_F32) / half)
    ang = pos[:, None] * inv[None, :]
    cos = jnp.cos(ang)
    sin = jnp.sin(ang)
    return jnp.concatenate([cos, cos], -1), jnp.concatenate([-sin, sin], -1)


def _decay_tables():
    lg = jnp.log1p(-jnp.exp2(-5.0 - jnp.arange(RET_HEADS, dtype=_F32)))
    idx = jnp.arange(BLOCK, dtype=_F32)
    rel = idx[:, None] - idx[None, :]
    causal = rel >= 0
    dmask = jnp.where(causal[None], jnp.exp(jnp.where(causal, rel, 0.0)[None] * lg[:, None, None]), 0.0)
    zeta = jnp.exp((BLOCK - 1.0 - idx)[None, :] * lg[:, None])
    xi = jnp.exp((idx + 1.0)[None, :] * lg[:, None])
    along_lanes = lambda col: jnp.broadcast_to(col[:, :, None], (RET_HEADS, BLOCK, LANES))
    return dmask, along_lanes(xi), along_lanes(zeta), jnp.exp(BLOCK * lg)


def kernel(x, meta, ln0_g, ln0_b, w_in, b_forget, w_out, ln1_g, ln1_b, w_router, router_bias, we_gate, we_up,
           we_down, ws_gate, ws_up, ws_down, ln2_g, ln2_b):
    nb, s, d = x.shape
    assert d == D_MODEL and meta.shape == (N_META, D_MODEL) and w_in.shape[0] == 1
    assert s % TM_PROJ == 0 and s % (FOX_Q_TILES * T_FOX) == 0 and (nb * s) % TT == 0
    n = nb * s
    x2d = x.reshape(n, d)
    row2 = lambda v: v.reshape(1, -1).astype(_F32)
    main_cols = 7 * HEAD_W
    w_all = jnp.concatenate(
        [w_in[0, :, :main_cols], w_in[0, :, main_cols:], jnp.zeros((d, LANES - FOX_HEADS), w_in.dtype)],
        axis=1).astype(_BF16)
    bf_pad = jnp.concatenate([b_forget[0].astype(_F32), jnp.zeros((LANES - FOX_HEADS,), _F32)]).reshape(1, LANES)
    g0, b0 = row2(ln0_g), row2(ln0_b)

    cos_x, sin_x = _rope_tables(jnp.arange(s, dtype=_F32) + float(N_META))
    cos_m, sin_m = _rope_tables(jnp.arange(BLOCK, dtype=_F32) - float(PAD))
    meta_blk = jnp.concatenate([jnp.zeros((PAD, d), _F32), meta.astype(_F32)], axis=0)

    h0, rq, rk, rv, rg, fq, fk, fv = _inproj(x2d, g0, b0, w_all, bf_pad, cos_x, sin_x, nb=nb, meta=False)
    _, _, rk_m, rv_m, _, _, fk_m, fv_m = _inproj(meta_blk, g0, b0, w_all, bf_pad, cos_m, sin_m, nb=1, meta=True)

    dmask, xi, zeta, gch = _decay_tables()
    per_batch = lambda a: a.reshape(nb, s, a.shape[-1])
    ret = _retention(per_batch(rq), per_batch(rk), per_batch(rv), per_batch(rg), rk_m, rv_m, dmask, xi, zeta,
                     gch).reshape(n, HEAD_W)
    fox = _fox(per_batch(fq), per_batch(fk), per_batch(fv), fk_m, fv_m, nb=nb).reshape(n, HEAD_W)

    w_gu = jnp.concatenate([ws_gate[0], ws_up[0]], axis=1).astype(_BF16)
    trow, sel, gates, rank, cnt = _post(
        h0, ret, fox, w_out[0].astype(_BF16), row2(ln1_g[0]), row2(ln1_b[0]),
        jnp.transpose(w_router[0]).astype(_BF16), router_bias[0].astype(_F32).reshape(N_EXPERTS, 1))

    nblk = n * TOP_K // EBLK + N_EXPERTS
    nblk_pad = -(-nblk // LANES) * LANES
    dest, blk_e, fill, used, tab = _plan(sel, rank, cnt, nblk_pad)
    dest_flat = jnp.transpose(dest).reshape(-1)
    used = used.reshape(-1)

    xs, base = _dispatch(dest_flat, tab, used, trow, w_gu, ws_down[0].astype(_BF16), nblk * EBLK)
    y = _experts(blk_e.reshape(-1), used, fill.reshape(-1), xs, we_gate[0], we_up[0], we_down[0])
    out = _combine(dest_flat, y, base, jnp.transpose(gates), row2(ln2_g[0]), row2(ln2_b[0]))
    return out.reshape(nb, s, d)
```

```python
import functools

import jax
import jax.numpy as jnp
import numpy as np
from jax import lax
from jax.experimental import pallas as pl
from jax.experimental.pallas import tpu as pltpu

D_MODEL = 1024
N_META = 16
BLOCK = 128
PAD = BLOCK - N_META
RET_HEADS = 4
RET_DK = 128
FOX_HEADS = 8
FOX_HD = 64
N_EXPERTS = 64
TOP_K = 6
N_GROUPS = 8
GROUP_SIZE = N_EXPERTS // N_GROUPS
TOPK_GROUPS = 4
EXPERT_FF = 256
SHARED_FF = 256
ROUTED_SCALE = 2.5
ROPE_BASE = 10000.0
LN_EPS = 1e-5
NEG_INF = -1e30
ALPHA = 2.0 ** 0.25
HEAD_W = 512
LOG2E = 1.4426950408889634

LANES = 128
FOX_W = FOX_HEADS * LANES
SUBLANES = 8
ROW_TILES = D_MODEL // LANES
ROW_DTYPE = jnp.float32

TM_PROJ = 512
T_FOX = 512
FOX_Q_TILES = 4
FOX_CHUNK = 64
FOX_K_TERMS = 32
TT = 256
TT_DISPATCH = 512
EBLK = 1024
ISSUE_UNROLL = 4
TAB_PAD0, TAB_PADN = 0, 1
V7X_VMEM_BYTES = 64 * 1024 * 1024
VMEM_LIMIT = V7X_VMEM_BYTES * 3 // 4

_F32 = jnp.float32
_BF16 = jnp.bfloat16


def _ln(x, g, b):
    xc = x - jnp.mean(x, -1, keepdims=True)
    var = jnp.mean(xc * xc, -1, keepdims=True)
    return xc * lax.rsqrt(var + LN_EPS) * g + b


def _dot(a, b):
    return jnp.dot(a, b, preferred_element_type=_F32)


def _dot_nt(a, b):
    return lax.dot_general(a, b, (((1,), (1,)), ((), ())), preferred_element_type=_F32)


def _dot_tn(a, b):
    return lax.dot_general(a, b, (((0,), (0,)), ((), ())), preferred_element_type=_F32)


def _silu(x):
    return x * jax.nn.sigmoid(x)


def _params(n_axes):
    return pltpu.CompilerParams(dimension_semantics=("arbitrary",) * n_axes, vmem_limit_bytes=VMEM_LIMIT)


def _inproj_body(x_ref, g_ref, b_ref, w_ref, bf_ref, cos_ref, sin_ref, own_ref, tq_ref, tk_ref, oq_ref, ok_ref, ov_ref,
                 h_ref, rq_ref, rk_ref, rv_ref, rg_ref, fq_ref, fk_ref, fv_ref, carry_ref, *, meta):
    tm = x_ref.shape[0]
    if not meta:
        @pl.when(pl.program_id(1) == 0)
        def _():
            carry_ref[...] = jnp.zeros_like(carry_ref)

    h = _ln(x_ref[...], g_ref[...], b_ref[...])
    if meta:
        valid = lax.broadcasted_iota(jnp.int32, (tm, 1), 0) >= PAD
        h = jnp.where(valid, h, 0.0)
    h_ref[...] = h
    hb = h.astype(_BF16)
    cos = cos_ref[...]
    sin = sin_ref[...]

    def proj(g):
        return _dot(hb, w_ref[:, g * HEAD_W:(g + 1) * HEAD_W])

    def rope_store(p, out_ref, scale):
        for hd in range(RET_HEADS):
            t = p[:, hd * LANES:(hd + 1) * LANES]
            r = t * cos + pltpu.roll(t, LANES // 2, axis=1) * sin
            out_ref[:, hd * LANES:(hd + 1) * LANES] = (r * scale).astype(_BF16)

    z = _dot(hb, w_ref[:, 7 * HEAD_W:7 * HEAD_W + LANES]) + bf_ref[...]
    rope_store(proj(0), rq_ref, 1.0)
    logf = jnp.minimum(z, 0.0) - jnp.log1p(jnp.exp(-jnp.abs(z)))
    if meta:
        logf = jnp.where(valid, logf, 0.0)
    l1 = logf.astype(_BF16)
    r1 = logf - l1.astype(_F32)
    l2 = r1.astype(_BF16)
    l3 = (r1 - l2.astype(_F32)).astype(_BF16)
    row = lax.broadcasted_iota(jnp.int32, (tm, tm), 0)
    col = lax.broadcasted_iota(jnp.int32, (tm, tm), 1)
    tri = (col <= row).astype(_BF16)
    rope_store(proj(1), rk_ref, RET_DK ** -0.5)
    c = _dot(tri, l1) + _dot(tri, l2) + _dot(tri, l3)
    rv_ref[...] = proj(2).astype(_BF16)
    rg_ref[...] = _silu(proj(3)).astype(_BF16)
    if meta:
        c = c - c[tm - 1:tm, :]
    else:
        c = c + carry_ref[...]
        carry_ref[...] = c[tm - 1:tm, :]

    head_lane = lax.broadcasted_iota(jnp.int32, (1, LANES), 1) < FOX_HEADS
    cl = jnp.where(head_lane, c, 0.0) * LOG2E
    c1 = cl.astype(_BF16).astype(_F32)
    r1 = cl - c1
    c2 = r1.astype(_BF16).astype(_F32)
    c3 = (r1 - c2).astype(_BF16).astype(_F32)
    csplit = c1 + pltpu.roll(c2, FOX_HEADS, axis=1) + pltpu.roll(c3, 2 * FOX_HEADS, axis=1)
    half = FOX_HD
    q_even, q_odd = pltpu.roll(csplit, half, axis=1), csplit
    k_even, k_odd = pltpu.roll(csplit, half + FOX_K_TERMS, axis=1), pltpu.roll(csplit, FOX_K_TERMS, axis=1)
    own = own_ref[...] > 0.0

    def per_head(p):
        return jnp.concatenate([p[:, (hd // 2) * LANES:(hd // 2 + 1) * LANES] for hd in range(FOX_HEADS)], axis=1)

    def by_parity(even, odd):
        return jnp.concatenate([even, odd] * (FOX_HEADS // 2), axis=1)

    q_extra = jnp.where(tq_ref[...] > 0.0, by_parity(q_even, q_odd), oq_ref[...])
    k_extra = jnp.where(tk_ref[...] > 0.0, -by_parity(k_even, k_odd), ok_ref[...])
    fq_ref[...] = jnp.where(own, per_head(proj(4) * (FOX_HD ** -0.5 * LOG2E)), q_extra).astype(_BF16)
    fk_ref[...] = jnp.where(own, per_head(proj(5)), k_extra).astype(_BF16)
    fv_ref[...] = jnp.where(own, per_head(proj(6)), ov_ref[...]).astype(_BF16)


def _fox_lane_tables():
    own, tq, tk, oq, ok, ov = (np.zeros((1, FOX_W), np.float32) for _ in range(6))
    for hd in range(FOX_HEADS):
        data = hd * LANES + (hd % 2) * FOX_HD
        extra = hd * LANES + (1 - hd % 2) * FOX_HD
        own[0, data:data + FOX_HD] = 1.0
        for term in range(3):
            lane = extra + term * FOX_HEADS + hd
            tq[0, lane] = 1.0
            ok[0, lane] = 1.0
            tk[0, lane + FOX_K_TERMS] = 1.0
            oq[0, lane + FOX_K_TERMS] = 1.0
        ov[0, extra] = 1.0
    return tuple(jnp.asarray(t) for t in (own, tq, tk, oq, ok, ov))


def _inproj(x2d, ln_g, ln_b, w_all, bf_pad, cos_t, sin_t, *, nb, meta):
    n = x2d.shape[0]
    s = n // nb
    tm = min(TM_PROJ, s)
    nj = s // tm
    row_spec = lambda w: pl.BlockSpec((tm, w), lambda b, j: (b * nj + j, 0))
    const = lambda shape: pl.BlockSpec(shape, lambda b, j: (0, 0))
    pos_spec = pl.BlockSpec((tm, LANES), lambda b, j: (j, 0))
    tables = _fox_lane_tables()
    outs = ([jax.ShapeDtypeStruct((n, D_MODEL), _F32)] + [jax.ShapeDtypeStruct((n, HEAD_W), _BF16)] * 4
            + [jax.ShapeDtypeStruct((n, FOX_W), _BF16)] * 3)
    return pl.pallas_call(
        functools.partial(_inproj_body, meta=meta),
        grid=(nb, nj),
        in_specs=[row_spec(D_MODEL), const((1, D_MODEL)), const((1, D_MODEL)), const(w_all.shape),
                  const((1, LANES)), pos_spec, pos_spec] + [const(t.shape) for t in tables],
        out_specs=[row_spec(D_MODEL)] + [row_spec(HEAD_W)] * 4 + [row_spec(FOX_W)] * 3,
        out_shape=outs,
        scratch_shapes=[pltpu.VMEM((1, LANES), _F32)],
        compiler_params=_params(2),
        name="inproj_meta" if meta else "inproj",
    )(x2d, ln_g, ln_b, w_all, bf_pad, cos_t, sin_t, *tables)


def _ret_body(q_ref, k_ref, v_ref, g_ref, km_ref, vm_ref, dm_ref, xi_ref, zeta_ref, gch_ref, o_ref, st_ref):
    def kv_update(k, v, hd):
        vz = (v.astype(_F32) * zeta_ref[hd]).astype(_BF16)
        return _dot_tn(k, vz)

    nb = q_ref.shape[0]

    @pl.when(pl.program_id(0) == 0)
    def _():
        for hd in range(RET_HEADS):
            sl = slice(hd * LANES, (hd + 1) * LANES)
            first = kv_update(km_ref[:, sl], vm_ref[:, sl], hd)
            for b in range(nb):
                st_ref[b, hd] = first

    group = 8
    for b0 in range(0, nb, group):
        chains = [(b, hd, slice(hd * LANES, (hd + 1) * LANES)) for b in range(b0, b0 + group) for hd in range(RET_HEADS)]
        scores = [_dot_nt(q_ref[b, :, sl], k_ref[b, :, sl]) * dm_ref[hd] for b, hd, sl in chains]
        cross = [_dot(q_ref[b, :, sl], st_ref[b, hd].astype(_BF16)) * xi_ref[hd] for b, hd, sl in chains]
        outs = [_dot(s.astype(_BF16), v_ref[b, :, sl]) + c for s, c, (b, hd, sl) in zip(scores, cross, chains)]
        for o, (b, hd, sl) in zip(outs, chains):
            oc = o - jnp.mean(o, -1, keepdims=True)
            y = oc * lax.rsqrt(jnp.mean(oc * oc, -1, keepdims=True) + LN_EPS)
            o_ref[b, :, sl] = (y * g_ref[b, :, sl].astype(_F32)).astype(_BF16)
        for b, hd, sl in chains:
            st_ref[b, hd] = gch_ref[hd] * st_ref[b, hd] + kv_update(k_ref[b, :, sl], v_ref[b, :, sl], hd)


def _retention(rq, rk, rv, rg, rk_m, rv_m, dmask, xi, zeta, gch):
    nb, s, _ = rq.shape
    row_spec = pl.BlockSpec((nb, BLOCK, HEAD_W), lambda j: (0, j, 0))
    meta_spec = pl.BlockSpec((BLOCK, HEAD_W), lambda j: (0, 0))
    tab = pl.BlockSpec((RET_HEADS, BLOCK, BLOCK), lambda j: (0, 0, 0))
    return pl.pallas_call(
        _ret_body,
        grid=(s // BLOCK,),
        in_specs=[row_spec] * 4 + [meta_spec] * 2 + [tab] * 3 + [pl.BlockSpec(memory_space=pltpu.SMEM)],
        out_specs=row_spec,
        out_shape=jax.ShapeDtypeStruct((nb, s, HEAD_W), _BF16),
        scratch_shapes=[pltpu.VMEM((nb, RET_HEADS, RET_DK, LANES), _F32)],
        compiler_params=_params(1),
        name="retention",
    )(rq, rk, rv, rg, rk_m, rv_m, dmask, xi, zeta, gch)


def _fox_body(q_ref, k_ref, v_ref, km_ref, vm_ref, o_ref, m_sc, acc_sc, s_sc, p_sc):
    t = T_FOX
    nq = q_ref.shape[0] // t
    first_tile = nq * pl.program_id(2)

    heads = [slice(hh * LANES, (hh + 1) * LANES) for hh in range(2)]
    chains = [(qt, hh) for qt in range(nq) for hh in range(2)]
    chain_id = lambda qt, hh: 2 * qt + hh

    def logits(qt, hh, k, buf, tk):
        s_sc[buf, chain_id(qt, hh), :, :tk] = _dot_nt(q_ref[qt * t:(qt + 1) * t, heads[hh]], k)

    def update(qt, hh, v, buf, tk, mask, first):
        ch = chain_id(qt, hh)
        for c in range(t // FOX_CHUNK):
            rows = slice(c * FOX_CHUNK, (c + 1) * FOX_CHUNK)
            s = s_sc[buf, ch, rows, :tk]
            if mask is not None:
                s = mask(s, c)
            mx = jnp.max(s, axis=1, keepdims=True)
            if first:
                m_new = mx
            else:
                m_prev = m_sc[ch, rows, :]
                m_new = jnp.maximum(m_prev, mx)
                acc_sc[ch, rows, :] = jnp.exp2(m_prev - m_new) * acc_sc[ch, rows, :]
            p_sc[ch, rows, :tk] = jnp.exp2(s - m_new).astype(_BF16)
            m_sc[ch, rows, :] = m_new
        pv = _dot(p_sc[ch, :, :tk], v)
        if first:
            acc_sc[ch] = pv
        else:
            acc_sc[ch] += pv

    def meta_mask(s, c):
        key = lax.broadcasted_iota(jnp.int32, s.shape, 1)
        return jnp.where(key >= PAD, s, NEG_INF)

    def causal_mask(s, c):
        key = lax.broadcasted_iota(jnp.int32, s.shape, 1)
        query = lax.broadcasted_iota(jnp.int32, s.shape, 0) + c * FOX_CHUNK
        return jnp.where(key <= query, s, NEG_INF)

    def key_tile(ref, ki, hh):
        return ref[pl.ds(pl.multiple_of(ki * t, t), t), heads[hh]]

    for qt, hh in chains:
        logits(qt, hh, km_ref[:, heads[hh]], 1, BLOCK)
    for qt, hh in chains:
        logits(qt, hh, key_tile(k_ref, 0, hh), 0, t)
        update(qt, hh, vm_ref[:, heads[hh]], 1, BLOCK, meta_mask, True)

    def step(ki, buf, active, mask_of, prefetch):
        for qt, hh in active:
            if (qt, hh) in prefetch:
                logits(qt, hh, key_tile(k_ref, ki + 1, hh), 1 - buf, t)
            update(qt, hh, key_tile(v_ref, ki, hh), buf, t, mask_of(qt), False)

    def pair_body(j, carry):
        step(2 * j, 0, chains, lambda qt: None, chains)
        step(2 * j + 1, 1, chains, lambda qt: None, chains)
        return carry

    lax.fori_loop(0, (nq // 2) * pl.program_id(2), pair_body, 0)
    for r in range(nq):
        active = [(qt, hh) for qt, hh in chains if qt >= r]
        step(first_tile + r, r % 2, active, lambda qt, r=r: causal_mask if qt == r else None,
             [(qt, hh) for qt, hh in active if qt > r])

    lane = lax.broadcasted_iota(jnp.int32, (t, LANES), 1)
    for qt in range(nq):
        outs = []
        for hh in range(2):
            acc = acc_sc[chain_id(qt, hh)]
            ones_lane = (1 - hh) * FOX_HD
            outs.append(acc / acc[:, ones_lane:ones_lane + 1])
        o_ref[qt * t:(qt + 1) * t, :] = jnp.where(lane < FOX_HD, outs[0], outs[1]).astype(_BF16)


def _fox(fq, fk, fv, fk_m, fv_m, *, nb):
    s = fq.shape[1]
    t = T_FOX
    pair_w = 2 * LANES
    n_chain = 2 * FOX_Q_TILES
    return pl.pallas_call(
        _fox_body,
        grid=(nb, FOX_HEADS // 2, s // (FOX_Q_TILES * t)),
        in_specs=[
            pl.BlockSpec((None, FOX_Q_TILES * t, pair_w), lambda b, p, i: (b, i, p)),
            pl.BlockSpec((None, s, pair_w), lambda b, p, i: (b, 0, p)),
            pl.BlockSpec((None, s, pair_w), lambda b, p, i: (b, 0, p)),
            pl.BlockSpec((BLOCK, pair_w), lambda b, p, i: (0, p)),
            pl.BlockSpec((BLOCK, pair_w), lambda b, p, i: (0, p)),
        ],
        out_specs=pl.BlockSpec((None, FOX_Q_TILES * t, LANES), lambda b, p, i: (b, i, p)),
        out_shape=jax.ShapeDtypeStruct((nb, s, HEAD_W), _BF16),
        scratch_shapes=[pltpu.VMEM((n_chain, t, 1), _F32), pltpu.VMEM((n_chain, t, LANES), _F32),
                        pltpu.VMEM((2, n_chain, t, t), _F32), pltpu.VMEM((n_chain, t, t), _BF16)],
        compiler_params=_params(3),
        name="fox",
    )(fq, fk, fv, fk_m, fv_m)


def _route(scores, biased):
    w = scores.shape[1]
    sub = lax.broadcasted_iota(jnp.int32, (GROUP_SIZE, w), 0).astype(_F32)
    groups = [biased[g * GROUP_SIZE:(g + 1) * GROUP_SIZE, :] for g in range(N_GROUPS)]
    gscore = []
    for v in groups:
        m1 = jnp.max(v, axis=0, keepdims=True)
        i1 = jnp.min(jnp.where(v == m1, sub, float(GROUP_SIZE)), axis=0, keepdims=True)
        m2 = jnp.max(jnp.where(sub == i1, -jnp.inf, v), axis=0, keepdims=True)
        gscore.append(m1 + m2)
    masked = []
    for g in range(N_GROUPS):
        beaten = jnp.zeros((1, w), _F32)
        for o in range(N_GROUPS):
            if o == g:
                continue
            wins = (gscore[o] >= gscore[g]) if o < g else (gscore[o] > gscore[g])
            beaten = beaten + wins.astype(_F32)
        masked.append(jnp.where(beaten < float(TOPK_GROUPS), groups[g], NEG_INF))
    work = jnp.concatenate(masked, axis=0)
    eid = lax.broadcasted_iota(jnp.int32, (N_EXPERTS, w), 0).astype(_F32)
    sels, raws = [], []
    member = None
    for _ in range(TOP_K):
        m = jnp.max(work, axis=0, keepdims=True)
        idx = jnp.min(jnp.where(work == m, eid, float(N_EXPERTS)), axis=0, keepdims=True)
        hot = eid == idx
        sels.append(idx)
        raws.append(jnp.sum(jnp.where(hot, scores, 0.0), axis=0, keepdims=True))
        work = jnp.where(hot, -jnp.inf, work)
        member = hot if member is None else (member | hot)
    return sels, raws, member


def _post_body(h0_ref, ret_ref, fox_ref, wo_ref, g1_ref, b1_ref, wr_ref, rb_ref,
               trow_ref, sel_ref, gate_ref, rank_ref, cnt_ref, run_ref):
    tm = h0_ref.shape[0]
    i = pl.program_id(0)

    @pl.when(i == 0)
    def _():
        run_ref[...] = jnp.zeros_like(run_ref)

    parts = [slice(c * tm // 4, (c + 1) * tm // 4) for c in range(4)]
    ys = [_dot(ret_ref[rs, :], wo_ref[:HEAD_W, :]) + _dot(fox_ref[rs, :], wo_ref[HEAD_W:, :]) for rs in parts]
    h1s = [_ln(ALPHA * h0_ref[rs, :] + y, g1_ref[...], b1_ref[...]) for rs, y in zip(parts, ys)]
    for rs, h1 in zip(parts, h1s):
        _store_rows(trow_ref, h1, rs.start)

    scores = jnp.concatenate([jax.nn.sigmoid(_dot_nt(wr_ref[...], h1.astype(_BF16))) for h1 in h1s], axis=1)
    biased = scores + rb_ref[...]
    chunks = [slice(c * LANES, (c + 1) * LANES) for c in range(tm // LANES)]
    routed = [_route(scores[:, cs], biased[:, cs]) for cs in chunks]
    member_f = jnp.concatenate([member.astype(_F32) for _, _, member in routed], axis=1)
    row = lax.broadcasted_iota(jnp.int32, (tm, tm), 0)
    col = lax.broadcasted_iota(jnp.int32, (tm, tm), 1)
    before = (row < col).astype(_BF16)
    rank_e = _dot(member_f.astype(_BF16), before) + run_ref[...]
    eid = lax.broadcasted_iota(jnp.int32, (N_EXPERTS, LANES), 0).astype(_F32)
    sel_rows, gate_rows, rank_rows = [], [], []
    for k in range(TOP_K):
        sel_k, gate_k, rank_k = [], [], []
        for cs, (sels, raws, _) in zip(chunks, routed):
            total = raws[0]
            for r in raws[1:]:
                total = total + r
            sel_k.append(sels[k].astype(jnp.int32))
            gate_k.append(raws[k] * (ROUTED_SCALE / total))
            rank_k.append(jnp.sum(jnp.where(eid == sels[k], rank_e[:, cs], 0.0), axis=0, keepdims=True).astype(jnp.int32))
        sel_rows.append(jnp.concatenate(sel_k, axis=1))
        gate_rows.append(jnp.concatenate(gate_k, axis=1))
        rank_rows.append(jnp.concatenate(rank_k, axis=1))
    pad_rows = sel_ref.shape[0] - TOP_K
    sel_ref[...] = jnp.concatenate(sel_rows + [jnp.zeros((pad_rows, tm), jnp.int32)], axis=0)
    gate_ref[...] = jnp.concatenate(gate_rows + [jnp.zeros((pad_rows, tm), _F32)], axis=0)
    rank_ref[...] = jnp.concatenate(rank_rows + [jnp.zeros((pad_rows, tm), jnp.int32)], axis=0)
    run_ref[...] = run_ref[...] + jnp.sum(member_f, axis=1, keepdims=True)
    cnt_ref[...] = run_ref[...]


def _post(h0, ret, fox, w_out, ln1_g, ln1_b, w_rt, rbias):
    n = h0.shape[0]
    tm = TM_PROJ
    row_spec = lambda w: pl.BlockSpec((tm, w), lambda i: (i, 0))
    col_spec = pl.BlockSpec((SUBLANES, tm), lambda i: (0, i))
    const = lambda shape: pl.BlockSpec(shape, lambda i: (0, 0))
    return pl.pallas_call(
        _post_body,
        grid=(n // tm,),
        in_specs=[row_spec(D_MODEL), row_spec(HEAD_W), row_spec(HEAD_W),
                  const(w_out.shape), const((1, D_MODEL)), const((1, D_MODEL)), const(w_rt.shape),
                  const((N_EXPERTS, 1))],
        out_specs=[pl.BlockSpec((tm * ROW_TILES, LANES), lambda i: (i, 0)),
                   col_spec, col_spec, col_spec, const((N_EXPERTS, 1))],
        out_shape=[jax.ShapeDtypeStruct((n * ROW_TILES, LANES), ROW_DTYPE),
                   jax.ShapeDtypeStruct((SUBLANES, n), jnp.int32), jax.ShapeDtypeStruct((SUBLANES, n), _F32),
                   jax.ShapeDtypeStruct((SUBLANES, n), jnp.int32), jax.ShapeDtypeStruct((N_EXPERTS, 1), _F32)],
        scratch_shapes=[pltpu.VMEM((N_EXPERTS, 1), _F32)],
        compiler_params=_params(1),
        name="post_mixer",
    )(h0, ret, fox, w_out, ln1_g, ln1_b, w_rt, rbias)


def _plan_body(sel_ref, rank_ref, cnt_ref, dest_ref, blk_ref, fill_ref, used_ref, tab_ref):
    cnt = cnt_ref[...]
    padded = jnp.ceil(cnt * (1.0 / EBLK)) * EBLK
    er = lax.broadcasted_iota(jnp.int32, (N_EXPERTS, N_EXPERTS), 0)
    ec = lax.broadcasted_iota(jnp.int32, (N_EXPERTS, N_EXPERTS), 1)
    padded_row = jnp.sum(jnp.where(er == ec, padded, 0.0), axis=0, keepdims=True)
    pstart = jnp.sum(jnp.where(ec < er, padded_row, 0.0), axis=1, keepdims=True)
    pend = pstart + padded
    sel = sel_ref[...]
    dest = rank_ref[...]
    for e in range(N_EXPERTS):
        dest = dest + jnp.where(sel == e, pstart[e:e + 1, :].astype(jnp.int32), 0)
    dest_ref[...] = dest
    nblk = blk_ref.shape[1]
    first_row = (lax.broadcasted_iota(jnp.int32, (N_EXPERTS, nblk), 1) * EBLK).astype(_F32)
    owner = jnp.minimum(jnp.sum((pend <= first_row).astype(_F32), axis=0, keepdims=True), N_EXPERTS - 1.0)
    blk_ref[...] = owner.astype(jnp.int32)
    mine = lax.broadcasted_iota(jnp.int32, (N_EXPERTS, nblk), 0).astype(_F32) == owner
    live_end = jnp.sum(jnp.where(mine, pstart + cnt, 0.0), axis=0, keepdims=True)
    fill_ref[...] = jnp.clip(live_end - first_row[:1, :], 0.0, float(EBLK)).astype(jnp.int32)
    used_ref[...] = (pend[N_EXPERTS - 1:, :] * (1.0 / EBLK)).astype(jnp.int32)
    as_row = lambda col: jnp.sum(jnp.where(er == ec, col, 0.0), axis=0, keepdims=True).astype(jnp.int32)
    rows = {TAB_PAD0: pstart + cnt, TAB_PADN: padded - cnt}
    blank = jnp.zeros((1, N_EXPERTS), jnp.int32)
    tab_ref[...] = jnp.concatenate([as_row(rows[r]) if r in rows else blank for r in range(SUBLANES)], axis=0)


def _plan(sel, rank, cnt, nblk_pad):
    n = sel.shape[1]
    full = lambda shape: pl.BlockSpec(shape, lambda i: (0, 0))
    return pl.pallas_call(
        _plan_body,
        grid=(1,),
        in_specs=[full(sel.shape), full(rank.shape), full(cnt.shape)],
        out_specs=[full(sel.shape), full((1, nblk_pad)), full((1, nblk_pad)), full((1, 1)),
                   full((SUBLANES, N_EXPERTS))],
        out_shape=[jax.ShapeDtypeStruct((SUBLANES, n), jnp.int32), jax.ShapeDtypeStruct((1, nblk_pad), jnp.int32),
                   jax.ShapeDtypeStruct((1, nblk_pad), jnp.int32), jax.ShapeDtypeStruct((1, 1), jnp.int32),
                   jax.ShapeDtypeStruct((SUBLANES, N_EXPERTS), jnp.int32)],
        compiler_params=_params(1),
        name="plan",
    )(sel, rank, cnt)


def _store_rows(ref, v, first_row=0):
    m = v.shape[0]
    for s in range(ROW_TILES):
        ref[pl.ds(first_row * ROW_TILES + s, m, stride=ROW_TILES), :] = v[:, s * LANES:(s + 1) * LANES].astype(ROW_DTYPE)


def _load_rows(ref, first_row, m):
    return jnp.concatenate([ref[pl.ds(first_row * ROW_TILES + s, m, stride=ROW_TILES), :] for s in range(ROW_TILES)],
                           axis=1)


def _row_copy(src, src_row, dst, dst_row, sem):
    return pltpu.make_async_copy(src.at[pl.ds(pl.multiple_of(src_row * ROW_TILES, ROW_TILES), ROW_TILES), :],
                                 dst.at[pl.ds(pl.multiple_of(dst_row * ROW_TILES, ROW_TILES), ROW_TILES), :], sem)


def _dispatch_body(dest_ref, tab_ref, used_ref, t_ref, wgu_ref, wd_ref, xs_ref, base_ref, zero_sc, sem, zsem):
    tt = t_ref.shape[0] // ROW_TILES
    half_blk = EBLK // 2
    n_half = xs_ref.shape[0] // (half_blk * ROW_TILES)

    @pl.when(pl.program_id(0) == 0)
    def _():
        zero_sc[...] = jnp.zeros_like(zero_sc)

        def zero_copy(first_row, rows):
            return pltpu.make_async_copy(
                zero_sc.at[pl.ds(0, rows * ROW_TILES), :],
                xs_ref.at[pl.ds(pl.multiple_of(first_row * ROW_TILES, ROW_TILES), rows * ROW_TILES), :], zsem)

        def for_padding(act):
            def per_expert(e, carry):
                row = tab_ref[TAB_PAD0, e]
                for bit in range(EBLK.bit_length() - 2, -1, -1):
                    take = (tab_ref[TAB_PADN, e] & (1 << bit)) != 0

                    @pl.when(take)
                    def _():
                        act(zero_copy(row, 1 << bit))

                    row = row + jnp.where(take, 1 << bit, 0)
                return carry

            def per_tail(hb, carry):
                act(zero_copy(hb * half_blk, half_blk))
                return carry

            lax.fori_loop(0, N_EXPERTS, per_expert, 0)
            lax.fori_loop(2 * used_ref[0], n_half, per_tail, 0)

        for_padding(lambda cp: cp.start())
        for_padding(lambda cp: cp.wait())

    def issue(j, carry):
        for u in range(ISSUE_UNROLL):
            i = ISSUE_UNROLL * j + u
            for k in range(TOP_K):
                _row_copy(t_ref, i, xs_ref, dest_ref[i * SUBLANES + k], sem).start(priority=k % 2)
        return carry

    lax.fori_loop(0, tt // ISSUE_UNROLL, issue, 0)

    h1 = _load_rows(t_ref, 0, tt)
    gu = _dot(h1.astype(_BF16), wgu_ref[...])
    mid = (_silu(gu[:, :SHARED_FF]) * gu[:, SHARED_FF:]).astype(_BF16)
    base_ref[...] = ALPHA * h1 + _dot(mid, wd_ref[...])

    for _ in range(TOP_K):
        pltpu.make_async_copy(t_ref, xs_ref.at[pl.ds(0, tt * ROW_TILES), :], sem).wait()


def _dispatch(dest_flat, tab, used, trow, w_gu, w_sd, total_rows):
    n = trow.shape[0] // ROW_TILES
    tt = TT_DISPATCH
    smem = pl.BlockSpec(memory_space=pltpu.SMEM)
    const = lambda shape: pl.BlockSpec(shape, lambda i: (0, 0))
    return pl.pallas_call(
        _dispatch_body,
        grid=(n // tt,),
        in_specs=[pl.BlockSpec((tt * SUBLANES,), lambda i: (i,), memory_space=pltpu.SMEM), smem, smem,
                  pl.BlockSpec((tt * ROW_TILES, LANES), lambda i: (i, 0)), const(w_gu.shape), const(w_sd.shape)],
        out_specs=[pl.BlockSpec(memory_space=pl.ANY), pl.BlockSpec((tt, D_MODEL), lambda i: (i, 0))],
        out_shape=[jax.ShapeDtypeStruct((total_rows * ROW_TILES, LANES), ROW_DTYPE),
                   jax.ShapeDtypeStruct((n, D_MODEL), _F32)],
        scratch_shapes=[pltpu.VMEM((EBLK // 2 * ROW_TILES, LANES), ROW_DTYPE), pltpu.SemaphoreType.DMA(()),
                        pltpu.SemaphoreType.DMA(())],
        compiler_params=_params(1),
        name="dispatch",
    )(dest_flat, tab, used, trow, w_gu, w_sd)


def _expert_body(blk_ref, used_ref, fill_ref, xs_ref, wg_ref, wu_ref, wd_ref, y_ref, wgu_sc, wd_sc):
    i = pl.program_id(0)
    prev = blk_ref[jnp.maximum(i - 1, 0)]
    fresh = (i == 0) | (blk_ref[i] != prev)

    @pl.when(fresh)
    def _():
        wgu_sc[:, :EXPERT_FF] = wg_ref[...].astype(_BF16)
        wgu_sc[:, EXPERT_FF:] = wu_ref[...].astype(_BF16)
        wd_sc[...] = wd_ref[...].astype(_BF16)

    @pl.when(i < used_ref[0])
    def _():
        live = lax.broadcasted_iota(jnp.int32, (EBLK, 1), 0) < fill_ref[i]
        x = _load_rows(xs_ref, 0, EBLK)
        x = jnp.where(live, x, jnp.zeros_like(x)).astype(_BF16)
        gu = _dot(x, wgu_sc[...])
        mid = (_silu(gu[:, :EXPERT_FF]) * gu[:, EXPERT_FF:]).astype(_BF16)
        _store_rows(y_ref, _dot(mid, wd_sc[...]))

    @pl.when(i >= used_ref[0])
    def _():
        y_ref[...] = jnp.zeros_like(y_ref)


def _experts(blk_e, used, fill, xs, we_gate, we_up, we_down):
    nblk = xs.shape[0] // (EBLK * ROW_TILES)
    last = lambda i, used: jnp.minimum(i, jnp.maximum(used[0] - 1, 0))
    w_spec = lambda shape: pl.BlockSpec((None,) + shape, lambda i, blk, used, fill: (blk[i], 0, 0))
    return pl.pallas_call(
        _expert_body,
        grid_spec=pltpu.PrefetchScalarGridSpec(
            num_scalar_prefetch=3,
            grid=(nblk,),
            in_specs=[pl.BlockSpec((EBLK * ROW_TILES, LANES), lambda i, blk, used, fill: (last(i, used), 0)),
                      w_spec((D_MODEL, EXPERT_FF)), w_spec((D_MODEL, EXPERT_FF)), w_spec((EXPERT_FF, D_MODEL))],
            out_specs=pl.BlockSpec((EBLK * ROW_TILES, LANES), lambda i, blk, used, fill: (i, 0)),
            scratch_shapes=[pltpu.VMEM((D_MODEL, 2 * EXPERT_FF), _BF16), pltpu.VMEM((EXPERT_FF, D_MODEL), _BF16)],
        ),
        out_shape=jax.ShapeDtypeStruct(xs.shape, ROW_DTYPE),
        compiler_params=_params(1),
        name="experts",
    )(blk_e, used, fill, xs, we_gate, we_up, we_down)


def _combine_body(dest_ref, dnext_ref, y_ref, base_ref, gate_ref, g2_ref, b2_ref, o_ref, z_sc, sems):
    i = pl.program_id(0)
    slot = lax.rem(i, 2)
    tile_rows = TT * TOP_K * ROW_TILES

    def gather(dref, into):
        def issue(j, carry):
            for u in range(ISSUE_UNROLL):
                t = ISSUE_UNROLL * j + u
                for k in range(TOP_K):
                    _row_copy(y_ref, dref[t * SUBLANES + k], z_sc.at[into], k * TT + t,
                              sems.at[into]).start(priority=k % 2)
            return carry

        lax.fori_loop(0, TT // ISSUE_UNROLL, issue, 0)

    @pl.when(i == 0)
    def _():
        gather(dest_ref, 0)

    @pl.when(i + 1 < pl.num_programs(0))
    def _():
        gather(dnext_ref, 1 - slot)

    pltpu.make_async_copy(y_ref.at[pl.ds(0, tile_rows), :], z_sc.at[slot], sems.at[slot]).wait()
    gates = gate_ref[...]
    acc = base_ref[...]
    for k in range(TOP_K):
        acc = acc + gates[:, k:k + 1] * _load_rows(z_sc.at[slot], k * TT, TT).astype(_F32)
    o_ref[...] = _ln(acc, g2_ref[...], b2_ref[...])


def _combine(dest_flat, y, base, gates_t, ln2_g, ln2_b):
    n = base.shape[0]
    steps = n // TT
    const = lambda shape: pl.BlockSpec(shape, lambda i: (0, 0))
    dest_spec = lambda ahead: pl.BlockSpec((TT * SUBLANES,), lambda i: (jnp.minimum(i + ahead, steps - 1),),
                                           memory_space=pltpu.SMEM)
    return pl.pallas_call(
        _combine_body,
        grid=(steps,),
        in_specs=[dest_spec(0), dest_spec(1),
                  pl.BlockSpec(memory_space=pl.ANY),
                  pl.BlockSpec((TT, D_MODEL), lambda i: (i, 0)),
                  pl.BlockSpec((TT, SUBLANES), lambda i: (i, 0)),
                  const((1, D_MODEL)), const((1, D_MODEL))],
        out_specs=pl.BlockSpec((TT, D_MODEL), lambda i: (i, 0)),
        out_shape=jax.ShapeDtypeStruct((n, D_MODEL), _F32),
        scratch_shapes=[pltpu.VMEM((2, TT * TOP_K * ROW_TILES, LANES), ROW_DTYPE), pltpu.SemaphoreType.DMA((2,))],
        compiler_params=_params(1),
        name="combine",
    )(dest_flat, dest_flat, y, base, gates_t, ln2_g, ln2_b)


def _rope_tables(pos):
    half = RET_DK // 2
    inv = ROPE_BASE ** (-jnp.arange(half, dtype=_F32) / half)
    ang = pos[:, None] * inv[None, :]
    cos = jnp.cos(ang)
    sin = jnp.sin(ang)
    return jnp.concatenate([cos, cos], -1), jnp.concatenate([-sin, sin], -1)


def _decay_tables():
    lg = jnp.log1p(-jnp.exp2(-5.0 - jnp.arange(RET_HEADS, dtype=_F32)))
    idx = jnp.arange(BLOCK, dtype=_F32)
    rel = idx[:, None] - idx[None, :]
    causal = rel >= 0
    dmask = jnp.where(causal[None], jnp.exp(jnp.where(causal, rel, 0.0)[None] * lg[:, None, None]), 0.0)
    zeta = jnp.exp((BLOCK - 1.0 - idx)[None, :] * lg[:, None])
    xi = jnp.exp((idx + 1.0)[None, :] * lg[:, None])
    along_lanes = lambda col: jnp.broadcast_to(col[:, :, None], (RET_HEADS, BLOCK, LANES))
    return dmask, along_lanes(xi), along_lanes(zeta), jnp.exp(BLOCK * lg)


def kernel(x, meta, ln0_g, ln0_b, w_in, b_forget, w_out, ln1_g, ln1_b, w_router, router_bias, we_gate, we_up,
           we_down, ws_gate, ws_up, ws_down, ln2_g, ln2_b):
    nb, s, d = x.shape
    assert d == D_MODEL and meta.shape == (N_META, D_MODEL) and w_in.shape[0] == 1
    assert s % TM_PROJ == 0 and s % (FOX_Q_TILES * T_FOX) == 0 and (nb * s) % TT == 0
    n = nb * s
    x2d = x.reshape(n, d)
    row2 = lambda v: v.reshape(1, -1).astype(_F32)
    main_cols = 7 * HEAD_W
    w_all = jnp.concatenate(
        [w_in[0, :, :main_cols], w_in[0, :, main_cols:], jnp.zeros((d, LANES - FOX_HEADS), w_in.dtype)],
        axis=1).astype(_BF16)
    bf_pad = jnp.concatenate([b_forget[0].astype(_F32), jnp.zeros((LANES - FOX_HEADS,), _F32)]).reshape(1, LANES)
    g0, b0 = row2(ln0_g), row2(ln0_b)

    cos_x, sin_x = _rope_tables(jnp.arange(s, dtype=_F32) + float(N_META))
    cos_m, sin_m = _rope_tables(jnp.arange(BLOCK, dtype=_F32) - float(PAD))
    meta_blk = jnp.concatenate([jnp.zeros((PAD, d), _F32), meta.astype(_F32)], axis=0)

    h0, rq, rk, rv, rg, fq, fk, fv = _inproj(x2d, g0, b0, w_all, bf_pad, cos_x, sin_x, nb=nb, meta=False)
    _, _, rk_m, rv_m, _, _, fk_m, fv_m = _inproj(meta_blk, g0, b0, w_all, bf_pad, cos_m, sin_m, nb=1, meta=True)

    dmask, xi, zeta, gch = _decay_tables()
    per_batch = lambda a: a.reshape(nb, s, a.shape[-1])
    ret = _retention(per_batch(rq), per_batch(rk), per_batch(rv), per_batch(rg), rk_m, rv_m, dmask, xi, zeta,
                     gch).reshape(n, HEAD_W)
    fox = _fox(per_batch(fq), per_batch(fk), per_batch(fv), fk_m, fv_m, nb=nb).reshape(n, HEAD_W)

    w_gu = jnp.concatenate([ws_gate[0], ws_up[0]], axis=1).astype(_BF16)
    trow, sel, gates, rank, cnt = _post(
        h0, ret, fox, w_out[0].astype(_BF16), row2(ln1_g[0]), row2(ln1_b[0]),
        jnp.transpose(w_router[0]).astype(_BF16), router_bias[0].astype(_F32).reshape(N_EXPERTS, 1))

    nblk = n * TOP_K // EBLK + N_EXPERTS
    nblk_pad = -(-nblk // LANES) * LANES
    dest, blk_e, fill, used, tab = _plan(sel, rank, cnt, nblk_pad)
    dest_flat = jnp.transpose(dest).reshape(-1)
    used = used.reshape(-1)

    xs, base = _dispatch(dest_flat, tab, used, trow, w_gu, ws_down[0].astype(_BF16), nblk * EBLK)
    y = _experts(blk_e.reshape(-1), used, fill.reshape(-1), xs, we_gate[0], we_up[0], we_down[0])
    out = _combine(dest_flat, y, base, jnp.transpose(gates), row2(ln2_g[0]), row2(ln2_b[0]))
    return out.reshape(nb, s, d)
```

```python
import functools

import jax
import jax.numpy as jnp
import numpy as np
from jax import lax
from jax.experimental import pallas as pl
from jax.experimental.pallas import tpu as pltpu

D_MODEL = 1024
N_META = 16
BLOCK = 128
PAD = BLOCK - N_META
RET_HEADS = 4
RET_DK = 128
FOX_HEADS = 8
FOX_HD = 64
N_EXPERTS = 64
TOP_K = 6
N_GROUPS = 8
GROUP_SIZE = N_EXPERTS // N_GROUPS
TOPK_GROUPS = 4
EXPERT_FF = 256
SHARED_FF = 256
ROUTED_SCALE = 2.5
ROPE_BASE = 10000.0
LN_EPS = 1e-5
NEG_INF = -1e30
ALPHA = 2.0 ** 0.25
HEAD_W = 512
LOG2E = 1.4426950408889634

LANES = 128
FOX_W = FOX_HEADS * LANES
SUBLANES = 8
ROW_TILES = D_MODEL // LANES
ROW_DTYPE = jnp.float32

TM_PROJ = 512
T_FOX = 512
FOX_Q_TILES = 4
FOX_CHUNK = 64
FOX_K_TERMS = 32
TT = 256
TT_DISPATCH = 512
EBLK = 1024
ISSUE_UNROLL = 4
TAB_PAD0, TAB_PADN = 0, 1
V7X_VMEM_BYTES = 64 * 1024 * 1024
VMEM_LIMIT = V7X_VMEM_BYTES * 3 // 4

_F32 = jnp.float32
_BF16 = jnp.bfloat16


def _ln(x, g, b):
    xc = x - jnp.mean(x, -1, keepdims=True)
    var = jnp.mean(xc * xc, -1, keepdims=True)
    return xc * lax.rsqrt(var + LN_EPS) * g + b


def _dot(a, b):
    return jnp.dot(a, b, preferred_element_type=_F32)


def _dot_nt(a, b):
    return lax.dot_general(a, b, (((1,), (1,)), ((), ())), preferred_element_type=_F32)


def _dot_tn(a, b):
    return lax.dot_general(a, b, (((0,), (0,)), ((), ())), preferred_element_type=_F32)


def _silu(x):
    return x * jax.nn.sigmoid(x)


def _params(n_axes):
    return pltpu.CompilerParams(dimension_semantics=("arbitrary",) * n_axes, vmem_limit_bytes=VMEM_LIMIT)


def _inproj_body(x_ref, g_ref, b_ref, w_ref, bf_ref, cos_ref, sin_ref, own_ref, tq_ref, tk_ref, oq_ref, ok_ref, ov_ref,
                 h_ref, rq_ref, rk_ref, rv_ref, rg_ref, fq_ref, fk_ref, fv_ref, carry_ref, *, meta):
    tm = x_ref.shape[0]
    if not meta:
        @pl.when(pl.program_id(1) == 0)
        def _():
            carry_ref[...] = jnp.zeros_like(carry_ref)

    h = _ln(x_ref[...], g_ref[...], b_ref[...])
    if meta:
        valid = lax.broadcasted_iota(jnp.int32, (tm, 1), 0) >= PAD
        h = jnp.where(valid, h, 0.0)
    h_ref[...] = h
    hb = h.astype(_BF16)
    cos = cos_ref[...]
    sin = sin_ref[...]

    def proj(g):
        return _dot(hb, w_ref[:, g * HEAD_W:(g + 1) * HEAD_W])

    def rope_store(p, out_ref, scale):
        for hd in range(RET_HEADS):
            t = p[:, hd * LANES:(hd + 1) * LANES]
            r = t * cos + pltpu.roll(t, LANES // 2, axis=1) * sin
            out_ref[:, hd * LANES:(hd + 1) * LANES] = (r * scale).astype(_BF16)

    z = _dot(hb, w_ref[:, 7 * HEAD_W:7 * HEAD_W + LANES]) + bf_ref[...]
    rope_store(proj(0), rq_ref, 1.0)
    logf = jnp.minimum(z, 0.0) - jnp.log1p(jnp.exp(-jnp.abs(z)))
    if meta:
        logf = jnp.where(valid, logf, 0.0)
    l1 = logf.astype(_BF16)
    r1 = logf - l1.astype(_F32)
    l2 = r1.astype(_BF16)
    l3 = (r1 - l2.astype(_F32)).astype(_BF16)
    row = lax.broadcasted_iota(jnp.int32, (tm, tm), 0)
    col = lax.broadcasted_iota(jnp.int32, (tm, tm), 1)
    tri = (col <= row).astype(_BF16)
    rope_store(proj(1), rk_ref, RET_DK ** -0.5)
    c = _dot(tri, l1) + _dot(tri, l2) + _dot(tri, l3)
    rv_ref[...] = proj(2).astype(_BF16)
    rg_ref[...] = _silu(proj(3)).astype(_BF16)
    if meta:
        c = c - c[tm - 1:tm, :]
    else:
        c = c + carry_ref[...]
        carry_ref[...] = c[tm - 1:tm, :]

    head_lane = lax.broadcasted_iota(jnp.int32, (1, LANES), 1) < FOX_HEADS
    cl = jnp.where(head_lane, c, 0.0) * LOG2E
    c1 = cl.astype(_BF16).astype(_F32)
    r1 = cl - c1
    c2 = r1.astype(_BF16).astype(_F32)
    c3 = (r1 - c2).astype(_BF16).astype(_F32)
    csplit = c1 + pltpu.roll(c2, FOX_HEADS, axis=1) + pltpu.roll(c3, 2 * FOX_HEADS, axis=1)
    half = FOX_HD
    q_even, q_odd = pltpu.roll(csplit, half, axis=1), csplit
    k_even, k_odd = pltpu.roll(csplit, half + FOX_K_TERMS, axis=1), pltpu.roll(csplit, FOX_K_TERMS, axis=1)
    own = own_ref[...] > 0.0

    def per_head(p):
        return jnp.concatenate([p[:, (hd // 2) * LANES:(hd // 2 + 1) * LANES] for hd in range(FOX_HEADS)], axis=1)

    def by_parity(even, odd):
        return jnp.concatenate([even, odd] * (FOX_HEADS // 2), axis=1)

    q_extra = jnp.where(tq_ref[...] > 0.0, by_parity(q_even, q_odd), oq_ref[...])
    k_extra = jnp.where(tk_ref[...] > 0.0, -by_parity(k_even, k_odd), ok_ref[...])
    fq_ref[...] = jnp.where(own, per_head(proj(4) * (FOX_HD ** -0.5 * LOG2E)), q_extra).astype(_BF16)
    fk_ref[...] = jnp.where(own, per_head(proj(5)), k_extra).astype(_BF16)
    fv_ref[...] = jnp.where(own, per_head(proj(6)), ov_ref[...]).astype(_BF16)


def _fox_lane_tables():
    own, tq, tk, oq, ok, ov = (np.zeros((1, FOX_W), np.float32) for _ in range(6))
    for hd in range(FOX_HEADS):
        data = hd * LANES + (hd % 2) * FOX_HD
        extra = hd * LANES + (1 - hd % 2) * FOX_HD
        own[0, data:data + FOX_HD] = 1.0
        for term in range(3):
            lane = extra + term * FOX_HEADS + hd
            tq[0, lane] = 1.0
            ok[0, lane] = 1.0
            tk[0, lane + FOX_K_TERMS] = 1.0
            oq[0, lane + FOX_K_TERMS] = 1.0
        ov[0, extra] = 1.0
    return tuple(jnp.asarray(t) for t in (own, tq, tk, oq, ok, ov))


def _inproj(x2d, ln_g, ln_b, w_all, bf_pad, cos_t, sin_t, *, nb, meta):
    n = x2d.shape[0]
    s = n // nb
    tm = min(TM_PROJ, s)
    nj = s // tm
    row_spec = lambda w: pl.BlockSpec((tm, w), lambda b, j: (b * nj + j, 0))
    const = lambda shape: pl.BlockSpec(shape, lambda b, j: (0, 0))
    pos_spec = pl.BlockSpec((tm, LANES), lambda b, j: (j, 0))
    tables = _fox_lane_tables()
    outs = ([jax.ShapeDtypeStruct((n, D_MODEL), _F32)] + [jax.ShapeDtypeStruct((n, HEAD_W), _BF16)] * 4
            + [jax.ShapeDtypeStruct((n, FOX_W), _BF16)] * 3)
    return pl.pallas_call(
        functools.partial(_inproj_body, meta=meta),
        grid=(nb, nj),
        in_specs=[row_spec(D_MODEL), const((1, D_MODEL)), const((1, D_MODEL)), const(w_all.shape),
                  const((1, LANES)), pos_spec, pos_spec] + [const(t.shape) for t in tables],
        out_specs=[row_spec(D_MODEL)] + [row_spec(HEAD_W)] * 4 + [row_spec(FOX_W)] * 3,
        out_shape=outs,
        scratch_shapes=[pltpu.VMEM((1, LANES), _F32)],
        compiler_params=_params(2),
        name="inproj_meta" if meta else "inproj",
    )(x2d, ln_g, ln_b, w_all, bf_pad, cos_t, sin_t, *tables)


def _ret_body(q_ref, k_ref, v_ref, g_ref, km_ref, vm_ref, dm_ref, xi_ref, zeta_ref, gch_ref, o_ref, st_ref):
    def kv_update(k, v, hd):
        vz = (v.astype(_F32) * zeta_ref[hd]).astype(_BF16)
        return _dot_tn(k, vz)

    nb = q_ref.shape[0]

    @pl.when(pl.program_id(0) == 0)
    def _():
        for hd in range(RET_HEADS):
            sl = slice(hd * LANES, (hd + 1) * LANES)
            first = kv_update(km_ref[:, sl], vm_ref[:, sl], hd)
            for b in range(nb):
                st_ref[b, hd] = first

    group = 8
    for b0 in range(0, nb, group):
        chains = [(b, hd, slice(hd * LANES, (hd + 1) * LANES)) for b in range(b0, b0 + group) for hd in range(RET_HEADS)]
        scores = [_dot_nt(q_ref[b, :, sl], k_ref[b, :, sl]) * dm_ref[hd] for b, hd, sl in chains]
        cross = [_dot(q_ref[b, :, sl], st_ref[b, hd].astype(_BF16)) * xi_ref[hd] for b, hd, sl in chains]
        outs = [_dot(s.astype(_BF16), v_ref[b, :, sl]) + c for s, c, (b, hd, sl) in zip(scores, cross, chains)]
        for o, (b, hd, sl) in zip(outs, chains):
            oc = o - jnp.mean(o, -1, keepdims=True)
            y = oc * lax.rsqrt(jnp.mean(oc * oc, -1, keepdims=True) + LN_EPS)
            o_ref[b, :, sl] = (y * g_ref[b, :, sl].astype(_F32)).astype(_BF16)
        for b, hd, sl in chains:
            st_ref[b, hd] = gch_ref[hd] * st_ref[b, hd] + kv_update(k_ref[b, :, sl], v_ref[b, :, sl], hd)


def _retention(rq, rk, rv, rg, rk_m, rv_m, dmask, xi, zeta, gch):
    nb, s, _ = rq.shape
    row_spec = pl.BlockSpec((nb, BLOCK, HEAD_W), lambda j: (0, j, 0))
    meta_spec = pl.BlockSpec((BLOCK, HEAD_W), lambda j: (0, 0))
    tab = pl.BlockSpec((RET_HEADS, BLOCK, BLOCK), lambda j: (0, 0, 0))
    return pl.pallas_call(
        _ret_body,
        grid=(s // BLOCK,),
        in_specs=[row_spec] * 4 + [meta_spec] * 2 + [tab] * 3 + [pl.BlockSpec(memory_space=pltpu.SMEM)],
        out_specs=row_spec,
        out_shape=jax.ShapeDtypeStruct((nb, s, HEAD_W), _BF16),
        scratch_shapes=[pltpu.VMEM((nb, RET_HEADS, RET_DK, LANES), _F32)],
        compiler_params=_params(1),
        name="retention",
    )(rq, rk, rv, rg, rk_m, rv_m, dmask, xi, zeta, gch)


def _fox_body(q_ref, k_ref, v_ref, km_ref, vm_ref, o_ref, m_sc, acc_sc, s_sc, p_sc):
    t = T_FOX
    nq = q_ref.shape[0] // t
    first_tile = nq * pl.program_id(2)

    heads = [slice(hh * LANES, (hh + 1) * LANES) for hh in range(2)]
    chains = [(qt, hh) for qt in range(nq) for hh in range(2)]
    chain_id = lambda qt, hh: 2 * qt + hh

    def logits(qt, hh, k, buf, tk):
        s_sc[buf, chain_id(qt, hh), :, :tk] = _dot_nt(q_ref[qt * t:(qt + 1) * t, heads[hh]], k)

    def update(qt, hh, v, buf, tk, mask, first):
        ch = chain_id(qt, hh)
        for c in range(t // FOX_CHUNK):
            rows = slice(c * FOX_CHUNK, (c + 1) * FOX_CHUNK)
            s = s_sc[buf, ch, rows, :tk]
            if mask is not None:
                s = mask(s, c)
            mx = jnp.max(s, axis=1, keepdims=True)
            if first:
                m_new = mx
            else:
                m_prev = m_sc[ch, rows, :]
                m_new = jnp.maximum(m_prev, mx)
                acc_sc[ch, rows, :] = jnp.exp2(m_prev - m_new) * acc_sc[ch, rows, :]
            p_sc[ch, rows, :tk] = jnp.exp2(s - m_new).astype(_BF16)
            m_sc[ch, rows, :] = m_new
        pv = _dot(p_sc[ch, :, :tk], v)
        if first:
            acc_sc[ch] = pv
        else:
            acc_sc[ch] += pv

    def meta_mask(s, c):
        key = lax.broadcasted_iota(jnp.int32, s.shape, 1)
        return jnp.where(key >= PAD, s, NEG_INF)

    def causal_mask(s, c):
        key = lax.broadcasted_iota(jnp.int32, s.shape, 1)
        query = lax.broadcasted_iota(jnp.int32, s.shape, 0) + c * FOX_CHUNK
        return jnp.where(key <= query, s, NEG_INF)

    def key_tile(ref, ki, hh):
        return ref[pl.ds(pl.multiple_of(ki * t, t), t), heads[hh]]

    for qt, hh in chains:
        logits(qt, hh, km_ref[:, heads[hh]], 1, BLOCK)
    for qt, hh in chains:
        logits(qt, hh, key_tile(k_ref, 0, hh), 0, t)
        update(qt, hh, vm_ref[:, heads[hh]], 1, BLOCK, meta_mask, True)

    def step(ki, buf, active, mask_of, prefetch):
        for qt, hh in active:
            if (qt, hh) in prefetch:
                logits(qt, hh, key_tile(k_ref, ki + 1, hh), 1 - buf, t)
            update(qt, hh, key_tile(v_ref, ki, hh), buf, t, mask_of(qt), False)

    def pair_body(j, carry):
        step(2 * j, 0, chains, lambda qt: None, chains)
        step(2 * j + 1, 1, chains, lambda qt: None, chains)
        return carry

    lax.fori_loop(0, (nq // 2) * pl.program_id(2), pair_body, 0)
    for r in range(nq):
        active = [(qt, hh) for qt, hh in chains if qt >= r]
        step(first_tile + r, r % 2, active, lambda qt, r=r: causal_mask if qt == r else None,
             [(qt, hh) for qt, hh in active if qt > r])

    lane = lax.broadcasted_iota(jnp.int32, (t, LANES), 1)
    for qt in range(nq):
        outs = []
        for hh in range(2):
            acc = acc_sc[chain_id(qt, hh)]
            ones_lane = (1 - hh) * FOX_HD
            outs.append(acc / acc[:, ones_lane:ones_lane + 1])
        o_ref[qt * t:(qt + 1) * t, :] = jnp.where(lane < FOX_HD, outs[0], outs[1]).astype(_BF16)


def _fox(fq, fk, fv, fk_m, fv_m, *, nb):
    s = fq.shape[1]
    t = T_FOX
    pair_w = 2 * LANES
    n_chain = 2 * FOX_Q_TILES
    return pl.pallas_call(
        _fox_body,
        grid=(nb, FOX_HEADS // 2, s // (FOX_Q_TILES * t)),
        in_specs=[
            pl.BlockSpec((None, FOX_Q_TILES * t, pair_w), lambda b, p, i: (b, i, p)),
            pl.BlockSpec((None, s, pair_w), lambda b, p, i: (b, 0, p)),
            pl.BlockSpec((None, s, pair_w), lambda b, p, i: (b, 0, p)),
            pl.BlockSpec((BLOCK, pair_w), lambda b, p, i: (0, p)),
            pl.BlockSpec((BLOCK, pair_w), lambda b, p, i: (0, p)),
        ],
        out_specs=pl.BlockSpec((None, FOX_Q_TILES * t, LANES), lambda b, p, i: (b, i, p)),
        out_shape=jax.ShapeDtypeStruct((nb, s, HEAD_W), _BF16),
        scratch_shapes=[pltpu.VMEM((n_chain, t, 1), _F32), pltpu.VMEM((n_chain, t, LANES), _F32),
                        pltpu.VMEM((2, n_chain, t, t), _F32), pltpu.VMEM((n_chain, t, t), _BF16)],
        compiler_params=_params(3),
        name="fox",
    )(fq, fk, fv, fk_m, fv_m)


def _route(scores, biased):
    w = scores.shape[1]
    sub = lax.broadcasted_iota(jnp.int32, (GROUP_SIZE, w), 0).astype(_F32)
    groups = [biased[g * GROUP_SIZE:(g + 1) * GROUP_SIZE, :] for g in range(N_GROUPS)]
    gscore = []
    for v in groups:
        m1 = jnp.max(v, axis=0, keepdims=True)
        i1 = jnp.min(jnp.where(v == m1, sub, float(GROUP_SIZE)), axis=0, keepdims=True)
        m2 = jnp.max(jnp.where(sub == i1, -jnp.inf, v), axis=0, keepdims=True)
        gscore.append(m1 + m2)
    masked = []
    for g in range(N_GROUPS):
        beaten = jnp.zeros((1, w), _F32)
        for o in range(N_GROUPS):
            if o == g:
                continue
            wins = (gscore[o] >= gscore[g]) if o < g else (gscore[o] > gscore[g])
            beaten = beaten + wins.astype(_F32)
        masked.append(jnp.where(beaten < float(TOPK_GROUPS), groups[g], NEG_INF))
    work = jnp.concatenate(masked, axis=0)
    eid = lax.broadcasted_iota(jnp.int32, (N_EXPERTS, w), 0).astype(_F32)
    sels, raws = [], []
    member = None
    for _ in range(TOP_K):
        m = jnp.max(work, axis=0, keepdims=True)
        idx = jnp.min(jnp.where(work == m, eid, float(N_EXPERTS)), axis=0, keepdims=True)
        hot = eid == idx
        sels.append(idx)
        raws.append(jnp.sum(jnp.where(hot, scores, 0.0), axis=0, keepdims=True))
        work = jnp.where(hot, -jnp.inf, work)
        member = hot if member is None else (member | hot)
    return sels, raws, member


def _post_body(h0_ref, ret_ref, fox_ref, wo_ref, g1_ref, b1_ref, wr_ref, rb_ref,
               trow_ref, sel_ref, gate_ref, rank_ref, cnt_ref, run_ref):
    tm = h0_ref.shape[0]
    i = pl.program_id(0)

    @pl.when(i == 0)
    def _():
        run_ref[...] = jnp.zeros_like(run_ref)

    parts = [slice(c * tm // 4, (c + 1) * tm // 4) for c in range(4)]
    ys = [_dot(ret_ref[rs, :], wo_ref[:HEAD_W, :]) + _dot(fox_ref[rs, :], wo_ref[HEAD_W:, :]) for rs in parts]
    h1s = [_ln(ALPHA * h0_ref[rs, :] + y, g1_ref[...], b1_ref[...]) for rs, y in zip(parts, ys)]
    for rs, h1 in zip(parts, h1s):
        _store_rows(trow_ref, h1, rs.start)

    scores = jnp.concatenate([jax.nn.sigmoid(_dot_nt(wr_ref[...], h1.astype(_BF16))) for h1 in h1s], axis=1)
    biased = scores + rb_ref[...]
    chunks = [slice(c * LANES, (c + 1) * LANES) for c in range(tm // LANES)]
    routed = [_route(scores[:, cs], biased[:, cs]) for cs in chunks]
    member_f = jnp.concatenate([member.astype(_F32) for _, _, member in routed], axis=1)
    row = lax.broadcasted_iota(jnp.int32, (tm, tm), 0)
    col = lax.broadcasted_iota(jnp.int32, (tm, tm), 1)
    before = (row < col).astype(_BF16)
    rank_e = _dot(member_f.astype(_BF16), before) + run_ref[...]
    eid = lax.broadcasted_iota(jnp.int32, (N_EXPERTS, LANES), 0).astype(_F32)
    sel_rows, gate_rows, rank_rows = [], [], []
    for k in range(TOP_K):
        sel_k, gate_k, rank_k = [], [], []
        for cs, (sels, raws, _) in zip(chunks, routed):
            total = raws[0]
            for r in raws[1:]:
                total = total + r
            sel_k.append(sels[k].astype(jnp.int32))
            gate_k.append(raws[k] * (ROUTED_SCALE / total))
            rank_k.append(jnp.sum(jnp.where(eid == sels[k], rank_e[:, cs], 0.0), axis=0, keepdims=True).astype(jnp.int32))
        sel_rows.append(jnp.concatenate(sel_k, axis=1))
        gate_rows.append(jnp.concatenate(gate_k, axis=1))
        rank_rows.append(jnp.concatenate(rank_k, axis=1))
    pad_rows = sel_ref.shape[0] - TOP_K
    sel_ref[...] = jnp.concatenate(sel_rows + [jnp.zeros((pad_rows, tm), jnp.int32)], axis=0)
    gate_ref[...] = jnp.concatenate(gate_rows + [jnp.zeros((pad_rows, tm), _F32)], axis=0)
    rank_ref[...] = jnp.concatenate(rank_rows + [jnp.zeros((pad_rows, tm), jnp.int32)], axis=0)
    run_ref[...] = run_ref[...] + jnp.sum(member_f, axis=1, keepdims=True)
    cnt_ref[...] = run_ref[...]


def _post(h0, ret, fox, w_out, ln1_g, ln1_b, w_rt, rbias):
    n = h0.shape[0]
    tm = TM_PROJ
    row_spec = lambda w: pl.BlockSpec((tm, w), lambda i: (i, 0))
    col_spec = pl.BlockSpec((SUBLANES, tm), lambda i: (0, i))
    const = lambda shape: pl.BlockSpec(shape, lambda i: (0, 0))
    return pl.pallas_call(
        _post_body,
        grid=(n // tm,),
        in_specs=[row_spec(D_MODEL), row_spec(HEAD_W), row_spec(HEAD_W),
                  const(w_out.shape), const((1, D_MODEL)), const((1, D_MODEL)), const(w_rt.shape),
                  const((N_EXPERTS, 1))],
        out_specs=[pl.BlockSpec((tm * ROW_TILES, LANES), lambda i: (i, 0)),
                   col_spec, col_spec, col_spec, const((N_EXPERTS, 1))],
        out_shape=[jax.ShapeDtypeStruct((n * ROW_TILES, LANES), ROW_DTYPE),
                   jax.ShapeDtypeStruct((SUBLANES, n), jnp.int32), jax.ShapeDtypeStruct((SUBLANES, n), _F32),
                   jax.ShapeDtypeStruct((SUBLANES, n), jnp.int32), jax.ShapeDtypeStruct((N_EXPERTS, 1), _F32)],
        scratch_shapes=[pltpu.VMEM((N_EXPERTS, 1), _F32)],
        compiler_params=_params(1),
        name="post_mixer",
    )(h0, ret, fox, w_out, ln1_g, ln1_b, w_rt, rbias)


def _plan_body(sel_ref, rank_ref, cnt_ref, dest_ref, blk_ref, fill_ref, used_ref, tab_ref):
    cnt = cnt_ref[...]
    padded = jnp.ceil(cnt * (1.0 / EBLK)) * EBLK
    er = lax.broadcasted_iota(jnp.int32, (N_EXPERTS, N_EXPERTS), 0)
    ec = lax.broadcasted_iota(jnp.int32, (N_EXPERTS, N_EXPERTS), 1)
    padded_row = jnp.sum(jnp.where(er == ec, padded, 0.0), axis=0, keepdims=True)
    pstart = jnp.sum(jnp.where(ec < er, padded_row, 0.0), axis=1, keepdims=True)
    pend = pstart + padded
    sel = sel_ref[...]
    dest = rank_ref[...]
    for e in range(N_EXPERTS):
        dest = dest + jnp.where(sel == e, pstart[e:e + 1, :].astype(jnp.int32), 0)
    dest_ref[...] = dest
    nblk = blk_ref.shape[1]
    first_row = (lax.broadcasted_iota(jnp.int32, (N_EXPERTS, nblk), 1) * EBLK).astype(_F32)
    owner = jnp.minimum(jnp.sum((pend <= first_row).astype(_F32), axis=0, keepdims=True), N_EXPERTS - 1.0)
    blk_ref[...] = owner.astype(jnp.int32)
    mine = lax.broadcasted_iota(jnp.int32, (N_EXPERTS, nblk), 0).astype(_F32) == owner
    live_end = jnp.sum(jnp.where(mine, pstart + cnt, 0.0), axis=0, keepdims=True)
    fill_ref[...] = jnp.clip(live_end - first_row[:1, :], 0.0, float(EBLK)).astype(jnp.int32)
    used_ref[...] = (pend[N_EXPERTS - 1:, :] * (1.0 / EBLK)).astype(jnp.int32)
    as_row = lambda col: jnp.sum(jnp.where(er == ec, col, 0.0), axis=0, keepdims=True).astype(jnp.int32)
    rows = {TAB_PAD0: pstart + cnt, TAB_PADN: padded - cnt}
    blank = jnp.zeros((1, N_EXPERTS), jnp.int32)
    tab_ref[...] = jnp.concatenate([as_row(rows[r]) if r in rows else blank for r in range(SUBLANES)], axis=0)


def _plan(sel, rank, cnt, nblk_pad):
    n = sel.shape[1]
    full = lambda shape: pl.BlockSpec(shape, lambda i: (0, 0))
    return pl.pallas_call(
        _plan_body,
        grid=(1,),
        in_specs=[full(sel.shape), full(rank.shape), full(cnt.shape)],
        out_specs=[full(sel.shape), full((1, nblk_pad)), full((1, nblk_pad)), full((1, 1)),
                   full((SUBLANES, N_EXPERTS))],
        out_shape=[jax.ShapeDtypeStruct((SUBLANES, n), jnp.int32), jax.ShapeDtypeStruct((1, nblk_pad), jnp.int32),
                   jax.ShapeDtypeStruct((1, nblk_pad), jnp.int32), jax.ShapeDtypeStruct((1, 1), jnp.int32),
                   jax.ShapeDtypeStruct((SUBLANES, N_EXPERTS), jnp.int32)],
        compiler_params=_params(1),
        name="plan",
    )(sel, rank, cnt)


def _store_rows(ref, v, first_row=0):
    m = v.shape[0]
    for s in range(ROW_TILES):
        ref[pl.ds(first_row * ROW_TILES + s, m, stride=ROW_TILES), :] = v[:, s * LANES:(s + 1) * LANES].astype(ROW_DTYPE)


def _load_rows(ref, first_row, m):
    return jnp.concatenate([ref[pl.ds(first_row * ROW_TILES + s, m, stride=ROW_TILES), :] for s in range(ROW_TILES)],
                           axis=1)


def _row_copy(src, src_row, dst, dst_row, sem):
    return pltpu.make_async_copy(src.at[pl.ds(pl.multiple_of(src_row * ROW_TILES, ROW_TILES), ROW_TILES), :],
                                 dst.at[pl.ds(pl.multiple_of(dst_row * ROW_TILES, ROW_TILES), ROW_TILES), :], sem)


def _dispatch_body(dest_ref, tab_ref, used_ref, t_ref, wgu_ref, wd_ref, xs_ref, base_ref, zero_sc, sem, zsem):
    tt = t_ref.shape[0] // ROW_TILES
    half_blk = EBLK // 2
    n_half = xs_ref.shape[0] // (half_blk * ROW_TILES)

    @pl.when(pl.program_id(0) == 0)
    def _():
        zero_sc[...] = jnp.zeros_like(zero_sc)

        def zero_copy(first_row, rows):
            return pltpu.make_async_copy(
                zero_sc.at[pl.ds(0, rows * ROW_TILES), :],
                xs_ref.at[pl.ds(pl.multiple_of(first_row * ROW_TILES, ROW_TILES), rows * ROW_TILES), :], zsem)

        def for_padding(act):
            def per_expert(e, carry):
                row = tab_ref[TAB_PAD0, e]
                for bit in range(EBLK.bit_length() - 2, -1, -1):
                    take = (tab_ref[TAB_PADN, e] & (1 << bit)) != 0

                    @pl.when(take)
                    def _():
                        act(zero_copy(row, 1 << bit))

                    row = row + jnp.where(take, 1 << bit, 0)
                return carry

            def per_tail(hb, carry):
                act(zero_copy(hb * half_blk, half_blk))
                return carry

            lax.fori_loop(0, N_EXPERTS, per_expert, 0)
            lax.fori_loop(2 * used_ref[0], n_half, per_tail, 0)

        for_padding(lambda cp: cp.start())
        for_padding(lambda cp: cp.wait())

    def issue(j, carry):
        for u in range(ISSUE_UNROLL):
            i = ISSUE_UNROLL * j + u
            for k in range(TOP_K):
                _row_copy(t_ref, i, xs_ref, dest_ref[k, i], sem).start(priority=k % 2)
        return carry

    lax.fori_loop(0, tt // ISSUE_UNROLL, issue, 0)

    h1 = _load_rows(t_ref, 0, tt)
    gu = _dot(h1.astype(_BF16), wgu_ref[...])
    mid = (_silu(gu[:, :SHARED_FF]) * gu[:, SHARED_FF:]).astype(_BF16)
    base_ref[...] = ALPHA * h1 + _dot(mid, wd_ref[...])

    for _ in range(TOP_K):
        pltpu.make_async_copy(t_ref, xs_ref.at[pl.ds(0, tt * ROW_TILES), :], sem).wait()


def _dispatch(dest_flat, tab, used, trow, w_gu, w_sd, total_rows):
    n = trow.shape[0] // ROW_TILES
    tt = TT_DISPATCH
    smem = pl.BlockSpec(memory_space=pltpu.SMEM)
    const = lambda shape: pl.BlockSpec(shape, lambda i: (0, 0))
    return pl.pallas_call(
        _dispatch_body,
        grid=(n // tt,),
        in_specs=[pl.BlockSpec((SUBLANES, tt), lambda i: (0, i), memory_space=pltpu.SMEM), smem, smem,
                  pl.BlockSpec((tt * ROW_TILES, LANES), lambda i: (i, 0)), const(w_gu.shape), const(w_sd.shape)],
        out_specs=[pl.BlockSpec(memory_space=pl.ANY), pl.BlockSpec((tt, D_MODEL), lambda i: (i, 0))],
        out_shape=[jax.ShapeDtypeStruct((total_rows * ROW_TILES, LANES), ROW_DTYPE),
                   jax.ShapeDtypeStruct((n, D_MODEL), _F32)],
        scratch_shapes=[pltpu.VMEM((EBLK // 2 * ROW_TILES, LANES), ROW_DTYPE), pltpu.SemaphoreType.DMA(()),
                        pltpu.SemaphoreType.DMA(())],
        compiler_params=_params(1),
        name="dispatch",
    )(dest_flat, tab, used, trow, w_gu, w_sd)


def _expert_body(blk_ref, used_ref, fill_ref, xs_ref, wg_ref, wu_ref, wd_ref, y_ref, wgu_sc, wd_sc):
    i = pl.program_id(0)
    prev = blk_ref[jnp.maximum(i - 1, 0)]
    fresh = (i == 0) | (blk_ref[i] != prev)

    @pl.when(fresh)
    def _():
        wgu_sc[:, :EXPERT_FF] = wg_ref[...].astype(_BF16)
        wgu_sc[:, EXPERT_FF:] = wu_ref[...].astype(_BF16)
        wd_sc[...] = wd_ref[...].astype(_BF16)

    @pl.when(i < used_ref[0])
    def _():
        live = lax.broadcasted_iota(jnp.int32, (EBLK, 1), 0) < fill_ref[i]
        x = _load_rows(xs_ref, 0, EBLK)
        x = jnp.where(live, x, jnp.zeros_like(x)).astype(_BF16)
        gu = _dot(x, wgu_sc[...])
        mid = (_silu(gu[:, :EXPERT_FF]) * gu[:, EXPERT_FF:]).astype(_BF16)
        _store_rows(y_ref, _dot(mid, wd_sc[...]))

    @pl.when(i >= used_ref[0])
    def _():
        y_ref[...] = jnp.zeros_like(y_ref)


def _experts(blk_e, used, fill, xs, we_gate, we_up, we_down):
    nblk = xs.shape[0] // (EBLK * ROW_TILES)
    last = lambda i, used: jnp.minimum(i, jnp.maximum(used[0] - 1, 0))
    w_spec = lambda shape: pl.BlockSpec((None,) + shape, lambda i, blk, used, fill: (blk[i], 0, 0))
    return pl.pallas_call(
        _expert_body,
        grid_spec=pltpu.PrefetchScalarGridSpec(
            num_scalar_prefetch=3,
            grid=(nblk,),
            in_specs=[pl.BlockSpec((EBLK * ROW_TILES, LANES), lambda i, blk, used, fill: (last(i, used), 0)),
                      w_spec((D_MODEL, EXPERT_FF)), w_spec((D_MODEL, EXPERT_FF)), w_spec((EXPERT_FF, D_MODEL))],
            out_specs=pl.BlockSpec((EBLK * ROW_TILES, LANES), lambda i, blk, used, fill: (i, 0)),
            scratch_shapes=[pltpu.VMEM((D_MODEL, 2 * EXPERT_FF), _BF16), pltpu.VMEM((EXPERT_FF, D_MODEL), _BF16)],
        ),
        out_shape=jax.ShapeDtypeStruct(xs.shape, ROW_DTYPE),
        compiler_params=_params(1),
        name="experts",
    )(blk_e, used, fill, xs, we_gate, we_up, we_down)


def _combine_body(dest_ref, dnext_ref, y_ref, base_ref, gate_ref, g2_ref, b2_ref, o_ref, z_sc, sems):
    i = pl.program_id(0)
    slot = lax.rem(i, 2)
    tile_rows = TT * TOP_K * ROW_TILES

    def gather(dref, into):
        def issue(j, carry):
            for u in range(ISSUE_UNROLL):
                t = ISSUE_UNROLL * j + u
                for k in range(TOP_K):
                    _row_copy(y_ref, dref[k, t], z_sc.at[into], k * TT + t,
                              sems.at[into]).start(priority=k % 2)
            return carry

        lax.fori_loop(0, TT // ISSUE_UNROLL, issue, 0)

    @pl.when(i == 0)
    def _():
        gather(dest_ref, 0)

    @pl.when(i + 1 < pl.num_programs(0))
    def _():
        gather(dnext_ref, 1 - slot)

    pltpu.make_async_copy(y_ref.at[pl.ds(0, tile_rows), :], z_sc.at[slot], sems.at[slot]).wait()
    gates = gate_ref[...]
    acc = base_ref[...]
    for k in range(TOP_K):
        acc = acc + gates[:, k:k + 1] * _load_rows(z_sc.at[slot], k * TT, TT).astype(_F32)
    o_ref[...] = _ln(acc, g2_ref[...], b2_ref[...])


def _combine(dest_flat, y, base, gates_t, ln2_g, ln2_b):
    n = base.shape[0]
    steps = n // TT
    const = lambda shape: pl.BlockSpec(shape, lambda i: (0, 0))
    dest_spec = lambda ahead: pl.BlockSpec((SUBLANES, TT), lambda i: (0, jnp.minimum(i + ahead, steps - 1)),
                                           memory_space=pltpu.SMEM)
    return pl.pallas_call(
        _combine_body,
        grid=(steps,),
        in_specs=[dest_spec(0), dest_spec(1),
                  pl.BlockSpec(memory_space=pl.ANY),
                  pl.BlockSpec((TT, D_MODEL), lambda i: (i, 0)),
                  pl.BlockSpec((TT, SUBLANES), lambda i: (i, 0)),
                  const((1, D_MODEL)), const((1, D_MODEL))],
        out_specs=pl.BlockSpec((TT, D_MODEL), lambda i: (i, 0)),
        out_shape=jax.ShapeDtypeStruct((n, D_MODEL), _F32),
        scratch_shapes=[pltpu.VMEM((2, TT * TOP_K * ROW_TILES, LANES), ROW_DTYPE), pltpu.SemaphoreType.DMA((2,))],
        compiler_params=_params(1),
        name="combine",
    )(dest_flat, dest_flat, y, base, gates_t, ln2_g, ln2_b)


def _rope_tables(pos):
    half = RET_DK // 2
    inv = ROPE_BASE ** (-jnp.arange(half, dtype=_F32) / half)
    ang = pos[:, None] * inv[None, :]
    cos = jnp.cos(ang)
    sin = jnp.sin(ang)
    return jnp.concatenate([cos, cos], -1), jnp.concatenate([-sin, sin], -1)


def _decay_tables():
    lg = jnp.log1p(-jnp.exp2(-5.0 - jnp.arange(RET_HEADS, dtype=_F32)))
    idx = jnp.arange(BLOCK, dtype=_F32)
    rel = idx[:, None] - idx[None, :]
    causal = rel >= 0
    dmask = jnp.where(causal[None], jnp.exp(jnp.where(causal, rel, 0.0)[None] * lg[:, None, None]), 0.0)
    zeta = jnp.exp((BLOCK - 1.0 - idx)[None, :] * lg[:, None])
    xi = jnp.exp((idx + 1.0)[None, :] * lg[:, None])
    along_lanes = lambda col: jnp.broadcast_to(col[:, :, None], (RET_HEADS, BLOCK, LANES))
    return dmask, along_lanes(xi), along_lanes(zeta), jnp.exp(BLOCK * lg)


def kernel(x, meta, ln0_g, ln0_b, w_in, b_forget, w_out, ln1_g, ln1_b, w_router, router_bias, we_gate, we_up,
           we_down, ws_gate, ws_up, ws_down, ln2_g, ln2_b):
    nb, s, d = x.shape
    assert d == D_MODEL and meta.shape == (N_META, D_MODEL) and w_in.shape[0] == 1
    assert s % TM_PROJ == 0 and s % (FOX_Q_TILES * T_FOX) == 0 and (nb * s) % TT == 0
    n = nb * s
    x2d = x.reshape(n, d)
    row2 = lambda v: v.reshape(1, -1).astype(_F32)
    main_cols = 7 * HEAD_W
    w_all = jnp.concatenate(
        [w_in[0, :, :main_cols], w_in[0, :, main_cols:], jnp.zeros((d, LANES - FOX_HEADS), w_in.dtype)],
        axis=1).astype(_BF16)
    bf_pad = jnp.concatenate([b_forget[0].astype(_F32), jnp.zeros((LANES - FOX_HEADS,), _F32)]).reshape(1, LANES)
    g0, b0 = row2(ln0_g), row2(ln0_b)

    cos_x, sin_x = _rope_tables(jnp.arange(s, dtype=_F32) + float(N_META))
    cos_m, sin_m = _rope_tables(jnp.arange(BLOCK, dtype=_F32) - float(PAD))
    meta_blk = jnp.concatenate([jnp.zeros((PAD, d), _F32), meta.astype(_F32)], axis=0)

    h0, rq, rk, rv, rg, fq, fk, fv = _inproj(x2d, g0, b0, w_all, bf_pad, cos_x, sin_x, nb=nb, meta=False)
    _, _, rk_m, rv_m, _, _, fk_m, fv_m = _inproj(meta_blk, g0, b0, w_all, bf_pad, cos_m, sin_m, nb=1, meta=True)

    dmask, xi, zeta, gch = _decay_tables()
    per_batch = lambda a: a.reshape(nb, s, a.shape[-1])
    ret = _retention(per_batch(rq), per_batch(rk), per_batch(rv), per_batch(rg), rk_m, rv_m, dmask, xi, zeta,
                     gch).reshape(n, HEAD_W)
    fox = _fox(per_batch(fq), per_batch(fk), per_batch(fv), fk_m, fv_m, nb=nb).reshape(n, HEAD_W)

    w_gu = jnp.concatenate([ws_gate[0], ws_up[0]], axis=1).astype(_BF16)
    trow, sel, gates, rank, cnt = _post(
        h0, ret, fox, w_out[0].astype(_BF16), row2(ln1_g[0]), row2(ln1_b[0]),
        jnp.transpose(w_router[0]).astype(_BF16), router_bias[0].astype(_F32).reshape(N_EXPERTS, 1))

    nblk = n * TOP_K // EBLK + N_EXPERTS
    nblk_pad = -(-nblk // LANES) * LANES
    dest, blk_e, fill, used, tab = _plan(sel, rank, cnt, nblk_pad)
    dest_flat = dest
    used = used.reshape(-1)

    xs, base = _dispatch(dest_flat, tab, used, trow, w_gu, ws_down[0].astype(_BF16), nblk * EBLK)
    y = _experts(blk_e.reshape(-1), used, fill.reshape(-1), xs, we_gate[0], we_up[0], we_down[0])
    out = _combine(dest_flat, y, base, jnp.transpose(gates), row2(ln2_g[0]), row2(ln2_b[0]))
    return out.reshape(nb, s, d)
```
